```python
import math
import jax, jax.numpy as jnp
from jax import lax
import numpy as np

D_MODEL = 1024
BATCH = 16
SEQ = 4096
DEPTH = 1
DEC_BATCH = 128
DEC_SEQ = 1
PAST_LEN = 8192
PAGE_SIZE = 128

SSM_GROUP = 16
D_SSM = D_MODEL // 2
N_SSM_GROUPS = D_SSM // SSM_GROUP
SSM_STATE = 64
DT_MIN = 1e-3
DT_MAX = 1e-1
N_HEADS = 8
N_KV_HEADS = 2
HEAD_DIM = 64
D_ATTN = N_HEADS * HEAD_DIM
ROT_DIM = HEAD_DIM // 4
ROPE_THETA = 500000.0
N_IDX_HEADS = 8
IDX_DIM = 64
IDX_ROT_DIM = IDX_DIM // 4
TOPK_MAX = 256
Q_BLOCK = 128
D_FF = 4 * D_MODEL
EPS = 1e-6
SPLITS = (D_SSM, D_ATTN, N_KV_HEADS * HEAD_DIM, N_KV_HEADS * HEAD_DIM,
          N_IDX_HEADS * IDX_DIM, IDX_DIM, N_IDX_HEADS, D_MODEL, D_MODEL)
IN_COLS = D_SSM + D_ATTN + 2 * N_KV_HEADS * HEAD_DIM + N_IDX_HEADS * IDX_DIM + IDX_DIM + N_IDX_HEADS + 2 * D_MODEL

kernel_name = 'hybrid_s5_dsa_gated_decoder_step'


def rms_norm(x, g):
    xf = x.astype(jnp.float32)
    y = xf * lax.rsqrt(jnp.mean(xf * xf, axis=-1, keepdims=True) + EPS)
    return (y * g.astype(jnp.float32)).astype(x.dtype)


def partial_rope(x, pos, rot_dim):
    half = rot_dim // 2
    inv = ROPE_THETA ** (-jnp.arange(half, dtype=jnp.float32) / half)
    ang = pos.astype(jnp.float32)[:, None] * inv[None, :]
    cos = jnp.cos(ang)[:, None, :]
    sin = jnp.sin(ang)[:, None, :]
    xr = x[..., :rot_dim].astype(jnp.float32)
    x1, x2 = xr[..., :half], xr[..., half:]
    rot = jnp.concatenate([x1 * cos - x2 * sin, x1 * sin + x2 * cos], axis=-1)
    return jnp.concatenate([rot.astype(x.dtype), x[..., rot_dim:]], axis=-1)


def _ssm_combine(e1, e2):
    a1, b1 = e1
    a2, b2 = e2
    return a1 * a2, a2 * b1 + b2


def s5_branch(u, h0_re, h0_im, a_re, a_im, log_dt, b_re, b_im, c_re, c_im, d_skip, w_glu, b_glu):
    f32 = jnp.float32
    bsz, t_len, _ = u.shape
    a_c = lax.complex(a_re.astype(f32), a_im.astype(f32))
    dt = jnp.exp(log_dt.astype(f32))[:, None]
    a_bar = jnp.exp(a_c * dt)
    b_mat = lax.complex(b_re.astype(f32), b_im.astype(f32))
    b_bar = ((a_bar - 1.0) / a_c)[:, :, None] * b_mat
    c_mat = lax.complex(c_re.astype(f32), c_im.astype(f32))
    uf = u.astype(f32)
    ug = uf.reshape(bsz, t_len, N_SSM_GROUPS, SSM_GROUP).astype(jnp.complex64)
    bu = jnp.einsum('gpn,btgn->btgp', b_bar, ug)
    if h0_re is not None:
        h0 = lax.complex(h0_re.astype(f32), h0_im.astype(f32))
        bu = bu.at[:, 0].add(a_bar * h0)
    a_seq = jnp.broadcast_to(a_bar, bu.shape)
    _, h = lax.associative_scan(_ssm_combine, (a_seq, bu), axis=1)
    y = jnp.real(jnp.einsum('gnp,btgp->btgn', c_mat, h)).reshape(bsz, t_len, D_SSM)
    y = y + d_skip.astype(f32) * uf
    z = jax.nn.gelu(y).astype(u.dtype) @ w_glu + b_glu
    val, gate = jnp.split(z, 2, axis=-1)
    out = val * jax.nn.sigmoid(gate)
    h_last = h[:, -1]
    return out, jnp.real(h_last), jnp.imag(h_last)


def index_topk(qi, wi, ki, q_pos, k_pos, k_top):
    f32 = jnp.float32
    dots = jnp.einsum('bthd,bsd->btsh', qi.astype(f32), ki.astype(f32)) * (IDX_DIM ** -0.5)
    score = jnp.einsum('btsh,bth->bts', jax.nn.relu(dots), wi.astype(f32))
    causal = k_pos[None, None, :] <= q_pos[None, :, None]
    score = jnp.where(causal, score, -jnp.inf)
    _, idx = lax.top_k(score, k_top)
    return idx


def sparse_attend(q, k_sel, v_sel, valid):
    f32 = jnp.float32
    bsz, t_len = q.shape[:2]
    qg = q.reshape(bsz, t_len, N_KV_HEADS, N_HEADS // N_KV_HEADS, HEAD_DIM).astype(f32)
    s = jnp.einsum('btgrd,btkgd->btgrk', qg, k_sel.astype(f32)) * (HEAD_DIM ** -0.5)
    s = jnp.where(valid[:, :, None, None, :], s, -jnp.inf)
    p = jax.nn.softmax(s, axis=-1)
    o = jnp.einsum('btgrk,btkgd->btgrd', p, v_sel.astype(f32))
    return o.reshape(bsz, t_len, D_ATTN).astype(q.dtype)


def prompt_sparse_attention(q, k, v, qi, ki, wi, pos):
    bsz, s_len = q.shape[:2]
    k_top = min(TOPK_MAX, s_len // 4)
    nb = s_len // Q_BLOCK
    bidx = jnp.arange(bsz)[:, None, None]

    def to_blocks(a):
        return jnp.moveaxis(a.reshape(bsz, nb, Q_BLOCK, *a.shape[2:]), 1, 0)

    def block(args):
        q_b, qi_b, wi_b, pos_b = args
        idx = index_topk(qi_b, wi_b, ki, pos_b, pos, k_top)
        valid = idx <= pos_b[None, :, None]
        return sparse_attend(q_b, k[bidx, idx], v[bidx, idx], valid)

    out = lax.map(block, (to_blocks(q), to_blocks(qi), to_blocks(wi), pos.reshape(nb, Q_BLOCK)))
    return jnp.moveaxis(out, 0, 1).reshape(bsz, s_len, D_ATTN)


def sample_sparse_attention(q, k, v, qi, ki, wi, pos, cache_k, cache_v, cache_idx_k, page_table):
    bsz, t_len = q.shape[:2]
    n_pages = PAST_LEN // PAGE_SIZE
    l_keys = PAST_LEN + t_len
    k_top = min(TOPK_MAX, l_keys // 4)
    ki_past = cache_idx_k[page_table].reshape(bsz, n_pages * PAGE_SIZE, IDX_DIM)
    ki_all = jnp.concatenate([ki_past.astype(ki.dtype), ki], axis=1)
    k_pos = jnp.arange(l_keys, dtype=jnp.int32)
    idx = index_topk(qi, wi, ki_all, pos, k_pos, k_top)
    bidx = jnp.arange(bsz)[:, None, None]
    in_past = (idx < PAST_LEN)[..., None, None]
    pidx = jnp.minimum(idx, PAST_LEN - 1)
    phys = page_table[bidx, pidx // PAGE_SIZE]
    off = pidx % PAGE_SIZE
    nidx = jnp.clip(idx - PAST_LEN, 0, t_len - 1)
    k_sel = jnp.where(in_past, cache_k[phys, off], k[bidx, nidx])
    v_sel = jnp.where(in_past, cache_v[phys, off], v[bidx, nidx])
    valid = idx <= pos[None, :, None]
    return sparse_attend(q, k_sel, v_sel, valid)


def decoder_layer(x, pos, attend, h0_re, h0_im, p):
    bsz, t_len, _ = x.shape
    h = rms_norm(x, p['norm1_g'])
    proj = h @ p['w_in']
    split_points = [int(c) for c in np.cumsum(SPLITS)[:-1]]
    u, q, k, v, qi, ki, wi, g_ssm, g_attn = jnp.split(proj, split_points, axis=-1)
    q = partial_rope(q.reshape(bsz, t_len, N_HEADS, HEAD_DIM), pos, ROT_DIM)
    k = partial_rope(k.reshape(bsz, t_len, N_KV_HEADS, HEAD_DIM), pos, ROT_DIM)
    v = v.reshape(bsz, t_len, N_KV_HEADS, HEAD_DIM)
    qi = partial_rope(qi.reshape(bsz, t_len, N_IDX_HEADS, IDX_DIM), pos, IDX_ROT_DIM)
    ki = partial_rope(ki.reshape(bsz, t_len, 1, IDX_DIM), pos, IDX_ROT_DIM)[:, :, 0]
    wi = wi * (N_IDX_HEADS ** -0.5)
    ssm_out, hT_re, hT_im = s5_branch(u, h0_re, h0_im, p['ssm_a_re'], p['ssm_a_im'], p['ssm_log_dt'],
                                      p['ssm_b_re'], p['ssm_b_im'], p['ssm_c_re'], p['ssm_c_im'],
                                      p['ssm_d'], p['w_glu'], p['b_glu'])
    attn_out = attend(q, k, v, qi, ki, wi, pos) @ p['w_attn_out']
    mix = jax.nn.sigmoid(g_ssm) * ssm_out + jax.nn.sigmoid(g_attn) * attn_out
    x = x + mix @ p['w_o']
    hh = rms_norm(x, p['norm2_g'])
    x = x + jnp.square(jax.nn.relu(hh @ p['w_up'])) @ p['w_down']
    return x, k, v, ki, hT_re, hT_im


def setup_inputs(seed: int = 0) -> dict:
    key = jax.random.key(seed)
    ks = jax.random.split(key, 32)
    f32 = jnp.float32
    n_pages = PAST_LEN // PAGE_SIZE
    n_used = DEC_BATCH * n_pages
    n_pool = n_used + max(1, n_used // 4)

    def nrm(k, shape, scale):
        return scale * jax.random.normal(k, shape, f32)

    g, pst, n = N_SSM_GROUPS, SSM_STATE, SSM_GROUP
    page_table = jax.random.permutation(ks[7], n_pool)[:n_used].reshape(DEC_BATCH, n_pages).astype(jnp.int32)
    return {
        'x_prompt': nrm(ks[0], (BATCH, SEQ, D_MODEL), 1.0),
        'x_sample': nrm(ks[1], (DEC_BATCH, DEC_SEQ, D_MODEL), 1.0),
        'cache_k': nrm(ks[2], (n_pool, PAGE_SIZE, N_KV_HEADS, HEAD_DIM), 1.0),
        'cache_v': nrm(ks[3], (n_pool, PAGE_SIZE, N_KV_HEADS, HEAD_DIM), 1.0),
        'cache_idx_k': nrm(ks[4], (n_pool, PAGE_SIZE, IDX_DIM), 1.0),
        'state_ssm_re': nrm(ks[5], (DEC_BATCH, g, pst), 0.5),
        'state_ssm_im': nrm(ks[6], (DEC_BATCH, g, pst), 0.5),
        'page_table': page_table,
        'norm1_g': 1.0 + nrm(ks[8], (D_MODEL,), 0.01),
        'w_in': nrm(ks[9], (D_MODEL, IN_COLS), D_MODEL ** -0.5),
        'ssm_a_re': -0.5 + nrm(ks[10], (g, pst), 0.01),
        'ssm_a_im': math.pi * jnp.arange(pst, dtype=f32)[None, :] + nrm(ks[11], (g, pst), 0.01),
        'ssm_log_dt': jax.random.uniform(ks[12], (g,), f32, math.log(DT_MIN), math.log(DT_MAX)),
        'ssm_b_re': nrm(ks[13], (g, pst, n), (2 * n) ** -0.5),
        'ssm_b_im': nrm(ks[14], (g, pst, n), (2 * n) ** -0.5),
        'ssm_c_re': nrm(ks[15], (g, n, pst), pst ** -0.5),
        'ssm_c_im': nrm(ks[16], (g, n, pst), pst ** -0.5),
        'ssm_d': nrm(ks[17], (D_SSM,), 1.0),
        'w_glu': nrm(ks[18], (D_SSM, 2 * D_MODEL), D_SSM ** -0.5),
        'b_glu': nrm(ks[19], (2 * D_MODEL,), 0.01),
        'w_attn_out': nrm(ks[20], (D_ATTN, D_MODEL), D_ATTN ** -0.5),
        'w_o': nrm(ks[21], (D_MODEL, D_MODEL), D_MODEL ** -0.5),
        'norm2_g': 1.0 + nrm(ks[22], (D_MODEL,), 0.01),
        'w_up': nrm(ks[23], (D_MODEL, D_FF), D_MODEL ** -0.5),
        'w_down': nrm(ks[24], (D_FF, D_MODEL), D_FF ** -0.5),
        'normf_g': 1.0 + nrm(ks[25], (D_MODEL,), 0.01),
    }


def reference(x_prompt, x_sample, cache_k, cache_v, cache_idx_k, state_ssm_re, state_ssm_im, page_table,
              norm1_g, w_in, ssm_a_re, ssm_a_im, ssm_log_dt, ssm_b_re, ssm_b_im, ssm_c_re, ssm_c_im,
              ssm_d, w_glu, b_glu, w_attn_out, w_o, norm2_g, w_up, w_down, normf_g):
    p = {'norm1_g': norm1_g, 'w_in': w_in, 'ssm_a_re': ssm_a_re, 'ssm_a_im': ssm_a_im,
         'ssm_log_dt': ssm_log_dt, 'ssm_b_re': ssm_b_re, 'ssm_b_im': ssm_b_im,
         'ssm_c_re': ssm_c_re, 'ssm_c_im': ssm_c_im, 'ssm_d': ssm_d, 'w_glu': w_glu, 'b_glu': b_glu,
         'w_attn_out': w_attn_out, 'w_o': w_o, 'norm2_g': norm2_g, 'w_up': w_up, 'w_down': w_down}
    pos_p = jnp.arange(x_prompt.shape[1], dtype=jnp.int32)
    pos_s = PAST_LEN + jnp.arange(x_sample.shape[1], dtype=jnp.int32)

    def attend_sample(q, k, v, qi, ki, wi, pos):
        return sample_sparse_attention(q, k, v, qi, ki, wi, pos, cache_k, cache_v, cache_idx_k, page_table)

    hp, hs = x_prompt, x_sample
    for _ in range(DEPTH):
        hp, k_p, v_p, ki_p, re_p, im_p = decoder_layer(hp, pos_p, prompt_sparse_attention, None, None, p)
        hs, k_s, v_s, ki_s, re_s, im_s = decoder_layer(hs, pos_s, attend_sample, state_ssm_re, state_ssm_im, p)
    y_prompt = rms_norm(hp, normf_g)
    y_sample = rms_norm(hs, normf_g)
    return (y_prompt, y_sample, k_p, v_p, ki_p, re_p, im_p, k_s, v_s, ki_s, re_s, im_s)
```

```python
import functools
import math

import jax
import jax.numpy as jnp
from jax import lax
from jax.experimental import pallas as pl
from jax.experimental.pallas import tpu as pltpu

F32 = jnp.float32
BF16 = jnp.bfloat16
I32 = jnp.int32

SSM_GROUP = 16
SSM_STATE = 64
N_HEADS = 8
N_KV_HEADS = 2
HEAD_DIM = 64
ROT_DIM = HEAD_DIM // 4
N_IDX_HEADS = 8
IDX_DIM = 64
IDX_ROT_DIM = IDX_DIM // 4
ROPE_THETA = 500000.0
TOPK_MAX = 256
EPS = 1e-6

LANES = 128
SUBLANES = 8
VMEM_LIMIT = 56 * 1024 * 1024

NEG_INF = float("-inf")
FLT_MAX = float(jnp.finfo(jnp.float32).max)
KEY_LOW = -2139095040
NT_DIMS = (((1,), (1,)), ((), ()))


def _const_spec(shape):
    nd = len(shape)
    return pl.BlockSpec(shape, lambda *_: (0,) * nd, pipeline_mode=pl.Buffered(1))


def _rms_norm(x, g):
    ms = jnp.mean(x * x, axis=-1, keepdims=True)
    return x * lax.rsqrt(ms + EPS) * g


def _sigmoid(x):
    return 1.0 / (1.0 + jnp.exp(-x))


def _rope(x, cos_t, sin_a, sin_b):
    return (x * cos_t + pltpu.roll(x, LANES - ROT_DIM // 2, axis=1) * sin_a
            + pltpu.roll(x, ROT_DIM // 2, axis=1) * sin_b)


def _in_proj_kernel(x_ref, g_ref, w_ref, cos_ref, sa_ref, sb_ref,
                    u_ref, q_ref, k_ref, kb_ref, v_ref, vb_ref, qi_ref, ki_ref, kib_ref, wi_ref,
                    sgs_ref, sga_ref, *, d_ssm, d_attn, d_kv, d_qi, d_model):
    h = _rms_norm(x_ref[...], g_ref[...]).astype(BF16)
    cos_t, sin_a, sin_b = cos_ref[...], sa_ref[...], sb_ref[...]

    def proj(c0, width):
        return jnp.dot(h, w_ref[:, c0:c0 + width], preferred_element_type=F32)

    off = 0
    u_ref[...] = proj(off, d_ssm)
    off += d_ssm
    for c in range(d_attn // LANES):
        r = _rope(proj(off + c * LANES, LANES), cos_t, sin_a, sin_b)
        q_ref[:, c * LANES:(c + 1) * LANES] = (r * (HEAD_DIM ** -0.5)).astype(BF16)
    off += d_attn
    for c in range(d_kv // LANES):
        r = _rope(proj(off + c * LANES, LANES), cos_t, sin_a, sin_b)
        k_ref[:, c * LANES:(c + 1) * LANES] = r
        kb_ref[:, c * LANES:(c + 1) * LANES] = r.astype(BF16)
    off += d_kv
    vv = proj(off, d_kv)
    v_ref[...] = vv
    vb_ref[...] = vv.astype(BF16)
    off += d_kv
    for c in range(d_qi // LANES):
        r = _rope(proj(off + c * LANES, LANES), cos_t, sin_a, sin_b)
        qi_ref[:, c * LANES:(c + 1) * LANES] = r.astype(BF16)
    off += d_qi
    kw = proj(off, LANES)
    lane = lax.broadcasted_iota(I32, kw.shape, 1)
    kr = jnp.where(lane < IDX_DIM, _rope(kw, cos_t, sin_a, sin_b), kw)
    ki_ref[...] = kr[:, :IDX_DIM]
    kib_ref[...] = kr[:, :IDX_DIM].astype(BF16)
    wi_ref[...] = kw[:, IDX_DIM:IDX_DIM + N_IDX_HEADS] * (N_IDX_HEADS ** -0.5)
    off += LANES
    sgs_ref[...] = _sigmoid(proj(off, d_model))
    off += d_model
    sga_ref[...] = _sigmoid(proj(off, d_model))


def _rope_tables(pos, n_rows):
    half = ROT_DIM // 2
    inv = ROPE_THETA ** (-jnp.arange(half, dtype=F32) / half)
    ang = pos.astype(F32)[:, None] * inv[None, :]
    cos, sin = jnp.cos(ang), jnp.sin(ang)
    ones = jnp.ones((n_rows, HEAD_DIM - ROT_DIM), F32)
    zeros = jnp.zeros((n_rows, HEAD_DIM - ROT_DIM), F32)
    zh = jnp.zeros((n_rows, half), F32)
    cos_t = jnp.concatenate([cos, cos, ones], axis=1)
    sin_a = jnp.concatenate([-sin, zh, zeros], axis=1)
    sin_b = jnp.concatenate([zh, sin, zeros], axis=1)
    rep = LANES // HEAD_DIM
    return tuple(jnp.tile(t, (1, rep)) for t in (cos_t, sin_a, sin_b))


def _in_proj(x2d, pos_tab, tab_blocks, norm_g, w_pack, tm, dims):
    n, d_model = x2d.shape
    d_ssm, d_attn, d_kv, d_qi = dims
    cos_t, sin_a, sin_b = pos_tab
    grid = (n // tm,)
    row = lambda w: pl.BlockSpec((tm, w), lambda i: (i, 0))
    tab = pl.BlockSpec((tm, LANES), lambda i: (i % tab_blocks, 0))
    kern = functools.partial(_in_proj_kernel, d_ssm=d_ssm, d_attn=d_attn, d_kv=d_kv, d_qi=d_qi,
                             d_model=d_model)
    out_shapes = [
        jax.ShapeDtypeStruct((n, d_ssm), F32),
        jax.ShapeDtypeStruct((n, d_attn), BF16),
        jax.ShapeDtypeStruct((n, d_kv), F32),
        jax.ShapeDtypeStruct((n, d_kv), BF16),
        jax.ShapeDtypeStruct((n, d_kv), F32),
        jax.ShapeDtypeStruct((n, d_kv), BF16),
        jax.ShapeDtypeStruct((n, d_qi), BF16),
        jax.ShapeDtypeStruct((n, IDX_DIM), F32),
        jax.ShapeDtypeStruct((n, IDX_DIM), BF16),
        jax.ShapeDtypeStruct((n, N_IDX_HEADS), F32),
        jax.ShapeDtypeStruct((n, d_model), F32),
        jax.ShapeDtypeStruct((n, d_model), F32),
    ]
    out_specs = [row(s.shape[1]) for s in out_shapes]
    return pl.pallas_call(
        kern, grid=grid,
        in_specs=[row(d_model), _const_spec((1, d_model)), _const_spec(w_pack.shape), tab, tab, tab],
        out_specs=out_specs, out_shape=out_shapes,
        compiler_params=pltpu.CompilerParams(dimension_semantics=("arbitrary",),
                                             vmem_limit_bytes=VMEM_LIMIT),
        name="in_proj",
    )(x2d, norm_g.reshape(1, d_model), w_pack, cos_t, sin_a, sin_b)


def _s5_readout(y, u, dsk_ref, wglu_ref, bglu_ref, d_model):
    y = y + dsk_ref[...] * u
    cdf = 0.5 * (1.0 + jnp.tanh(math.sqrt(2.0 / math.pi) * (y + 0.044715 * (y * y * y))))
    gl = (y * cdf).astype(BF16)
    z = jnp.dot(gl, wglu_ref[...], preferred_element_type=F32) + bglu_ref[...]
    return z[:, :d_model] * _sigmoid(z[:, d_model:])


def _s5_prompt_kernel(u_ref, bre_ref, bim_ref, tab_ref, cre_ref, cim_ref, dsk_ref, wglu_ref, bglu_ref,
                      out_ref, sre_ref, sim_ref, hre, him, cr, ci, *, tc, d_model):
    @pl.when(pl.program_id(1) == 0)
    def _():
        cr[...] = jnp.zeros_like(cr)
        ci[...] = jnp.zeros_like(ci)

    u = u_ref[...]
    ub = u.astype(BF16)
    hre[...] = jnp.dot(ub, bre_ref[...], preferred_element_type=F32)
    him[...] = jnp.dot(ub, bim_ref[...], preferred_element_type=F32)

    def group(gi, carry):
        car, cai = carry
        r0 = pl.multiple_of(gi * SUBLANES, SUBLANES)
        xr = hre[pl.ds(r0, SUBLANES), :]
        xi = him[pl.ds(r0, SUBLANES), :]
        for t in range(3):
            ar, ai = tab_ref[2 * t], tab_ref[2 * t + 1]
            sr = pltpu.roll(xr, 1 << t, axis=0)
            si = pltpu.roll(xi, 1 << t, axis=0)
            xr, xi = xr + (ar * sr - ai * si), xi + (ar * si + ai * sr)
        pr, pi = tab_ref[6], tab_ref[7]
        xr, xi = xr + (pr * car - pi * cai), xi + (pr * cai + pi * car)
        hre[pl.ds(r0, SUBLANES), :] = xr
        him[pl.ds(r0, SUBLANES), :] = xi
        return xr[SUBLANES - 1:SUBLANES, :], xi[SUBLANES - 1:SUBLANES, :]

    car, cai = lax.fori_loop(0, tc // SUBLANES, group, (cr[...], ci[...]))
    cr[...] = car
    ci[...] = cai
    sre_ref[...] = car
    sim_ref[...] = cai
    y = (jnp.dot(hre[...].astype(BF16), cre_ref[...], preferred_element_type=F32)
         + jnp.dot(him[...].astype(BF16), cim_ref[...], preferred_element_type=F32))
    out_ref[...] = _s5_readout(y, u, dsk_ref, wglu_ref, bglu_ref, d_model)


def _s5_sample_kernel(u_ref, h0r_ref, h0i_ref, are_ref, aim_ref, bre_ref, bim_ref, cre_ref, cim_ref,
                      dsk_ref, wglu_ref, bglu_ref, out_ref, sre_ref, sim_ref, *, d_model):
    u = u_ref[...]
    ub = u.astype(BF16)
    ar, ai = are_ref[...], aim_ref[...]
    h0r, h0i = h0r_ref[...], h0i_ref[...]
    hr = (ar * h0r - ai * h0i) + jnp.dot(ub, bre_ref[...], preferred_element_type=F32)
    hi = (ar * h0i + ai * h0r) + jnp.dot(ub, bim_ref[...], preferred_element_type=F32)
    sre_ref[...] = hr
    sim_ref[...] = hi
    y = (jnp.dot(hr.astype(BF16), cre_ref[...], preferred_element_type=F32)
         + jnp.dot(hi.astype(BF16), cim_ref[...], preferred_element_type=F32))
    out_ref[...] = _s5_readout(y, u, dsk_ref, wglu_ref, bglu_ref, d_model)


def _s5_params(a_re, a_im, log_dt, b_re, b_im, c_re, c_im):
    g, p = a_re.shape
    a_c = lax.complex(a_re.astype(F32), a_im.astype(F32))
    dt = jnp.exp(log_dt.astype(F32))[:, None]
    a_bar = jnp.exp(a_c * dt)
    b_bar = ((a_bar - 1.0) / a_c)[:, :, None] * lax.complex(b_re.astype(F32), b_im.astype(F32))
    eye = jnp.eye(g, dtype=F32)
    n = b_re.shape[2]

    def bmat(b):
        return jnp.einsum('gpn,gh->gnhp', b, eye).reshape(g * n, g * p)

    def cmat(c):
        return jnp.einsum('gnp,gh->gphn', c, eye).reshape(g * p, g * n)

    bre, bim = bmat(jnp.real(b_bar)).astype(BF16), bmat(jnp.imag(b_bar)).astype(BF16)
    cre, cim = cmat(c_re.astype(F32)).astype(BF16), cmat(-c_im.astype(F32)).astype(BF16)
    a1 = a_bar.reshape(1, g * p)
    a2 = a1 * a1
    a3 = a2 * a1
    a4 = a2 * a2
    pw = [a1, a2, a3, a4, a4 * a1, a4 * a2, a4 * a3, a4 * a4]
    rows = jnp.arange(SUBLANES)[:, None]
    tabs = []
    for sh in (1, 2, 4):
        t = jnp.where(rows >= sh, jnp.broadcast_to(pw[sh - 1], (SUBLANES, g * p)), 0.0)
        tabs += [jnp.real(t), jnp.imag(t)]
    pcat = jnp.concatenate(pw, axis=0)
    tabs += [jnp.real(pcat), jnp.imag(pcat)]
    tab = jnp.stack(tabs, axis=0).astype(F32)
    return jnp.real(a1), jnp.imag(a1), bre, bim, cre, cim, tab


def _s5_prompt(u3, s5p, dsk, wglu, bglu, tc):
    bsz, t_len, d_ssm = u3.shape
    _, _, bre, bim, cre, cim, tab = s5p
    sd = bre.shape[1]
    d_model = wglu.shape[1] // 2
    kern = functools.partial(_s5_prompt_kernel, tc=tc, d_model=d_model)
    return pl.pallas_call(
        kern, grid=(bsz, t_len // tc),
        in_specs=[pl.BlockSpec((None, tc, d_ssm), lambda b, c: (b, c, 0)),
                  _const_spec(bre.shape), _const_spec(bim.shape), _const_spec(tab.shape),
                  _const_spec(cre.shape), _const_spec(cim.shape), _const_spec((1, d_ssm)),
                  _const_spec(wglu.shape), _const_spec((1, 2 * d_model))],
        out_specs=[pl.BlockSpec((None, tc, d_model), lambda b, c: (b, c, 0)),
                   pl.BlockSpec((None, 1, sd), lambda b, c: (b, 0, 0)),
                   pl.BlockSpec((None, 1, sd), lambda b, c: (b, 0, 0))],
        out_shape=[jax.ShapeDtypeStruct((bsz, t_len, d_model), F32),
                   jax.ShapeDtypeStruct((bsz, 1, sd), F32),
                   jax.ShapeDtypeStruct((bsz, 1, sd), F32)],
        scratch_shapes=[pltpu.VMEM((tc, sd), F32), pltpu.VMEM((tc, sd), F32),
                        pltpu.VMEM((1, sd), F32), pltpu.VMEM((1, sd), F32)],
        compiler_params=pltpu.CompilerParams(dimension_semantics=("arbitrary", "arbitrary"),
                                             vmem_limit_bytes=VMEM_LIMIT),
        name="s5_prompt",
    )(u3, bre, bim, tab, cre, cim, dsk.reshape(1, d_ssm), wglu, bglu.reshape(1, 2 * d_model))


def _s5_sample(u2, h0r, h0i, s5p, dsk, wglu, bglu):
    n, d_ssm = u2.shape
    are, aim, bre, bim, cre, cim, _ = s5p
    sd = bre.shape[1]
    d_model = wglu.shape[1] // 2
    kern = functools.partial(_s5_sample_kernel, d_model=d_model)
    return pl.pallas_call(
        kern,
        out_shape=[jax.ShapeDtypeStruct((n, d_model), F32),
                   jax.ShapeDtypeStruct((n, sd), F32),
                   jax.ShapeDtypeStruct((n, sd), F32)],
        compiler_params=pltpu.CompilerParams(vmem_limit_bytes=VMEM_LIMIT),
        name="s5_sample",
    )(u2, h0r, h0i, are, aim, bre, bim, cre, cim, dsk.reshape(1, d_ssm), wglu,
      bglu.reshape(1, 2 * d_model))


def _f2k(x):
    b = lax.bitcast_convert_type(x, I32)
    return jnp.where(b < 0, b ^ 0x7FFFFFFF, b)


def _k2f(k):
    return lax.bitcast_convert_type(jnp.where(k < 0, k ^ 0x7FFFFFFF, k), F32)


def _threshold_search(count_ge, amax, forced, k_top):
    kf = float(k_top)
    lo0 = jnp.where(forced, KEY_LOW, _f2k(-amax))
    hi0 = jnp.where(forced, KEY_LOW + 1, _f2k(amax) + 1)

    def cond(st):
        lo, hi, _ = st
        return jnp.max(jnp.where(hi > lo + 1, 1.0, 0.0)) > 0.5

    def body(st):
        lo, hi, it = st
        active = hi > lo + 1
        mid_i = (lo >> 1) + (hi >> 1) + (lo & hi & 1)
        mk = _f2k(0.5 * _k2f(lo) + 0.5 * _k2f(hi))
        use_f = jnp.logical_and(it % 2 == 0, jnp.logical_and(mk > lo, mk < hi))
        mid = jnp.where(use_f, mk, mid_i)
        c = count_ge(_k2f(mid))
        ge = c >= kf
        nlo = jnp.where(ge, mid, lo)
        nhi = jnp.where(c == kf, mid + 1, jnp.where(ge, hi, mid))
        return jnp.where(active, nlo, lo), jnp.where(active, nhi, hi), it + 1

    lo, _, _ = lax.while_loop(cond, body, (lo0, hi0, jnp.int32(0)))
    return _k2f(lo)


def _tri_exclusive(n):
    r = lax.broadcasted_iota(I32, (n, n), 0)
    c = lax.broadcasted_iota(I32, (n, n), 1)
    return (r < c).astype(BF16)


def _dsa_prompt_kernel(qi_ref, wi_ref, q_ref, ki_ref, k_ref, v_ref, o_ref,
                       sc_ref, qis_ref, qs_ref, vaug_ref, m_ref, acc_ref, *, tq, tk, k_top):
    i = pl.program_id(1)
    nkb = ((i + 1) * tq + tk - 1) // tk
    hpg = N_HEADS // N_KV_HEADS

    @pl.when(i == 0)
    def _():
        v = v_ref[...].astype(F32)
        lane = lax.broadcasted_iota(I32, v.shape, 1)
        one = jnp.where(lane == HEAD_DIM, 1.0, 0.0)
        vaug_ref[0] = jnp.where(lane < HEAD_DIM, v, one).astype(BF16)
        vaug_ref[1] = jnp.where(lane < HEAD_DIM, pltpu.roll(v, HEAD_DIM, axis=1), one).astype(BF16)

    qi = qi_ref[...]
    q = q_ref[...]
    zpad = jnp.zeros((tq, HEAD_DIM), BF16)
    for h in range(N_IDX_HEADS):
        qis_ref[h * tq:(h + 1) * tq, :] = qi[:, h * IDX_DIM:(h + 1) * IDX_DIM]
    for h in range(N_HEADS):
        qh = q[:, h * HEAD_DIM:(h + 1) * HEAD_DIM]
        pair = [qh, zpad] if h // hpg == 0 else [zpad, qh]
        qs_ref[h * tq:(h + 1) * tq, :] = jnp.concatenate(pair, axis=1)

    w = wi_ref[...] * (IDX_DIM ** -0.5)
    row = i * tq + lax.broadcasted_iota(I32, (tq, 1), 0)

    def score_blk(j, amax):
        k0 = pl.multiple_of(j * tk, tk)
        d = lax.dot_general(qis_ref[...], ki_ref[pl.ds(k0, tk), :], NT_DIMS, preferred_element_type=F32)
        acc = jnp.zeros((tq, tk), F32)
        for h in range(N_IDX_HEADS):
            acc = acc + jnp.maximum(d[h * tq:(h + 1) * tq], 0.0) * w[:, h:h + 1]
        col = j * tk + lax.broadcasted_iota(I32, (1, tk), 1)
        causal = col <= row
        sc_ref[j] = jnp.where(causal, acc, NEG_INF)
        return jnp.maximum(amax, jnp.max(jnp.where(causal, jnp.abs(acc), 0.0), axis=1, keepdims=True))

    amax = lax.fori_loop(0, nkb, score_blk, jnp.zeros((tq, 1), F32))

    def count_cmp(thr, strict):
        thr_b = jnp.broadcast_to(thr, (tq, LANES))

        def body(j, part):
            sc = sc_ref[j]
            for c in range(tk // LANES):
                blk = sc[:, c * LANES:(c + 1) * LANES]
                hit = (blk > thr_b) if strict else (blk >= thr_b)
                part = part + jnp.where(hit, 1.0, 0.0)
            return part

        part = lax.fori_loop(0, nkb, body, jnp.zeros((tq, LANES), F32))
        return jnp.sum(part, axis=1, keepdims=True)

    kf = float(k_top)
    forced = (row + 1) <= k_top
    thr = _threshold_search(lambda t: count_cmp(t, False), amax, forced, k_top)
    c_ge = count_cmp(thr, False)
    tie = c_ge > kf

    @pl.when(jnp.max(jnp.where(tie, 1.0, 0.0)) > 0.5)
    def _():
        c_gt = count_cmp(thr, True)
        need = jnp.where(tie, kf - c_gt, 1e30)
        tri = _tri_exclusive(tk)

        def fix(j, run):
            sc = sc_ref[j]
            eq = sc == thr
            eqb = jnp.where(eq, 1.0, 0.0).astype(BF16)
            before = jnp.dot(eqb, tri, preferred_element_type=F32) + run
            sc_ref[j] = jnp.where(jnp.logical_and(eq, before >= need), NEG_INF, sc)
            return run + jnp.sum(jnp.where(eq, 1.0, 0.0), axis=1, keepdims=True)

        lax.fori_loop(0, nkb, fix, jnp.zeros((tq, 1), F32))

    m_ref[...] = jnp.full(m_ref.shape, NEG_INF, F32)
    acc_ref[...] = jnp.zeros(acc_ref.shape, F32)

    def att_blk(j, carry):
        k0 = pl.multiple_of(j * tk, tk)
        s = lax.dot_general(qs_ref[...], k_ref[pl.ds(k0, tk), :], NT_DIMS, preferred_element_type=F32)
        sel = sc_ref[j] >= thr
        for h in range(N_HEADS):
            rows = slice(h * tq, (h + 1) * tq)
            sh = jnp.where(sel, s[rows], NEG_INF)
            m_old = m_ref[rows, :]
            m_new = jnp.maximum(m_old, jnp.max(sh, axis=1, keepdims=True))
            m_safe = jnp.where(m_new == NEG_INF, 0.0, m_new)
            p = jnp.exp(sh - m_safe)
            alpha = jnp.exp(m_old - m_safe)
            pv = jnp.dot(p.astype(BF16), vaug_ref[h // hpg, pl.ds(k0, tk), :], preferred_element_type=F32)
            acc_ref[rows, :] = acc_ref[rows, :] * alpha + pv
            m_ref[rows, :] = m_new
        return carry

    lax.fori_loop(0, nkb, att_blk, 0)

    for h in range(N_HEADS):
        a = acc_ref[h * tq:(h + 1) * tq, :]
        o = a[:, :HEAD_DIM] / a[:, HEAD_DIM:HEAD_DIM + 1]
        o_ref[:, h * HEAD_DIM:(h + 1) * HEAD_DIM] = o.astype(o_ref.dtype)


def _dsa_prompt(qi, wi, q, kib, kb, vb, tq, tk):
    bsz, t_len, d_qi = qi.shape
    d_attn = q.shape[2]
    d_kv = kb.shape[2]
    k_top = min(TOPK_MAX, t_len // 4)
    kern = functools.partial(_dsa_prompt_kernel, tq=tq, tk=tk, k_top=k_top)
    tile = lambda w: pl.BlockSpec((None, tq, w), lambda b, i: (b, i, 0))
    full = lambda w: pl.BlockSpec((None, t_len, w), lambda b, i: (b, 0, 0))
    return pl.pallas_call(
        kern, grid=(bsz, t_len // tq),
        in_specs=[tile(d_qi), tile(N_IDX_HEADS), tile(d_attn), full(IDX_DIM), full(d_kv), full(d_kv)],
        out_specs=tile(d_attn),
        out_shape=jax.ShapeDtypeStruct((bsz, t_len, d_attn), BF16),
        scratch_shapes=[pltpu.VMEM((t_len // tk, tq, tk), F32),
                        pltpu.VMEM((N_IDX_HEADS * tq, IDX_DIM), BF16),
                        pltpu.VMEM((N_HEADS * tq, N_KV_HEADS * HEAD_DIM), BF16),
                        pltpu.VMEM((N_KV_HEADS, t_len, LANES), BF16),
                        pltpu.VMEM((N_HEADS * tq, 1), F32),
                        pltpu.VMEM((N_HEADS * tq, LANES), F32)],
        compiler_params=pltpu.CompilerParams(dimension_semantics=("arbitrary", "arbitrary"),
                                             vmem_limit_bytes=VMEM_LIMIT),
        name="dsa_prompt",
    )(qi, wi, q, kib, kb, vb)


def _idx_score_kernel(pt_ref, q_ref, w_ref, *refs, n_pg):
    pages, o_ref = refs[:n_pg], refs[n_pg]
    b = pl.program_id(1)
    q = q_ref[...]
    w = w_ref[...] * (IDX_DIM ** -0.5)
    parts = []
    for p in range(n_pg):
        d = lax.dot_general(q, pages[p][...].astype(BF16), NT_DIMS, preferred_element_type=F32)
        parts.append(jnp.sum(jnp.maximum(d, 0.0) * w, axis=0, keepdims=True))
    o_ref[pl.ds(b % SUBLANES, 1), :] = jnp.concatenate(parts, axis=1)


def _idx_scores(page_table, qis, wcol, cache_idx_k, n_pg):
    sq, n_pages = page_table.shape
    page = cache_idx_k.shape[1]
    kern = functools.partial(_idx_score_kernel, n_pg=n_pg)
    page_specs = [pl.BlockSpec((None, page, IDX_DIM),
                               lambda j, b, pt, p=p: (pt[b * n_pages + j * n_pg + p], 0, 0))
                  for p in range(n_pg)]
    grid_spec = pltpu.PrefetchScalarGridSpec(
        num_scalar_prefetch=1, grid=(n_pages // n_pg, sq),
        in_specs=[pl.BlockSpec((None, N_IDX_HEADS, IDX_DIM), lambda j, b, pt: (b, 0, 0)),
                  pl.BlockSpec((None, N_IDX_HEADS, 1), lambda j, b, pt: (b, 0, 0))] + page_specs,
        out_specs=pl.BlockSpec((SUBLANES, n_pg * page), lambda j, b, pt: (b // SUBLANES, j)))
    return pl.pallas_call(
        kern, grid_spec=grid_spec,
        out_shape=jax.ShapeDtypeStruct((sq, n_pages * page), F32),
        compiler_params=pltpu.CompilerParams(dimension_semantics=("arbitrary", "arbitrary")),
        name="sample_idx_scores",
    )(page_table.reshape(-1), qis, wcol, *([cache_idx_k] * n_pg))


def _sample_select_kernel(sc_ref, qi_ref, ki_ref, wi_ref, mask_ref, selfsel_ref, *, ch, k_top):
    sq, l_past = sc_ref.shape
    nch = l_past // ch
    w = wi_ref[...] * (IDX_DIM ** -0.5)
    qf = qi_ref[...].astype(F32)
    kf32 = ki_ref[...].astype(F32)
    s_self = jnp.zeros((sq, 1), F32)
    for h in range(N_IDX_HEADS):
        dh = jnp.sum(qf[:, h * IDX_DIM:(h + 1) * IDX_DIM] * kf32, axis=1, keepdims=True)
        s_self = s_self + jnp.maximum(dh, 0.0) * w[:, h:h + 1]

    amax = jnp.abs(s_self)
    for c in range(nch):
        amax = jnp.maximum(amax, jnp.max(jnp.abs(sc_ref[:, c * ch:(c + 1) * ch]), axis=1, keepdims=True))

    def count_cmp(thr, strict):
        thr_b = jnp.broadcast_to(thr, (sq, LANES))
        part = jnp.zeros((sq, LANES), F32)
        for c in range(l_past // LANES):
            blk = sc_ref[:, c * LANES:(c + 1) * LANES]
            hit = (blk > thr_b) if strict else (blk >= thr_b)
            part = part + jnp.where(hit, 1.0, 0.0)
        self_hit = (s_self > thr) if strict else (s_self >= thr)
        return jnp.sum(part, axis=1, keepdims=True) + jnp.where(self_hit, 1.0, 0.0)

    kf = float(k_top)
    forced = amax < 0.0
    thr = _threshold_search(lambda t: count_cmp(t, False), amax, forced, k_top)
    need = kf - count_cmp(thr, True)
    tri = _tri_exclusive(ch)
    run = jnp.zeros((sq, 1), F32)
    for c in range(nch):
        sc = sc_ref[:, c * ch:(c + 1) * ch]
        eq = sc == thr
        before = jnp.dot(jnp.where(eq, 1.0, 0.0).astype(BF16), tri, preferred_element_type=F32) + run
        keep = jnp.logical_or(sc > thr, jnp.logical_and(eq, before < need))
        mask_ref[:, c * ch:(c + 1) * ch] = jnp.where(keep, 1.0, 0.0)
        run = run + jnp.sum(jnp.where(eq, 1.0, 0.0), axis=1, keepdims=True)
    self_keep = jnp.logical_or(s_self > thr, jnp.logical_and(s_self == thr, run < need))
    selfsel_ref[...] = jnp.where(self_keep, 1.0, 0.0)


def _sample_select(scores, qi, kib, wi, ch, k_top):
    sq, l_past = scores.shape
    kern = functools.partial(_sample_select_kernel, ch=ch, k_top=k_top)
    return pl.pallas_call(
        kern,
        out_shape=[jax.ShapeDtypeStruct((sq, l_past), F32), jax.ShapeDtypeStruct((sq, 1), F32)],
        compiler_params=pltpu.CompilerParams(vmem_limit_bytes=VMEM_LIMIT),
        name="sample_select",
    )(scores, qi, kib, wi)


def _sample_attn_kernel(pt_ref, q_ref, mask_ref, ks_ref, vs_ref, ss_ref, *refs, n_pg):
    kpages, vpages = refs[:n_pg], refs[n_pg:2 * n_pg]
    o_ref, m_scr, l_scr, acc_scr = refs[2 * n_pg:]
    j = pl.program_id(1)
    hpg = N_HEADS // N_KV_HEADS

    @pl.when(j == 0)
    def _():
        m_scr[...] = jnp.full(m_scr.shape, NEG_INF, F32)
        l_scr[...] = jnp.zeros(l_scr.shape, F32)
        acc_scr[...] = jnp.zeros(acc_scr.shape, F32)

    q = q_ref[...]
    page = kpages[0].shape[0]
    s = jnp.concatenate(
        [lax.dot_general(q, kpages[p][...].astype(BF16), NT_DIMS, preferred_element_type=F32)
         for p in range(n_pg)], axis=1)
    sm = jnp.where(mask_ref[...] > 0.5, s, NEG_INF)
    m_old = m_scr[...]
    m_new = jnp.maximum(m_old, jnp.max(sm, axis=1, keepdims=True))
    m_safe = jnp.where(m_new == NEG_INF, 0.0, m_new)
    p_ = jnp.exp(sm - m_safe)
    alpha = jnp.exp(m_old - m_safe)
    pb = p_.astype(BF16)
    pv = jnp.zeros(acc_scr.shape, F32)
    for p in range(n_pg):
        pv = pv + jnp.dot(pb[:, p * page:(p + 1) * page], vpages[p][...].astype(BF16),
                          preferred_element_type=F32)
    l_scr[...] = l_scr[...] * alpha + jnp.sum(pb.astype(F32), axis=1, keepdims=True)
    acc_scr[...] = acc_scr[...] * alpha + pv
    m_scr[...] = m_new

    @pl.when(j == pl.num_programs(1) - 1)
    def _():
        s_self = jnp.sum(q.astype(F32) * ks_ref[...].astype(F32), axis=1, keepdims=True)
        s_self = jnp.where(ss_ref[...] > 0.5, s_self, NEG_INF)
        m_o = m_scr[...]
        m_n = jnp.maximum(m_o, s_self)
        m_s = jnp.where(m_n == NEG_INF, 0.0, m_n)
        p_self = jnp.exp(s_self - m_s).astype(BF16).astype(F32)
        al = jnp.exp(m_o - m_s)
        l_fin = l_scr[...] * al + p_self
        acc = acc_scr[...] * al + p_self * vs_ref[...].astype(F32)
        o = acc / l_fin
        hrow = lax.broadcasted_iota(I32, o.shape, 0)
        o = jnp.where(hrow < hpg, o, pltpu.roll(o, HEAD_DIM, axis=1))
        o_ref[...] = o[:, :HEAD_DIM].astype(o_ref.dtype)


def _sample_attn(page_table, qs, mask3, kself, vself, selfsel, cache_k2, cache_v2, n_pg):
    sq, n_pages = page_table.shape
    page, d_kv = cache_k2.shape[1], cache_k2.shape[2]
    kern = functools.partial(_sample_attn_kernel, n_pg=n_pg)
    pspec = lambda p: pl.BlockSpec((None, page, d_kv),
                                   lambda b, j, pt: (pt[b * n_pages + j * n_pg + p], 0, 0))
    per_seq = lambda r, w: pl.BlockSpec((None, r, w), lambda b, j, pt: (b, 0, 0))
    grid_spec = pltpu.PrefetchScalarGridSpec(
        num_scalar_prefetch=1, grid=(sq, n_pages // n_pg),
        in_specs=[per_seq(N_HEADS, d_kv),
                  pl.BlockSpec((None, 1, n_pg * page), lambda b, j, pt: (b, 0, j)),
                  per_seq(1, d_kv), per_seq(1, d_kv), per_seq(1, 1)]
                 + [pspec(p) for p in range(n_pg)] + [pspec(p) for p in range(n_pg)],
        out_specs=per_seq(N_HEADS, HEAD_DIM),
        scratch_shapes=[pltpu.VMEM((N_HEADS, 1), F32), pltpu.VMEM((N_HEADS, 1), F32),
                        pltpu.VMEM((N_HEADS, d_kv), F32)])
    return pl.pallas_call(
        kern, grid_spec=grid_spec,
        out_shape=jax.ShapeDtypeStruct((sq, N_HEADS, HEAD_DIM), BF16),
        compiler_params=pltpu.CompilerParams(dimension_semantics=("arbitrary", "arbitrary")),
        name="sample_attn",
    )(page_table.reshape(-1), qs, mask3, kself, vself, selfsel,
      *([cache_k2] * n_pg), *([cache_v2] * n_pg))


def _tail_kernel(x_ref, attn_ref, ssm_ref, sgs_ref, sga_ref, wao_ref, wo_ref, g2_ref, wup_ref, wdn_ref,
                 gf_ref, y_ref):
    attn_out = jnp.dot(attn_ref[...], wao_ref[...], preferred_element_type=F32)
    mix = sgs_ref[...] * ssm_ref[...] + sga_ref[...] * attn_out
    x1 = x_ref[...] + jnp.dot(mix.astype(BF16), wo_ref[...], preferred_element_type=F32)
    hh = _rms_norm(x1, g2_ref[...]).astype(BF16)
    up = jnp.dot(hh, wup_ref[...], preferred_element_type=F32)
    r = jnp.square(jnp.maximum(up, 0.0)).astype(BF16)
    x2 = x1 + jnp.dot(r, wdn_ref[...], preferred_element_type=F32)
    y_ref[...] = _rms_norm(x2, gf_ref[...])


def _tail(x2d, attn, ssm, sgs, sga, wao, wo, g2, wup, wdn, gf, tm):
    n, d_model = x2d.shape
    row = lambda w: pl.BlockSpec((tm, w), lambda i: (i, 0))
    return pl.pallas_call(
        _tail_kernel, grid=(n // tm,),
        in_specs=[row(d_model), row(attn.shape[1]), row(d_model), row(d_model), row(d_model),
                  _const_spec(wao.shape), _const_spec(wo.shape), _const_spec((1, d_model)),
                  _const_spec(wup.shape), _const_spec(wdn.shape), _const_spec((1, d_model))],
        out_specs=row(d_model),
        out_shape=jax.ShapeDtypeStruct((n, d_model), F32),
        compiler_params=pltpu.CompilerParams(dimension_semantics=("arbitrary",),
                                             vmem_limit_bytes=VMEM_LIMIT),
        name="tail",
    )(x2d, attn, ssm, sgs, sga, wao, wo, g2.reshape(1, d_model), wup, wdn, gf.reshape(1, d_model))


def _tiles(n_rows, t_len):
    tm = min(256, n_rows)
    tc = min(256, t_len)
    tq = min(256, t_len)
    return tm, tc, tq, tq


def kernel(x_prompt, x_sample, cache_k, cache_v, cache_idx_k, state_ssm_re, state_ssm_im, page_table,
           norm1_g, w_in, ssm_a_re, ssm_a_im, ssm_log_dt, ssm_b_re, ssm_b_im, ssm_c_re, ssm_c_im,
           ssm_d, w_glu, b_glu, w_attn_out, w_o, norm2_g, w_up, w_down, normf_g):
    bsz, t_len, d_model = x_prompt.shape
    sq, s_len, _ = x_sample.shape
    assert s_len == 1, "the sample path handles one new token per sequence"
    n_pool, page = cache_idx_k.shape[0], cache_idx_k.shape[1]
    n_pages = page_table.shape[1]
    past = n_pages * page
    d_ssm = ssm_d.shape[0]
    d_attn = N_HEADS * HEAD_DIM
    d_kv = N_KV_HEADS * HEAD_DIM
    d_qi = N_IDX_HEADS * IDX_DIM
    dims = (d_ssm, d_attn, d_kv, d_qi)

    c_ki = d_ssm + d_attn + 2 * d_kv + d_qi
    c_g = c_ki + IDX_DIM + N_IDX_HEADS
    w_pack = jnp.concatenate(
        [w_in[:, :c_g], jnp.zeros((d_model, LANES - IDX_DIM - N_IDX_HEADS), w_in.dtype), w_in[:, c_g:]],
        axis=1).astype(BF16)
    wglu_b, wao_b, wo_b = w_glu.astype(BF16), w_attn_out.astype(BF16), w_o.astype(BF16)
    wup_b, wdn_b = w_up.astype(BF16), w_down.astype(BF16)
    s5p = _s5_params(ssm_a_re, ssm_a_im, ssm_log_dt, ssm_b_re, ssm_b_im, ssm_c_re, ssm_c_im)

    n_p = bsz * t_len
    tm, tc, tq, tk = _tiles(n_p, t_len)
    xp = x_prompt.reshape(n_p, d_model)
    tabs_p = _rope_tables(jnp.arange(t_len, dtype=I32), t_len)
    (u, q, k, kb, v, vb, qi, ki, kib, wi, sgs, sga) = _in_proj(xp, tabs_p, t_len // tm, norm1_g, w_pack, tm, dims)
    ssm_out, re_p, im_p = _s5_prompt(u.reshape(bsz, t_len, d_ssm), s5p, ssm_d, wglu_b, b_glu, tc)
    r3 = lambda a: a.reshape(bsz, t_len, a.shape[-1])
    attn = _dsa_prompt(r3(qi), r3(wi), r3(q), r3(kib), r3(kb), r3(vb), tq, tk)
    y_p = _tail(xp, attn.reshape(n_p, d_attn), ssm_out.reshape(n_p, d_model), sgs, sga,
                wao_b, wo_b, norm2_g, wup_b, wdn_b, normf_g, tm)

    xs = x_sample.reshape(sq, d_model)
    tabs_s = _rope_tables(jnp.full((sq,), past, I32), sq)
    (u_s, q_s, k_s, kb_s, v_s, vb_s, qi_s, ki_s, kib_s, wi_s, sgs_s, sga_s) = _in_proj(
        xs, tabs_s, 1, norm1_g, w_pack, sq, dims)
    sd = state_ssm_re.shape[1] * state_ssm_re.shape[2]
    ssm_s, re_s, im_s = _s5_sample(u_s, state_ssm_re.reshape(sq, sd), state_ssm_im.reshape(sq, sd),
                                   s5p, ssm_d, wglu_b, b_glu)
    n_pg = math.gcd(n_pages, 16)
    scores = _idx_scores(page_table, qi_s.reshape(sq, N_IDX_HEADS, IDX_DIM),
                         wi_s.reshape(sq, N_IDX_HEADS, 1), cache_idx_k, n_pg)
    k_top_s = min(TOPK_MAX, (past + s_len) // 4)
    mask, selfsel = _sample_select(scores, qi_s, kib_s, wi_s, min(512, past), k_top_s)
    q4 = q_s.reshape(sq, N_KV_HEADS, N_HEADS // N_KV_HEADS, HEAD_DIM)
    qs_pad = (q4[:, :, :, None, :] * jnp.eye(N_KV_HEADS, dtype=BF16)[None, :, None, :, None]
              ).reshape(sq, N_HEADS, d_kv)
    attn_s = _sample_attn(page_table, qs_pad, mask.reshape(sq, 1, past),
                          kb_s.reshape(sq, 1, d_kv), vb_s.reshape(sq, 1, d_kv), selfsel.reshape(sq, 1, 1),
                          cache_k.reshape(n_pool, page, d_kv), cache_v.reshape(n_pool, page, d_kv), n_pg)
    y_s = _tail(xs, attn_s.reshape(sq, d_attn), ssm_s, sgs_s, sga_s,
                wao_b, wo_b, norm2_g, wup_b, wdn_b, normf_g, sq)

    g_ssm, p_ssm = state_ssm_re.shape[1], state_ssm_re.shape[2]
    return (y_p.reshape(bsz, t_len, d_model), y_s.reshape(sq, s_len, d_model),
            k.reshape(bsz, t_len, N_KV_HEADS, HEAD_DIM), v.reshape(bsz, t_len, N_KV_HEADS, HEAD_DIM),
            ki.reshape(bsz, t_len, IDX_DIM),
            re_p.reshape(bsz, g_ssm, p_ssm), im_p.reshape(bsz, g_ssm, p_ssm),
            k_s.reshape(sq, s_len, N_KV_HEADS, HEAD_DIM), v_s.reshape(sq, s_len, N_KV_HEADS, HEAD_DIM),
            ki_s.reshape(sq, s_len, IDX_DIM),
            re_s.reshape(sq, g_ssm, p_ssm), im_s.reshape(sq, g_ssm, p_ssm))
```

```python
import functools
import math

import jax
import jax.numpy as jnp
from jax import lax
from jax.experimental import pallas as pl
from jax.experimental.pallas import tpu as pltpu

F32 = jnp.float32
BF16 = jnp.bfloat16
I32 = jnp.int32

SSM_GROUP = 16
SSM_STATE = 64
N_HEADS = 8
N_KV_HEADS = 2
HEAD_DIM = 64
ROT_DIM = HEAD_DIM // 4
N_IDX_HEADS = 8
IDX_DIM = 64
IDX_ROT_DIM = IDX_DIM // 4
ROPE_THETA = 500000.0
TOPK_MAX = 256
EPS = 1e-6

LANES = 128
SUBLANES = 8
ROW_CHUNK = 128
VMEM_LIMIT = 56 * 1024 * 1024

NEG_INF = float("-inf")
FLT_MAX = float(jnp.finfo(jnp.float32).max)
KEY_LOW = -2139095040
NT_DIMS = (((1,), (1,)), ((), ()))


def _const_spec(shape):
    nd = len(shape)
    return pl.BlockSpec(shape, lambda *_: (0,) * nd, pipeline_mode=pl.Buffered(1))


def _rms_norm(x, g):
    ms = jnp.mean(x * x, axis=-1, keepdims=True)
    return x * lax.rsqrt(ms + EPS) * g


def _sigmoid(x):
    return 1.0 / (1.0 + jnp.exp(-x))


def _rope(x, cos_t, sin_a, sin_b):
    return (x * cos_t + pltpu.roll(x, LANES - ROT_DIM // 2, axis=1) * sin_a
            + pltpu.roll(x, ROT_DIM // 2, axis=1) * sin_b)


def _in_proj_kernel(x_ref, g_ref, w_ref, cos_ref, sa_ref, sb_ref,
                    u_ref, q_ref, k_ref, kb_ref, v_ref, vb_ref, qi_ref, ki_ref, kib_ref, wi_ref,
                    sgs_ref, sga_ref, *, d_ssm, d_attn, d_kv, d_qi, d_model):
    h = _rms_norm(x_ref[...], g_ref[...]).astype(BF16)
    cos_t, sin_a, sin_b = cos_ref[...], sa_ref[...], sb_ref[...]

    def proj(c0, width):
        return jnp.dot(h, w_ref[:, c0:c0 + width], preferred_element_type=F32)

    off = 0
    u_ref[...] = proj(off, d_ssm)
    off += d_ssm
    for c in range(d_attn // LANES):
        r = _rope(proj(off + c * LANES, LANES), cos_t, sin_a, sin_b)
        q_ref[:, c * LANES:(c + 1) * LANES] = (r * (HEAD_DIM ** -0.5)).astype(BF16)
    off += d_attn
    for c in range(d_kv // LANES):
        r = _rope(proj(off + c * LANES, LANES), cos_t, sin_a, sin_b)
        k_ref[:, c * LANES:(c + 1) * LANES] = r
        kb_ref[:, c * LANES:(c + 1) * LANES] = r.astype(BF16)
    off += d_kv
    vv = proj(off, d_kv)
    v_ref[...] = vv
    vb_ref[...] = vv.astype(BF16)
    off += d_kv
    for c in range(d_qi // LANES):
        r = _rope(proj(off + c * LANES, LANES), cos_t, sin_a, sin_b)
        qi_ref[:, c * LANES:(c + 1) * LANES] = r.astype(BF16)
    off += d_qi
    kw = proj(off, LANES)
    lane = lax.broadcasted_iota(I32, kw.shape, 1)
    kr = jnp.where(lane < IDX_DIM, _rope(kw, cos_t, sin_a, sin_b), kw)
    ki_ref[...] = kr[:, :IDX_DIM]
    kib_ref[...] = kr[:, :IDX_DIM].astype(BF16)
    wi_ref[...] = kw[:, IDX_DIM:IDX_DIM + N_IDX_HEADS] * (N_IDX_HEADS ** -0.5)
    off += LANES
    sgs_ref[...] = _sigmoid(proj(off, d_model))
    off += d_model
    sga_ref[...] = _sigmoid(proj(off, d_model))


def _rope_tables(pos, n_rows):
    half = ROT_DIM // 2
    inv = ROPE_THETA ** (-jnp.arange(half, dtype=F32) / half)
    ang = pos.astype(F32)[:, None] * inv[None, :]
    cos, sin = jnp.cos(ang), jnp.sin(ang)
    ones = jnp.ones((n_rows, HEAD_DIM - ROT_DIM), F32)
    zeros = jnp.zeros((n_rows, HEAD_DIM - ROT_DIM), F32)
    zh = jnp.zeros((n_rows, half), F32)
    cos_t = jnp.concatenate([cos, cos, ones], axis=1)
    sin_a = jnp.concatenate([-sin, zh, zeros], axis=1)
    sin_b = jnp.concatenate([zh, sin, zeros], axis=1)
    rep = LANES // HEAD_DIM
    return tuple(jnp.tile(t, (1, rep)) for t in (cos_t, sin_a, sin_b))


def _in_proj(x2d, pos_tab, tab_blocks, norm_g, w_pack, tm, dims):
    n, d_model = x2d.shape
    d_ssm, d_attn, d_kv, d_qi = dims
    cos_t, sin_a, sin_b = pos_tab
    grid = (n // tm,)
    row = lambda w: pl.BlockSpec((tm, w), lambda i: (i, 0))
    tab = pl.BlockSpec((tm, LANES), lambda i: (i % tab_blocks, 0))
    kern = functools.partial(_in_proj_kernel, d_ssm=d_ssm, d_attn=d_attn, d_kv=d_kv, d_qi=d_qi,
                             d_model=d_model)
    out_shapes = [
        jax.ShapeDtypeStruct((n, d_ssm), F32),
        jax.ShapeDtypeStruct((n, d_attn), BF16),
        jax.ShapeDtypeStruct((n, d_kv), F32),
        jax.ShapeDtypeStruct((n, d_kv), BF16),
        jax.ShapeDtypeStruct((n, d_kv), F32),
        jax.ShapeDtypeStruct((n, d_kv), BF16),
        jax.ShapeDtypeStruct((n, d_qi), BF16),
        jax.ShapeDtypeStruct((n, IDX_DIM), F32),
        jax.ShapeDtypeStruct((n, IDX_DIM), BF16),
        jax.ShapeDtypeStruct((n, N_IDX_HEADS), F32),
        jax.ShapeDtypeStruct((n, d_model), F32),
        jax.ShapeDtypeStruct((n, d_model), F32),
    ]
    out_specs = [row(s.shape[1]) for s in out_shapes]
    return pl.pallas_call(
        kern, grid=grid,
        in_specs=[row(d_model), _const_spec((1, d_model)), _const_spec(w_pack.shape), tab, tab, tab],
        out_specs=out_specs, out_shape=out_shapes,
        compiler_params=pltpu.CompilerParams(dimension_semantics=("arbitrary",),
                                             vmem_limit_bytes=VMEM_LIMIT),
        name="in_proj",
    )(x2d, norm_g.reshape(1, d_model), w_pack, cos_t, sin_a, sin_b)


def _s5_readout(y, u, dsk_ref, wglu_ref, bglu_ref, d_model):
    y = y + dsk_ref[...] * u
    cdf = 0.5 * (1.0 + jnp.tanh(math.sqrt(2.0 / math.pi) * (y + 0.044715 * (y * y * y))))
    gl = (y * cdf).astype(BF16)
    z = jnp.dot(gl, wglu_ref[...], preferred_element_type=F32) + bglu_ref[...]
    return z[:, :d_model] * _sigmoid(z[:, d_model:])


def _s5_prompt_kernel(u_ref, bre_ref, bim_ref, tab_ref, cre_ref, cim_ref, dsk_ref, wglu_ref, bglu_ref,
                      out_ref, sre_ref, sim_ref, hre, him, cr, ci, *, tc, d_model):
    @pl.when(pl.program_id(1) == 0)
    def _():
        cr[...] = jnp.zeros_like(cr)
        ci[...] = jnp.zeros_like(ci)

    u = u_ref[...]
    ub = u.astype(BF16)
    hre[...] = jnp.dot(ub, bre_ref[...], preferred_element_type=F32)
    him[...] = jnp.dot(ub, bim_ref[...], preferred_element_type=F32)

    def group(gi, carry):
        car, cai = carry
        r0 = pl.multiple_of(gi * SUBLANES, SUBLANES)
        xr = hre[pl.ds(r0, SUBLANES), :]
        xi = him[pl.ds(r0, SUBLANES), :]
        for t in range(3):
            ar, ai = tab_ref[2 * t], tab_ref[2 * t + 1]
            sr = pltpu.roll(xr, 1 << t, axis=0)
            si = pltpu.roll(xi, 1 << t, axis=0)
            xr, xi = xr + (ar * sr - ai * si), xi + (ar * si + ai * sr)
        pr, pi = tab_ref[6], tab_ref[7]
        xr, xi = xr + (pr * car - pi * cai), xi + (pr * cai + pi * car)
        hre[pl.ds(r0, SUBLANES), :] = xr
        him[pl.ds(r0, SUBLANES), :] = xi
        return xr[SUBLANES - 1:SUBLANES, :], xi[SUBLANES - 1:SUBLANES, :]

    car, cai = lax.fori_loop(0, tc // SUBLANES, group, (cr[...], ci[...]))
    cr[...] = car
    ci[...] = cai
    sre_ref[...] = car
    sim_ref[...] = cai
    y = (jnp.dot(hre[...].astype(BF16), cre_ref[...], preferred_element_type=F32)
         + jnp.dot(him[...].astype(BF16), cim_ref[...], preferred_element_type=F32))
    out_ref[...] = _s5_readout(y, u, dsk_ref, wglu_ref, bglu_ref, d_model)


def _s5_sample_kernel(u_ref, h0r_ref, h0i_ref, are_ref, aim_ref, bre_ref, bim_ref, cre_ref, cim_ref,
                      dsk_ref, wglu_ref, bglu_ref, out_ref, sre_ref, sim_ref, *, d_model):
    u = u_ref[...]
    ub = u.astype(BF16)
    ar, ai = are_ref[...], aim_ref[...]
    h0r, h0i = h0r_ref[...], h0i_ref[...]
    hr = (ar * h0r - ai * h0i) + jnp.dot(ub, bre_ref[...], preferred_element_type=F32)
    hi = (ar * h0i + ai * h0r) + jnp.dot(ub, bim_ref[...], preferred_element_type=F32)
    sre_ref[...] = hr
    sim_ref[...] = hi
    y = (jnp.dot(hr.astype(BF16), cre_ref[...], preferred_element_type=F32)
         + jnp.dot(hi.astype(BF16), cim_ref[...], preferred_element_type=F32))
    out_ref[...] = _s5_readout(y, u, dsk_ref, wglu_ref, bglu_ref, d_model)


def _s5_params(a_re, a_im, log_dt, b_re, b_im, c_re, c_im):
    g, p = a_re.shape
    a_c = lax.complex(a_re.astype(F32), a_im.astype(F32))
    dt = jnp.exp(log_dt.astype(F32))[:, None]
    a_bar = jnp.exp(a_c * dt)
    b_bar = ((a_bar - 1.0) / a_c)[:, :, None] * lax.complex(b_re.astype(F32), b_im.astype(F32))
    eye = jnp.eye(g, dtype=F32)
    n = b_re.shape[2]

    def bmat(b):
        return jnp.einsum('gpn,gh->gnhp', b, eye).reshape(g * n, g * p)

    def cmat(c):
        return jnp.einsum('gnp,gh->gphn', c, eye).reshape(g * p, g * n)

    bre, bim = bmat(jnp.real(b_bar)).astype(BF16), bmat(jnp.imag(b_bar)).astype(BF16)
    cre, cim = cmat(c_re.astype(F32)).astype(BF16), cmat(-c_im.astype(F32)).astype(BF16)
    a1 = a_bar.reshape(1, g * p)
    a2 = a1 * a1
    a3 = a2 * a1
    a4 = a2 * a2
    pw = [a1, a2, a3, a4, a4 * a1, a4 * a2, a4 * a3, a4 * a4]
    rows = jnp.arange(SUBLANES)[:, None]
    tabs = []
    for sh in (1, 2, 4):
        t = jnp.where(rows >= sh, jnp.broadcast_to(pw[sh - 1], (SUBLANES, g * p)), 0.0)
        tabs += [jnp.real(t), jnp.imag(t)]
    pcat = jnp.concatenate(pw, axis=0)
    tabs += [jnp.real(pcat), jnp.imag(pcat)]
    tab = jnp.stack(tabs, axis=0).astype(F32)
    return jnp.real(a1), jnp.imag(a1), bre, bim, cre, cim, tab


def _s5_prompt(u3, s5p, dsk, wglu, bglu, tc):
    bsz, t_len, d_ssm = u3.shape
    _, _, bre, bim, cre, cim, tab = s5p
    sd = bre.shape[1]
    d_model = wglu.shape[1] // 2
    kern = functools.partial(_s5_prompt_kernel, tc=tc, d_model=d_model)
    return pl.pallas_call(
        kern, grid=(bsz, t_len // tc),
        in_specs=[pl.BlockSpec((None, tc, d_ssm), lambda b, c: (b, c, 0)),
                  _const_spec(bre.shape), _const_spec(bim.shape), _const_spec(tab.shape),
                  _const_spec(cre.shape), _const_spec(cim.shape), _const_spec((1, d_ssm)),
                  _const_spec(wglu.shape), _const_spec((1, 2 * d_model))],
        out_specs=[pl.BlockSpec((None, tc, d_model), lambda b, c: (b, c, 0)),
                   pl.BlockSpec((None, 1, sd), lambda b, c: (b, 0, 0)),
                   pl.BlockSpec((None, 1, sd), lambda b, c: (b, 0, 0))],
        out_shape=[jax.ShapeDtypeStruct((bsz, t_len, d_model), F32),
                   jax.ShapeDtypeStruct((bsz, 1, sd), F32),
                   jax.ShapeDtypeStruct((bsz, 1, sd), F32)],
        scratch_shapes=[pltpu.VMEM((tc, sd), F32), pltpu.VMEM((tc, sd), F32),
                        pltpu.VMEM((1, sd), F32), pltpu.VMEM((1, sd), F32)],
        compiler_params=pltpu.CompilerParams(dimension_semantics=("arbitrary", "arbitrary"),
                                             vmem_limit_bytes=VMEM_LIMIT),
        name="s5_prompt",
    )(u3, bre, bim, tab, cre, cim, dsk.reshape(1, d_ssm), wglu, bglu.reshape(1, 2 * d_model))


def _s5_sample(u2, h0r, h0i, s5p, dsk, wglu, bglu):
    n, d_ssm = u2.shape
    are, aim, bre, bim, cre, cim, _ = s5p
    sd = bre.shape[1]
    d_model = wglu.shape[1] // 2
    kern = functools.partial(_s5_sample_kernel, d_model=d_model)
    return pl.pallas_call(
        kern,
        out_shape=[jax.ShapeDtypeStruct((n, d_model), F32),
                   jax.ShapeDtypeStruct((n, sd), F32),
                   jax.ShapeDtypeStruct((n, sd), F32)],
        compiler_params=pltpu.CompilerParams(vmem_limit_bytes=VMEM_LIMIT),
        name="s5_sample",
    )(u2, h0r, h0i, are, aim, bre, bim, cre, cim, dsk.reshape(1, d_ssm), wglu,
      bglu.reshape(1, 2 * d_model))


def _f2k(x):
    b = lax.bitcast_convert_type(x, I32)
    return jnp.where(b < 0, b ^ 0x7FFFFFFF, b)


def _k2f(k):
    return lax.bitcast_convert_type(jnp.where(k < 0, k ^ 0x7FFFFFFF, k), F32)


def _threshold_search(count_ge, amax, forced, k_top):
    kf = float(k_top)
    lo0 = jnp.where(forced, KEY_LOW, _f2k(-amax))
    hi0 = jnp.where(forced, KEY_LOW + 1, _f2k(amax) + 1)

    def cond(st):
        lo, hi, _ = st
        return jnp.max(jnp.where(hi > lo + 1, 1.0, 0.0)) > 0.5

    def body(st):
        lo, hi, it = st
        active = hi > lo + 1
        mid_i = (lo >> 1) + (hi >> 1) + (lo & hi & 1)
        mk = _f2k(0.5 * _k2f(lo) + 0.5 * _k2f(hi))
        use_f = jnp.logical_and(it % 2 == 0, jnp.logical_and(mk > lo, mk < hi))
        mid = jnp.where(use_f, mk, mid_i)
        c = count_ge(_k2f(mid))
        ge = c >= kf
        nlo = jnp.where(ge, mid, lo)
        nhi = jnp.where(c == kf, mid + 1, jnp.where(ge, hi, mid))
        return jnp.where(active, nlo, lo), jnp.where(active, nhi, hi), it + 1

    lo, _, _ = lax.while_loop(cond, body, (lo0, hi0, jnp.int32(0)))
    return _k2f(lo)


def _tri_exclusive(n):
    r = lax.broadcasted_iota(I32, (n, n), 0)
    c = lax.broadcasted_iota(I32, (n, n), 1)
    return (r < c).astype(BF16)


def _dsa_prompt_kernel(qi_ref, wi_ref, q_ref, ki_ref, k_ref, v_ref, o_ref,
                       sc_ref, qis_ref, qs_ref, vaug_ref, wrep_ref, amx_ref, lo_ref, hi_ref, cnt_ref,
                       thr_ref, mx_ref, acc_ref, *, tq, tk, k_top):
    i = pl.program_id(1)
    nkb = i + 1
    hpg = N_HEADS // N_KV_HEADS
    nch = tk // LANES
    kf = float(k_top)
    ones_b = jnp.ones((LANES, LANES), BF16)

    @pl.when(i == 0)
    def _():
        v = v_ref[...].astype(F32)
        lane = lax.broadcasted_iota(I32, v.shape, 1)
        one = jnp.where(lane == HEAD_DIM, 1.0, 0.0)
        vaug_ref[0] = jnp.where(lane < HEAD_DIM, v, one).astype(BF16)
        vaug_ref[1] = jnp.where(lane < HEAD_DIM, pltpu.roll(v, HEAD_DIM, axis=1), one).astype(BF16)

    qi = qi_ref[...]
    q = q_ref[...]
    zpad = jnp.zeros((tq, HEAD_DIM), BF16)
    for h in range(N_IDX_HEADS):
        qis_ref[h * tq:(h + 1) * tq, :] = qi[:, h * IDX_DIM:(h + 1) * IDX_DIM]
    for h in range(N_HEADS):
        qh = q[:, h * HEAD_DIM:(h + 1) * HEAD_DIM]
        pair = [qh, zpad] if h // hpg == 0 else [zpad, qh]
        qs_ref[h * tq:(h + 1) * tq, :] = jnp.concatenate(pair, axis=1)

    w = wi_ref[...] * (IDX_DIM ** -0.5)
    for h in range(N_IDX_HEADS):
        wrep_ref[h] = jnp.broadcast_to(w[:, h:h + 1], (tq, LANES))
    row = i * tq + lax.broadcasted_iota(I32, (tq, LANES), 0)
    lane = lax.broadcasted_iota(I32, (tq, LANES), 1)

    def score_blk(j, diagonal):
        k0 = pl.multiple_of(j * tk, tk)
        d = lax.dot_general(qis_ref[...], ki_ref[pl.ds(k0, tk), :], NT_DIMS, preferred_element_type=F32)
        amx = amx_ref[...]
        for c in range(nch):
            acc = jnp.zeros((tq, LANES), F32)
            for h in range(N_IDX_HEADS):
                acc = acc + jnp.maximum(d[h * tq:(h + 1) * tq, c * LANES:(c + 1) * LANES], 0.0) * wrep_ref[h]
            mag = jnp.abs(acc)
            if diagonal:
                causal = (j * tk + c * LANES + lane) <= row
                acc = jnp.where(causal, acc, NEG_INF)
                mag = jnp.where(causal, mag, 0.0)
            sc_ref[j, :, c * LANES:(c + 1) * LANES] = acc
            amx = jnp.maximum(amx, mag)
        amx_ref[...] = amx

    amx_ref[...] = jnp.zeros(amx_ref.shape, F32)

    def score_body(j, carry):
        score_blk(j, False)
        return carry

    lax.fori_loop(0, nkb - 1, score_body, 0)
    score_blk(nkb - 1, True)

    amax = jnp.broadcast_to(jnp.max(amx_ref[...], axis=1, keepdims=True), (tq, LANES))
    forced = (row + 1) <= k_top
    lo_ref[...] = jnp.where(forced, KEY_LOW, _f2k(-amax))
    hi_ref[...] = jnp.where(forced, KEY_LOW + 1, _f2k(amax) + 1)
    chunks = [slice(r, r + ROW_CHUNK) for r in range(0, tq, ROW_CHUNK)]

    def count_rows(rows, thr, strict):
        def body(j, part):
            for c in range(nch):
                blk = sc_ref[j, rows, c * LANES:(c + 1) * LANES]
                hit = (blk > thr) if strict else (blk >= thr)
                part = part + jnp.where(hit, 1.0, 0.0)
            return part

        return lax.fori_loop(0, nkb, body, jnp.zeros(thr.shape, F32))

    def count_all(strict):
        part = jnp.concatenate([count_rows(rows, thr_ref[rows, :], strict) for rows in chunks], axis=0)
        return jnp.dot(part.astype(BF16), ones_b, preferred_element_type=F32)

    def search_pass(value_space):
        lo, hi = lo_ref[...], hi_ref[...]
        active = hi > lo + 1
        span = hi - lo
        mid = lo + lax.shift_right_logical(span, jnp.ones_like(span))
        if value_space:
            mk = _f2k(0.5 * _k2f(lo) + 0.5 * _k2f(hi))
            mid = jnp.where(jnp.logical_and(mk > lo, mk < hi), mk, mid)
        thr_ref[...] = _k2f(mid)
        c = count_all(False)
        ge = c >= kf
        nlo = jnp.where(ge, mid, lo)
        nhi = jnp.where(c == kf, mid + 1, jnp.where(ge, hi, mid))
        lo, hi = jnp.where(active, nlo, lo), jnp.where(active, nhi, hi)
        lo_ref[...] = lo
        hi_ref[...] = hi
        return (jnp.max(jnp.where(hi > lo + 1, 1.0, 0.0)) > 0.5).astype(I32)

    def search_body(_):
        search_pass(True)
        return search_pass(False)

    lax.while_loop(lambda unresolved: unresolved > 0, search_body, jnp.int32(1))
    thr_ref[...] = _k2f(lo_ref[...])
    cnt_ref[...] = count_all(False)

    @pl.when(jnp.max(jnp.where(cnt_ref[...] > kf, 1.0, 0.0)) > 0.5)
    def _():
        tri = _tri_exclusive(tk)
        ones_k = jnp.ones((tk, LANES), BF16)
        wide = lambda x: jnp.concatenate([x] * nch, axis=1)
        c_gt = count_all(True)
        for rows in chunks:
            thr = thr_ref[rows, :]
            need = jnp.where(cnt_ref[rows, :] > kf, kf - c_gt[rows, :], 1e30)

            def fix(j, run, rows=rows, thr=thr, need=need):
                sc = sc_ref[j, rows, :]
                eq = sc == wide(thr)
                eqb = jnp.where(eq, 1.0, 0.0).astype(BF16)
                before = jnp.dot(eqb, tri, preferred_element_type=F32) + wide(run)
                drop = jnp.logical_and(eq, before >= wide(need))
                sc_ref[j, rows, :] = jnp.where(drop, NEG_INF, sc)
                return run + jnp.dot(eqb, ones_k, preferred_element_type=F32)

            lax.fori_loop(0, nkb, fix, jnp.zeros((ROW_CHUNK, LANES), F32))

    mx_ref[...] = jnp.full(mx_ref.shape, NEG_INF, F32)

    def max_blk(j, carry):
        k0 = pl.multiple_of(j * tk, tk)
        s = lax.dot_general(qs_ref[...], k_ref[pl.ds(k0, tk), :], NT_DIMS, preferred_element_type=F32)
        for c in range(nch):
            cs = slice(c * LANES, (c + 1) * LANES)
            bias = jnp.where(sc_ref[j, :, cs] >= thr_ref[...], 0.0, NEG_INF)
            sc_ref[j, :, cs] = bias
            for h in range(N_HEADS):
                rows = slice(h * tq, (h + 1) * tq)
                mx_ref[rows, :] = jnp.maximum(mx_ref[rows, :], s[rows, cs] + bias)
        return carry

    lax.fori_loop(0, nkb, max_blk, 0)
    for h in range(N_HEADS):
        rows = slice(h * tq, (h + 1) * tq)
        m = jnp.max(mx_ref[rows, :], axis=1, keepdims=True)
        mx_ref[rows, :] = jnp.broadcast_to(jnp.where(m == NEG_INF, 0.0, m), (tq, LANES))

    acc_ref[...] = jnp.zeros(acc_ref.shape, F32)

    def att_blk(j, carry):
        k0 = pl.multiple_of(j * tk, tk)
        s = lax.dot_general(qs_ref[...], k_ref[pl.ds(k0, tk), :], NT_DIMS, preferred_element_type=F32)
        for h in range(N_HEADS):
            rows = slice(h * tq, (h + 1) * tq)
            m = mx_ref[rows, :]
            p = jnp.concatenate(
                [jnp.exp(s[rows, c * LANES:(c + 1) * LANES] + sc_ref[j, :, c * LANES:(c + 1) * LANES] - m)
                 for c in range(nch)], axis=1).astype(BF16)
            acc_ref[rows, :] += jnp.dot(p, vaug_ref[h // hpg, pl.ds(k0, tk), :], preferred_element_type=F32)
        return carry

    lax.fori_loop(0, nkb, att_blk, 0)

    for h in range(N_HEADS):
        a = acc_ref[h * tq:(h + 1) * tq, :]
        o = a[:, :HEAD_DIM] / a[:, HEAD_DIM:HEAD_DIM + 1]
        o_ref[:, h * HEAD_DIM:(h + 1) * HEAD_DIM] = o.astype(o_ref.dtype)


def _dsa_prompt(qi, wi, q, kib, kb, vb, tq, tk):
    bsz, t_len, d_qi = qi.shape
    d_attn = q.shape[2]
    d_kv = kb.shape[2]
    k_top = min(TOPK_MAX, t_len // 4)
    assert tq == tk and tq % ROW_CHUNK == 0 and t_len % tq == 0
    kern = functools.partial(_dsa_prompt_kernel, tq=tq, tk=tk, k_top=k_top)
    tile = lambda w: pl.BlockSpec((None, tq, w), lambda b, i: (b, i, 0))
    full = lambda w: pl.BlockSpec((None, t_len, w), lambda b, i: (b, 0, 0))
    return pl.pallas_call(
        kern, grid=(bsz, t_len // tq),
        in_specs=[tile(d_qi), tile(N_IDX_HEADS), tile(d_attn), full(IDX_DIM), full(d_kv), full(d_kv)],
        out_specs=tile(d_attn),
        out_shape=jax.ShapeDtypeStruct((bsz, t_len, d_attn), BF16),
        scratch_shapes=[pltpu.VMEM((t_len // tk, tq, tk), F32),
                        pltpu.VMEM((N_IDX_HEADS * tq, IDX_DIM), BF16),
                        pltpu.VMEM((N_HEADS * tq, N_KV_HEADS * HEAD_DIM), BF16),
                        pltpu.VMEM((N_KV_HEADS, t_len, LANES), BF16),
                        pltpu.VMEM((N_IDX_HEADS, tq, LANES), F32),
                        pltpu.VMEM((tq, LANES), F32),
                        pltpu.VMEM((tq, LANES), I32),
                        pltpu.VMEM((tq, LANES), I32),
                        pltpu.VMEM((tq, LANES), F32),
                        pltpu.VMEM((tq, LANES), F32),
                        pltpu.VMEM((N_HEADS * tq, LANES), F32),
                        pltpu.VMEM((N_HEADS * tq, LANES), F32)],
        compiler_params=pltpu.CompilerParams(dimension_semantics=("arbitrary", "arbitrary"),
                                             vmem_limit_bytes=VMEM_LIMIT),
        name="dsa_prompt",
    )(qi, wi, q, kib, kb, vb)


def _idx_score_kernel(pt_ref, q_ref, w_ref, *refs, n_pg):
    pages, o_ref = refs[:n_pg], refs[n_pg]
    b = pl.program_id(1)
    q = q_ref[...]
    w = w_ref[...] * (IDX_DIM ** -0.5)
    parts = []
    for p in range(n_pg):
        d = lax.dot_general(q, pages[p][...].astype(BF16), NT_DIMS, preferred_element_type=F32)
        parts.append(jnp.sum(jnp.maximum(d, 0.0) * w, axis=0, keepdims=True))
    o_ref[pl.ds(b % SUBLANES, 1), :] = jnp.concatenate(parts, axis=1)


def _idx_scores(page_table, qis, wcol, cache_idx_k, n_pg):
    sq, n_pages = page_table.shape
    page = cache_idx_k.shape[1]
    kern = functools.partial(_idx_score_kernel, n_pg=n_pg)
    page_specs = [pl.BlockSpec((None, page, IDX_DIM),
                               lambda j, b, pt, p=p: (pt[b * n_pages + j * n_pg + p], 0, 0))
                  for p in range(n_pg)]
    grid_spec = pltpu.PrefetchScalarGridSpec(
        num_scalar_prefetch=1, grid=(n_pages // n_pg, sq),
        in_specs=[pl.BlockSpec((None, N_IDX_HEADS, IDX_DIM), lambda j, b, pt: (b, 0, 0)),
                  pl.BlockSpec((None, N_IDX_HEADS, 1), lambda j, b, pt: (b, 0, 0))] + page_specs,
        out_specs=pl.BlockSpec((SUBLANES, n_pg * page), lambda j, b, pt: (b // SUBLANES, j)))
    return pl.pallas_call(
        kern, grid_spec=grid_spec,
        out_shape=jax.ShapeDtypeStruct((sq, n_pages * page), F32),
        compiler_params=pltpu.CompilerParams(dimension_semantics=("arbitrary", "arbitrary")),
        name="sample_idx_scores",
    )(page_table.reshape(-1), qis, wcol, *([cache_idx_k] * n_pg))


def _sample_select_kernel(sc_ref, qi_ref, ki_ref, wi_ref, mask_ref, selfsel_ref, *, ch, k_top):
    sq, l_past = sc_ref.shape
    nch = l_past // ch
    w = wi_ref[...] * (IDX_DIM ** -0.5)
    qf = qi_ref[...].astype(F32)
    kf32 = ki_ref[...].astype(F32)
    s_self = jnp.zeros((sq, 1), F32)
    for h in range(N_IDX_HEADS):
        dh = jnp.sum(qf[:, h * IDX_DIM:(h + 1) * IDX_DIM] * kf32, axis=1, keepdims=True)
        s_self = s_self + jnp.maximum(dh, 0.0) * w[:, h:h + 1]

    amax = jnp.abs(s_self)
    for c in range(nch):
        amax = jnp.maximum(amax, jnp.max(jnp.abs(sc_ref[:, c * ch:(c + 1) * ch]), axis=1, keepdims=True))

    def count_cmp(thr, strict):
        thr_b = jnp.broadcast_to(thr, (sq, LANES))
        part = jnp.zeros((sq, LANES), F32)
        for c in range(l_past // LANES):
            blk = sc_ref[:, c * LANES:(c + 1) * LANES]
            hit = (blk > thr_b) if strict else (blk >= thr_b)
            part = part + jnp.where(hit, 1.0, 0.0)
        self_hit = (s_self > thr) if strict else (s_self >= thr)
        return jnp.sum(part, axis=1, keepdims=True) + jnp.where(self_hit, 1.0, 0.0)

    kf = float(k_top)
    forced = amax < 0.0
    thr = _threshold_search(lambda t: count_cmp(t, False), amax, forced, k_top)
    need = kf - count_cmp(thr, True)
    tri = _tri_exclusive(ch)
    run = jnp.zeros((sq, 1), F32)
    for c in range(nch):
        sc = sc_ref[:, c * ch:(c + 1) * ch]
        eq = sc == thr
        before = jnp.dot(jnp.where(eq, 1.0, 0.0).astype(BF16), tri, preferred_element_type=F32) + run
        keep = jnp.logical_or(sc > thr, jnp.logical_and(eq, before < need))
        mask_ref[:, c * ch:(c + 1) * ch] = jnp.where(keep, 1.0, 0.0)
        run = run + jnp.sum(jnp.where(eq, 1.0, 0.0), axis=1, keepdims=True)
    self_keep = jnp.logical_or(s_self > thr, jnp.logical_and(s_self == thr, run < need))
    selfsel_ref[...] = jnp.where(self_keep, 1.0, 0.0)


def _sample_select(scores, qi, kib, wi, ch, k_top):
    sq, l_past = scores.shape
    kern = functools.partial(_sample_select_kernel, ch=ch, k_top=k_top)
    return pl.pallas_call(
        kern,
        out_shape=[jax.ShapeDtypeStruct((sq, l_past), F32), jax.ShapeDtypeStruct((sq, 1), F32)],
        compiler_params=pltpu.CompilerParams(vmem_limit_bytes=VMEM_LIMIT),
        name="sample_select",
    )(scores, qi, kib, wi)


def _sample_attn_kernel(pt_ref, q_ref, mask_ref, ks_ref, vs_ref, ss_ref, *refs, n_pg):
    kpages, vpages = refs[:n_pg], refs[n_pg:2 * n_pg]
    o_ref, m_scr, l_scr, acc_scr = refs[2 * n_pg:]
    j = pl.program_id(1)
    hpg = N_HEADS // N_KV_HEADS

    @pl.when(j == 0)
    def _():
        m_scr[...] = jnp.full(m_scr.shape, NEG_INF, F32)
        l_scr[...] = jnp.zeros(l_scr.shape, F32)
        acc_scr[...] = jnp.zeros(acc_scr.shape, F32)

    q = q_ref[...]
    page = kpages[0].shape[0]
    s = jnp.concatenate(
        [lax.dot_general(q, kpages[p][...].astype(BF16), NT_DIMS, preferred_element_type=F32)
         for p in range(n_pg)], axis=1)
    sm = jnp.where(mask_ref[...] > 0.5, s, NEG_INF)
    m_old = m_scr[...]
    m_new = jnp.maximum(m_old, jnp.max(sm, axis=1, keepdims=True))
    m_safe = jnp.where(m_new == NEG_INF, 0.0, m_new)
    p_ = jnp.exp(sm - m_safe)
    alpha = jnp.exp(m_old - m_safe)
    pb = p_.astype(BF16)
    pv = jnp.zeros(acc_scr.shape, F32)
    for p in range(n_pg):
        pv = pv + jnp.dot(pb[:, p * page:(p + 1) * page], vpages[p][...].astype(BF16),
                          preferred_element_type=F32)
    l_scr[...] = l_scr[...] * alpha + jnp.sum(pb.astype(F32), axis=1, keepdims=True)
    acc_scr[...] = acc_scr[...] * alpha + pv
    m_scr[...] = m_new

    @pl.when(j == pl.num_programs(1) - 1)
    def _():
        s_self = jnp.sum(q.astype(F32) * ks_ref[...].astype(F32), axis=1, keepdims=True)
        s_self = jnp.where(ss_ref[...] > 0.5, s_self, NEG_INF)
        m_o = m_scr[...]
        m_n = jnp.maximum(m_o, s_self)
        m_s = jnp.where(m_n == NEG_INF, 0.0, m_n)
        p_self = jnp.exp(s_self - m_s).astype(BF16).astype(F32)
        al = jnp.exp(m_o - m_s)
        l_fin = l_scr[...] * al + p_self
        acc = acc_scr[...] * al + p_self * vs_ref[...].astype(F32)
        o = acc / l_fin
        hrow = lax.broadcasted_iota(I32, o.shape, 0)
        o = jnp.where(hrow < hpg, o, pltpu.roll(o, HEAD_DIM, axis=1))
        o_ref[...] = o[:, :HEAD_DIM].astype(o_ref.dtype)


def _sample_attn(page_table, qs, mask3, kself, vself, selfsel, cache_k2, cache_v2, n_pg):
    sq, n_pages = page_table.shape
    page, d_kv = cache_k2.shape[1], cache_k2.shape[2]
    kern = functools.partial(_sample_attn_kernel, n_pg=n_pg)
    pspec = lambda p: pl.BlockSpec((None, page, d_kv),
                                   lambda b, j, pt: (pt[b * n_pages + j * n_pg + p], 0, 0))
    per_seq = lambda r, w: pl.BlockSpec((None, r, w), lambda b, j, pt: (b, 0, 0))
    grid_spec = pltpu.PrefetchScalarGridSpec(
        num_scalar_prefetch=1, grid=(sq, n_pages // n_pg),
        in_specs=[per_seq(N_HEADS, d_kv),
                  pl.BlockSpec((None, 1, n_pg * page), lambda b, j, pt: (b, 0, j)),
                  per_seq(1, d_kv), per_seq(1, d_kv), per_seq(1, 1)]
                 + [pspec(p) for p in range(n_pg)] + [pspec(p) for p in range(n_pg)],
        out_specs=per_seq(N_HEADS, HEAD_DIM),
        scratch_shapes=[pltpu.VMEM((N_HEADS, 1), F32), pltpu.VMEM((N_HEADS, 1), F32),
                        pltpu.VMEM((N_HEADS, d_kv), F32)])
    return pl.pallas_call(
        kern, grid_spec=grid_spec,
        out_shape=jax.ShapeDtypeStruct((sq, N_HEADS, HEAD_DIM), BF16),
        compiler_params=pltpu.CompilerParams(dimension_semantics=("arbitrary", "arbitrary")),
        name="sample_attn",
    )(page_table.reshape(-1), qs, mask3, kself, vself, selfsel,
      *([cache_k2] * n_pg), *([cache_v2] * n_pg))


def _tail_kernel(x_ref, attn_ref, ssm_ref, sgs_ref, sga_ref, wao_ref, wo_ref, g2_ref, wup_ref, wdn_ref,
                 gf_ref, y_ref):
    attn_out = jnp.dot(attn_ref[...], wao_ref[...], preferred_element_type=F32)
    mix = sgs_ref[...] * ssm_ref[...] + sga_ref[...] * attn_out
    x1 = x_ref[...] + jnp.dot(mix.astype(BF16), wo_ref[...], preferred_element_type=F32)
    hh = _rms_norm(x1, g2_ref[...]).astype(BF16)
    up = jnp.dot(hh, wup_ref[...], preferred_element_type=F32)
    r = jnp.square(jnp.maximum(up, 0.0)).astype(BF16)
    x2 = x1 + jnp.dot(r, wdn_ref[...], preferred_element_type=F32)
    y_ref[...] = _rms_norm(x2, gf_ref[...])


def _tail(x2d, attn, ssm, sgs, sga, wao, wo, g2, wup, wdn, gf, tm):
    n, d_model = x2d.shape
    row = lambda w: pl.BlockSpec((tm, w), lambda i: (i, 0))
    return pl.pallas_call(
        _tail_kernel, grid=(n // tm,),
        in_specs=[row(d_model), row(attn.shape[1]), row(d_model), row(d_model), row(d_model),
                  _const_spec(wao.shape), _const_spec(wo.shape), _const_spec((1, d_model)),
                  _const_spec(wup.shape), _const_spec(wdn.shape), _const_spec((1, d_model))],
        out_specs=row(d_model),
        out_shape=jax.ShapeDtypeStruct((n, d_model), F32),
        compiler_params=pltpu.CompilerParams(dimension_semantics=("arbitrary",),
                                             vmem_limit_bytes=VMEM_LIMIT),
        name="tail",
    )(x2d, attn, ssm, sgs, sga, wao, wo, g2.reshape(1, d_model), wup, wdn, gf.reshape(1, d_model))


def _tiles(n_rows, t_len):
    tm = min(256, n_rows)
    tc = min(256, t_len)
    tq = min(256, t_len)
    return tm, tc, tq, tq


def kernel(x_prompt, x_sample, cache_k, cache_v, cache_idx_k, state_ssm_re, state_ssm_im, page_table,
           norm1_g, w_in, ssm_a_re, ssm_a_im, ssm_log_dt, ssm_b_re, ssm_b_im, ssm_c_re, ssm_c_im,
           ssm_d, w_glu, b_glu, w_attn_out, w_o, norm2_g, w_up, w_down, normf_g):
    bsz, t_len, d_model = x_prompt.shape
    sq, s_len, _ = x_sample.shape
    assert s_len == 1, "the sample path handles one new token per sequence"
    n_pool, page = cache_idx_k.shape[0], cache_idx_k.shape[1]
    n_pages = page_table.shape[1]
    past = n_pages * page
    d_ssm = ssm_d.shape[0]
    d_attn = N_HEADS * HEAD_DIM
    d_kv = N_KV_HEADS * HEAD_DIM
    d_qi = N_IDX_HEADS * IDX_DIM
    dims = (d_ssm, d_attn, d_kv, d_qi)

    c_ki = d_ssm + d_attn + 2 * d_kv + d_qi
    c_g = c_ki + IDX_DIM + N_IDX_HEADS
    w_pack = jnp.concatenate(
        [w_in[:, :c_g], jnp.zeros((d_model, LANES - IDX_DIM - N_IDX_HEADS), w_in.dtype), w_in[:, c_g:]],
        axis=1).astype(BF16)
    wglu_b, wao_b, wo_b = w_glu.astype(BF16), w_attn_out.astype(BF16), w_o.astype(BF16)
    wup_b, wdn_b = w_up.astype(BF16), w_down.astype(BF16)
    s5p = _s5_params(ssm_a_re, ssm_a_im, ssm_log_dt, ssm_b_re, ssm_b_im, ssm_c_re, ssm_c_im)

    n_p = bsz * t_len
    tm, tc, tq, tk = _tiles(n_p, t_len)
    xp = x_prompt.reshape(n_p, d_model)
    tabs_p = _rope_tables(jnp.arange(t_len, dtype=I32), t_len)
    (u, q, k, kb, v, vb, qi, ki, kib, wi, sgs, sga) = _in_proj(xp, tabs_p, t_len // tm, norm1_g, w_pack, tm, dims)
    ssm_out, re_p, im_p = _s5_prompt(u.reshape(bsz, t_len, d_ssm), s5p, ssm_d, wglu_b, b_glu, tc)
    r3 = lambda a: a.reshape(bsz, t_len, a.shape[-1])
    attn = _dsa_prompt(r3(qi), r3(wi), r3(q), r3(kib), r3(kb), r3(vb), tq, tk)
    y_p = _tail(xp, attn.reshape(n_p, d_attn), ssm_out.reshape(n_p, d_model), sgs, sga,
                wao_b, wo_b, norm2_g, wup_b, wdn_b, normf_g, tm)

    xs = x_sample.reshape(sq, d_model)
    tabs_s = _rope_tables(jnp.full((sq,), past, I32), sq)
    (u_s, q_s, k_s, kb_s, v_s, vb_s, qi_s, ki_s, kib_s, wi_s, sgs_s, sga_s) = _in_proj(
        xs, tabs_s, 1, norm1_g, w_pack, sq, dims)
    sd = state_ssm_re.shape[1] * state_ssm_re.shape[2]
    ssm_s, re_s, im_s = _s5_sample(u_s, state_ssm_re.reshape(sq, sd), state_ssm_im.reshape(sq, sd),
                                   s5p, ssm_d, wglu_b, b_glu)
    n_pg = math.gcd(n_pages, 16)
    scores = _idx_scores(page_table, qi_s.reshape(sq, N_IDX_HEADS, IDX_DIM),
                         wi_s.reshape(sq, N_IDX_HEADS, 1), cache_idx_k, n_pg)
    k_top_s = min(TOPK_MAX, (past + s_len) // 4)
    mask, selfsel = _sample_select(scores, qi_s, kib_s, wi_s, min(512, past), k_top_s)
    q4 = q_s.reshape(sq, N_KV_HEADS, N_HEADS // N_KV_HEADS, HEAD_DIM)
    qs_pad = (q4[:, :, :, None, :] * jnp.eye(N_KV_HEADS, dtype=BF16)[None, :, None, :, None]
              ).reshape(sq, N_HEADS, d_kv)
    attn_s = _sample_attn(page_table, qs_pad, mask.reshape(sq, 1, past),
                          kb_s.reshape(sq, 1, d_kv), vb_s.reshape(sq, 1, d_kv), selfsel.reshape(sq, 1, 1),
                          cache_k.reshape(n_pool, page, d_kv), cache_v.reshape(n_pool, page, d_kv), n_pg)
    y_s = _tail(xs, attn_s.reshape(sq, d_attn), ssm_s, sgs_s, sga_s,
                wao_b, wo_b, norm2_g, wup_b, wdn_b, normf_g, sq)

    g_ssm, p_ssm = state_ssm_re.shape[1], state_ssm_re.shape[2]
    return (y_p.reshape(bsz, t_len, d_model), y_s.reshape(sq, s_len, d_model),
            k.reshape(bsz, t_len, N_KV_HEADS, HEAD_DIM), v.reshape(bsz, t_len, N_KV_HEADS, HEAD_DIM),
            ki.reshape(bsz, t_len, IDX_DIM),
            re_p.reshape(bsz, g_ssm, p_ssm), im_p.reshape(bsz, g_ssm, p_ssm),
            k_s.reshape(sq, s_len, N_KV_HEADS, HEAD_DIM), v_s.reshape(sq, s_len, N_KV_HEADS, HEAD_DIM),
            ki_s.reshape(sq, s_len, IDX_DIM),
            re_s.reshape(sq, g_ssm, p_ssm), im_s.reshape(sq, g_ssm, p_ssm))
```

```python
import functools
import math

import jax
import jax.numpy as jnp
from jax import lax
from jax.experimental import pallas as pl
from jax.experimental.pallas import tpu as pltpu

F32 = jnp.float32
BF16 = jnp.bfloat16
I32 = jnp.int32

SSM_GROUP = 16
SSM_STATE = 64
N_HEADS = 8
N_KV_HEADS = 2
HEAD_DIM = 64
ROT_DIM = HEAD_DIM // 4
N_IDX_HEADS = 8
IDX_DIM = 64
IDX_ROT_DIM = IDX_DIM // 4
ROPE_THETA = 500000.0
TOPK_MAX = 256
EPS = 1e-6

LANES = 128
SUBLANES = 8
COUNT_WAYS = 4
VMEM_LIMIT = 56 * 1024 * 1024

NEG_INF = float("-inf")
FLT_MAX = float(jnp.finfo(jnp.float32).max)
KEY_LOW = -2139095040
NT_DIMS = (((1,), (1,)), ((), ()))


def _const_spec(shape):
    nd = len(shape)
    return pl.BlockSpec(shape, lambda *_: (0,) * nd, pipeline_mode=pl.Buffered(1))


def _rms_norm(x, g):
    ms = jnp.mean(x * x, axis=-1, keepdims=True)
    return x * lax.rsqrt(ms + EPS) * g


def _sigmoid(x):
    return 1.0 / (1.0 + jnp.exp(-x))


def _rope(x, cos_t, sin_a, sin_b):
    return (x * cos_t + pltpu.roll(x, LANES - ROT_DIM // 2, axis=1) * sin_a
            + pltpu.roll(x, ROT_DIM // 2, axis=1) * sin_b)


def _in_proj_kernel(x_ref, g_ref, w_ref, cos_ref, sa_ref, sb_ref,
                    u_ref, q_ref, k_ref, kb_ref, v_ref, vb_ref, qi_ref, ki_ref, kib_ref, wi_ref,
                    sgs_ref, sga_ref, *, d_ssm, d_attn, d_kv, d_qi, d_model):
    h = _rms_norm(x_ref[...], g_ref[...]).astype(BF16)
    cos_t, sin_a, sin_b = cos_ref[...], sa_ref[...], sb_ref[...]

    def proj(c0, width):
        return jnp.dot(h, w_ref[:, c0:c0 + width], preferred_element_type=F32)

    off = 0
    u_ref[...] = proj(off, d_ssm)
    off += d_ssm
    for c in range(d_attn // LANES):
        r = _rope(proj(off + c * LANES, LANES), cos_t, sin_a, sin_b)
        q_ref[:, c * LANES:(c + 1) * LANES] = (r * (HEAD_DIM ** -0.5)).astype(BF16)
    off += d_attn
    for c in range(d_kv // LANES):
        r = _rope(proj(off + c * LANES, LANES), cos_t, sin_a, sin_b)
        k_ref[:, c * LANES:(c + 1) * LANES] = r
        kb_ref[:, c * LANES:(c + 1) * LANES] = r.astype(BF16)
    off += d_kv
    vv = proj(off, d_kv)
    v_ref[...] = vv
    vb_ref[...] = vv.astype(BF16)
    off += d_kv
    for c in range(d_qi // LANES):
        r = _rope(proj(off + c * LANES, LANES), cos_t, sin_a, sin_b)
        qi_ref[:, c * LANES:(c + 1) * LANES] = r.astype(BF16)
    off += d_qi
    kw = proj(off, LANES)
    lane = lax.broadcasted_iota(I32, kw.shape, 1)
    kr = jnp.where(lane < IDX_DIM, _rope(kw, cos_t, sin_a, sin_b), kw)
    ki_ref[...] = kr[:, :IDX_DIM]
    kib_ref[...] = kr[:, :IDX_DIM].astype(BF16)
    wi_ref[...] = kw[:, IDX_DIM:IDX_DIM + N_IDX_HEADS] * (N_IDX_HEADS ** -0.5)
    off += LANES
    sgs_ref[...] = _sigmoid(proj(off, d_model))
    off += d_model
    sga_ref[...] = _sigmoid(proj(off, d_model))


def _rope_tables(pos, n_rows):
    half = ROT_DIM // 2
    inv = ROPE_THETA ** (-jnp.arange(half, dtype=F32) / half)
    ang = pos.astype(F32)[:, None] * inv[None, :]
    cos, sin = jnp.cos(ang), jnp.sin(ang)
    ones = jnp.ones((n_rows, HEAD_DIM - ROT_DIM), F32)
    zeros = jnp.zeros((n_rows, HEAD_DIM - ROT_DIM), F32)
    zh = jnp.zeros((n_rows, half), F32)
    cos_t = jnp.concatenate([cos, cos, ones], axis=1)
    sin_a = jnp.concatenate([-sin, zh, zeros], axis=1)
    sin_b = jnp.concatenate([zh, sin, zeros], axis=1)
    rep = LANES // HEAD_DIM
    return tuple(jnp.tile(t, (1, rep)) for t in (cos_t, sin_a, sin_b))


def _in_proj(x2d, pos_tab, tab_blocks, norm_g, w_pack, tm, dims):
    n, d_model = x2d.shape
    d_ssm, d_attn, d_kv, d_qi = dims
    cos_t, sin_a, sin_b = pos_tab
    grid = (n // tm,)
    row = lambda w: pl.BlockSpec((tm, w), lambda i: (i, 0))
    tab = pl.BlockSpec((tm, LANES), lambda i: (i % tab_blocks, 0))
    kern = functools.partial(_in_proj_kernel, d_ssm=d_ssm, d_attn=d_attn, d_kv=d_kv, d_qi=d_qi,
                             d_model=d_model)
    out_shapes = [
        jax.ShapeDtypeStruct((n, d_ssm), F32),
        jax.ShapeDtypeStruct((n, d_attn), BF16),
        jax.ShapeDtypeStruct((n, d_kv), F32),
        jax.ShapeDtypeStruct((n, d_kv), BF16),
        jax.ShapeDtypeStruct((n, d_kv), F32),
        jax.ShapeDtypeStruct((n, d_kv), BF16),
        jax.ShapeDtypeStruct((n, d_qi), BF16),
        jax.ShapeDtypeStruct((n, IDX_DIM), F32),
        jax.ShapeDtypeStruct((n, IDX_DIM), BF16),
        jax.ShapeDtypeStruct((n, N_IDX_HEADS), F32),
        jax.ShapeDtypeStruct((n, d_model), F32),
        jax.ShapeDtypeStruct((n, d_model), F32),
    ]
    out_specs = [row(s.shape[1]) for s in out_shapes]
    return pl.pallas_call(
        kern, grid=grid,
        in_specs=[row(d_model), _const_spec((1, d_model)), _const_spec(w_pack.shape), tab, tab, tab],
        out_specs=out_specs, out_shape=out_shapes,
        compiler_params=pltpu.CompilerParams(dimension_semantics=("arbitrary",),
                                             vmem_limit_bytes=VMEM_LIMIT),
        name="in_proj",
    )(x2d, norm_g.reshape(1, d_model), w_pack, cos_t, sin_a, sin_b)


def _s5_readout(y, u, dsk_ref, wglu_ref, bglu_ref, d_model):
    y = y + dsk_ref[...] * u
    cdf = 0.5 * (1.0 + jnp.tanh(math.sqrt(2.0 / math.pi) * (y + 0.044715 * (y * y * y))))
    gl = (y * cdf).astype(BF16)
    z = jnp.dot(gl, wglu_ref[...], preferred_element_type=F32) + bglu_ref[...]
    return z[:, :d_model] * _sigmoid(z[:, d_model:])


def _s5_prompt_kernel(u_ref, bre_ref, bim_ref, tab_ref, cre_ref, cim_ref, dsk_ref, wglu_ref, bglu_ref,
                      out_ref, sre_ref, sim_ref, hre, him, cr, ci, *, tc, d_model):
    @pl.when(pl.program_id(1) == 0)
    def _():
        cr[...] = jnp.zeros_like(cr)
        ci[...] = jnp.zeros_like(ci)

    u = u_ref[...]
    ub = u.astype(BF16)
    hre[...] = jnp.dot(ub, bre_ref[...], preferred_element_type=F32)
    him[...] = jnp.dot(ub, bim_ref[...], preferred_element_type=F32)

    def group(gi, carry):
        car, cai = carry
        r0 = pl.multiple_of(gi * SUBLANES, SUBLANES)
        xr = hre[pl.ds(r0, SUBLANES), :]
        xi = him[pl.ds(r0, SUBLANES), :]
        for t in range(3):
            ar, ai = tab_ref[2 * t], tab_ref[2 * t + 1]
            sr = pltpu.roll(xr, 1 << t, axis=0)
            si = pltpu.roll(xi, 1 << t, axis=0)
            xr, xi = xr + (ar * sr - ai * si), xi + (ar * si + ai * sr)
        pr, pi = tab_ref[6], tab_ref[7]
        xr, xi = xr + (pr * car - pi * cai), xi + (pr * cai + pi * car)
        hre[pl.ds(r0, SUBLANES), :] = xr
        him[pl.ds(r0, SUBLANES), :] = xi
        return xr[SUBLANES - 1:SUBLANES, :], xi[SUBLANES - 1:SUBLANES, :]

    car, cai = lax.fori_loop(0, tc // SUBLANES, group, (cr[...], ci[...]))
    cr[...] = car
    ci[...] = cai
    sre_ref[...] = car
    sim_ref[...] = cai
    y = (jnp.dot(hre[...].astype(BF16), cre_ref[...], preferred_element_type=F32)
         + jnp.dot(him[...].astype(BF16), cim_ref[...], preferred_element_type=F32))
    out_ref[...] = _s5_readout(y, u, dsk_ref, wglu_ref, bglu_ref, d_model)


def _s5_sample_kernel(u_ref, h0r_ref, h0i_ref, are_ref, aim_ref, bre_ref, bim_ref, cre_ref, cim_ref,
                      dsk_ref, wglu_ref, bglu_ref, out_ref, sre_ref, sim_ref, *, d_model):
    u = u_ref[...]
    ub = u.astype(BF16)
    ar, ai = are_ref[...], aim_ref[...]
    h0r, h0i = h0r_ref[...], h0i_ref[...]
    hr = (ar * h0r - ai * h0i) + jnp.dot(ub, bre_ref[...], preferred_element_type=F32)
    hi = (ar * h0i + ai * h0r) + jnp.dot(ub, bim_ref[...], preferred_element_type=F32)
    sre_ref[...] = hr
    sim_ref[...] = hi
    y = (jnp.dot(hr.astype(BF16), cre_ref[...], preferred_element_type=F32)
         + jnp.dot(hi.astype(BF16), cim_ref[...], preferred_element_type=F32))
    out_ref[...] = _s5_readout(y, u, dsk_ref, wglu_ref, bglu_ref, d_model)


def _s5_params(a_re, a_im, log_dt, b_re, b_im, c_re, c_im):
    g, p = a_re.shape
    a_c = lax.complex(a_re.astype(F32), a_im.astype(F32))
    dt = jnp.exp(log_dt.astype(F32))[:, None]
    a_bar = jnp.exp(a_c * dt)
    b_bar = ((a_bar - 1.0) / a_c)[:, :, None] * lax.complex(b_re.astype(F32), b_im.astype(F32))
    eye = jnp.eye(g, dtype=F32)
    n = b_re.shape[2]

    def bmat(b):
        return jnp.einsum('gpn,gh->gnhp', b, eye).reshape(g * n, g * p)

    def cmat(c):
        return jnp.einsum('gnp,gh->gphn', c, eye).reshape(g * p, g * n)

    bre, bim = bmat(jnp.real(b_bar)).astype(BF16), bmat(jnp.imag(b_bar)).astype(BF16)
    cre, cim = cmat(c_re.astype(F32)).astype(BF16), cmat(-c_im.astype(F32)).astype(BF16)
    a1 = a_bar.reshape(1, g * p)
    a2 = a1 * a1
    a3 = a2 * a1
    a4 = a2 * a2
    pw = [a1, a2, a3, a4, a4 * a1, a4 * a2, a4 * a3, a4 * a4]
    rows = jnp.arange(SUBLANES)[:, None]
    tabs = []
    for sh in (1, 2, 4):
        t = jnp.where(rows >= sh, jnp.broadcast_to(pw[sh - 1], (SUBLANES, g * p)), 0.0)
        tabs += [jnp.real(t), jnp.imag(t)]
    pcat = jnp.concatenate(pw, axis=0)
    tabs += [jnp.real(pcat), jnp.imag(pcat)]
    tab = jnp.stack(tabs, axis=0).astype(F32)
    return jnp.real(a1), jnp.imag(a1), bre, bim, cre, cim, tab


def _s5_prompt(u3, s5p, dsk, wglu, bglu, tc):
    bsz, t_len, d_ssm = u3.shape
    _, _, bre, bim, cre, cim, tab = s5p
    sd = bre.shape[1]
    d_model = wglu.shape[1] // 2
    kern = functools.partial(_s5_prompt_kernel, tc=tc, d_model=d_model)
    return pl.pallas_call(
        kern, grid=(bsz, t_len // tc),
        in_specs=[pl.BlockSpec((None, tc, d_ssm), lambda b, c: (b, c, 0)),
                  _const_spec(bre.shape), _const_spec(bim.shape), _const_spec(tab.shape),
                  _const_spec(cre.shape), _const_spec(cim.shape), _const_spec((1, d_ssm)),
                  _const_spec(wglu.shape), _const_spec((1, 2 * d_model))],
        out_specs=[pl.BlockSpec((None, tc, d_model), lambda b, c: (b, c, 0)),
                   pl.BlockSpec((None, 1, sd), lambda b, c: (b, 0, 0)),
                   pl.BlockSpec((None, 1, sd), lambda b, c: (b, 0, 0))],
        out_shape=[jax.ShapeDtypeStruct((bsz, t_len, d_model), F32),
                   jax.ShapeDtypeStruct((bsz, 1, sd), F32),
                   jax.ShapeDtypeStruct((bsz, 1, sd), F32)],
        scratch_shapes=[pltpu.VMEM((tc, sd), F32), pltpu.VMEM((tc, sd), F32),
                        pltpu.VMEM((1, sd), F32), pltpu.VMEM((1, sd), F32)],
        compiler_params=pltpu.CompilerParams(dimension_semantics=("arbitrary", "arbitrary"),
                                             vmem_limit_bytes=VMEM_LIMIT),
        name="s5_prompt",
    )(u3, bre, bim, tab, cre, cim, dsk.reshape(1, d_ssm), wglu, bglu.reshape(1, 2 * d_model))


def _s5_sample(u2, h0r, h0i, s5p, dsk, wglu, bglu):
    n, d_ssm = u2.shape
    are, aim, bre, bim, cre, cim, _ = s5p
    sd = bre.shape[1]
    d_model = wglu.shape[1] // 2
    kern = functools.partial(_s5_sample_kernel, d_model=d_model)
    return pl.pallas_call(
        kern,
        out_shape=[jax.ShapeDtypeStruct((n, d_model), F32),
                   jax.ShapeDtypeStruct((n, sd), F32),
                   jax.ShapeDtypeStruct((n, sd), F32)],
        compiler_params=pltpu.CompilerParams(vmem_limit_bytes=VMEM_LIMIT),
        name="s5_sample",
    )(u2, h0r, h0i, are, aim, bre, bim, cre, cim, dsk.reshape(1, d_ssm), wglu,
      bglu.reshape(1, 2 * d_model))


def _f2k(x):
    b = lax.bitcast_convert_type(x, I32)
    return jnp.where(b < 0, b ^ 0x7FFFFFFF, b)


def _k2f(k):
    return lax.bitcast_convert_type(jnp.where(k < 0, k ^ 0x7FFFFFFF, k), F32)


def _threshold_search(count_ge, amax, forced, k_top):
    kf = float(k_top)
    lo0 = jnp.where(forced, KEY_LOW, _f2k(-amax))
    hi0 = jnp.where(forced, KEY_LOW + 1, _f2k(amax) + 1)

    def cond(st):
        lo, hi, _ = st
        return jnp.max(jnp.where(hi > lo + 1, 1.0, 0.0)) > 0.5

    def body(st):
        lo, hi, it = st
        active = hi > lo + 1
        mid_i = (lo >> 1) + (hi >> 1) + (lo & hi & 1)
        mk = _f2k(0.5 * _k2f(lo) + 0.5 * _k2f(hi))
        use_f = jnp.logical_and(it % 2 == 0, jnp.logical_and(mk > lo, mk < hi))
        mid = jnp.where(use_f, mk, mid_i)
        c = count_ge(_k2f(mid))
        ge = c >= kf
        nlo = jnp.where(ge, mid, lo)
        nhi = jnp.where(c == kf, mid + 1, jnp.where(ge, hi, mid))
        return jnp.where(active, nlo, lo), jnp.where(active, nhi, hi), it + 1

    lo, _, _ = lax.while_loop(cond, body, (lo0, hi0, jnp.int32(0)))
    return _k2f(lo)


def _tri_exclusive(n):
    r = lax.broadcasted_iota(I32, (n, n), 0)
    c = lax.broadcasted_iota(I32, (n, n), 1)
    return (r < c).astype(BF16)


def _dsa_prompt_kernel(qi_ref, wt_ref, q_ref, ki_ref, k_ref, vt_ref, o_ref,
                       sc_ref, qis_ref, qs_ref, m_ref, acc_ref, *, tq, tk, k_top):
    i = pl.program_id(1)
    nkb = i + 1
    hpg = N_HEADS // N_KV_HEADS
    ngr = tk // SUBLANES
    kf = float(k_top)

    qi = qi_ref[...]
    q = q_ref[...]
    zpad = jnp.zeros((tq, HEAD_DIM), BF16)
    for h in range(N_IDX_HEADS):
        qis_ref[h * tq:(h + 1) * tq, :] = qi[:, h * IDX_DIM:(h + 1) * IDX_DIM]
    for h in range(N_HEADS):
        qh = q[:, h * HEAD_DIM:(h + 1) * HEAD_DIM]
        pair = [qh, zpad] if h // hpg == 0 else [zpad, qh]
        qs_ref[h * tq:(h + 1) * tq, :] = jnp.concatenate(pair, axis=1)

    w8 = wt_ref[...] * (IDX_DIM ** -0.5)
    qpos = i * tq + lax.broadcasted_iota(I32, (SUBLANES, tq), 1)

    def score_blk(j, amx, diagonal):
        k0 = pl.multiple_of(j * tk, tk)
        d = lax.dot_general(ki_ref[pl.ds(k0, tk), :], qis_ref[...], NT_DIMS, preferred_element_type=F32)
        mags = []
        for c in range(tq // LANES):
            acc = jnp.zeros((tk, LANES), F32)
            for h in range(N_IDX_HEADS):
                cols = slice(h * tq + c * LANES, h * tq + (c + 1) * LANES)
                acc = acc + jnp.maximum(d[:, cols], 0.0) * w8[h:h + 1, c * LANES:(c + 1) * LANES]
            mag = jnp.abs(acc)
            if diagonal:
                kpos = j * tk + lax.broadcasted_iota(I32, (tk, LANES), 0)
                causal = kpos <= i * tq + c * LANES + lax.broadcasted_iota(I32, (tk, LANES), 1)
                acc = jnp.where(causal, acc, NEG_INF)
                mag = jnp.where(causal, mag, 0.0)
            sc_ref[j, :, c * LANES:(c + 1) * LANES] = acc
            mags.append(jnp.max(mag.reshape(ngr, SUBLANES, LANES), axis=0))
        return jnp.maximum(amx, jnp.concatenate(mags, axis=1))

    amx = lax.fori_loop(0, nkb - 1, lambda j, a: score_blk(j, a, False), jnp.zeros((SUBLANES, tq), F32))
    amx = score_blk(nkb - 1, amx, True)

    rep = lambda x: jnp.broadcast_to(x, (SUBLANES, tq))
    amax = rep(jnp.max(amx, axis=0, keepdims=True))
    forced = (qpos + 1) <= k_top
    lo0 = jnp.where(forced, KEY_LOW, _f2k(-amax))
    hi0 = jnp.where(forced, KEY_LOW + 1, _f2k(amax) + 1)

    def count(thr, strict):
        def body(j, part):
            blk = sc_ref[j].reshape(ngr // COUNT_WAYS, COUNT_WAYS, SUBLANES, tq)
            hit = (blk > thr[None, None]) if strict else (blk >= thr[None, None])
            return part + jnp.sum(jnp.where(hit, 1.0, 0.0), axis=0)

        part = lax.fori_loop(0, nkb, body, jnp.zeros((COUNT_WAYS, SUBLANES, tq), F32))
        return rep(jnp.sum(jnp.sum(part, axis=0), axis=0, keepdims=True))

    def search_pass(lo, hi, value_space):
        active = hi > lo + 1
        span = hi - lo
        mid = lo + lax.shift_right_logical(span, jnp.ones_like(span))
        if value_space:
            mk = _f2k(0.5 * _k2f(lo) + 0.5 * _k2f(hi))
            mid = jnp.where(jnp.logical_and(mk > lo, mk < hi), mk, mid)
        c = count(_k2f(mid), False)
        ge = c >= kf
        nlo = jnp.where(ge, mid, lo)
        nhi = jnp.where(c == kf, mid + 1, jnp.where(ge, hi, mid))
        return jnp.where(active, nlo, lo), jnp.where(active, nhi, hi)

    def search_body(st):
        lo, hi, _ = st
        lo, hi = search_pass(lo, hi, True)
        lo, hi = search_pass(lo, hi, False)
        return lo, hi, (jnp.max(jnp.where(hi > lo + 1, 1.0, 0.0)) > 0.5).astype(I32)

    lo, _, _ = lax.while_loop(lambda st: st[2] > 0, search_body, (lo0, hi0, jnp.int32(1)))
    thr = _k2f(lo)
    c_ge = count(thr, False)
    thr1 = thr[0:1, :]

    @pl.when(jnp.max(jnp.where(c_ge > kf, 1.0, 0.0)) > 0.5)
    def _():
        need = jnp.where(c_ge > kf, kf - count(thr, True), 1e30)[0:1, :]
        r = lax.broadcasted_iota(I32, (tk, tk), 0)
        c = lax.broadcasted_iota(I32, (tk, tk), 1)
        earlier = (c < r).astype(BF16)
        ones_k = jnp.ones((SUBLANES, tk), BF16)

        def fix(j, run):
            sc = sc_ref[j]
            eq = sc == thr1
            eqb = jnp.where(eq, 1.0, 0.0).astype(BF16)
            before = jnp.dot(earlier, eqb, preferred_element_type=F32) + run[0:1, :]
            sc_ref[j] = jnp.where(jnp.logical_and(eq, before >= need), NEG_INF, sc)
            return run + jnp.dot(ones_k, eqb, preferred_element_type=F32)

        lax.fori_loop(0, nkb, fix, jnp.zeros((SUBLANES, tq), F32))

    m_ref[...] = jnp.full(m_ref.shape, NEG_INF, F32)
    acc_ref[...] = jnp.zeros(acc_ref.shape, F32)

    def att_blk(j, carry):
        k0 = pl.multiple_of(j * tk, tk)
        s = lax.dot_general(k_ref[pl.ds(k0, tk), :], qs_ref[...], NT_DIMS, preferred_element_type=F32)
        bias = jnp.where(sc_ref[j] >= thr1, 0.0, NEG_INF)
        for h in range(N_HEADS):
            x = s[:, h * tq:(h + 1) * tq] + bias
            bm = jnp.max(jnp.max(x.reshape(ngr, SUBLANES, tq), axis=0), axis=0, keepdims=True)
            m_old = m_ref[h:h + 1, :]
            m_new = jnp.maximum(m_old, bm)
            m_safe = jnp.where(m_new == NEG_INF, 0.0, m_new)
            p = jnp.exp(x - m_safe).astype(BF16)
            pv = jnp.dot(vt_ref[h // hpg, j], p, preferred_element_type=F32)
            acc_ref[h] = acc_ref[h] * jnp.exp(m_old - m_safe) + pv
            m_ref[h:h + 1, :] = m_new
        return carry

    lax.fori_loop(0, nkb, att_blk, 0)

    for pr in range(N_HEADS // 2):
        outs = []
        for h in (2 * pr, 2 * pr + 1):
            a = acc_ref[h]
            outs.append(a[:HEAD_DIM, :] / a[HEAD_DIM:HEAD_DIM + 1, :])
        o2 = jnp.concatenate(outs, axis=0)
        o_ref[:, 2 * pr * HEAD_DIM:(2 * pr + 2) * HEAD_DIM] = o2.T.astype(o_ref.dtype)


def _dsa_prompt(qi, wi, q, kib, kb, vb, tq, tk):
    bsz, t_len, d_qi = qi.shape
    d_attn = q.shape[2]
    d_kv = kb.shape[2]
    nkb = t_len // tk
    k_top = min(TOPK_MAX, t_len // 4)
    assert tq == tk and tq % LANES == 0 and t_len % tq == 0
    wt = jnp.swapaxes(wi, 1, 2)
    v5 = jnp.transpose(vb.reshape(bsz, nkb, tk, N_KV_HEADS, HEAD_DIM), (0, 3, 1, 4, 2))
    vt = jnp.concatenate([v5, jnp.ones((bsz, N_KV_HEADS, nkb, 1, tk), BF16),
                          jnp.zeros((bsz, N_KV_HEADS, nkb, LANES - HEAD_DIM - 1, tk), BF16)], axis=3)
    kern = functools.partial(_dsa_prompt_kernel, tq=tq, tk=tk, k_top=k_top)
    tile = lambda w: pl.BlockSpec((None, tq, w), lambda b, i: (b, i, 0))
    full = lambda w: pl.BlockSpec((None, t_len, w), lambda b, i: (b, 0, 0))
    return pl.pallas_call(
        kern, grid=(bsz, t_len // tq),
        in_specs=[tile(d_qi), pl.BlockSpec((None, N_IDX_HEADS, tq), lambda b, i: (b, 0, i)), tile(d_attn),
                  full(IDX_DIM), full(d_kv),
                  pl.BlockSpec((None, N_KV_HEADS, nkb, LANES, tk), lambda b, i: (b, 0, 0, 0, 0))],
        out_specs=tile(d_attn),
        out_shape=jax.ShapeDtypeStruct((bsz, t_len, d_attn), BF16),
        scratch_shapes=[pltpu.VMEM((nkb, tk, tq), F32),
                        pltpu.VMEM((N_IDX_HEADS * tq, IDX_DIM), BF16),
                        pltpu.VMEM((N_HEADS * tq, N_KV_HEADS * HEAD_DIM), BF16),
                        pltpu.VMEM((N_HEADS, tq), F32),
                        pltpu.VMEM((N_HEADS, LANES, tq), F32)],
        compiler_params=pltpu.CompilerParams(dimension_semantics=("arbitrary", "arbitrary"),
                                             vmem_limit_bytes=VMEM_LIMIT),
        name="dsa_prompt",
    )(qi, wt, q, kib, kb, vt)


def _idx_score_kernel(pt_ref, q_ref, w_ref, *refs, n_pg):
    pages, o_ref = refs[:n_pg], refs[n_pg]
    b = pl.program_id(1)
    q = q_ref[...]
    w = w_ref[...] * (IDX_DIM ** -0.5)
    parts = []
    for p in range(n_pg):
        d = lax.dot_general(q, pages[p][...].astype(BF16), NT_DIMS, preferred_element_type=F32)
        parts.append(jnp.sum(jnp.maximum(d, 0.0) * w, axis=0, keepdims=True))
    o_ref[pl.ds(b % SUBLANES, 1), :] = jnp.concatenate(parts, axis=1)


def _idx_scores(page_table, qis, wcol, cache_idx_k, n_pg):
    sq, n_pages = page_table.shape
    page = cache_idx_k.shape[1]
    kern = functools.partial(_idx_score_kernel, n_pg=n_pg)
    page_specs = [pl.BlockSpec((None, page, IDX_DIM),
                               lambda j, b, pt, p=p: (pt[b * n_pages + j * n_pg + p], 0, 0))
                  for p in range(n_pg)]
    grid_spec = pltpu.PrefetchScalarGridSpec(
        num_scalar_prefetch=1, grid=(n_pages // n_pg, sq),
        in_specs=[pl.BlockSpec((None, N_IDX_HEADS, IDX_DIM), lambda j, b, pt: (b, 0, 0)),
                  pl.BlockSpec((None, N_IDX_HEADS, 1), lambda j, b, pt: (b, 0, 0))] + page_specs,
        out_specs=pl.BlockSpec((SUBLANES, n_pg * page), lambda j, b, pt: (b // SUBLANES, j)))
    return pl.pallas_call(
        kern, grid_spec=grid_spec,
        out_shape=jax.ShapeDtypeStruct((sq, n_pages * page), F32),
        compiler_params=pltpu.CompilerParams(dimension_semantics=("arbitrary", "arbitrary")),
        name="sample_idx_scores",
    )(page_table.reshape(-1), qis, wcol, *([cache_idx_k] * n_pg))


def _sample_select_kernel(sc_ref, qi_ref, ki_ref, wi_ref, mask_ref, selfsel_ref, *, ch, k_top):
    sq, l_past = sc_ref.shape
    nch = l_past // ch
    w = wi_ref[...] * (IDX_DIM ** -0.5)
    qf = qi_ref[...].astype(F32)
    kf32 = ki_ref[...].astype(F32)
    s_self = jnp.zeros((sq, 1), F32)
    for h in range(N_IDX_HEADS):
        dh = jnp.sum(qf[:, h * IDX_DIM:(h + 1) * IDX_DIM] * kf32, axis=1, keepdims=True)
        s_self = s_self + jnp.maximum(dh, 0.0) * w[:, h:h + 1]

    amax = jnp.abs(s_self)
    for c in range(nch):
        amax = jnp.maximum(amax, jnp.max(jnp.abs(sc_ref[:, c * ch:(c + 1) * ch]), axis=1, keepdims=True))

    def count_cmp(thr, strict):
        thr_b = jnp.broadcast_to(thr, (sq, LANES))
        part = jnp.zeros((sq, LANES), F32)
        for c in range(l_past // LANES):
            blk = sc_ref[:, c * LANES:(c + 1) * LANES]
            hit = (blk > thr_b) if strict else (blk >= thr_b)
            part = part + jnp.where(hit, 1.0, 0.0)
        self_hit = (s_self > thr) if strict else (s_self >= thr)
        return jnp.sum(part, axis=1, keepdims=True) + jnp.where(self_hit, 1.0, 0.0)

    kf = float(k_top)
    forced = amax < 0.0
    thr = _threshold_search(lambda t: count_cmp(t, False), amax, forced, k_top)
    need = kf - count_cmp(thr, True)
    tri = _tri_exclusive(ch)
    run = jnp.zeros((sq, 1), F32)
    for c in range(nch):
        sc = sc_ref[:, c * ch:(c + 1) * ch]
        eq = sc == thr
        before = jnp.dot(jnp.where(eq, 1.0, 0.0).astype(BF16), tri, preferred_element_type=F32) + run
        keep = jnp.logical_or(sc > thr, jnp.logical_and(eq, before < need))
        mask_ref[:, c * ch:(c + 1) * ch] = jnp.where(keep, 1.0, 0.0)
        run = run + jnp.sum(jnp.where(eq, 1.0, 0.0), axis=1, keepdims=True)
    self_keep = jnp.logical_or(s_self > thr, jnp.logical_and(s_self == thr, run < need))
    selfsel_ref[...] = jnp.where(self_keep, 1.0, 0.0)


def _sample_select(scores, qi, kib, wi, ch, k_top):
    sq, l_past = scores.shape
    kern = functools.partial(_sample_select_kernel, ch=ch, k_top=k_top)
    return pl.pallas_call(
        kern,
        out_shape=[jax.ShapeDtypeStruct((sq, l_past), F32), jax.ShapeDtypeStruct((sq, 1), F32)],
        compiler_params=pltpu.CompilerParams(vmem_limit_bytes=VMEM_LIMIT),
        name="sample_select",
    )(scores, qi, kib, wi)


def _sample_attn_kernel(pt_ref, q_ref, mask_ref, ks_ref, vs_ref, ss_ref, *refs, n_pg):
    kpages, vpages = refs[:n_pg], refs[n_pg:2 * n_pg]
    o_ref, m_scr, l_scr, acc_scr = refs[2 * n_pg:]
    j = pl.program_id(1)
    hpg = N_HEADS // N_KV_HEADS

    @pl.when(j == 0)
    def _():
        m_scr[...] = jnp.full(m_scr.shape, NEG_INF, F32)
        l_scr[...] = jnp.zeros(l_scr.shape, F32)
        acc_scr[...] = jnp.zeros(acc_scr.shape, F32)

    q = q_ref[...]
    page = kpages[0].shape[0]
    s = jnp.concatenate(
        [lax.dot_general(q, kpages[p][...].astype(BF16), NT_DIMS, preferred_element_type=F32)
         for p in range(n_pg)], axis=1)
    sm = jnp.where(mask_ref[...] > 0.5, s, NEG_INF)
    m_old = m_scr[...]
    m_new = jnp.maximum(m_old, jnp.max(sm, axis=1, keepdims=True))
    m_safe = jnp.where(m_new == NEG_INF, 0.0, m_new)
    p_ = jnp.exp(sm - m_safe)
    alpha = jnp.exp(m_old - m_safe)
    pb = p_.astype(BF16)
    pv = jnp.zeros(acc_scr.shape, F32)
    for p in range(n_pg):
        pv = pv + jnp.dot(pb[:, p * page:(p + 1) * page], vpages[p][...].astype(BF16),
                          preferred_element_type=F32)
    l_scr[...] = l_scr[...] * alpha + jnp.sum(pb.astype(F32), axis=1, keepdims=True)
    acc_scr[...] = acc_scr[...] * alpha + pv
    m_scr[...] = m_new

    @pl.when(j == pl.num_programs(1) - 1)
    def _():
        s_self = jnp.sum(q.astype(F32) * ks_ref[...].astype(F32), axis=1, keepdims=True)
        s_self = jnp.where(ss_ref[...] > 0.5, s_self, NEG_INF)
        m_o = m_scr[...]
        m_n = jnp.maximum(m_o, s_self)
        m_s = jnp.where(m_n == NEG_INF, 0.0, m_n)
        p_self = jnp.exp(s_self - m_s).astype(BF16).astype(F32)
        al = jnp.exp(m_o - m_s)
        l_fin = l_scr[...] * al + p_self
        acc = acc_scr[...] * al + p_self * vs_ref[...].astype(F32)
        o = acc / l_fin
        hrow = lax.broadcasted_iota(I32, o.shape, 0)
        o = jnp.where(hrow < hpg, o, pltpu.roll(o, HEAD_DIM, axis=1))
        o_ref[...] = o[:, :HEAD_DIM].astype(o_ref.dtype)


def _sample_attn(page_table, qs, mask3, kself, vself, selfsel, cache_k2, cache_v2, n_pg):
    sq, n_pages = page_table.shape
    page, d_kv = cache_k2.shape[1], cache_k2.shape[2]
    kern = functools.partial(_sample_attn_kernel, n_pg=n_pg)
    pspec = lambda p: pl.BlockSpec((None, page, d_kv),
                                   lambda b, j, pt: (pt[b * n_pages + j * n_pg + p], 0, 0))
    per_seq = lambda r, w: pl.BlockSpec((None, r, w), lambda b, j, pt: (b, 0, 0))
    grid_spec = pltpu.PrefetchScalarGridSpec(
        num_scalar_prefetch=1, grid=(sq, n_pages // n_pg),
        in_specs=[per_seq(N_HEADS, d_kv),
                  pl.BlockSpec((None, 1, n_pg * page), lambda b, j, pt: (b, 0, j)),
                  per_seq(1, d_kv), per_seq(1, d_kv), per_seq(1, 1)]
                 + [pspec(p) for p in range(n_pg)] + [pspec(p) for p in range(n_pg)],
        out_specs=per_seq(N_HEADS, HEAD_DIM),
        scratch_shapes=[pltpu.VMEM((N_HEADS, 1), F32), pltpu.VMEM((N_HEADS, 1), F32),
                        pltpu.VMEM((N_HEADS, d_kv), F32)])
    return pl.pallas_call(
        kern, grid_spec=grid_spec,
        out_shape=jax.ShapeDtypeStruct((sq, N_HEADS, HEAD_DIM), BF16),
        compiler_params=pltpu.CompilerParams(dimension_semantics=("arbitrary", "arbitrary")),
        name="sample_attn",
    )(page_table.reshape(-1), qs, mask3, kself, vself, selfsel,
      *([cache_k2] * n_pg), *([cache_v2] * n_pg))


def _tail_kernel(x_ref, attn_ref, ssm_ref, sgs_ref, sga_ref, wao_ref, wo_ref, g2_ref, wup_ref, wdn_ref,
                 gf_ref, y_ref):
    attn_out = jnp.dot(attn_ref[...], wao_ref[...], preferred_element_type=F32)
    mix = sgs_ref[...] * ssm_ref[...] + sga_ref[...] * attn_out
    x1 = x_ref[...] + jnp.dot(mix.astype(BF16), wo_ref[...], preferred_element_type=F32)
    hh = _rms_norm(x1, g2_ref[...]).astype(BF16)
    up = jnp.dot(hh, wup_ref[...], preferred_element_type=F32)
    r = jnp.square(jnp.maximum(up, 0.0)).astype(BF16)
    x2 = x1 + jnp.dot(r, wdn_ref[...], preferred_element_type=F32)
    y_ref[...] = _rms_norm(x2, gf_ref[...])


def _tail(x2d, attn, ssm, sgs, sga, wao, wo, g2, wup, wdn, gf, tm):
    n, d_model = x2d.shape
    row = lambda w: pl.BlockSpec((tm, w), lambda i: (i, 0))
    return pl.pallas_call(
        _tail_kernel, grid=(n // tm,),
        in_specs=[row(d_model), row(attn.shape[1]), row(d_model), row(d_model), row(d_model),
                  _const_spec(wao.shape), _const_spec(wo.shape), _const_spec((1, d_model)),
                  _const_spec(wup.shape), _const_spec(wdn.shape), _const_spec((1, d_model))],
        out_specs=row(d_model),
        out_shape=jax.ShapeDtypeStruct((n, d_model), F32),
        compiler_params=pltpu.CompilerParams(dimension_semantics=("arbitrary",),
                                             vmem_limit_bytes=VMEM_LIMIT),
        name="tail",
    )(x2d, attn, ssm, sgs, sga, wao, wo, g2.reshape(1, d_model), wup, wdn, gf.reshape(1, d_model))


def _tiles(n_rows, t_len):
    tm = min(256, n_rows)
    tc = min(256, t_len)
    tq = min(256, t_len)
    return tm, tc, tq, tq


def kernel(x_prompt, x_sample, cache_k, cache_v, cache_idx_k, state_ssm_re, state_ssm_im, page_table,
           norm1_g, w_in, ssm_a_re, ssm_a_im, ssm_log_dt, ssm_b_re, ssm_b_im, ssm_c_re, ssm_c_im,
           ssm_d, w_glu, b_glu, w_attn_out, w_o, norm2_g, w_up, w_down, normf_g):
    bsz, t_len, d_model = x_prompt.shape
    sq, s_len, _ = x_sample.shape
    assert s_len == 1, "the sample path handles one new token per sequence"
    n_pool, page = cache_idx_k.shape[0], cache_idx_k.shape[1]
    n_pages = page_table.shape[1]
    past = n_pages * page
    d_ssm = ssm_d.shape[0]
    d_attn = N_HEADS * HEAD_DIM
    d_kv = N_KV_HEADS * HEAD_DIM
    d_qi = N_IDX_HEADS * IDX_DIM
    dims = (d_ssm, d_attn, d_kv, d_qi)

    c_ki = d_ssm + d_attn + 2 * d_kv + d_qi
    c_g = c_ki + IDX_DIM + N_IDX_HEADS
    w_pack = jnp.concatenate(
        [w_in[:, :c_g], jnp.zeros((d_model, LANES - IDX_DIM - N_IDX_HEADS), w_in.dtype), w_in[:, c_g:]],
        axis=1).astype(BF16)
    wglu_b, wao_b, wo_b = w_glu.astype(BF16), w_attn_out.astype(BF16), w_o.astype(BF16)
    wup_b, wdn_b = w_up.astype(BF16), w_down.astype(BF16)
    s5p = _s5_params(ssm_a_re, ssm_a_im, ssm_log_dt, ssm_b_re, ssm_b_im, ssm_c_re, ssm_c_im)

    n_p = bsz * t_len
    tm, tc, tq, tk = _tiles(n_p, t_len)
    xp = x_prompt.reshape(n_p, d_model)
    tabs_p = _rope_tables(jnp.arange(t_len, dtype=I32), t_len)
    (u, q, k, kb, v, vb, qi, ki, kib, wi, sgs, sga) = _in_proj(xp, tabs_p, t_len // tm, norm1_g, w_pack, tm, dims)
    ssm_out, re_p, im_p = _s5_prompt(u.reshape(bsz, t_len, d_ssm), s5p, ssm_d, wglu_b, b_glu, tc)
    r3 = lambda a: a.reshape(bsz, t_len, a.shape[-1])
    attn = _dsa_prompt(r3(qi), r3(wi), r3(q), r3(kib), r3(kb), r3(vb), tq, tk)
    y_p = _tail(xp, attn.reshape(n_p, d_attn), ssm_out.reshape(n_p, d_model), sgs, sga,
                wao_b, wo_b, norm2_g, wup_b, wdn_b, normf_g, tm)

    xs = x_sample.reshape(sq, d_model)
    tabs_s = _rope_tables(jnp.full((sq,), past, I32), sq)
    (u_s, q_s, k_s, kb_s, v_s, vb_s, qi_s, ki_s, kib_s, wi_s, sgs_s, sga_s) = _in_proj(
        xs, tabs_s, 1, norm1_g, w_pack, sq, dims)
    sd = state_ssm_re.shape[1] * state_ssm_re.shape[2]
    ssm_s, re_s, im_s = _s5_sample(u_s, state_ssm_re.reshape(sq, sd), state_ssm_im.reshape(sq, sd),
                                   s5p, ssm_d, wglu_b, b_glu)
    n_pg = math.gcd(n_pages, 16)
    scores = _idx_scores(page_table, qi_s.reshape(sq, N_IDX_HEADS, IDX_DIM),
                         wi_s.reshape(sq, N_IDX_HEADS, 1), cache_idx_k, n_pg)
    k_top_s = min(TOPK_MAX, (past + s_len) // 4)
    mask, selfsel = _sample_select(scores, qi_s, kib_s, wi_s, min(512, past), k_top_s)
    q4 = q_s.reshape(sq, N_KV_HEADS, N_HEADS // N_KV_HEADS, HEAD_DIM)
    qs_pad = (q4[:, :, :, None, :] * jnp.eye(N_KV_HEADS, dtype=BF16)[None, :, None, :, None]
              ).reshape(sq, N_HEADS, d_kv)
    attn_s = _sample_attn(page_table, qs_pad, mask.reshape(sq, 1, past),
                          kb_s.reshape(sq, 1, d_kv), vb_s.reshape(sq, 1, d_kv), selfsel.reshape(sq, 1, 1),
                          cache_k.reshape(n_pool, page, d_kv), cache_v.reshape(n_pool, page, d_kv), n_pg)
    y_s = _tail(xs, attn_s.reshape(sq, d_attn), ssm_s, sgs_s, sga_s,
                wao_b, wo_b, norm2_g, wup_b, wdn_b, normf_g, sq)

    g_ssm, p_ssm = state_ssm_re.shape[1], state_ssm_re.shape[2]
    return (y_p.reshape(bsz, t_len, d_model), y_s.reshape(sq, s_len, d_model),
            k.reshape(bsz, t_len, N_KV_HEADS, HEAD_DIM), v.reshape(bsz, t_len, N_KV_HEADS, HEAD_DIM),
            ki.reshape(bsz, t_len, IDX_DIM),
            re_p.reshape(bsz, g_ssm, p_ssm), im_p.reshape(bsz, g_ssm, p_ssm),
            k_s.reshape(sq, s_len, N_KV_HEADS, HEAD_DIM), v_s.reshape(sq, s_len, N_KV_HEADS, HEAD_DIM),
            ki_s.reshape(sq, s_len, IDX_DIM),
            re_s.reshape(sq, g_ssm, p_ssm), im_s.reshape(sq, g_ssm, p_ssm))
```

```python
import functools
import math

import jax
import jax.numpy as jnp
from jax import lax
from jax.experimental import pallas as pl
from jax.experimental.pallas import tpu as pltpu

F32 = jnp.float32
BF16 = jnp.bfloat16
I32 = jnp.int32

SSM_GROUP = 16
SSM_STATE = 64
N_HEADS = 8
N_KV_HEADS = 2
HEAD_DIM = 64
ROT_DIM = HEAD_DIM // 4
N_IDX_HEADS = 8
IDX_DIM = 64
IDX_ROT_DIM = IDX_DIM // 4
ROPE_THETA = 500000.0
TOPK_MAX = 256
EPS = 1e-6

LANES = 128
SUBLANES = 8
COUNT_WAYS = 4
VT_ROWS = 80
VMEM_LIMIT = 56 * 1024 * 1024

NEG_INF = float("-inf")
FLT_MAX = float(jnp.finfo(jnp.float32).max)
KEY_LOW = -2139095040
NT_DIMS = (((1,), (1,)), ((), ()))


def _const_spec(shape):
    nd = len(shape)
    return pl.BlockSpec(shape, lambda *_: (0,) * nd, pipeline_mode=pl.Buffered(1))


def _rms_norm(x, g):
    ms = jnp.mean(x * x, axis=-1, keepdims=True)
    return x * lax.rsqrt(ms + EPS) * g


def _sigmoid(x):
    return 1.0 / (1.0 + jnp.exp(-x))


def _rope(x, cos_t, sin_a, sin_b):
    return (x * cos_t + pltpu.roll(x, LANES - ROT_DIM // 2, axis=1) * sin_a
            + pltpu.roll(x, ROT_DIM // 2, axis=1) * sin_b)


def _in_proj_kernel(x_ref, g_ref, w_ref, cos_ref, sa_ref, sb_ref,
                    u_ref, q_ref, kt_ref, kb_ref, vt_ref, vb_ref, vtb_ref, qi_ref, kit_ref, kib_ref,
                    wi_ref, wit_ref, sgs_ref, sga_ref, *, d_ssm, d_attn, d_kv, d_qi, d_model):
    assert d_kv == LANES
    tm = x_ref.shape[0]
    h = _rms_norm(x_ref[...], g_ref[...]).astype(BF16)
    cos_t, sin_a, sin_b = cos_ref[...], sa_ref[...], sb_ref[...]

    def proj(c0, width):
        return jnp.dot(h, w_ref[:, c0:c0 + width], preferred_element_type=F32)

    off = 0
    u_ref[...] = proj(off, d_ssm)
    off += d_ssm
    for c in range(d_attn // LANES):
        r = _rope(proj(off + c * LANES, LANES), cos_t, sin_a, sin_b)
        q_ref[:, c * LANES:(c + 1) * LANES] = (r * (HEAD_DIM ** -0.5)).astype(BF16)
    off += d_attn
    r = _rope(proj(off, d_kv), cos_t, sin_a, sin_b)
    kt_ref[...] = r.T
    kb_ref[...] = r.astype(BF16)
    off += d_kv
    vv = proj(off, d_kv)
    vt = vv.T
    vt_ref[...] = vt
    vb_ref[...] = vv.astype(BF16)
    sub = lax.broadcasted_iota(I32, (VT_ROWS - HEAD_DIM, tm), 0)
    ones_pad = jnp.where(sub == 0, 1.0, 0.0)
    for g in range(N_KV_HEADS):
        vtb_ref[g] = jnp.concatenate([vt[g * HEAD_DIM:(g + 1) * HEAD_DIM, :], ones_pad], axis=0).astype(BF16)
    off += d_kv
    for c in range(d_qi // LANES):
        r = _rope(proj(off + c * LANES, LANES), cos_t, sin_a, sin_b)
        qi_ref[:, c * LANES:(c + 1) * LANES] = r.astype(BF16)
    off += d_qi
    kw = proj(off, LANES)
    lane = lax.broadcasted_iota(I32, kw.shape, 1)
    kr = jnp.where(lane < IDX_DIM, _rope(kw, cos_t, sin_a, sin_b), kw * (N_IDX_HEADS ** -0.5))
    krt = kr.T
    kit_ref[...] = krt[:IDX_DIM, :]
    kib_ref[...] = kr[:, :IDX_DIM].astype(BF16)
    wi_ref[...] = kr[:, IDX_DIM:IDX_DIM + N_IDX_HEADS]
    wit_ref[...] = krt[IDX_DIM:IDX_DIM + N_IDX_HEADS, :]
    off += LANES
    sgs_ref[...] = _sigmoid(proj(off, d_model))
    off += d_model
    sga_ref[...] = _sigmoid(proj(off, d_model))


def _rope_tables(pos, n_rows):
    half = ROT_DIM // 2
    inv = ROPE_THETA ** (-jnp.arange(half, dtype=F32) / half)
    ang = pos.astype(F32)[:, None] * inv[None, :]
    cos, sin = jnp.cos(ang), jnp.sin(ang)
    ones = jnp.ones((n_rows, HEAD_DIM - ROT_DIM), F32)
    zeros = jnp.zeros((n_rows, HEAD_DIM - ROT_DIM), F32)
    zh = jnp.zeros((n_rows, half), F32)
    cos_t = jnp.concatenate([cos, cos, ones], axis=1)
    sin_a = jnp.concatenate([-sin, zh, zeros], axis=1)
    sin_b = jnp.concatenate([zh, sin, zeros], axis=1)
    rep = LANES // HEAD_DIM
    return tuple(jnp.tile(t, (1, rep)) for t in (cos_t, sin_a, sin_b))


def _in_proj(x2d, n_seq, pos_tab, norm_g, w_pack, tm, dims):
    n, d_model = x2d.shape
    d_ssm, d_attn, d_kv, d_qi = dims
    cos_t, sin_a, sin_b = pos_tab
    t_len = n // n_seq
    nt = t_len // tm
    grid = (n // tm,)
    row = lambda w: pl.BlockSpec((tm, w), lambda i: (i, 0))
    tab = pl.BlockSpec((tm, LANES), lambda i: (i % nt, 0))
    feat = lambda r: pl.BlockSpec((None, r, tm), lambda i: (i // nt, 0, i % nt))
    kern = functools.partial(_in_proj_kernel, d_ssm=d_ssm, d_attn=d_attn, d_kv=d_kv, d_qi=d_qi,
                             d_model=d_model)
    outs = [
        (jax.ShapeDtypeStruct((n, d_ssm), F32), row(d_ssm)),
        (jax.ShapeDtypeStruct((n, d_attn), BF16), row(d_attn)),
        (jax.ShapeDtypeStruct((n_seq, d_kv, t_len), F32), feat(d_kv)),
        (jax.ShapeDtypeStruct((n, d_kv), BF16), row(d_kv)),
        (jax.ShapeDtypeStruct((n_seq, d_kv, t_len), F32), feat(d_kv)),
        (jax.ShapeDtypeStruct((n, d_kv), BF16), row(d_kv)),
        (jax.ShapeDtypeStruct((n_seq, nt, N_KV_HEADS, VT_ROWS, tm), BF16),
         pl.BlockSpec((None, None, N_KV_HEADS, VT_ROWS, tm), lambda i: (i // nt, i % nt, 0, 0, 0))),
        (jax.ShapeDtypeStruct((n, d_qi), BF16), row(d_qi)),
        (jax.ShapeDtypeStruct((n_seq, IDX_DIM, t_len), F32), feat(IDX_DIM)),
        (jax.ShapeDtypeStruct((n, IDX_DIM), BF16), row(IDX_DIM)),
        (jax.ShapeDtypeStruct((n, N_IDX_HEADS), F32), row(N_IDX_HEADS)),
        (jax.ShapeDtypeStruct((n_seq, N_IDX_HEADS, t_len), F32), feat(N_IDX_HEADS)),
        (jax.ShapeDtypeStruct((n, d_model), F32), row(d_model)),
        (jax.ShapeDtypeStruct((n, d_model), F32), row(d_model)),
    ]
    out_shapes = [o[0] for o in outs]
    out_specs = [o[1] for o in outs]
    return pl.pallas_call(
        kern, grid=grid,
        in_specs=[row(d_model), _const_spec((1, d_model)), _const_spec(w_pack.shape), tab, tab, tab],
        out_specs=out_specs, out_shape=out_shapes,
        compiler_params=pltpu.CompilerParams(dimension_semantics=("arbitrary",),
                                             vmem_limit_bytes=VMEM_LIMIT),
        name="in_proj",
    )(x2d, norm_g.reshape(1, d_model), w_pack, cos_t, sin_a, sin_b)


def _s5_readout(y, u, dsk_ref, wglu_ref, bglu_ref, d_model):
    y = y + dsk_ref[...] * u
    cdf = 0.5 * (1.0 + jnp.tanh(math.sqrt(2.0 / math.pi) * (y + 0.044715 * (y * y * y))))
    gl = (y * cdf).astype(BF16)
    z = jnp.dot(gl, wglu_ref[...], preferred_element_type=F32) + bglu_ref[...]
    return z[:, :d_model] * _sigmoid(z[:, d_model:])


def _s5_prompt_kernel(u_ref, bre_ref, bim_ref, tab_ref, cre_ref, cim_ref, dsk_ref, wglu_ref, bglu_ref,
                      out_ref, sre_ref, sim_ref, hre, him, cr, ci, *, tc, d_model):
    @pl.when(pl.program_id(1) == 0)
    def _():
        cr[...] = jnp.zeros_like(cr)
        ci[...] = jnp.zeros_like(ci)

    u = u_ref[...]
    ub = u.astype(BF16)
    hre[...] = jnp.dot(ub, bre_ref[...], preferred_element_type=F32)
    him[...] = jnp.dot(ub, bim_ref[...], preferred_element_type=F32)

    def group(gi, carry):
        car, cai = carry
        r0 = pl.multiple_of(gi * SUBLANES, SUBLANES)
        xr = hre[pl.ds(r0, SUBLANES), :]
        xi = him[pl.ds(r0, SUBLANES), :]
        for t in range(3):
            ar, ai = tab_ref[2 * t], tab_ref[2 * t + 1]
            sr = pltpu.roll(xr, 1 << t, axis=0)
            si = pltpu.roll(xi, 1 << t, axis=0)
            xr, xi = xr + (ar * sr - ai * si), xi + (ar * si + ai * sr)
        pr, pi = tab_ref[6], tab_ref[7]
        xr, xi = xr + (pr * car - pi * cai), xi + (pr * cai + pi * car)
        hre[pl.ds(r0, SUBLANES), :] = xr
        him[pl.ds(r0, SUBLANES), :] = xi
        return xr[SUBLANES - 1:SUBLANES, :], xi[SUBLANES - 1:SUBLANES, :]

    car, cai = lax.fori_loop(0, tc // SUBLANES, group, (cr[...], ci[...]))
    cr[...] = car
    ci[...] = cai
    sre_ref[...] = car
    sim_ref[...] = cai
    y = (jnp.dot(hre[...].astype(BF16), cre_ref[...], preferred_element_type=F32)
         + jnp.dot(him[...].astype(BF16), cim_ref[...], preferred_element_type=F32))
    out_ref[...] = _s5_readout(y, u, dsk_ref, wglu_ref, bglu_ref, d_model)


def _s5_sample_kernel(u_ref, h0r_ref, h0i_ref, are_ref, aim_ref, bre_ref, bim_ref, cre_ref, cim_ref,
                      dsk_ref, wglu_ref, bglu_ref, out_ref, sre_ref, sim_ref, *, d_model):
    u = u_ref[...]
    ub = u.astype(BF16)
    ar, ai = are_ref[...], aim_ref[...]
    h0r, h0i = h0r_ref[...], h0i_ref[...]
    hr = (ar * h0r - ai * h0i) + jnp.dot(ub, bre_ref[...], preferred_element_type=F32)
    hi = (ar * h0i + ai * h0r) + jnp.dot(ub, bim_ref[...], preferred_element_type=F32)
    sre_ref[...] = hr
    sim_ref[...] = hi
    y = (jnp.dot(hr.astype(BF16), cre_ref[...], preferred_element_type=F32)
         + jnp.dot(hi.astype(BF16), cim_ref[...], preferred_element_type=F32))
    out_ref[...] = _s5_readout(y, u, dsk_ref, wglu_ref, bglu_ref, d_model)


def _s5_params(a_re, a_im, log_dt, b_re, b_im, c_re, c_im):
    g, p = a_re.shape
    a_c = lax.complex(a_re.astype(F32), a_im.astype(F32))
    dt = jnp.exp(log_dt.astype(F32))[:, None]
    a_bar = jnp.exp(a_c * dt)
    b_bar = ((a_bar - 1.0) / a_c)[:, :, None] * lax.complex(b_re.astype(F32), b_im.astype(F32))
    eye = jnp.eye(g, dtype=F32)
    n = b_re.shape[2]

    def bmat(b):
        return jnp.einsum('gpn,gh->gnhp', b, eye).reshape(g * n, g * p)

    def cmat(c):
        return jnp.einsum('gnp,gh->gphn', c, eye).reshape(g * p, g * n)

    bre, bim = bmat(jnp.real(b_bar)).astype(BF16), bmat(jnp.imag(b_bar)).astype(BF16)
    cre, cim = cmat(c_re.astype(F32)).astype(BF16), cmat(-c_im.astype(F32)).astype(BF16)
    a1 = a_bar.reshape(1, g * p)
    a2 = a1 * a1
    a3 = a2 * a1
    a4 = a2 * a2
    pw = [a1, a2, a3, a4, a4 * a1, a4 * a2, a4 * a3, a4 * a4]
    rows = jnp.arange(SUBLANES)[:, None]
    tabs = []
    for sh in (1, 2, 4):
        t = jnp.where(rows >= sh, jnp.broadcast_to(pw[sh - 1], (SUBLANES, g * p)), 0.0)
        tabs += [jnp.real(t), jnp.imag(t)]
    pcat = jnp.concatenate(pw, axis=0)
    tabs += [jnp.real(pcat), jnp.imag(pcat)]
    tab = jnp.stack(tabs, axis=0).astype(F32)
    return jnp.real(a1), jnp.imag(a1), bre, bim, cre, cim, tab


def _s5_prompt(u3, s5p, dsk, wglu, bglu, tc):
    bsz, t_len, d_ssm = u3.shape
    _, _, bre, bim, cre, cim, tab = s5p
    sd = bre.shape[1]
    d_model = wglu.shape[1] // 2
    kern = functools.partial(_s5_prompt_kernel, tc=tc, d_model=d_model)
    return pl.pallas_call(
        kern, grid=(bsz, t_len // tc),
        in_specs=[pl.BlockSpec((None, tc, d_ssm), lambda b, c: (b, c, 0)),
                  _const_spec(bre.shape), _const_spec(bim.shape), _const_spec(tab.shape),
                  _const_spec(cre.shape), _const_spec(cim.shape), _const_spec((1, d_ssm)),
                  _const_spec(wglu.shape), _const_spec((1, 2 * d_model))],
        out_specs=[pl.BlockSpec((None, tc, d_model), lambda b, c: (b, c, 0)),
                   pl.BlockSpec((None, 1, sd), lambda b, c: (b, 0, 0)),
                   pl.BlockSpec((None, 1, sd), lambda b, c: (b, 0, 0))],
        out_shape=[jax.ShapeDtypeStruct((bsz, t_len, d_model), F32),
                   jax.ShapeDtypeStruct((bsz, 1, sd), F32),
                   jax.ShapeDtypeStruct((bsz, 1, sd), F32)],
        scratch_shapes=[pltpu.VMEM((tc, sd), F32), pltpu.VMEM((tc, sd), F32),
                        pltpu.VMEM((1, sd), F32), pltpu.VMEM((1, sd), F32)],
        compiler_params=pltpu.CompilerParams(dimension_semantics=("arbitrary", "arbitrary"),
                                             vmem_limit_bytes=VMEM_LIMIT),
        name="s5_prompt",
    )(u3, bre, bim, tab, cre, cim, dsk.reshape(1, d_ssm), wglu, bglu.reshape(1, 2 * d_model))


def _s5_sample(u2, h0r, h0i, s5p, dsk, wglu, bglu):
    n, d_ssm = u2.shape
    are, aim, bre, bim, cre, cim, _ = s5p
    sd = bre.shape[1]
    d_model = wglu.shape[1] // 2
    kern = functools.partial(_s5_sample_kernel, d_model=d_model)
    return pl.pallas_call(
        kern,
        out_shape=[jax.ShapeDtypeStruct((n, d_model), F32),
                   jax.ShapeDtypeStruct((n, sd), F32),
                   jax.ShapeDtypeStruct((n, sd), F32)],
        compiler_params=pltpu.CompilerParams(vmem_limit_bytes=VMEM_LIMIT),
        name="s5_sample",
    )(u2, h0r, h0i, are, aim, bre, bim, cre, cim, dsk.reshape(1, d_ssm), wglu,
      bglu.reshape(1, 2 * d_model))


def _f2k(x):
    b = lax.bitcast_convert_type(x, I32)
    return jnp.where(b < 0, b ^ 0x7FFFFFFF, b)


def _k2f(k):
    return lax.bitcast_convert_type(jnp.where(k < 0, k ^ 0x7FFFFFFF, k), F32)


def _threshold_search(count_ge, amax, forced, k_top):
    kf = float(k_top)
    lo0 = jnp.where(forced, KEY_LOW, _f2k(-amax))
    hi0 = jnp.where(forced, KEY_LOW + 1, _f2k(amax) + 1)

    def cond(st):
        lo, hi, _ = st
        return jnp.max(jnp.where(hi > lo + 1, 1.0, 0.0)) > 0.5

    def body(st):
        lo, hi, it = st
        active = hi > lo + 1
        mid_i = (lo >> 1) + (hi >> 1) + (lo & hi & 1)
        mk = _f2k(0.5 * _k2f(lo) + 0.5 * _k2f(hi))
        use_f = jnp.logical_and(it % 2 == 0, jnp.logical_and(mk > lo, mk < hi))
        mid = jnp.where(use_f, mk, mid_i)
        c = count_ge(_k2f(mid))
        ge = c >= kf
        nlo = jnp.where(ge, mid, lo)
        nhi = jnp.where(c == kf, mid + 1, jnp.where(ge, hi, mid))
        return jnp.where(active, nlo, lo), jnp.where(active, nhi, hi), it + 1

    lo, _, _ = lax.while_loop(cond, body, (lo0, hi0, jnp.int32(0)))
    return _k2f(lo)


def _tri_exclusive(n):
    r = lax.broadcasted_iota(I32, (n, n), 0)
    c = lax.broadcasted_iota(I32, (n, n), 1)
    return (r < c).astype(BF16)


def _dsa_prompt_kernel(qi_ref, wt_ref, q_ref, ki_ref, k_ref, vt_ref, o_ref,
                       sc_ref, qis_ref, qs_ref, m_ref, acc_ref, *, tq, tk, k_top):
    i = pl.program_id(1)
    nkb = i + 1
    hpg = N_HEADS // N_KV_HEADS
    ngr = tk // SUBLANES
    kf = float(k_top)

    qi = qi_ref[...]
    q = q_ref[...]
    zpad = jnp.zeros((tq, HEAD_DIM), BF16)
    for h in range(N_IDX_HEADS):
        qis_ref[h * tq:(h + 1) * tq, :] = qi[:, h * IDX_DIM:(h + 1) * IDX_DIM]
    for h in range(N_HEADS):
        qh = q[:, h * HEAD_DIM:(h + 1) * HEAD_DIM]
        pair = [qh, zpad] if h // hpg == 0 else [zpad, qh]
        qs_ref[h * tq:(h + 1) * tq, :] = jnp.concatenate(pair, axis=1)

    w8 = wt_ref[...] * (IDX_DIM ** -0.5)
    qpos = i * tq + lax.broadcasted_iota(I32, (SUBLANES, tq), 1)

    def score_blk(j, amx, diagonal):
        k0 = pl.multiple_of(j * tk, tk)
        d = lax.dot_general(ki_ref[pl.ds(k0, tk), :], qis_ref[...], NT_DIMS, preferred_element_type=F32)
        mags = []
        for c in range(tq // LANES):
            acc = jnp.zeros((tk, LANES), F32)
            for h in range(N_IDX_HEADS):
                cols = slice(h * tq + c * LANES, h * tq + (c + 1) * LANES)
                acc = acc + jnp.maximum(d[:, cols], 0.0) * w8[h:h + 1, c * LANES:(c + 1) * LANES]
            mag = jnp.abs(acc)
            if diagonal:
                kpos = j * tk + lax.broadcasted_iota(I32, (tk, LANES), 0)
                causal = kpos <= i * tq + c * LANES + lax.broadcasted_iota(I32, (tk, LANES), 1)
                acc = jnp.where(causal, acc, NEG_INF)
                mag = jnp.where(causal, mag, 0.0)
            sc_ref[j, :, c * LANES:(c + 1) * LANES] = acc
            mags.append(jnp.max(mag.reshape(ngr, SUBLANES, LANES), axis=0))
        return jnp.maximum(amx, jnp.concatenate(mags, axis=1))

    amx = lax.fori_loop(0, nkb - 1, lambda j, a: score_blk(j, a, False), jnp.zeros((SUBLANES, tq), F32))
    amx = score_blk(nkb - 1, amx, True)

    rep = lambda x: jnp.broadcast_to(x, (SUBLANES, tq))
    amax = rep(jnp.max(amx, axis=0, keepdims=True))
    forced = (qpos + 1) <= k_top
    lo0 = jnp.where(forced, KEY_LOW, _f2k(-amax))
    hi0 = jnp.where(forced, KEY_LOW + 1, _f2k(amax) + 1)

    def count(thr, strict):
        def body(j, part):
            blk = sc_ref[j].reshape(ngr // COUNT_WAYS, COUNT_WAYS, SUBLANES, tq)
            hit = (blk > thr[None, None]) if strict else (blk >= thr[None, None])
            return part + jnp.sum(jnp.where(hit, 1.0, 0.0), axis=0)

        part = lax.fori_loop(0, nkb, body, jnp.zeros((COUNT_WAYS, SUBLANES, tq), F32))
        return rep(jnp.sum(jnp.sum(part, axis=0), axis=0, keepdims=True))

    def search_pass(lo, hi, value_space):
        active = hi > lo + 1
        span = hi - lo
        mid = lo + lax.shift_right_logical(span, jnp.ones_like(span))
        if value_space:
            mk = _f2k(0.5 * _k2f(lo) + 0.5 * _k2f(hi))
            mid = jnp.where(jnp.logical_and(mk > lo, mk < hi), mk, mid)
        c = count(_k2f(mid), False)
        ge = c >= kf
        nlo = jnp.where(ge, mid, lo)
        nhi = jnp.where(c == kf, mid + 1, jnp.where(ge, hi, mid))
        return jnp.where(active, nlo, lo), jnp.where(active, nhi, hi)

    def search_body(st):
        lo, hi, _ = st
        lo, hi = search_pass(lo, hi, True)
        lo, hi = search_pass(lo, hi, False)
        return lo, hi, (jnp.max(jnp.where(hi > lo + 1, 1.0, 0.0)) > 0.5).astype(I32)

    lo, _, _ = lax.while_loop(lambda st: st[2] > 0, search_body, (lo0, hi0, jnp.int32(1)))
    thr = _k2f(lo)
    c_ge = count(thr, False)
    thr1 = thr[0:1, :]

    @pl.when(jnp.max(jnp.where(c_ge > kf, 1.0, 0.0)) > 0.5)
    def _():
        need = jnp.where(c_ge > kf, kf - count(thr, True), 1e30)[0:1, :]
        r = lax.broadcasted_iota(I32, (tk, tk), 0)
        c = lax.broadcasted_iota(I32, (tk, tk), 1)
        earlier = (c < r).astype(BF16)
        ones_k = jnp.ones((SUBLANES, tk), BF16)

        def fix(j, run):
            sc = sc_ref[j]
            eq = sc == thr1
            eqb = jnp.where(eq, 1.0, 0.0).astype(BF16)
            before = jnp.dot(earlier, eqb, preferred_element_type=F32) + run[0:1, :]
            sc_ref[j] = jnp.where(jnp.logical_and(eq, before >= need), NEG_INF, sc)
            return run + jnp.dot(ones_k, eqb, preferred_element_type=F32)

        lax.fori_loop(0, nkb, fix, jnp.zeros((SUBLANES, tq), F32))

    m_ref[...] = jnp.full(m_ref.shape, NEG_INF, F32)
    acc_ref[...] = jnp.zeros(acc_ref.shape, F32)

    def att_blk(j, carry):
        k0 = pl.multiple_of(j * tk, tk)
        s = lax.dot_general(k_ref[pl.ds(k0, tk), :], qs_ref[...], NT_DIMS, preferred_element_type=F32)
        bias = jnp.where(sc_ref[j] >= thr1, 0.0, NEG_INF)
        for h in range(N_HEADS):
            x = s[:, h * tq:(h + 1) * tq] + bias
            bm = jnp.max(jnp.max(x.reshape(ngr, SUBLANES, tq), axis=0), axis=0, keepdims=True)
            m_old = m_ref[h:h + 1, :]
            m_new = jnp.maximum(m_old, bm)
            m_safe = jnp.where(m_new == NEG_INF, 0.0, m_new)
            p = jnp.exp(x - m_safe).astype(BF16)
            pv = jnp.dot(vt_ref[j, h // hpg], p, preferred_element_type=F32)
            acc_ref[h] = acc_ref[h] * jnp.exp(m_old - m_safe) + pv
            m_ref[h:h + 1, :] = m_new
        return carry

    lax.fori_loop(0, nkb, att_blk, 0)

    for pr in range(N_HEADS // 2):
        outs = []
        for h in (2 * pr, 2 * pr + 1):
            a = acc_ref[h]
            outs.append(a[:HEAD_DIM, :] / a[HEAD_DIM:HEAD_DIM + 1, :])
        o2 = jnp.concatenate(outs, axis=0)
        o_ref[:, 2 * pr * HEAD_DIM:(2 * pr + 2) * HEAD_DIM] = o2.T.astype(o_ref.dtype)


def _dsa_prompt(qi, wt, q, kib, kb, vt, tq, tk):
    bsz, t_len, d_qi = qi.shape
    d_attn = q.shape[2]
    d_kv = kb.shape[2]
    nkb = t_len // tk
    k_top = min(TOPK_MAX, t_len // 4)
    assert tq == tk and tq % LANES == 0 and t_len % tq == 0 and vt.shape[1] == nkb
    kern = functools.partial(_dsa_prompt_kernel, tq=tq, tk=tk, k_top=k_top)
    tile = lambda w: pl.BlockSpec((None, tq, w), lambda b, i: (b, i, 0))
    full = lambda w: pl.BlockSpec((None, t_len, w), lambda b, i: (b, 0, 0))
    return pl.pallas_call(
        kern, grid=(bsz, t_len // tq),
        in_specs=[tile(d_qi), pl.BlockSpec((None, N_IDX_HEADS, tq), lambda b, i: (b, 0, i)), tile(d_attn),
                  full(IDX_DIM), full(d_kv),
                  pl.BlockSpec((None, nkb, N_KV_HEADS, VT_ROWS, tk), lambda b, i: (b, 0, 0, 0, 0))],
        out_specs=tile(d_attn),
        out_shape=jax.ShapeDtypeStruct((bsz, t_len, d_attn), BF16),
        scratch_shapes=[pltpu.VMEM((nkb, tk, tq), F32),
                        pltpu.VMEM((N_IDX_HEADS * tq, IDX_DIM), BF16),
                        pltpu.VMEM((N_HEADS * tq, N_KV_HEADS * HEAD_DIM), BF16),
                        pltpu.VMEM((N_HEADS, tq), F32),
                        pltpu.VMEM((N_HEADS, VT_ROWS, tq), F32)],
        compiler_params=pltpu.CompilerParams(dimension_semantics=("arbitrary", "arbitrary"),
                                             vmem_limit_bytes=VMEM_LIMIT),
        name="dsa_prompt",
    )(qi, wt, q, kib, kb, vt)


def _idx_score_kernel(pt_ref, q_ref, w_ref, *refs, n_pg):
    pages, o_ref = refs[:n_pg], refs[n_pg]
    b = pl.program_id(1)
    q = q_ref[...]
    w = w_ref[...] * (IDX_DIM ** -0.5)
    parts = []
    for p in range(n_pg):
        d = jnp.dot(q, pages[p][...].astype(BF16), preferred_element_type=F32)
        parts.append(jnp.sum(jnp.maximum(d, 0.0) * w, axis=0, keepdims=True))
    o_ref[pl.ds(b % SUBLANES, 1), :] = jnp.concatenate(parts, axis=1)


def _idx_scores(page_table, qis, wcol, cache_idx_kt, n_pg):
    sq, n_pages = page_table.shape
    page = cache_idx_kt.shape[2]
    kern = functools.partial(_idx_score_kernel, n_pg=n_pg)
    page_specs = [pl.BlockSpec((None, IDX_DIM, page),
                               lambda j, b, pt, p=p: (pt[b * n_pages + j * n_pg + p], 0, 0))
                  for p in range(n_pg)]
    grid_spec = pltpu.PrefetchScalarGridSpec(
        num_scalar_prefetch=1, grid=(n_pages // n_pg, sq),
        in_specs=[pl.BlockSpec((None, N_IDX_HEADS, IDX_DIM), lambda j, b, pt: (b, 0, 0)),
                  pl.BlockSpec((None, N_IDX_HEADS, 1), lambda j, b, pt: (b, 0, 0))] + page_specs,
        out_specs=pl.BlockSpec((SUBLANES, n_pg * page), lambda j, b, pt: (b // SUBLANES, j)))
    return pl.pallas_call(
        kern, grid_spec=grid_spec,
        out_shape=jax.ShapeDtypeStruct((sq, n_pages * page), F32),
        compiler_params=pltpu.CompilerParams(dimension_semantics=("arbitrary", "arbitrary")),
        name="sample_idx_scores",
    )(page_table.reshape(-1), qis, wcol, *([cache_idx_kt] * n_pg))


def _sample_select_kernel(sc_ref, qi_ref, ki_ref, wi_ref, mask_ref, selfsel_ref, *, ch, k_top):
    sq, l_past = sc_ref.shape
    nch = l_past // ch
    w = wi_ref[...] * (IDX_DIM ** -0.5)
    qf = qi_ref[...].astype(F32)
    kf32 = ki_ref[...].astype(F32)
    s_self = jnp.zeros((sq, 1), F32)
    for h in range(N_IDX_HEADS):
        dh = jnp.sum(qf[:, h * IDX_DIM:(h + 1) * IDX_DIM] * kf32, axis=1, keepdims=True)
        s_self = s_self + jnp.maximum(dh, 0.0) * w[:, h:h + 1]

    amax = jnp.abs(s_self)
    for c in range(nch):
        amax = jnp.maximum(amax, jnp.max(jnp.abs(sc_ref[:, c * ch:(c + 1) * ch]), axis=1, keepdims=True))

    def count_cmp(thr, strict):
        thr_b = jnp.broadcast_to(thr, (sq, LANES))
        part = jnp.zeros((sq, LANES), F32)
        for c in range(l_past // LANES):
            blk = sc_ref[:, c * LANES:(c + 1) * LANES]
            hit = (blk > thr_b) if strict else (blk >= thr_b)
            part = part + jnp.where(hit, 1.0, 0.0)
        self_hit = (s_self > thr) if strict else (s_self >= thr)
        return jnp.sum(part, axis=1, keepdims=True) + jnp.where(self_hit, 1.0, 0.0)

    kf = float(k_top)
    forced = amax < 0.0
    thr = _threshold_search(lambda t: count_cmp(t, False), amax, forced, k_top)
    need = kf - count_cmp(thr, True)
    tri = _tri_exclusive(ch)
    run = jnp.zeros((sq, 1), F32)
    for c in range(nch):
        sc = sc_ref[:, c * ch:(c + 1) * ch]
        eq = sc == thr
        before = jnp.dot(jnp.where(eq, 1.0, 0.0).astype(BF16), tri, preferred_element_type=F32) + run
        keep = jnp.logical_or(sc > thr, jnp.logical_and(eq, before < need))
        mask_ref[:, c * ch:(c + 1) * ch] = jnp.where(keep, 1.0, 0.0)
        run = run + jnp.sum(jnp.where(eq, 1.0, 0.0), axis=1, keepdims=True)
    self_keep = jnp.logical_or(s_self > thr, jnp.logical_and(s_self == thr, run < need))
    selfsel_ref[...] = jnp.where(self_keep, 1.0, 0.0)


def _sample_select(scores, qi, kib, wi, ch, k_top):
    sq, l_past = scores.shape
    kern = functools.partial(_sample_select_kernel, ch=ch, k_top=k_top)
    return pl.pallas_call(
        kern,
        out_shape=[jax.ShapeDtypeStruct((sq, l_past), F32), jax.ShapeDtypeStruct((sq, 1), F32)],
        compiler_params=pltpu.CompilerParams(vmem_limit_bytes=VMEM_LIMIT),
        name="sample_select",
    )(scores, qi, kib, wi)


def _sample_attn_kernel(pt_ref, q_ref, mask_ref, ks_ref, vs_ref, ss_ref, *refs, n_pg):
    kpages, vpages = refs[:n_pg], refs[n_pg:2 * n_pg]
    o_ref, m_scr, l_scr, acc_scr = refs[2 * n_pg:]
    j = pl.program_id(1)
    hpg = N_HEADS // N_KV_HEADS

    @pl.when(j == 0)
    def _():
        m_scr[...] = jnp.full(m_scr.shape, NEG_INF, F32)
        l_scr[...] = jnp.zeros(l_scr.shape, F32)
        acc_scr[...] = jnp.zeros(acc_scr.shape, F32)

    q = q_ref[...]
    page = kpages[0].shape[1]
    s = jnp.concatenate(
        [jnp.dot(q, kpages[p][...].astype(BF16), preferred_element_type=F32)
         for p in range(n_pg)], axis=1)
    sm = jnp.where(mask_ref[...] > 0.5, s, NEG_INF)
    m_old = m_scr[...]
    m_new = jnp.maximum(m_old, jnp.max(sm, axis=1, keepdims=True))
    m_safe = jnp.where(m_new == NEG_INF, 0.0, m_new)
    p_ = jnp.exp(sm - m_safe)
    alpha = jnp.exp(m_old - m_safe)
    pb = p_.astype(BF16)
    pv = jnp.zeros(acc_scr.shape, F32)
    for p in range(n_pg):
        pv = pv + lax.dot_general(pb[:, p * page:(p + 1) * page], vpages[p][...].astype(BF16), NT_DIMS,
                                  preferred_element_type=F32)
    l_scr[...] = l_scr[...] * alpha + jnp.sum(pb.astype(F32), axis=1, keepdims=True)
    acc_scr[...] = acc_scr[...] * alpha + pv
    m_scr[...] = m_new

    @pl.when(j == pl.num_programs(1) - 1)
    def _():
        s_self = jnp.sum(q.astype(F32) * ks_ref[...].astype(F32), axis=1, keepdims=True)
        s_self = jnp.where(ss_ref[...] > 0.5, s_self, NEG_INF)
        m_o = m_scr[...]
        m_n = jnp.maximum(m_o, s_self)
        m_s = jnp.where(m_n == NEG_INF, 0.0, m_n)
        p_self = jnp.exp(s_self - m_s).astype(BF16).astype(F32)
        al = jnp.exp(m_o - m_s)
        l_fin = l_scr[...] * al + p_self
        acc = acc_scr[...] * al + p_self * vs_ref[...].astype(F32)
        o = acc / l_fin
        hrow = lax.broadcasted_iota(I32, o.shape, 0)
        o = jnp.where(hrow < hpg, o, pltpu.roll(o, HEAD_DIM, axis=1))
        o_ref[...] = o[:, :HEAD_DIM].astype(o_ref.dtype)


def _sample_attn(page_table, qs, mask3, kself, vself, selfsel, cache_k2, cache_v2, n_pg):
    sq, n_pages = page_table.shape
    d_kv, page = cache_k2.shape[1], cache_k2.shape[2]
    kern = functools.partial(_sample_attn_kernel, n_pg=n_pg)
    pspec = lambda p: pl.BlockSpec((None, d_kv, page),
                                   lambda b, j, pt: (pt[b * n_pages + j * n_pg + p], 0, 0))
    per_seq = lambda r, w: pl.BlockSpec((None, r, w), lambda b, j, pt: (b, 0, 0))
    grid_spec = pltpu.PrefetchScalarGridSpec(
        num_scalar_prefetch=1, grid=(sq, n_pages // n_pg),
        in_specs=[per_seq(N_HEADS, d_kv),
                  pl.BlockSpec((None, 1, n_pg * page), lambda b, j, pt: (b, 0, j)),
                  per_seq(1, d_kv), per_seq(1, d_kv), per_seq(1, 1)]
                 + [pspec(p) for p in range(n_pg)] + [pspec(p) for p in range(n_pg)],
        out_specs=per_seq(N_HEADS, HEAD_DIM),
        scratch_shapes=[pltpu.VMEM((N_HEADS, 1), F32), pltpu.VMEM((N_HEADS, 1), F32),
                        pltpu.VMEM((N_HEADS, d_kv), F32)])
    return pl.pallas_call(
        kern, grid_spec=grid_spec,
        out_shape=jax.ShapeDtypeStruct((sq, N_HEADS, HEAD_DIM), BF16),
        compiler_params=pltpu.CompilerParams(dimension_semantics=("arbitrary", "arbitrary")),
        name="sample_attn",
    )(page_table.reshape(-1), qs, mask3, kself, vself, selfsel,
      *([cache_k2] * n_pg), *([cache_v2] * n_pg))


def _tail_kernel(x_ref, attn_ref, ssm_ref, sgs_ref, sga_ref, wao_ref, wo_ref, g2_ref, wup_ref, wdn_ref,
                 gf_ref, y_ref):
    attn_out = jnp.dot(attn_ref[...], wao_ref[...], preferred_element_type=F32)
    mix = sgs_ref[...] * ssm_ref[...] + sga_ref[...] * attn_out
    x1 = x_ref[...] + jnp.dot(mix.astype(BF16), wo_ref[...], preferred_element_type=F32)
    hh = _rms_norm(x1, g2_ref[...]).astype(BF16)
    up = jnp.dot(hh, wup_ref[...], preferred_element_type=F32)
    r = jnp.square(jnp.maximum(up, 0.0)).astype(BF16)
    x2 = x1 + jnp.dot(r, wdn_ref[...], preferred_element_type=F32)
    y_ref[...] = _rms_norm(x2, gf_ref[...])


def _tail(x2d, attn, ssm, sgs, sga, wao, wo, g2, wup, wdn, gf, tm):
    n, d_model = x2d.shape
    row = lambda w: pl.BlockSpec((tm, w), lambda i: (i, 0))
    return pl.pallas_call(
        _tail_kernel, grid=(n // tm,),
        in_specs=[row(d_model), row(attn.shape[1]), row(d_model), row(d_model), row(d_model),
                  _const_spec(wao.shape), _const_spec(wo.shape), _const_spec((1, d_model)),
                  _const_spec(wup.shape), _const_spec(wdn.shape), _const_spec((1, d_model))],
        out_specs=row(d_model),
        out_shape=jax.ShapeDtypeStruct((n, d_model), F32),
        compiler_params=pltpu.CompilerParams(dimension_semantics=("arbitrary",),
                                             vmem_limit_bytes=VMEM_LIMIT),
        name="tail",
    )(x2d, attn, ssm, sgs, sga, wao, wo, g2.reshape(1, d_model), wup, wdn, gf.reshape(1, d_model))


def _tiles(n_rows, t_len):
    tm = min(256, n_rows)
    tc = min(256, t_len)
    tq = min(256, t_len)
    return tm, tc, tq, tq


def kernel(x_prompt, x_sample, cache_k, cache_v, cache_idx_k, state_ssm_re, state_ssm_im, page_table,
           norm1_g, w_in, ssm_a_re, ssm_a_im, ssm_log_dt, ssm_b_re, ssm_b_im, ssm_c_re, ssm_c_im,
           ssm_d, w_glu, b_glu, w_attn_out, w_o, norm2_g, w_up, w_down, normf_g):
    bsz, t_len, d_model = x_prompt.shape
    sq, s_len, _ = x_sample.shape
    assert s_len == 1, "the sample path handles one new token per sequence"
    n_pool, page = cache_idx_k.shape[0], cache_idx_k.shape[1]
    n_pages = page_table.shape[1]
    past = n_pages * page
    d_ssm = ssm_d.shape[0]
    d_attn = N_HEADS * HEAD_DIM
    d_kv = N_KV_HEADS * HEAD_DIM
    d_qi = N_IDX_HEADS * IDX_DIM
    dims = (d_ssm, d_attn, d_kv, d_qi)

    c_ki = d_ssm + d_attn + 2 * d_kv + d_qi
    c_g = c_ki + IDX_DIM + N_IDX_HEADS
    w_pack = jnp.concatenate(
        [w_in[:, :c_g], jnp.zeros((d_model, LANES - IDX_DIM - N_IDX_HEADS), w_in.dtype), w_in[:, c_g:]],
        axis=1).astype(BF16)
    wglu_b, wao_b, wo_b = w_glu.astype(BF16), w_attn_out.astype(BF16), w_o.astype(BF16)
    wup_b, wdn_b = w_up.astype(BF16), w_down.astype(BF16)
    s5p = _s5_params(ssm_a_re, ssm_a_im, ssm_log_dt, ssm_b_re, ssm_b_im, ssm_c_re, ssm_c_im)

    n_p = bsz * t_len
    tm, tc, tq, tk = _tiles(n_p, t_len)
    xp = x_prompt.reshape(n_p, d_model)
    tabs_p = _rope_tables(jnp.arange(t_len, dtype=I32), t_len)
    assert tm == tk
    (u, q, kt, kb, vt, _, vtb, qi, kit, kib, _, wit, sgs, sga) = _in_proj(
        xp, bsz, tabs_p, norm1_g, w_pack, tm, dims)
    ssm_out, re_p, im_p = _s5_prompt(u.reshape(bsz, t_len, d_ssm), s5p, ssm_d, wglu_b, b_glu, tc)
    r3 = lambda a: a.reshape(bsz, t_len, a.shape[-1])
    attn = _dsa_prompt(r3(qi), wit, r3(q), r3(kib), r3(kb), vtb, tq, tk)
    y_p = _tail(xp, attn.reshape(n_p, d_attn), ssm_out.reshape(n_p, d_model), sgs, sga,
                wao_b, wo_b, norm2_g, wup_b, wdn_b, normf_g, tm)

    xs = x_sample.reshape(sq, d_model)
    tabs_s = _rope_tables(jnp.full((sq,), past, I32), sq)
    (u_s, q_s, kt_s, kb_s, vt_s, vb_s, _, qi_s, kit_s, kib_s, wi_s, _, sgs_s, sga_s) = _in_proj(
        xs, 1, tabs_s, norm1_g, w_pack, sq, dims)
    sd = state_ssm_re.shape[1] * state_ssm_re.shape[2]
    ssm_s, re_s, im_s = _s5_sample(u_s, state_ssm_re.reshape(sq, sd), state_ssm_im.reshape(sq, sd),
                                   s5p, ssm_d, wglu_b, b_glu)
    n_pg = math.gcd(n_pages, 16)
    scores = _idx_scores(page_table, qi_s.reshape(sq, N_IDX_HEADS, IDX_DIM),
                         wi_s.reshape(sq, N_IDX_HEADS, 1), jnp.transpose(cache_idx_k, (0, 2, 1)), n_pg)
    k_top_s = min(TOPK_MAX, (past + s_len) // 4)
    mask, selfsel = _sample_select(scores, qi_s, kib_s, wi_s, min(512, past), k_top_s)
    q4 = q_s.reshape(sq, N_KV_HEADS, N_HEADS // N_KV_HEADS, HEAD_DIM)
    qs_pad = (q4[:, :, :, None, :] * jnp.eye(N_KV_HEADS, dtype=BF16)[None, :, None, :, None]
              ).reshape(sq, N_HEADS, d_kv)
    feat_major = lambda c: jnp.transpose(c, (0, 2, 3, 1)).reshape(n_pool, d_kv, page)
    attn_s = _sample_attn(page_table, qs_pad, mask.reshape(sq, 1, past),
                          kb_s.reshape(sq, 1, d_kv), vb_s.reshape(sq, 1, d_kv), selfsel.reshape(sq, 1, 1),
                          feat_major(cache_k), feat_major(cache_v), n_pg)
    y_s = _tail(xs, attn_s.reshape(sq, d_attn), ssm_s, sgs_s, sga_s,
                wao_b, wo_b, norm2_g, wup_b, wdn_b, normf_g, sq)

    g_ssm, p_ssm = state_ssm_re.shape[1], state_ssm_re.shape[2]
    kv_out = lambda a: jnp.transpose(a.reshape(a.shape[0], N_KV_HEADS, HEAD_DIM, a.shape[2]), (0, 3, 1, 2))
    return (y_p.reshape(bsz, t_len, d_model), y_s.reshape(sq, s_len, d_model),
            kv_out(kt), kv_out(vt), jnp.transpose(kit, (0, 2, 1)),
            re_p.reshape(bsz, g_ssm, p_ssm), im_p.reshape(bsz, g_ssm, p_ssm),
            kv_out(kt_s).reshape(sq, s_len, N_KV_HEADS, HEAD_DIM),
            kv_out(vt_s).reshape(sq, s_len, N_KV_HEADS, HEAD_DIM),
            jnp.transpose(kit_s, (0, 2, 1)).reshape(sq, s_len, IDX_DIM),
            re_s.reshape(sq, g_ssm, p_ssm), im_s.reshape(sq, g_ssm, p_ssm))
```

```python
import functools
import math

import jax
import jax.numpy as jnp
from jax import lax
from jax.experimental import pallas as pl
from jax.experimental.pallas import tpu as pltpu

F32 = jnp.float32
BF16 = jnp.bfloat16
I32 = jnp.int32

SSM_GROUP = 16
SSM_STATE = 64
N_HEADS = 8
N_KV_HEADS = 2
HEAD_DIM = 64
ROT_DIM = HEAD_DIM // 4
N_IDX_HEADS = 8
IDX_DIM = 64
IDX_ROT_DIM = IDX_DIM // 4
ROPE_THETA = 500000.0
TOPK_MAX = 256
EPS = 1e-6
LOG2_E = math.log2(math.e)

LANES = 128
SUBLANES = 8
MXU_COLS = 256
COUNT_WAYS = 4
VT_ROWS = 80
VMEM_LIMIT = 56 * 1024 * 1024

NEG_INF = float("-inf")
FLT_MAX = float(jnp.finfo(jnp.float32).max)
KEY_LOW = -2139095040
NT_DIMS = (((1,), (1,)), ((), ()))


def _const_spec(shape):
    nd = len(shape)
    return pl.BlockSpec(shape, lambda *_: (0,) * nd, pipeline_mode=pl.Buffered(1))


def _rms_norm(x, g):
    ms = jnp.mean(x * x, axis=-1, keepdims=True)
    return x * lax.rsqrt(ms + EPS) * g


def _sigmoid(x):
    return 1.0 / (1.0 + jnp.exp(-x))


def _rope(x, cos_t, sin_a, sin_b):
    return (x * cos_t + pltpu.roll(x, LANES - ROT_DIM // 2, axis=1) * sin_a
            + pltpu.roll(x, ROT_DIM // 2, axis=1) * sin_b)


def _in_proj_kernel(x_ref, g_ref, w_ref, cos_ref, sa_ref, sb_ref,
                    u_ref, q_ref, kt_ref, kb_ref, vt_ref, vb_ref, vtb_ref, qi_ref, kit_ref, kib_ref,
                    wi_ref, wit_ref, sgs_ref, sga_ref, *, d_ssm, d_attn, d_kv, d_qi, d_model):
    assert d_kv == LANES
    tm = x_ref.shape[0]
    h = _rms_norm(x_ref[...], g_ref[...]).astype(BF16)
    cos_t, sin_a, sin_b = cos_ref[...], sa_ref[...], sb_ref[...]

    def proj(c0, width):
        return jnp.dot(h, w_ref[:, c0:c0 + width], preferred_element_type=F32)

    off = 0
    u_ref[...] = proj(off, d_ssm)
    off += d_ssm
    def rope_chunks(c0, width):
        for m in range(width // MXU_COLS):
            wide = proj(c0 + m * MXU_COLS, MXU_COLS)
            for c in range(MXU_COLS // LANES):
                yield (m * MXU_COLS // LANES + c,
                       _rope(wide[:, c * LANES:(c + 1) * LANES], cos_t, sin_a, sin_b))

    for c, r in rope_chunks(off, d_attn):
        q_ref[:, c * LANES:(c + 1) * LANES] = (r * (HEAD_DIM ** -0.5 * LOG2_E)).astype(BF16)
    off += d_attn
    kv = proj(off, 2 * d_kv)
    r = _rope(kv[:, :d_kv], cos_t, sin_a, sin_b)
    kt_ref[...] = r.T
    kb_ref[...] = r.astype(BF16)
    off += d_kv
    vv = kv[:, d_kv:]
    vt = vv.T
    vt_ref[...] = vt
    vb_ref[...] = vv.astype(BF16)
    sub = lax.broadcasted_iota(I32, (VT_ROWS - HEAD_DIM, tm), 0)
    ones_pad = jnp.where(sub == 0, 1.0, 0.0)
    for g in range(N_KV_HEADS):
        vtb_ref[g] = jnp.concatenate([vt[g * HEAD_DIM:(g + 1) * HEAD_DIM, :], ones_pad], axis=0).astype(BF16)
    off += d_kv
    for c, r in rope_chunks(off, d_qi):
        qi_ref[:, c * LANES:(c + 1) * LANES] = r.astype(BF16)
    off += d_qi
    kw = proj(off, LANES)
    lane = lax.broadcasted_iota(I32, kw.shape, 1)
    kr = jnp.where(lane < IDX_DIM, _rope(kw, cos_t, sin_a, sin_b), kw * (N_IDX_HEADS ** -0.5))
    krt = kr.T
    kit_ref[...] = krt[:IDX_DIM, :]
    kib_ref[...] = kr[:, :IDX_DIM].astype(BF16)
    wi_ref[...] = kr[:, IDX_DIM:IDX_DIM + N_IDX_HEADS]
    wit_ref[...] = krt[IDX_DIM:IDX_DIM + N_IDX_HEADS, :]
    off += LANES
    sgs_ref[...] = _sigmoid(proj(off, d_model))
    off += d_model
    sga_ref[...] = _sigmoid(proj(off, d_model))


def _rope_tables(pos, n_rows):
    half = ROT_DIM // 2
    inv = ROPE_THETA ** (-jnp.arange(half, dtype=F32) / half)
    ang = pos.astype(F32)[:, None] * inv[None, :]
    cos, sin = jnp.cos(ang), jnp.sin(ang)
    ones = jnp.ones((n_rows, HEAD_DIM - ROT_DIM), F32)
    zeros = jnp.zeros((n_rows, HEAD_DIM - ROT_DIM), F32)
    zh = jnp.zeros((n_rows, half), F32)
    cos_t = jnp.concatenate([cos, cos, ones], axis=1)
    sin_a = jnp.concatenate([-sin, zh, zeros], axis=1)
    sin_b = jnp.concatenate([zh, sin, zeros], axis=1)
    rep = LANES // HEAD_DIM
    return tuple(jnp.tile(t, (1, rep)) for t in (cos_t, sin_a, sin_b))


def _in_proj(x2d, n_seq, pos_tab, norm_g, w_pack, tm, dims):
    n, d_model = x2d.shape
    d_ssm, d_attn, d_kv, d_qi = dims
    cos_t, sin_a, sin_b = pos_tab
    t_len = n // n_seq
    nt = t_len // tm
    grid = (n // tm,)
    row = lambda w: pl.BlockSpec((tm, w), lambda i: (i, 0))
    tab = pl.BlockSpec((tm, LANES), lambda i: (i % nt, 0))
    feat = lambda r: pl.BlockSpec((None, r, tm), lambda i: (i // nt, 0, i % nt))
    kern = functools.partial(_in_proj_kernel, d_ssm=d_ssm, d_attn=d_attn, d_kv=d_kv, d_qi=d_qi,
                             d_model=d_model)
    outs = [
        (jax.ShapeDtypeStruct((n, d_ssm), F32), row(d_ssm)),
        (jax.ShapeDtypeStruct((n, d_attn), BF16), row(d_attn)),
        (jax.ShapeDtypeStruct((n_seq, d_kv, t_len), F32), feat(d_kv)),
        (jax.ShapeDtypeStruct((n, d_kv), BF16), row(d_kv)),
        (jax.ShapeDtypeStruct((n_seq, d_kv, t_len), F32), feat(d_kv)),
        (jax.ShapeDtypeStruct((n, d_kv), BF16), row(d_kv)),
        (jax.ShapeDtypeStruct((n_seq, nt, N_KV_HEADS, VT_ROWS, tm), BF16),
         pl.BlockSpec((None, None, N_KV_HEADS, VT_ROWS, tm), lambda i: (i // nt, i % nt, 0, 0, 0))),
        (jax.ShapeDtypeStruct((n, d_qi), BF16), row(d_qi)),
        (jax.ShapeDtypeStruct((n_seq, IDX_DIM, t_len), F32), feat(IDX_DIM)),
        (jax.ShapeDtypeStruct((n, IDX_DIM), BF16), row(IDX_DIM)),
        (jax.ShapeDtypeStruct((n, N_IDX_HEADS), F32), row(N_IDX_HEADS)),
        (jax.ShapeDtypeStruct((n_seq, N_IDX_HEADS, t_len), F32), feat(N_IDX_HEADS)),
        (jax.ShapeDtypeStruct((n, d_model), F32), row(d_model)),
        (jax.ShapeDtypeStruct((n, d_model), F32), row(d_model)),
    ]
    out_shapes = [o[0] for o in outs]
    out_specs = [o[1] for o in outs]
    return pl.pallas_call(
        kern, grid=grid,
        in_specs=[row(d_model), _const_spec((1, d_model)), _const_spec(w_pack.shape), tab, tab, tab],
        out_specs=out_specs, out_shape=out_shapes,
        compiler_params=pltpu.CompilerParams(dimension_semantics=("arbitrary",),
                                             vmem_limit_bytes=VMEM_LIMIT),
        name="in_proj",
    )(x2d, norm_g.reshape(1, d_model), w_pack, cos_t, sin_a, sin_b)


def _s5_readout(y, u, dsk_ref, wglu_ref, bglu_ref, d_model):
    y = y + dsk_ref[...] * u
    cdf = 0.5 * (1.0 + jnp.tanh(math.sqrt(2.0 / math.pi) * (y + 0.044715 * (y * y * y))))
    gl = (y * cdf).astype(BF16)
    z = jnp.dot(gl, wglu_ref[...], preferred_element_type=F32) + bglu_ref[...]
    return z[:, :d_model] * _sigmoid(z[:, d_model:])


def _s5_prompt_kernel(u_ref, bcat_ref, tab_ref, ccre_ref, ccim_ref, dsk_ref, wglu_ref, bglu_ref,
                      out_ref, sre_ref, sim_ref, hre, him, cr, ci, *, tc, d_model):
    @pl.when(pl.program_id(1) == 0)
    def _():
        cr[...] = jnp.zeros_like(cr)
        ci[...] = jnp.zeros_like(ci)

    u = u_ref[...]
    ub = u.astype(BF16)
    d_ssm, sd = u.shape[1], hre.shape[1]
    for c in range(sd // LANES):
        kc = (c * LANES * d_ssm // sd) // LANES
        bu = jnp.dot(ub[:, kc * LANES:(kc + 1) * LANES], bcat_ref[c], preferred_element_type=F32)
        hre[:, c * LANES:(c + 1) * LANES] = bu[:, :LANES]
        him[:, c * LANES:(c + 1) * LANES] = bu[:, LANES:]

    def group(gi, carry):
        car, cai = carry
        r0 = pl.multiple_of(gi * SUBLANES, SUBLANES)
        xr = hre[pl.ds(r0, SUBLANES), :]
        xi = him[pl.ds(r0, SUBLANES), :]
        for t in range(3):
            ar, ai = tab_ref[2 * t], tab_ref[2 * t + 1]
            sr = pltpu.roll(xr, 1 << t, axis=0)
            si = pltpu.roll(xi, 1 << t, axis=0)
            xr, xi = xr + (ar * sr - ai * si), xi + (ar * si + ai * sr)
        pr, pi = tab_ref[6], tab_ref[7]
        xr, xi = xr + (pr * car - pi * cai), xi + (pr * cai + pi * car)
        hre[pl.ds(r0, SUBLANES), :] = xr
        him[pl.ds(r0, SUBLANES), :] = xi
        return xr[SUBLANES - 1:SUBLANES, :], xi[SUBLANES - 1:SUBLANES, :]

    car, cai = lax.fori_loop(0, tc // SUBLANES, group, (cr[...], ci[...]))
    cr[...] = car
    ci[...] = cai
    sre_ref[...] = car
    sim_ref[...] = cai
    n_out = d_ssm // LANES
    span = sd // n_out
    y = jnp.concatenate(
        [jnp.dot(hre[:, j * span:(j + 1) * span].astype(BF16), ccre_ref[j], preferred_element_type=F32)
         + jnp.dot(him[:, j * span:(j + 1) * span].astype(BF16), ccim_ref[j], preferred_element_type=F32)
         for j in range(n_out)], axis=1)
    out_ref[...] = _s5_readout(y, u, dsk_ref, wglu_ref, bglu_ref, d_model)


def _s5_sample_kernel(u_ref, h0r_ref, h0i_ref, are_ref, aim_ref, bre_ref, bim_ref, cre_ref, cim_ref,
                      dsk_ref, wglu_ref, bglu_ref, out_ref, sre_ref, sim_ref, *, d_model):
    u = u_ref[...]
    ub = u.astype(BF16)
    ar, ai = are_ref[...], aim_ref[...]
    h0r, h0i = h0r_ref[...], h0i_ref[...]
    hr = (ar * h0r - ai * h0i) + jnp.dot(ub, bre_ref[...], preferred_element_type=F32)
    hi = (ar * h0i + ai * h0r) + jnp.dot(ub, bim_ref[...], preferred_element_type=F32)
    sre_ref[...] = hr
    sim_ref[...] = hi
    y = (jnp.dot(hr.astype(BF16), cre_ref[...], preferred_element_type=F32)
         + jnp.dot(hi.astype(BF16), cim_ref[...], preferred_element_type=F32))
    out_ref[...] = _s5_readout(y, u, dsk_ref, wglu_ref, bglu_ref, d_model)


def _s5_params(a_re, a_im, log_dt, b_re, b_im, c_re, c_im):
    g, p = a_re.shape
    a_c = lax.complex(a_re.astype(F32), a_im.astype(F32))
    dt = jnp.exp(log_dt.astype(F32))[:, None]
    a_bar = jnp.exp(a_c * dt)
    b_bar = ((a_bar - 1.0) / a_c)[:, :, None] * lax.complex(b_re.astype(F32), b_im.astype(F32))
    eye = jnp.eye(g, dtype=F32)
    n = b_re.shape[2]

    def bmat(b):
        return jnp.einsum('gpn,gh->gnhp', b, eye).reshape(g * n, g * p)

    def cmat(c):
        return jnp.einsum('gnp,gh->gphn', c, eye).reshape(g * p, g * n)

    bre, bim = bmat(jnp.real(b_bar)).astype(BF16), bmat(jnp.imag(b_bar)).astype(BF16)
    cre, cim = cmat(c_re.astype(F32)).astype(BF16), cmat(-c_im.astype(F32)).astype(BF16)
    d_in, sd = bre.shape
    assert d_in % LANES == 0 and sd % LANES == 0 and LANES % (LANES * d_in // sd) == 0
    bcat = jnp.stack([
        jnp.concatenate([m[(c * LANES * d_in // sd) // LANES * LANES:][:LANES, c * LANES:(c + 1) * LANES]
                         for m in (bre, bim)], axis=1)
        for c in range(sd // LANES)], axis=0)
    n_out = d_in // LANES
    span = sd // n_out
    ccre, ccim = (jnp.stack([m[j * span:(j + 1) * span, j * LANES:(j + 1) * LANES] for j in range(n_out)],
                            axis=0) for m in (cre, cim))
    a1 = a_bar.reshape(1, g * p)
    a2 = a1 * a1
    a3 = a2 * a1
    a4 = a2 * a2
    pw = [a1, a2, a3, a4, a4 * a1, a4 * a2, a4 * a3, a4 * a4]
    rows = jnp.arange(SUBLANES)[:, None]
    tabs = []
    for sh in (1, 2, 4):
        t = jnp.where(rows >= sh, jnp.broadcast_to(pw[sh - 1], (SUBLANES, g * p)), 0.0)
        tabs += [jnp.real(t), jnp.imag(t)]
    pcat = jnp.concatenate(pw, axis=0)
    tabs += [jnp.real(pcat), jnp.imag(pcat)]
    tab = jnp.stack(tabs, axis=0).astype(F32)
    return jnp.real(a1), jnp.imag(a1), bre, bim, cre, cim, tab, bcat, ccre, ccim


def _s5_prompt(u3, s5p, dsk, wglu, bglu, tc):
    bsz, t_len, d_ssm = u3.shape
    _, _, bre, _, _, _, tab, bcat, ccre, ccim = s5p
    sd = bre.shape[1]
    d_model = wglu.shape[1] // 2
    kern = functools.partial(_s5_prompt_kernel, tc=tc, d_model=d_model)
    return pl.pallas_call(
        kern, grid=(bsz, t_len // tc),
        in_specs=[pl.BlockSpec((None, tc, d_ssm), lambda b, c: (b, c, 0)),
                  _const_spec(bcat.shape), _const_spec(tab.shape),
                  _const_spec(ccre.shape), _const_spec(ccim.shape), _const_spec((1, d_ssm)),
                  _const_spec(wglu.shape), _const_spec((1, 2 * d_model))],
        out_specs=[pl.BlockSpec((None, tc, d_model), lambda b, c: (b, c, 0)),
                   pl.BlockSpec((None, 1, sd), lambda b, c: (b, 0, 0)),
                   pl.BlockSpec((None, 1, sd), lambda b, c: (b, 0, 0))],
        out_shape=[jax.ShapeDtypeStruct((bsz, t_len, d_model), F32),
                   jax.ShapeDtypeStruct((bsz, 1, sd), F32),
                   jax.ShapeDtypeStruct((bsz, 1, sd), F32)],
        scratch_shapes=[pltpu.VMEM((tc, sd), F32), pltpu.VMEM((tc, sd), F32),
                        pltpu.VMEM((1, sd), F32), pltpu.VMEM((1, sd), F32)],
        compiler_params=pltpu.CompilerParams(dimension_semantics=("arbitrary", "arbitrary"),
                                             vmem_limit_bytes=VMEM_LIMIT),
        name="s5_prompt",
    )(u3, bcat, tab, ccre, ccim, dsk.reshape(1, d_ssm), wglu, bglu.reshape(1, 2 * d_model))


def _s5_sample(u2, h0r, h0i, s5p, dsk, wglu, bglu):
    n, d_ssm = u2.shape
    are, aim, bre, bim, cre, cim = s5p[:6]
    sd = bre.shape[1]
    d_model = wglu.shape[1] // 2
    kern = functools.partial(_s5_sample_kernel, d_model=d_model)
    return pl.pallas_call(
        kern,
        out_shape=[jax.ShapeDtypeStruct((n, d_model), F32),
                   jax.ShapeDtypeStruct((n, sd), F32),
                   jax.ShapeDtypeStruct((n, sd), F32)],
        compiler_params=pltpu.CompilerParams(vmem_limit_bytes=VMEM_LIMIT),
        name="s5_sample",
    )(u2, h0r, h0i, are, aim, bre, bim, cre, cim, dsk.reshape(1, d_ssm), wglu,
      bglu.reshape(1, 2 * d_model))


def _f2k(x):
    b = lax.bitcast_convert_type(x, I32)
    return jnp.where(b < 0, b ^ 0x7FFFFFFF, b)


def _k2f(k):
    return lax.bitcast_convert_type(jnp.where(k < 0, k ^ 0x7FFFFFFF, k), F32)


def _threshold_search(count_ge, amax, forced, k_top):
    kf = float(k_top)
    lo0 = jnp.where(forced, KEY_LOW, _f2k(-amax))
    hi0 = jnp.where(forced, KEY_LOW + 1, _f2k(amax) + 1)

    def cond(st):
        lo, hi, _ = st
        return jnp.max(jnp.where(hi > lo + 1, 1.0, 0.0)) > 0.5

    def body(st):
        lo, hi, it = st
        active = hi > lo + 1
        mid_i = (lo >> 1) + (hi >> 1) + (lo & hi & 1)
        mk = _f2k(0.5 * _k2f(lo) + 0.5 * _k2f(hi))
        use_f = jnp.logical_and(it % 2 == 0, jnp.logical_and(mk > lo, mk < hi))
        mid = jnp.where(use_f, mk, mid_i)
        c = count_ge(_k2f(mid))
        ge = c >= kf
        nlo = jnp.where(ge, mid, lo)
        nhi = jnp.where(c == kf, mid + 1, jnp.where(ge, hi, mid))
        return jnp.where(active, nlo, lo), jnp.where(active, nhi, hi), it + 1

    lo, _, _ = lax.while_loop(cond, body, (lo0, hi0, jnp.int32(0)))
    return _k2f(lo)


def _tri_exclusive(n):
    r = lax.broadcasted_iota(I32, (n, n), 0)
    c = lax.broadcasted_iota(I32, (n, n), 1)
    return (r < c).astype(BF16)


def _dsa_prompt_kernel(qi_ref, wt_ref, q_ref, ki_ref, k_ref, vt_ref, o_ref,
                       sc_ref, scb_ref, qis_ref, qs_ref, m_ref, acc_ref, *, tq, tk, k_top):
    i = pl.program_id(1)
    nkb = i + 1
    hpg = N_HEADS // N_KV_HEADS
    ngr = tk // SUBLANES
    kf = float(k_top)

    qi = qi_ref[...]
    q = q_ref[...]
    zpad = jnp.zeros((tq, HEAD_DIM), BF16)
    for h in range(N_IDX_HEADS):
        qis_ref[h * tq:(h + 1) * tq, :] = qi[:, h * IDX_DIM:(h + 1) * IDX_DIM]
    for h in range(N_HEADS):
        qh = q[:, h * HEAD_DIM:(h + 1) * HEAD_DIM]
        pair = [qh, zpad] if h // hpg == 0 else [zpad, qh]
        qs_ref[h * tq:(h + 1) * tq, :] = jnp.concatenate(pair, axis=1)

    w8 = wt_ref[...] * (IDX_DIM ** -0.5)
    qpos = i * tq + lax.broadcasted_iota(I32, (SUBLANES, tq), 1)

    def score_blk(j, amx, diagonal):
        k0 = pl.multiple_of(j * tk, tk)
        d = lax.dot_general(ki_ref[pl.ds(k0, tk), :], qis_ref[...], NT_DIMS, preferred_element_type=F32)
        mags = []
        for c in range(tq // LANES):
            acc = jnp.zeros((tk, LANES), F32)
            for h in range(N_IDX_HEADS):
                cols = slice(h * tq + c * LANES, h * tq + (c + 1) * LANES)
                acc = acc + jnp.maximum(d[:, cols], 0.0) * w8[h:h + 1, c * LANES:(c + 1) * LANES]
            mag = jnp.abs(acc)
            if diagonal:
                kpos = j * tk + lax.broadcasted_iota(I32, (tk, LANES), 0)
                causal = kpos <= i * tq + c * LANES + lax.broadcasted_iota(I32, (tk, LANES), 1)
                acc = jnp.where(causal, acc, NEG_INF)
                mag = jnp.where(causal, mag, 0.0)
            sc_ref[j, :, c * LANES:(c + 1) * LANES] = acc
            top = lax.bitcast_convert_type(lax.bitcast_convert_type(acc, I32) & jnp.int32(-65536), F32)
            scb_ref[j, :, c * LANES:(c + 1) * LANES] = top.astype(BF16)
            mags.append(jnp.max(mag.reshape(ngr, SUBLANES, LANES), axis=0))
        return jnp.maximum(amx, jnp.concatenate(mags, axis=1))

    amx = lax.fori_loop(0, nkb - 1, lambda j, a: score_blk(j, a, False), jnp.zeros((SUBLANES, tq), F32))
    amx = score_blk(nkb - 1, amx, True)

    rep = lambda x: jnp.broadcast_to(x, (SUBLANES, tq))
    amax = rep(jnp.max(amx, axis=0, keepdims=True))
    forced = (qpos + 1) <= k_top

    packed = 2 * SUBLANES
    one_b, zero_b = jnp.ones((), BF16), jnp.zeros((), BF16)

    def count16(k16):
        bits = jnp.where(k16 < 0, k16 ^ 0x7FFF, k16) << 16
        tb = jnp.broadcast_to(lax.bitcast_convert_type(bits, F32)[0:1, :], (packed, tq)).astype(BF16)

        def body(j, part):
            blk = scb_ref[j].reshape(tk // packed // COUNT_WAYS, COUNT_WAYS, packed, tq)
            hit = jnp.where(blk >= tb[None, None], one_b, zero_b)
            for g in range(hit.shape[0]):
                part = part + hit[g]
            return part

        part = lax.fori_loop(0, nkb, body, jnp.zeros((COUNT_WAYS, packed, tq), BF16))
        return rep(jnp.sum(jnp.sum(part.astype(F32), axis=0), axis=0, keepdims=True))

    def coarse_pass(lo16, hi16):
        active = hi16 > lo16 + 1
        mid = lo16 + ((hi16 - lo16) >> 1)
        c = count16(mid)
        ge = c >= kf
        nlo = jnp.where(ge, mid, lo16)
        nhi = jnp.where(c == kf, mid, jnp.where(ge, hi16, mid))
        return jnp.where(active, nlo, lo16), jnp.where(active, nhi, hi16)

    def coarse_body(st):
        lo16, hi16, _ = st
        lo16, hi16 = coarse_pass(*coarse_pass(lo16, hi16))
        return lo16, hi16, (jnp.max(jnp.where(hi16 > lo16 + 1, 1.0, 0.0)) > 0.5).astype(I32)

    lo16 = jnp.where(forced, KEY_LOW >> 16, _f2k(-amax) >> 16)
    hi16 = jnp.where(forced, KEY_LOW >> 16, (_f2k(amax) >> 16) + 1)
    lo16, hi16, _ = lax.while_loop(lambda st: st[2] > 0, coarse_body, (lo16, hi16, jnp.int32(1)))
    lo0 = lo16 << 16
    hi0 = jnp.where(hi16 == lo16, lo0 + 1, hi16 << 16)

    def count(thr, strict):
        def body(j, part):
            blk = sc_ref[j].reshape(ngr // COUNT_WAYS, COUNT_WAYS, SUBLANES, tq)
            hit = (blk > thr[None, None]) if strict else (blk >= thr[None, None])
            return part + jnp.sum(jnp.where(hit, 1.0, 0.0), axis=0)

        part = lax.fori_loop(0, nkb, body, jnp.zeros((COUNT_WAYS, SUBLANES, tq), F32))
        return rep(jnp.sum(jnp.sum(part, axis=0), axis=0, keepdims=True))

    def search_pass(lo, hi, value_space):
        active = hi > lo + 1
        span = hi - lo
        mid = lo + lax.shift_right_logical(span, jnp.ones_like(span))
        if value_space:
            mk = _f2k(0.5 * _k2f(lo) + 0.5 * _k2f(hi))
            mid = jnp.where(jnp.logical_and(mk > lo, mk < hi), mk, mid)
        c = count(_k2f(mid), False)
        ge = c >= kf
        nlo = jnp.where(ge, mid, lo)
        nhi = jnp.where(c == kf, mid + 1, jnp.where(ge, hi, mid))
        return jnp.where(active, nlo, lo), jnp.where(active, nhi, hi)

    def search_body(st):
        lo, hi, _ = st
        lo, hi = search_pass(lo, hi, True)
        lo, hi = search_pass(lo, hi, False)
        return lo, hi, (jnp.max(jnp.where(hi > lo + 1, 1.0, 0.0)) > 0.5).astype(I32)

    lo, _, _ = lax.while_loop(lambda st: st[2] > 0, search_body, (lo0, hi0, jnp.int32(1)))
    thr = _k2f(lo)
    c_ge = count(thr, False)
    thr1 = thr[0:1, :]

    @pl.when(jnp.max(jnp.where(c_ge > kf, 1.0, 0.0)) > 0.5)
    def _():
        need = jnp.where(c_ge > kf, kf - count(thr, True), 1e30)[0:1, :]
        r = lax.broadcasted_iota(I32, (tk, tk), 0)
        c = lax.broadcasted_iota(I32, (tk, tk), 1)
        earlier = (c < r).astype(BF16)
        ones_k = jnp.ones((SUBLANES, tk), BF16)

        def fix(j, run):
            sc = sc_ref[j]
            eq = sc == thr1
            eqb = jnp.where(eq, 1.0, 0.0).astype(BF16)
            before = jnp.dot(earlier, eqb, preferred_element_type=F32) + run[0:1, :]
            sc_ref[j] = jnp.where(jnp.logical_and(eq, before >= need), NEG_INF, sc)
            return run + jnp.dot(ones_k, eqb, preferred_element_type=F32)

        lax.fori_loop(0, nkb, fix, jnp.zeros((SUBLANES, tq), F32))

    m_ref[...] = jnp.full(m_ref.shape, NEG_INF, F32)
    acc_ref[...] = jnp.zeros(acc_ref.shape, F32)

    def att_blk(j, carry):
        k0 = pl.multiple_of(j * tk, tk)
        s = lax.dot_general(k_ref[pl.ds(k0, tk), :], qs_ref[...], NT_DIMS, preferred_element_type=F32)
        bias = jnp.where(sc_ref[j] >= thr1, 0.0, NEG_INF)
        for h in range(N_HEADS):
            x = s[:, h * tq:(h + 1) * tq] + bias
            bm = jnp.max(jnp.max(x.reshape(ngr, SUBLANES, tq), axis=0), axis=0, keepdims=True)
            m_old = m_ref[h:h + 1, :]
            m_new = jnp.maximum(m_old, bm)
            m_safe = jnp.where(m_new == NEG_INF, 0.0, m_new)
            p = jnp.exp2(x - m_safe).astype(BF16)
            pv = jnp.dot(vt_ref[j, h // hpg], p, preferred_element_type=F32)
            acc_ref[h] = acc_ref[h] * jnp.exp2(m_old - m_safe) + pv
            m_ref[h:h + 1, :] = m_new
        return carry

    lax.fori_loop(0, nkb, att_blk, 0)

    for pr in range(N_HEADS // 2):
        outs = []
        for h in (2 * pr, 2 * pr + 1):
            a = acc_ref[h]
            outs.append(a[:HEAD_DIM, :] / a[HEAD_DIM:HEAD_DIM + 1, :])
        o2 = jnp.concatenate(outs, axis=0)
        o_ref[:, 2 * pr * HEAD_DIM:(2 * pr + 2) * HEAD_DIM] = o2.T.astype(o_ref.dtype)


def _dsa_prompt(qi, wt, q, kib, kb, vt, tq, tk):
    bsz, t_len, d_qi = qi.shape
    d_attn = q.shape[2]
    d_kv = kb.shape[2]
    nkb = t_len // tk
    k_top = min(TOPK_MAX, t_len // 4)
    assert tq == tk and tq % LANES == 0 and t_len % tq == 0 and vt.shape[1] == nkb
    assert nkb * (tk // (2 * SUBLANES * COUNT_WAYS)) <= 256
    kern = functools.partial(_dsa_prompt_kernel, tq=tq, tk=tk, k_top=k_top)
    tile = lambda w: pl.BlockSpec((None, tq, w), lambda b, i: (b, i, 0))
    full = lambda w: pl.BlockSpec((None, t_len, w), lambda b, i: (b, 0, 0))
    return pl.pallas_call(
        kern, grid=(bsz, t_len // tq),
        in_specs=[tile(d_qi), pl.BlockSpec((None, N_IDX_HEADS, tq), lambda b, i: (b, 0, i)), tile(d_attn),
                  full(IDX_DIM), full(d_kv),
                  pl.BlockSpec((None, nkb, N_KV_HEADS, VT_ROWS, tk), lambda b, i: (b, 0, 0, 0, 0))],
        out_specs=tile(d_attn),
        out_shape=jax.ShapeDtypeStruct((bsz, t_len, d_attn), BF16),
        scratch_shapes=[pltpu.VMEM((nkb, tk, tq), F32),
                        pltpu.VMEM((nkb, tk, tq), BF16),
                        pltpu.VMEM((N_IDX_HEADS * tq, IDX_DIM), BF16),
                        pltpu.VMEM((N_HEADS * tq, N_KV_HEADS * HEAD_DIM), BF16),
                        pltpu.VMEM((N_HEADS, tq), F32),
                        pltpu.VMEM((N_HEADS, VT_ROWS, tq), F32)],
        compiler_params=pltpu.CompilerParams(dimension_semantics=("arbitrary", "arbitrary"),
                                             vmem_limit_bytes=VMEM_LIMIT),
        name="dsa_prompt",
    )(qi, wt, q, kib, kb, vt)


def _idx_score_kernel(pt_ref, q_ref, w_ref, *refs, n_pg):
    pages, o_ref = refs[:n_pg], refs[n_pg]
    b = pl.program_id(1)
    q = q_ref[...]
    w = w_ref[...] * (IDX_DIM ** -0.5)
    parts = []
    for p in range(n_pg):
        d = jnp.dot(q, pages[p][...].astype(BF16), preferred_element_type=F32)
        parts.append(jnp.sum(jnp.maximum(d, 0.0) * w, axis=0, keepdims=True))
    o_ref[pl.ds(b % SUBLANES, 1), :] = jnp.concatenate(parts, axis=1)


def _idx_scores(page_table, qis, wcol, cache_idx_kt, n_pg):
    sq, n_pages = page_table.shape
    page = cache_idx_kt.shape[2]
    kern = functools.partial(_idx_score_kernel, n_pg=n_pg)
    page_specs = [pl.BlockSpec((None, IDX_DIM, page),
                               lambda j, b, pt, p=p: (pt[b * n_pages + j * n_pg + p], 0, 0))
                  for p in range(n_pg)]
    grid_spec = pltpu.PrefetchScalarGridSpec(
        num_scalar_prefetch=1, grid=(n_pages // n_pg, sq),
        in_specs=[pl.BlockSpec((None, N_IDX_HEADS, IDX_DIM), lambda j, b, pt: (b, 0, 0)),
                  pl.BlockSpec((None, N_IDX_HEADS, 1), lambda j, b, pt: (b, 0, 0))] + page_specs,
        out_specs=pl.BlockSpec((SUBLANES, n_pg * page), lambda j, b, pt: (b // SUBLANES, j)))
    return pl.pallas_call(
        kern, grid_spec=grid_spec,
        out_shape=jax.ShapeDtypeStruct((sq, n_pages * page), F32),
        compiler_params=pltpu.CompilerParams(dimension_semantics=("arbitrary", "arbitrary")),
        name="sample_idx_scores",
    )(page_table.reshape(-1), qis, wcol, *([cache_idx_kt] * n_pg))


def _sample_select_kernel(sc_ref, qi_ref, ki_ref, wi_ref, mask_ref, selfsel_ref, *, ch, k_top):
    sq, l_past = sc_ref.shape
    nch = l_past // ch
    w = wi_ref[...] * (IDX_DIM ** -0.5)
    qf = qi_ref[...].astype(F32)
    kf32 = ki_ref[...].astype(F32)
    s_self = jnp.zeros((sq, 1), F32)
    for h in range(N_IDX_HEADS):
        dh = jnp.sum(qf[:, h * IDX_DIM:(h + 1) * IDX_DIM] * kf32, axis=1, keepdims=True)
        s_self = s_self + jnp.maximum(dh, 0.0) * w[:, h:h + 1]

    amax = jnp.abs(s_self)
    for c in range(nch):
        amax = jnp.maximum(amax, jnp.max(jnp.abs(sc_ref[:, c * ch:(c + 1) * ch]), axis=1, keepdims=True))

    def count_cmp(thr, strict):
        thr_b = jnp.broadcast_to(thr, (sq, LANES))
        part = jnp.zeros((sq, LANES), F32)
        for c in range(l_past // LANES):
            blk = sc_ref[:, c * LANES:(c + 1) * LANES]
            hit = (blk > thr_b) if strict else (blk >= thr_b)
            part = part + jnp.where(hit, 1.0, 0.0)
        self_hit = (s_self > thr) if strict else (s_self >= thr)
        return jnp.sum(part, axis=1, keepdims=True) + jnp.where(self_hit, 1.0, 0.0)

    kf = float(k_top)
    forced = amax < 0.0
    thr = _threshold_search(lambda t: count_cmp(t, False), amax, forced, k_top)
    need = kf - count_cmp(thr, True)
    tri = _tri_exclusive(ch)
    run = jnp.zeros((sq, 1), F32)
    for c in range(nch):
        sc = sc_ref[:, c * ch:(c + 1) * ch]
        eq = sc == thr
        before = jnp.dot(jnp.where(eq, 1.0, 0.0).astype(BF16), tri, preferred_element_type=F32) + run
        keep = jnp.logical_or(sc > thr, jnp.logical_and(eq, before < need))
        mask_ref[:, c * ch:(c + 1) * ch] = jnp.where(keep, 1.0, 0.0)
        run = run + jnp.sum(jnp.where(eq, 1.0, 0.0), axis=1, keepdims=True)
    self_keep = jnp.logical_or(s_self > thr, jnp.logical_and(s_self == thr, run < need))
    selfsel_ref[...] = jnp.where(self_keep, 1.0, 0.0)


def _sample_select(scores, qi, kib, wi, ch, k_top):
    sq, l_past = scores.shape
    kern = functools.partial(_sample_select_kernel, ch=ch, k_top=k_top)
    return pl.pallas_call(
        kern,
        out_shape=[jax.ShapeDtypeStruct((sq, l_past), F32), jax.ShapeDtypeStruct((sq, 1), F32)],
        compiler_params=pltpu.CompilerParams(vmem_limit_bytes=VMEM_LIMIT),
        name="sample_select",
    )(scores, qi, kib, wi)


def _sample_attn_kernel(pt_ref, q_ref, mask_ref, ks_ref, vs_ref, ss_ref, *refs, n_pg):
    kpages, vpages = refs[:n_pg], refs[n_pg:2 * n_pg]
    o_ref, m_scr, l_scr, acc_scr = refs[2 * n_pg:]
    j = pl.program_id(1)
    hpg = N_HEADS // N_KV_HEADS

    @pl.when(j == 0)
    def _():
        m_scr[...] = jnp.full(m_scr.shape, NEG_INF, F32)
        l_scr[...] = jnp.zeros(l_scr.shape, F32)
        acc_scr[...] = jnp.zeros(acc_scr.shape, F32)

    q = q_ref[...]
    page = kpages[0].shape[1]
    s = jnp.concatenate(
        [jnp.dot(q, kpages[p][...].astype(BF16), preferred_element_type=F32)
         for p in range(n_pg)], axis=1)
    sm = jnp.where(mask_ref[...] > 0.5, s, NEG_INF)
    m_old = m_scr[...]
    m_new = jnp.maximum(m_old, jnp.max(sm, axis=1, keepdims=True))
    m_safe = jnp.where(m_new == NEG_INF, 0.0, m_new)
    p_ = jnp.exp2(sm - m_safe)
    alpha = jnp.exp2(m_old - m_safe)
    pb = p_.astype(BF16)
    pv = jnp.zeros(acc_scr.shape, F32)
    for p in range(n_pg):
        pv = pv + lax.dot_general(pb[:, p * page:(p + 1) * page], vpages[p][...].astype(BF16), NT_DIMS,
                                  preferred_element_type=F32)
    l_scr[...] = l_scr[...] * alpha + jnp.sum(pb.astype(F32), axis=1, keepdims=True)
    acc_scr[...] = acc_scr[...] * alpha + pv
    m_scr[...] = m_new

    @pl.when(j == pl.num_programs(1) - 1)
    def _():
        s_self = jnp.sum(q.astype(F32) * ks_ref[...].astype(F32), axis=1, keepdims=True)
        s_self = jnp.where(ss_ref[...] > 0.5, s_self, NEG_INF)
        m_o = m_scr[...]
        m_n = jnp.maximum(m_o, s_self)
        m_s = jnp.where(m_n == NEG_INF, 0.0, m_n)
        p_self = jnp.exp2(s_self - m_s).astype(BF16).astype(F32)
        al = jnp.exp2(m_o - m_s)
        l_fin = l_scr[...] * al + p_self
        acc = acc_scr[...] * al + p_self * vs_ref[...].astype(F32)
        o = acc / l_fin
        hrow = lax.broadcasted_iota(I32, o.shape, 0)
        o = jnp.where(hrow < hpg, o, pltpu.roll(o, HEAD_DIM, axis=1))
        o_ref[...] = o[:, :HEAD_DIM].astype(o_ref.dtype)


def _sample_attn(page_table, qs, mask3, kself, vself, selfsel, cache_k2, cache_v2, n_pg):
    sq, n_pages = page_table.shape
    d_kv, page = cache_k2.shape[1], cache_k2.shape[2]
    kern = functools.partial(_sample_attn_kernel, n_pg=n_pg)
    pspec = lambda p: pl.BlockSpec((None, d_kv, page),
                                   lambda b, j, pt: (pt[b * n_pages + j * n_pg + p], 0, 0))
    per_seq = lambda r, w: pl.BlockSpec((None, r, w), lambda b, j, pt: (b, 0, 0))
    grid_spec = pltpu.PrefetchScalarGridSpec(
        num_scalar_prefetch=1, grid=(sq, n_pages // n_pg),
        in_specs=[per_seq(N_HEADS, d_kv),
                  pl.BlockSpec((None, 1, n_pg * page), lambda b, j, pt: (b, 0, j)),
                  per_seq(1, d_kv), per_seq(1, d_kv), per_seq(1, 1)]
                 + [pspec(p) for p in range(n_pg)] + [pspec(p) for p in range(n_pg)],
        out_specs=per_seq(N_HEADS, HEAD_DIM),
        scratch_shapes=[pltpu.VMEM((N_HEADS, 1), F32), pltpu.VMEM((N_HEADS, 1), F32),
                        pltpu.VMEM((N_HEADS, d_kv), F32)])
    return pl.pallas_call(
        kern, grid_spec=grid_spec,
        out_shape=jax.ShapeDtypeStruct((sq, N_HEADS, HEAD_DIM), BF16),
        compiler_params=pltpu.CompilerParams(dimension_semantics=("arbitrary", "arbitrary")),
        name="sample_attn",
    )(page_table.reshape(-1), qs, mask3, kself, vself, selfsel,
      *([cache_k2] * n_pg), *([cache_v2] * n_pg))


def _tail_kernel(x_ref, attn_ref, ssm_ref, sgs_ref, sga_ref, wao_ref, wo_ref, g2_ref, wup_ref, wdn_ref,
                 gf_ref, y_ref):
    attn_out = jnp.dot(attn_ref[...], wao_ref[...], preferred_element_type=F32)
    mix = sgs_ref[...] * ssm_ref[...] + sga_ref[...] * attn_out
    x1 = x_ref[...] + jnp.dot(mix.astype(BF16), wo_ref[...], preferred_element_type=F32)
    hh = _rms_norm(x1, g2_ref[...]).astype(BF16)
    up = jnp.dot(hh, wup_ref[...], preferred_element_type=F32)
    r = jnp.square(jnp.maximum(up, 0.0)).astype(BF16)
    x2 = x1 + jnp.dot(r, wdn_ref[...], preferred_element_type=F32)
    y_ref[...] = _rms_norm(x2, gf_ref[...])


def _tail(x2d, attn, ssm, sgs, sga, wao, wo, g2, wup, wdn, gf, tm):
    n, d_model = x2d.shape
    row = lambda w: pl.BlockSpec((tm, w), lambda i: (i, 0))
    return pl.pallas_call(
        _tail_kernel, grid=(n // tm,),
        in_specs=[row(d_model), row(attn.shape[1]), row(d_model), row(d_model), row(d_model),
                  _const_spec(wao.shape), _const_spec(wo.shape), _const_spec((1, d_model)),
                  _const_spec(wup.shape), _const_spec(wdn.shape), _const_spec((1, d_model))],
        out_specs=row(d_model),
        out_shape=jax.ShapeDtypeStruct((n, d_model), F32),
        compiler_params=pltpu.CompilerParams(dimension_semantics=("arbitrary",),
                                             vmem_limit_bytes=VMEM_LIMIT),
        name="tail",
    )(x2d, attn, ssm, sgs, sga, wao, wo, g2.reshape(1, d_model), wup, wdn, gf.reshape(1, d_model))


def _tiles(n_rows, t_len):
    tm = min(256, n_rows)
    tc = min(256, t_len)
    tq = min(256, t_len)
    return tm, tc, tq, tq


def kernel(x_prompt, x_sample, cache_k, cache_v, cache_idx_k, state_ssm_re, state_ssm_im, page_table,
           norm1_g, w_in, ssm_a_re, ssm_a_im, ssm_log_dt, ssm_b_re, ssm_b_im, ssm_c_re, ssm_c_im,
           ssm_d, w_glu, b_glu, w_attn_out, w_o, norm2_g, w_up, w_down, normf_g):
    bsz, t_len, d_model = x_prompt.shape
    sq, s_len, _ = x_sample.shape
    assert s_len == 1, "the sample path handles one new token per sequence"
    n_pool, page = cache_idx_k.shape[0], cache_idx_k.shape[1]
    n_pages = page_table.shape[1]
    past = n_pages * page
    d_ssm = ssm_d.shape[0]
    d_attn = N_HEADS * HEAD_DIM
    d_kv = N_KV_HEADS * HEAD_DIM
    d_qi = N_IDX_HEADS * IDX_DIM
    dims = (d_ssm, d_attn, d_kv, d_qi)

    c_ki = d_ssm + d_attn + 2 * d_kv + d_qi
    c_g = c_ki + IDX_DIM + N_IDX_HEADS
    w_pack = jnp.concatenate(
        [w_in[:, :c_g], jnp.zeros((d_model, LANES - IDX_DIM - N_IDX_HEADS), w_in.dtype), w_in[:, c_g:]],
        axis=1).astype(BF16)
    wglu_b, wao_b, wo_b = w_glu.astype(BF16), w_attn_out.astype(BF16), w_o.astype(BF16)
    wup_b, wdn_b = w_up.astype(BF16), w_down.astype(BF16)
    s5p = _s5_params(ssm_a_re, ssm_a_im, ssm_log_dt, ssm_b_re, ssm_b_im, ssm_c_re, ssm_c_im)

    n_p = bsz * t_len
    tm, tc, tq, tk = _tiles(n_p, t_len)
    xp = x_prompt.reshape(n_p, d_model)
    tabs_p = _rope_tables(jnp.arange(t_len, dtype=I32), t_len)
    assert tm == tk
    (u, q, kt, kb, vt, _, vtb, qi, kit, kib, _, wit, sgs, sga) = _in_proj(
        xp, bsz, tabs_p, norm1_g, w_pack, tm, dims)
    ssm_out, re_p, im_p = _s5_prompt(u.reshape(bsz, t_len, d_ssm), s5p, ssm_d, wglu_b, b_glu, tc)
    r3 = lambda a: a.reshape(bsz, t_len, a.shape[-1])
    attn = _dsa_prompt(r3(qi), wit, r3(q), r3(kib), r3(kb), vtb, tq, tk)
    y_p = _tail(xp, attn.reshape(n_p, d_attn), ssm_out.reshape(n_p, d_model), sgs, sga,
                wao_b, wo_b, norm2_g, wup_b, wdn_b, normf_g, tm)

    xs = x_sample.reshape(sq, d_model)
    tabs_s = _rope_tables(jnp.full((sq,), past, I32), sq)
    (u_s, q_s, kt_s, kb_s, vt_s, vb_s, _, qi_s, kit_s, kib_s, wi_s, _, sgs_s, sga_s) = _in_proj(
        xs, 1, tabs_s, norm1_g, w_pack, sq, dims)
    sd = state_ssm_re.shape[1] * state_ssm_re.shape[2]
    ssm_s, re_s, im_s = _s5_sample(u_s, state_ssm_re.reshape(sq, sd), state_ssm_im.reshape(sq, sd),
                                   s5p, ssm_d, wglu_b, b_glu)
    n_pg = math.gcd(n_pages, 16)
    scores = _idx_scores(page_table, qi_s.reshape(sq, N_IDX_HEADS, IDX_DIM),
                         wi_s.reshape(sq, N_IDX_HEADS, 1), jnp.transpose(cache_idx_k, (0, 2, 1)), n_pg)
    k_top_s = min(TOPK_MAX, (past + s_len) // 4)
    mask, selfsel = _sample_select(scores, qi_s, kib_s, wi_s, min(512, past), k_top_s)
    q4 = q_s.reshape(sq, N_KV_HEADS, N_HEADS // N_KV_HEADS, HEAD_DIM)
    qs_pad = (q4[:, :, :, None, :] * jnp.eye(N_KV_HEADS, dtype=BF16)[None, :, None, :, None]
              ).reshape(sq, N_HEADS, d_kv)
    feat_major = lambda c: jnp.transpose(c, (0, 2, 3, 1)).reshape(n_pool, d_kv, page)
    attn_s = _sample_attn(page_table, qs_pad, mask.reshape(sq, 1, past),
                          kb_s.reshape(sq, 1, d_kv), vb_s.reshape(sq, 1, d_kv), selfsel.reshape(sq, 1, 1),
                          feat_major(cache_k), feat_major(cache_v), n_pg)
    y_s = _tail(xs, attn_s.reshape(sq, d_attn), ssm_s, sgs_s, sga_s,
                wao_b, wo_b, norm2_g, wup_b, wdn_b, normf_g, sq)

    g_ssm, p_ssm = state_ssm_re.shape[1], state_ssm_re.shape[2]
    kv_out = lambda a: jnp.transpose(a.reshape(a.shape[0], N_KV_HEADS, HEAD_DIM, a.shape[2]), (0, 3, 1, 2))
    return (y_p.reshape(bsz, t_len, d_model), y_s.reshape(sq, s_len, d_model),
            kv_out(kt), kv_out(vt), jnp.transpose(kit, (0, 2, 1)),
            re_p.reshape(bsz, g_ssm, p_ssm), im_p.reshape(bsz, g_ssm, p_ssm),
            kv_out(kt_s).reshape(sq, s_len, N_KV_HEADS, HEAD_DIM),
            kv_out(vt_s).reshape(sq, s_len, N_KV_HEADS, HEAD_DIM),
            jnp.transpose(kit_s, (0, 2, 1)).reshape(sq, s_len, IDX_DIM),
            re_s.reshape(sq, g_ssm, p_ssm), im_s.reshape(sq, g_ssm, p_ssm))
```

```python
import functools
import math

import jax
import jax.numpy as jnp
from jax import lax
from jax.experimental import pallas as pl
from jax.experimental.pallas import tpu as pltpu

F32 = jnp.float32
BF16 = jnp.bfloat16
I32 = jnp.int32

SSM_GROUP = 16
SSM_STATE = 64
N_HEADS = 8
N_KV_HEADS = 2
HEAD_DIM = 64
ROT_DIM = HEAD_DIM // 4
N_IDX_HEADS = 8
IDX_DIM = 64
IDX_ROT_DIM = IDX_DIM // 4
ROPE_THETA = 500000.0
TOPK_MAX = 256
EPS = 1e-6
LOG2_E = math.log2(math.e)

LANES = 128
SUBLANES = 8
MXU_COLS = 256
S5_SEQS = 8
S5_SCAN_COLS = 4
COUNT_WAYS = 4
VT_ROWS = 80
VMEM_LIMIT = 56 * 1024 * 1024

NEG_INF = float("-inf")
FLT_MAX = float(jnp.finfo(jnp.float32).max)
KEY_LOW = -2139095040
NT_DIMS = (((1,), (1,)), ((), ()))


def _const_spec(shape):
    nd = len(shape)
    return pl.BlockSpec(shape, lambda *_: (0,) * nd, pipeline_mode=pl.Buffered(1))


def _rms_norm(x, g):
    ms = jnp.mean(x * x, axis=-1, keepdims=True)
    return x * lax.rsqrt(ms + EPS) * g


def _sigmoid(x):
    return 1.0 / (1.0 + jnp.exp(-x))


def _rope(x, cos_t, sin_a, sin_b):
    return (x * cos_t + pltpu.roll(x, LANES - ROT_DIM // 2, axis=1) * sin_a
            + pltpu.roll(x, ROT_DIM // 2, axis=1) * sin_b)


def _in_proj_kernel(x_ref, g_ref, w_ref, cos_ref, sa_ref, sb_ref,
                    u_ref, q_ref, kt_ref, kb_ref, vt_ref, vb_ref, vtb_ref, qi_ref, kit_ref, kib_ref,
                    wi_ref, wit_ref, sgs_ref, sga_ref, *, d_ssm, d_attn, d_kv, d_qi, d_model):
    assert d_kv == LANES
    tm = x_ref.shape[0]
    h = _rms_norm(x_ref[...], g_ref[...]).astype(BF16)
    cos_t, sin_a, sin_b = cos_ref[...], sa_ref[...], sb_ref[...]

    def proj(c0, width):
        return jnp.dot(h, w_ref[:, c0:c0 + width], preferred_element_type=F32)

    off = 0
    u_ref[...] = proj(off, d_ssm)
    off += d_ssm
    def rope_chunks(c0, width):
        for m in range(width // MXU_COLS):
            wide = proj(c0 + m * MXU_COLS, MXU_COLS)
            for c in range(MXU_COLS // LANES):
                yield (m * MXU_COLS // LANES + c,
                       _rope(wide[:, c * LANES:(c + 1) * LANES], cos_t, sin_a, sin_b))

    for c, r in rope_chunks(off, d_attn):
        q_ref[:, c * LANES:(c + 1) * LANES] = (r * (HEAD_DIM ** -0.5 * LOG2_E)).astype(BF16)
    off += d_attn
    kv = proj(off, 2 * d_kv)
    r = _rope(kv[:, :d_kv], cos_t, sin_a, sin_b)
    kt_ref[...] = r.T
    kb_ref[...] = r.astype(BF16)
    off += d_kv
    vv = kv[:, d_kv:]
    vt = vv.T
    vt_ref[...] = vt
    vb_ref[...] = vv.astype(BF16)
    sub = lax.broadcasted_iota(I32, (VT_ROWS - HEAD_DIM, tm), 0)
    ones_pad = jnp.where(sub == 0, 1.0, 0.0)
    for g in range(N_KV_HEADS):
        vtb_ref[g] = jnp.concatenate([vt[g * HEAD_DIM:(g + 1) * HEAD_DIM, :], ones_pad], axis=0).astype(BF16)
    off += d_kv
    for c, r in rope_chunks(off, d_qi):
        qi_ref[:, c * LANES:(c + 1) * LANES] = r.astype(BF16)
    off += d_qi
    kw = proj(off, LANES)
    lane = lax.broadcasted_iota(I32, kw.shape, 1)
    kr = jnp.where(lane < IDX_DIM, _rope(kw, cos_t, sin_a, sin_b), kw * (N_IDX_HEADS ** -0.5))
    krt = kr.T
    kit_ref[...] = krt[:IDX_DIM, :]
    kib_ref[...] = kr[:, :IDX_DIM].astype(BF16)
    wi_ref[...] = kr[:, IDX_DIM:IDX_DIM + N_IDX_HEADS]
    wit_ref[...] = krt[IDX_DIM:IDX_DIM + N_IDX_HEADS, :]
    off += LANES
    sgs_ref[...] = _sigmoid(proj(off, d_model))
    off += d_model
    sga_ref[...] = _sigmoid(proj(off, d_model))


def _rope_tables(pos, n_rows):
    half = ROT_DIM // 2
    inv = ROPE_THETA ** (-jnp.arange(half, dtype=F32) / half)
    ang = pos.astype(F32)[:, None] * inv[None, :]
    cos, sin = jnp.cos(ang), jnp.sin(ang)
    ones = jnp.ones((n_rows, HEAD_DIM - ROT_DIM), F32)
    zeros = jnp.zeros((n_rows, HEAD_DIM - ROT_DIM), F32)
    zh = jnp.zeros((n_rows, half), F32)
    cos_t = jnp.concatenate([cos, cos, ones], axis=1)
    sin_a = jnp.concatenate([-sin, zh, zeros], axis=1)
    sin_b = jnp.concatenate([zh, sin, zeros], axis=1)
    rep = LANES // HEAD_DIM
    return tuple(jnp.tile(t, (1, rep)) for t in (cos_t, sin_a, sin_b))


def _in_proj(x2d, n_seq, pos_tab, norm_g, w_pack, tm, dims):
    n, d_model = x2d.shape
    d_ssm, d_attn, d_kv, d_qi = dims
    cos_t, sin_a, sin_b = pos_tab
    t_len = n // n_seq
    nt = t_len // tm
    grid = (n // tm,)
    row = lambda w: pl.BlockSpec((tm, w), lambda i: (i, 0))
    tab = pl.BlockSpec((tm, LANES), lambda i: (i % nt, 0))
    feat = lambda r: pl.BlockSpec((None, r, tm), lambda i: (i // nt, 0, i % nt))
    kern = functools.partial(_in_proj_kernel, d_ssm=d_ssm, d_attn=d_attn, d_kv=d_kv, d_qi=d_qi,
                             d_model=d_model)
    outs = [
        (jax.ShapeDtypeStruct((n, d_ssm), F32), row(d_ssm)),
        (jax.ShapeDtypeStruct((n, d_attn), BF16), row(d_attn)),
        (jax.ShapeDtypeStruct((n_seq, d_kv, t_len), F32), feat(d_kv)),
        (jax.ShapeDtypeStruct((n, d_kv), BF16), row(d_kv)),
        (jax.ShapeDtypeStruct((n_seq, d_kv, t_len), F32), feat(d_kv)),
        (jax.ShapeDtypeStruct((n, d_kv), BF16), row(d_kv)),
        (jax.ShapeDtypeStruct((n_seq, nt, N_KV_HEADS, VT_ROWS, tm), BF16),
         pl.BlockSpec((None, None, N_KV_HEADS, VT_ROWS, tm), lambda i: (i // nt, i % nt, 0, 0, 0))),
        (jax.ShapeDtypeStruct((n, d_qi), BF16), row(d_qi)),
        (jax.ShapeDtypeStruct((n_seq, IDX_DIM, t_len), F32), feat(IDX_DIM)),
        (jax.ShapeDtypeStruct((n, IDX_DIM), BF16), row(IDX_DIM)),
        (jax.ShapeDtypeStruct((n, N_IDX_HEADS), F32), row(N_IDX_HEADS)),
        (jax.ShapeDtypeStruct((n_seq, N_IDX_HEADS, t_len), F32), feat(N_IDX_HEADS)),
        (jax.ShapeDtypeStruct((n, d_model), F32), row(d_model)),
        (jax.ShapeDtypeStruct((n, d_model), F32), row(d_model)),
    ]
    out_shapes = [o[0] for o in outs]
    out_specs = [o[1] for o in outs]
    return pl.pallas_call(
        kern, grid=grid,
        in_specs=[row(d_model), _const_spec((1, d_model)), _const_spec(w_pack.shape), tab, tab, tab],
        out_specs=out_specs, out_shape=out_shapes,
        compiler_params=pltpu.CompilerParams(dimension_semantics=("arbitrary",),
                                             vmem_limit_bytes=VMEM_LIMIT),
        name="in_proj",
    )(x2d, norm_g.reshape(1, d_model), w_pack, cos_t, sin_a, sin_b)


def _s5_readout(y, u, dsk_ref, wglu_ref, bglu_ref, d_model):
    y = y + dsk_ref[...] * u
    cdf = 0.5 * (1.0 + jnp.tanh(math.sqrt(2.0 / math.pi) * (y + 0.044715 * (y * y * y))))
    gl = (y * cdf).astype(BF16)
    z = jnp.dot(gl, wglu_ref[...], preferred_element_type=F32) + bglu_ref[...]
    return z[:, :d_model] * _sigmoid(z[:, d_model:])


def _s5_prompt_kernel(u_ref, perm_ref, bcat_ref, are_ref, aim_ref, ccre_ref, ccim_ref, dsk_ref, wglu_ref,
                      bglu_ref, out_ref, sre_ref, sim_ref, hre, him, zs, cr, ci, *, tc, d_model):
    @pl.when(pl.program_id(1) == 0)
    def _():
        cr[...] = jnp.zeros_like(cr)
        ci[...] = jnp.zeros_like(ci)

    nseq = u_ref.shape[0]
    d_ssm, n_col = u_ref.shape[2], hre.shape[0]
    sd = n_col * LANES
    u = u_ref[...].reshape(nseq * tc, d_ssm)
    ub = jnp.dot(perm_ref[...], u.astype(BF16), preferred_element_type=F32).astype(BF16)
    for c in range(n_col):
        kc = (c * LANES * d_ssm // sd) // LANES
        bu = jnp.dot(ub[:, kc * LANES:(kc + 1) * LANES], bcat_ref[c], preferred_element_type=F32)
        hre[c] = bu[:, :LANES]
        him[c] = bu[:, LANES:]

    for c0 in range(0, n_col, S5_SCAN_COLS):
        cs = range(c0, c0 + S5_SCAN_COLS)
        ar = [jnp.broadcast_to(are_ref[:, c * LANES:(c + 1) * LANES], (nseq, LANES)) for c in cs]
        ai = [jnp.broadcast_to(aim_ref[:, c * LANES:(c + 1) * LANES], (nseq, LANES)) for c in cs]

        def step(t, carry, cs=cs, ar=ar, ai=ai):
            rows = pl.ds(pl.multiple_of(t * nseq, nseq), nseq)
            out = []
            for n, c in enumerate(cs):
                pr, pi = carry[n]
                xr = hre[c, rows, :] + (ar[n] * pr - ai[n] * pi)
                xi = him[c, rows, :] + (ar[n] * pi + ai[n] * pr)
                hre[c, rows, :] = xr
                him[c, rows, :] = xi
                out.append((xr, xi))
            return tuple(out)

        init = tuple((cr[:, c * LANES:(c + 1) * LANES], ci[:, c * LANES:(c + 1) * LANES]) for c in cs)
        fin = lax.fori_loop(0, tc, step, init, unroll=4)
        for n, c in enumerate(cs):
            cr[:, c * LANES:(c + 1) * LANES] = fin[n][0]
            ci[:, c * LANES:(c + 1) * LANES] = fin[n][1]
    sre_ref[...] = cr[...]
    sim_ref[...] = ci[...]
    n_out = d_ssm // LANES
    per = n_col // n_out
    wide = lambda ref, j: jnp.concatenate([ref[c] for c in range(j * per, (j + 1) * per)], axis=1).astype(BF16)
    for j in range(n_out):
        zs[j] = (jnp.dot(wide(hre, j), ccre_ref[j], preferred_element_type=F32)
                 + jnp.dot(wide(him, j), ccim_ref[j], preferred_element_type=F32))
    y = jnp.concatenate(
        [jnp.concatenate([zs[j, pl.ds(b, tc, stride=nseq), :] for b in range(nseq)], axis=0)
         for j in range(n_out)], axis=1)
    out_ref[...] = _s5_readout(y, u, dsk_ref, wglu_ref, bglu_ref, d_model).reshape(nseq, tc, d_model)


def _s5_sample_kernel(u_ref, h0r_ref, h0i_ref, are_ref, aim_ref, bre_ref, bim_ref, cre_ref, cim_ref,
                      dsk_ref, wglu_ref, bglu_ref, out_ref, sre_ref, sim_ref, *, d_model):
    u = u_ref[...]
    ub = u.astype(BF16)
    ar, ai = are_ref[...], aim_ref[...]
    h0r, h0i = h0r_ref[...], h0i_ref[...]
    hr = (ar * h0r - ai * h0i) + jnp.dot(ub, bre_ref[...], preferred_element_type=F32)
    hi = (ar * h0i + ai * h0r) + jnp.dot(ub, bim_ref[...], preferred_element_type=F32)
    sre_ref[...] = hr
    sim_ref[...] = hi
    y = (jnp.dot(hr.astype(BF16), cre_ref[...], preferred_element_type=F32)
         + jnp.dot(hi.astype(BF16), cim_ref[...], preferred_element_type=F32))
    out_ref[...] = _s5_readout(y, u, dsk_ref, wglu_ref, bglu_ref, d_model)


def _s5_params(a_re, a_im, log_dt, b_re, b_im, c_re, c_im):
    g, p = a_re.shape
    a_c = lax.complex(a_re.astype(F32), a_im.astype(F32))
    dt = jnp.exp(log_dt.astype(F32))[:, None]
    a_bar = jnp.exp(a_c * dt)
    b_bar = ((a_bar - 1.0) / a_c)[:, :, None] * lax.complex(b_re.astype(F32), b_im.astype(F32))
    eye = jnp.eye(g, dtype=F32)
    n = b_re.shape[2]

    def bmat(b):
        return jnp.einsum('gpn,gh->gnhp', b, eye).reshape(g * n, g * p)

    def cmat(c):
        return jnp.einsum('gnp,gh->gphn', c, eye).reshape(g * p, g * n)

    bre, bim = bmat(jnp.real(b_bar)).astype(BF16), bmat(jnp.imag(b_bar)).astype(BF16)
    cre, cim = cmat(c_re.astype(F32)).astype(BF16), cmat(-c_im.astype(F32)).astype(BF16)
    d_in, sd = bre.shape
    assert d_in % LANES == 0 and sd % LANES == 0 and LANES % (LANES * d_in // sd) == 0
    bcat = jnp.stack([
        jnp.concatenate([m[(c * LANES * d_in // sd) // LANES * LANES:][:LANES, c * LANES:(c + 1) * LANES]
                         for m in (bre, bim)], axis=1)
        for c in range(sd // LANES)], axis=0)
    n_out = d_in // LANES
    span = sd // n_out
    ccre, ccim = (jnp.stack([m[j * span:(j + 1) * span, j * LANES:(j + 1) * LANES] for j in range(n_out)],
                            axis=0) for m in (cre, cim))
    a1 = a_bar.reshape(1, g * p)
    return jnp.real(a1), jnp.imag(a1), bre, bim, cre, cim, bcat, ccre, ccim


def _s5_prompt(u3, s5p, dsk, wglu, bglu, tc):
    bsz, t_len, d_ssm = u3.shape
    are, aim, bre, _, _, _, bcat, ccre, ccim = s5p
    sd = bre.shape[1]
    d_model = wglu.shape[1] // 2
    nseq = math.gcd(bsz, S5_SEQS)
    assert sd % (S5_SCAN_COLS * LANES) == 0 and tc % SUBLANES == 0
    kern = functools.partial(_s5_prompt_kernel, tc=tc, d_model=d_model)
    state = pl.BlockSpec((nseq, sd), lambda b, c: (b, 0))
    dst = jnp.arange(nseq * tc)
    perm = (jnp.arange(nseq * tc)[None, :] == ((dst % nseq) * tc + dst // nseq)[:, None]).astype(BF16)
    return pl.pallas_call(
        kern, grid=(bsz // nseq, t_len // tc),
        in_specs=[pl.BlockSpec((nseq, tc, d_ssm), lambda b, c: (b, c, 0)), _const_spec(perm.shape),
                  _const_spec(bcat.shape), _const_spec(are.shape), _const_spec(aim.shape),
                  _const_spec(ccre.shape), _const_spec(ccim.shape), _const_spec((1, d_ssm)),
                  _const_spec(wglu.shape), _const_spec((1, 2 * d_model))],
        out_specs=[pl.BlockSpec((nseq, tc, d_model), lambda b, c: (b, c, 0)), state, state],
        out_shape=[jax.ShapeDtypeStruct((bsz, t_len, d_model), F32),
                   jax.ShapeDtypeStruct((bsz, sd), F32),
                   jax.ShapeDtypeStruct((bsz, sd), F32)],
        scratch_shapes=[pltpu.VMEM((sd // LANES, nseq * tc, LANES), F32),
                        pltpu.VMEM((sd // LANES, nseq * tc, LANES), F32),
                        pltpu.VMEM((d_ssm // LANES, nseq * tc, LANES), F32),
                        pltpu.VMEM((nseq, sd), F32), pltpu.VMEM((nseq, sd), F32)],
        compiler_params=pltpu.CompilerParams(dimension_semantics=("arbitrary", "arbitrary"),
                                             vmem_limit_bytes=VMEM_LIMIT),
        name="s5_prompt",
    )(u3, perm, bcat, are, aim, ccre, ccim, dsk.reshape(1, d_ssm), wglu, bglu.reshape(1, 2 * d_model))


def _s5_sample(u2, h0r, h0i, s5p, dsk, wglu, bglu):
    n, d_ssm = u2.shape
    are, aim, bre, bim, cre, cim = s5p[:6]
    sd = bre.shape[1]
    d_model = wglu.shape[1] // 2
    kern = functools.partial(_s5_sample_kernel, d_model=d_model)
    return pl.pallas_call(
        kern,
        out_shape=[jax.ShapeDtypeStruct((n, d_model), F32),
                   jax.ShapeDtypeStruct((n, sd), F32),
                   jax.ShapeDtypeStruct((n, sd), F32)],
        compiler_params=pltpu.CompilerParams(vmem_limit_bytes=VMEM_LIMIT),
        name="s5_sample",
    )(u2, h0r, h0i, are, aim, bre, bim, cre, cim, dsk.reshape(1, d_ssm), wglu,
      bglu.reshape(1, 2 * d_model))


def _f2k(x):
    b = lax.bitcast_convert_type(x, I32)
    return jnp.where(b < 0, b ^ 0x7FFFFFFF, b)


def _k2f(k):
    return lax.bitcast_convert_type(jnp.where(k < 0, k ^ 0x7FFFFFFF, k), F32)


def _threshold_search(count_ge, amax, forced, k_top):
    kf = float(k_top)
    lo0 = jnp.where(forced, KEY_LOW, _f2k(-amax))
    hi0 = jnp.where(forced, KEY_LOW + 1, _f2k(amax) + 1)

    def cond(st):
        lo, hi, _ = st
        return jnp.max(jnp.where(hi > lo + 1, 1.0, 0.0)) > 0.5

    def body(st):
        lo, hi, it = st
        active = hi > lo + 1
        mid_i = (lo >> 1) + (hi >> 1) + (lo & hi & 1)
        mk = _f2k(0.5 * _k2f(lo) + 0.5 * _k2f(hi))
        use_f = jnp.logical_and(it % 2 == 0, jnp.logical_and(mk > lo, mk < hi))
        mid = jnp.where(use_f, mk, mid_i)
        c = count_ge(_k2f(mid))
        ge = c >= kf
        nlo = jnp.where(ge, mid, lo)
        nhi = jnp.where(c == kf, mid + 1, jnp.where(ge, hi, mid))
        return jnp.where(active, nlo, lo), jnp.where(active, nhi, hi), it + 1

    lo, _, _ = lax.while_loop(cond, body, (lo0, hi0, jnp.int32(0)))
    return _k2f(lo)


def _tri_exclusive(n):
    r = lax.broadcasted_iota(I32, (n, n), 0)
    c = lax.broadcasted_iota(I32, (n, n), 1)
    return (r < c).astype(BF16)


def _dsa_prompt_kernel(qi_ref, wt_ref, q_ref, ki_ref, k_ref, vt_ref, o_ref,
                       sc_ref, scb_ref, qis_ref, qs_ref, m_ref, acc_ref, *, tq, tk, k_top):
    i = pl.program_id(1)
    nkb = i + 1
    hpg = N_HEADS // N_KV_HEADS
    ngr = tk // SUBLANES
    kf = float(k_top)

    qi = qi_ref[...]
    q = q_ref[...]
    zpad = jnp.zeros((tq, HEAD_DIM), BF16)
    for h in range(N_IDX_HEADS):
        qis_ref[h * tq:(h + 1) * tq, :] = qi[:, h * IDX_DIM:(h + 1) * IDX_DIM]
    for h in range(N_HEADS):
        qh = q[:, h * HEAD_DIM:(h + 1) * HEAD_DIM]
        pair = [qh, zpad] if h // hpg == 0 else [zpad, qh]
        qs_ref[h * tq:(h + 1) * tq, :] = jnp.concatenate(pair, axis=1)

    w8 = wt_ref[...] * (IDX_DIM ** -0.5)
    qpos = i * tq + lax.broadcasted_iota(I32, (SUBLANES, tq), 1)

    def score_blk(j, amx, diagonal):
        k0 = pl.multiple_of(j * tk, tk)
        d = lax.dot_general(ki_ref[pl.ds(k0, tk), :], qis_ref[...], NT_DIMS, preferred_element_type=F32)
        mags = []
        for c in range(tq // LANES):
            acc = jnp.zeros((tk, LANES), F32)
            for h in range(N_IDX_HEADS):
                cols = slice(h * tq + c * LANES, h * tq + (c + 1) * LANES)
                acc = acc + jnp.maximum(d[:, cols], 0.0) * w8[h:h + 1, c * LANES:(c + 1) * LANES]
            mag = jnp.abs(acc)
            if diagonal:
                kpos = j * tk + lax.broadcasted_iota(I32, (tk, LANES), 0)
                causal = kpos <= i * tq + c * LANES + lax.broadcasted_iota(I32, (tk, LANES), 1)
                acc = jnp.where(causal, acc, NEG_INF)
                mag = jnp.where(causal, mag, 0.0)
            sc_ref[j, :, c * LANES:(c + 1) * LANES] = acc
            top = lax.bitcast_convert_type(lax.bitcast_convert_type(acc, I32) & jnp.int32(-65536), F32)
            scb_ref[j, :, c * LANES:(c + 1) * LANES] = top.astype(BF16)
            mags.append(jnp.max(mag.reshape(ngr, SUBLANES, LANES), axis=0))
        return jnp.maximum(amx, jnp.concatenate(mags, axis=1))

    amx = lax.fori_loop(0, nkb - 1, lambda j, a: score_blk(j, a, False), jnp.zeros((SUBLANES, tq), F32))
    amx = score_blk(nkb - 1, amx, True)

    rep = lambda x: jnp.broadcast_to(x, (SUBLANES, tq))
    amax = rep(jnp.max(amx, axis=0, keepdims=True))
    forced = (qpos + 1) <= k_top

    packed = 2 * SUBLANES
    one_b, zero_b = jnp.ones((), BF16), jnp.zeros((), BF16)

    def count16(k16):
        bits = jnp.where(k16 < 0, k16 ^ 0x7FFF, k16) << 16
        tb = jnp.broadcast_to(lax.bitcast_convert_type(bits, F32)[0:1, :], (packed, tq)).astype(BF16)

        def body(j, part):
            blk = scb_ref[j].reshape(tk // packed // COUNT_WAYS, COUNT_WAYS, packed, tq)
            hit = jnp.where(blk >= tb[None, None], one_b, zero_b)
            for g in range(hit.shape[0]):
                part = part + hit[g]
            return part

        part = lax.fori_loop(0, nkb, body, jnp.zeros((COUNT_WAYS, packed, tq), BF16))
        return rep(jnp.sum(jnp.sum(part.astype(F32), axis=0), axis=0, keepdims=True))

    def coarse_pass(lo16, hi16):
        active = hi16 > lo16 + 1
        mid = lo16 + ((hi16 - lo16) >> 1)
        c = count16(mid)
        ge = c >= kf
        nlo = jnp.where(ge, mid, lo16)
        nhi = jnp.where(c == kf, mid, jnp.where(ge, hi16, mid))
        return jnp.where(active, nlo, lo16), jnp.where(active, nhi, hi16)

    def coarse_body(st):
        lo16, hi16, _ = st
        lo16, hi16 = coarse_pass(*coarse_pass(lo16, hi16))
        return lo16, hi16, (jnp.max(jnp.where(hi16 > lo16 + 1, 1.0, 0.0)) > 0.5).astype(I32)

    lo16 = jnp.where(forced, KEY_LOW >> 16, _f2k(-amax) >> 16)
    hi16 = jnp.where(forced, KEY_LOW >> 16, (_f2k(amax) >> 16) + 1)
    lo16, hi16, _ = lax.while_loop(lambda st: st[2] > 0, coarse_body, (lo16, hi16, jnp.int32(1)))
    lo0 = lo16 << 16
    hi0 = jnp.where(hi16 == lo16, lo0 + 1, hi16 << 16)

    def count(thr, strict):
        def body(j, part):
            blk = sc_ref[j].reshape(ngr // COUNT_WAYS, COUNT_WAYS, SUBLANES, tq)
            hit = (blk > thr[None, None]) if strict else (blk >= thr[None, None])
            return part + jnp.sum(jnp.where(hit, 1.0, 0.0), axis=0)

        part = lax.fori_loop(0, nkb, body, jnp.zeros((COUNT_WAYS, SUBLANES, tq), F32))
        return rep(jnp.sum(jnp.sum(part, axis=0), axis=0, keepdims=True))

    def search_pass(lo, hi, value_space):
        active = hi > lo + 1
        span = hi - lo
        mid = lo + lax.shift_right_logical(span, jnp.ones_like(span))
        if value_space:
            mk = _f2k(0.5 * _k2f(lo) + 0.5 * _k2f(hi))
            mid = jnp.where(jnp.logical_and(mk > lo, mk < hi), mk, mid)
        c = count(_k2f(mid), False)
        ge = c >= kf
        nlo = jnp.where(ge, mid, lo)
        nhi = jnp.where(c == kf, mid + 1, jnp.where(ge, hi, mid))
        return jnp.where(active, nlo, lo), jnp.where(active, nhi, hi)

    def search_body(st):
        lo, hi, _ = st
        lo, hi = search_pass(lo, hi, True)
        lo, hi = search_pass(lo, hi, False)
        return lo, hi, (jnp.max(jnp.where(hi > lo + 1, 1.0, 0.0)) > 0.5).astype(I32)

    lo, _, _ = lax.while_loop(lambda st: st[2] > 0, search_body, (lo0, hi0, jnp.int32(1)))
    thr = _k2f(lo)
    c_ge = count(thr, False)
    thr1 = thr[0:1, :]

    @pl.when(jnp.max(jnp.where(c_ge > kf, 1.0, 0.0)) > 0.5)
    def _():
        need = jnp.where(c_ge > kf, kf - count(thr, True), 1e30)[0:1, :]
        r = lax.broadcasted_iota(I32, (tk, tk), 0)
        c = lax.broadcasted_iota(I32, (tk, tk), 1)
        earlier = (c < r).astype(BF16)
        ones_k = jnp.ones((SUBLANES, tk), BF16)

        def fix(j, run):
            sc = sc_ref[j]
            eq = sc == thr1
            eqb = jnp.where(eq, 1.0, 0.0).astype(BF16)
            before = jnp.dot(earlier, eqb, preferred_element_type=F32) + run[0:1, :]
            sc_ref[j] = jnp.where(jnp.logical_and(eq, before >= need), NEG_INF, sc)
            return run + jnp.dot(ones_k, eqb, preferred_element_type=F32)

        lax.fori_loop(0, nkb, fix, jnp.zeros((SUBLANES, tq), F32))

    m_ref[...] = jnp.full(m_ref.shape, NEG_INF, F32)
    acc_ref[...] = jnp.zeros(acc_ref.shape, F32)

    def att_blk(j, carry):
        k0 = pl.multiple_of(j * tk, tk)
        s = lax.dot_general(k_ref[pl.ds(k0, tk), :], qs_ref[...], NT_DIMS, preferred_element_type=F32)
        bias = jnp.where(sc_ref[j] >= thr1, 0.0, NEG_INF)
        for h in range(N_HEADS):
            x = s[:, h * tq:(h + 1) * tq] + bias
            bm = jnp.max(jnp.max(x.reshape(ngr, SUBLANES, tq), axis=0), axis=0, keepdims=True)
            m_old = m_ref[h:h + 1, :]
            m_new = jnp.maximum(m_old, bm)
            m_safe = jnp.where(m_new == NEG_INF, 0.0, m_new)
            p = jnp.exp2(x - m_safe).astype(BF16)
            pv = jnp.dot(vt_ref[j, h // hpg], p, preferred_element_type=F32)
            acc_ref[h] = acc_ref[h] * jnp.exp2(m_old - m_safe) + pv
            m_ref[h:h + 1, :] = m_new
        return carry

    lax.fori_loop(0, nkb, att_blk, 0)

    for pr in range(N_HEADS // 2):
        outs = []
        for h in (2 * pr, 2 * pr + 1):
            a = acc_ref[h]
            outs.append(a[:HEAD_DIM, :] / a[HEAD_DIM:HEAD_DIM + 1, :])
        o2 = jnp.concatenate(outs, axis=0)
        o_ref[:, 2 * pr * HEAD_DIM:(2 * pr + 2) * HEAD_DIM] = o2.T.astype(o_ref.dtype)


def _dsa_prompt(qi, wt, q, kib, kb, vt, tq, tk):
    bsz, t_len, d_qi = qi.shape
    d_attn = q.shape[2]
    d_kv = kb.shape[2]
    nkb = t_len // tk
    k_top = min(TOPK_MAX, t_len // 4)
    assert tq == tk and tq % LANES == 0 and t_len % tq == 0 and vt.shape[1] == nkb
    assert nkb * (tk // (2 * SUBLANES * COUNT_WAYS)) <= 256
    kern = functools.partial(_dsa_prompt_kernel, tq=tq, tk=tk, k_top=k_top)
    tile = lambda w: pl.BlockSpec((None, tq, w), lambda b, i: (b, i, 0))
    full = lambda w: pl.BlockSpec((None, t_len, w), lambda b, i: (b, 0, 0))
    return pl.pallas_call(
        kern, grid=(bsz, t_len // tq),
        in_specs=[tile(d_qi), pl.BlockSpec((None, N_IDX_HEADS, tq), lambda b, i: (b, 0, i)), tile(d_attn),
                  full(IDX_DIM), full(d_kv),
                  pl.BlockSpec((None, nkb, N_KV_HEADS, VT_ROWS, tk), lambda b, i: (b, 0, 0, 0, 0))],
        out_specs=tile(d_attn),
        out_shape=jax.ShapeDtypeStruct((bsz, t_len, d_attn), BF16),
        scratch_shapes=[pltpu.VMEM((nkb, tk, tq), F32),
                        pltpu.VMEM((nkb, tk, tq), BF16),
                        pltpu.VMEM((N_IDX_HEADS * tq, IDX_DIM), BF16),
                        pltpu.VMEM((N_HEADS * tq, N_KV_HEADS * HEAD_DIM), BF16),
                        pltpu.VMEM((N_HEADS, tq), F32),
                        pltpu.VMEM((N_HEADS, VT_ROWS, tq), F32)],
        compiler_params=pltpu.CompilerParams(dimension_semantics=("arbitrary", "arbitrary"),
                                             vmem_limit_bytes=VMEM_LIMIT),
        name="dsa_prompt",
    )(qi, wt, q, kib, kb, vt)


def _idx_score_kernel(pt_ref, q_ref, w_ref, *refs, n_pg):
    pages, o_ref = refs[:n_pg], refs[n_pg]
    b = pl.program_id(1)
    q = q_ref[...]
    w = w_ref[...] * (IDX_DIM ** -0.5)
    parts = []
    for p in range(n_pg):
        d = jnp.dot(q, pages[p][...].astype(BF16), preferred_element_type=F32)
        parts.append(jnp.sum(jnp.maximum(d, 0.0) * w, axis=0, keepdims=True))
    o_ref[pl.ds(b % SUBLANES, 1), :] = jnp.concatenate(parts, axis=1)


def _idx_scores(page_table, qis, wcol, cache_idx_kt, n_pg):
    sq, n_pages = page_table.shape
    page = cache_idx_kt.shape[2]
    kern = functools.partial(_idx_score_kernel, n_pg=n_pg)
    page_specs = [pl.BlockSpec((None, IDX_DIM, page),
                               lambda j, b, pt, p=p: (pt[b * n_pages + j * n_pg + p], 0, 0))
                  for p in range(n_pg)]
    grid_spec = pltpu.PrefetchScalarGridSpec(
        num_scalar_prefetch=1, grid=(n_pages // n_pg, sq),
        in_specs=[pl.BlockSpec((None, N_IDX_HEADS, IDX_DIM), lambda j, b, pt: (b, 0, 0)),
                  pl.BlockSpec((None, N_IDX_HEADS, 1), lambda j, b, pt: (b, 0, 0))] + page_specs,
        out_specs=pl.BlockSpec((SUBLANES, n_pg * page), lambda j, b, pt: (b // SUBLANES, j)))
    return pl.pallas_call(
        kern, grid_spec=grid_spec,
        out_shape=jax.ShapeDtypeStruct((sq, n_pages * page), F32),
        compiler_params=pltpu.CompilerParams(dimension_semantics=("arbitrary", "arbitrary")),
        name="sample_idx_scores",
    )(page_table.reshape(-1), qis, wcol, *([cache_idx_kt] * n_pg))


def _sample_select_kernel(sc_ref, qi_ref, ki_ref, wi_ref, mask_ref, selfsel_ref, *, ch, k_top):
    sq, l_past = sc_ref.shape
    nch = l_past // ch
    w = wi_ref[...] * (IDX_DIM ** -0.5)
    qf = qi_ref[...].astype(F32)
    kf32 = ki_ref[...].astype(F32)
    s_self = jnp.zeros((sq, 1), F32)
    for h in range(N_IDX_HEADS):
        dh = jnp.sum(qf[:, h * IDX_DIM:(h + 1) * IDX_DIM] * kf32, axis=1, keepdims=True)
        s_self = s_self + jnp.maximum(dh, 0.0) * w[:, h:h + 1]

    amax = jnp.abs(s_self)
    for c in range(nch):
        amax = jnp.maximum(amax, jnp.max(jnp.abs(sc_ref[:, c * ch:(c + 1) * ch]), axis=1, keepdims=True))

    def count_cmp(thr, strict):
        thr_b = jnp.broadcast_to(thr, (sq, LANES))
        part = jnp.zeros((sq, LANES), F32)
        for c in range(l_past // LANES):
            blk = sc_ref[:, c * LANES:(c + 1) * LANES]
            hit = (blk > thr_b) if strict else (blk >= thr_b)
            part = part + jnp.where(hit, 1.0, 0.0)
        self_hit = (s_self > thr) if strict else (s_self >= thr)
        return jnp.sum(part, axis=1, keepdims=True) + jnp.where(self_hit, 1.0, 0.0)

    kf = float(k_top)
    forced = amax < 0.0
    thr = _threshold_search(lambda t: count_cmp(t, False), amax, forced, k_top)
    need = kf - count_cmp(thr, True)
    tri = _tri_exclusive(ch)
    run = jnp.zeros((sq, 1), F32)
    for c in range(nch):
        sc = sc_ref[:, c * ch:(c + 1) * ch]
        eq = sc == thr
        before = jnp.dot(jnp.where(eq, 1.0, 0.0).astype(BF16), tri, preferred_element_type=F32) + run
        keep = jnp.logical_or(sc > thr, jnp.logical_and(eq, before < need))
        mask_ref[:, c * ch:(c + 1) * ch] = jnp.where(keep, 1.0, 0.0)
        run = run + jnp.sum(jnp.where(eq, 1.0, 0.0), axis=1, keepdims=True)
    self_keep = jnp.logical_or(s_self > thr, jnp.logical_and(s_self == thr, run < need))
    selfsel_ref[...] = jnp.where(self_keep, 1.0, 0.0)


def _sample_select(scores, qi, kib, wi, ch, k_top):
    sq, l_past = scores.shape
    kern = functools.partial(_sample_select_kernel, ch=ch, k_top=k_top)
    return pl.pallas_call(
        kern,
        out_shape=[jax.ShapeDtypeStruct((sq, l_past), F32), jax.ShapeDtypeStruct((sq, 1), F32)],
        compiler_params=pltpu.CompilerParams(vmem_limit_bytes=VMEM_LIMIT),
        name="sample_select",
    )(scores, qi, kib, wi)


def _sample_attn_kernel(pt_ref, q_ref, mask_ref, ks_ref, vs_ref, ss_ref, *refs, n_pg):
    kpages, vpages = refs[:n_pg], refs[n_pg:2 * n_pg]
    o_ref, m_scr, l_scr, acc_scr = refs[2 * n_pg:]
    j = pl.program_id(1)
    hpg = N_HEADS // N_KV_HEADS

    @pl.when(j == 0)
    def _():
        m_scr[...] = jnp.full(m_scr.shape, NEG_INF, F32)
        l_scr[...] = jnp.zeros(l_scr.shape, F32)
        acc_scr[...] = jnp.zeros(acc_scr.shape, F32)

    q = q_ref[...]
    page = kpages[0].shape[1]
    s = jnp.concatenate(
        [jnp.dot(q, kpages[p][...].astype(BF16), preferred_element_type=F32)
         for p in range(n_pg)], axis=1)
    sm = jnp.where(mask_ref[...] > 0.5, s, NEG_INF)
    m_old = m_scr[...]
    m_new = jnp.maximum(m_old, jnp.max(sm, axis=1, keepdims=True))
    m_safe = jnp.where(m_new == NEG_INF, 0.0, m_new)
    p_ = jnp.exp2(sm - m_safe)
    alpha = jnp.exp2(m_old - m_safe)
    pb = p_.astype(BF16)
    pv = jnp.zeros(acc_scr.shape, F32)
    for p in range(n_pg):
        pv = pv + lax.dot_general(pb[:, p * page:(p + 1) * page], vpages[p][...].astype(BF16), NT_DIMS,
                                  preferred_element_type=F32)
    l_scr[...] = l_scr[...] * alpha + jnp.sum(pb.astype(F32), axis=1, keepdims=True)
    acc_scr[...] = acc_scr[...] * alpha + pv
    m_scr[...] = m_new

    @pl.when(j == pl.num_programs(1) - 1)
    def _():
        s_self = jnp.sum(q.astype(F32) * ks_ref[...].astype(F32), axis=1, keepdims=True)
        s_self = jnp.where(ss_ref[...] > 0.5, s_self, NEG_INF)
        m_o = m_scr[...]
        m_n = jnp.maximum(m_o, s_self)
        m_s = jnp.where(m_n == NEG_INF, 0.0, m_n)
        p_self = jnp.exp2(s_self - m_s).astype(BF16).astype(F32)
        al = jnp.exp2(m_o - m_s)
        l_fin = l_scr[...] * al + p_self
        acc = acc_scr[...] * al + p_self * vs_ref[...].astype(F32)
        o = acc / l_fin
        hrow = lax.broadcasted_iota(I32, o.shape, 0)
        o = jnp.where(hrow < hpg, o, pltpu.roll(o, HEAD_DIM, axis=1))
        o_ref[...] = o[:, :HEAD_DIM].astype(o_ref.dtype)


def _sample_attn(page_table, qs, mask3, kself, vself, selfsel, cache_k2, cache_v2, n_pg):
    sq, n_pages = page_table.shape
    d_kv, page = cache_k2.shape[1], cache_k2.shape[2]
    kern = functools.partial(_sample_attn_kernel, n_pg=n_pg)
    pspec = lambda p: pl.BlockSpec((None, d_kv, page),
                                   lambda b, j, pt: (pt[b * n_pages + j * n_pg + p], 0, 0))
    per_seq = lambda r, w: pl.BlockSpec((None, r, w), lambda b, j, pt: (b, 0, 0))
    grid_spec = pltpu.PrefetchScalarGridSpec(
        num_scalar_prefetch=1, grid=(sq, n_pages // n_pg),
        in_specs=[per_seq(N_HEADS, d_kv),
                  pl.BlockSpec((None, 1, n_pg * page), lambda b, j, pt: (b, 0, j)),
                  per_seq(1, d_kv), per_seq(1, d_kv), per_seq(1, 1)]
                 + [pspec(p) for p in range(n_pg)] + [pspec(p) for p in range(n_pg)],
        out_specs=per_seq(N_HEADS, HEAD_DIM),
        scratch_shapes=[pltpu.VMEM((N_HEADS, 1), F32), pltpu.VMEM((N_HEADS, 1), F32),
                        pltpu.VMEM((N_HEADS, d_kv), F32)])
    return pl.pallas_call(
        kern, grid_spec=grid_spec,
        out_shape=jax.ShapeDtypeStruct((sq, N_HEADS, HEAD_DIM), BF16),
        compiler_params=pltpu.CompilerParams(dimension_semantics=("arbitrary", "arbitrary")),
        name="sample_attn",
    )(page_table.reshape(-1), qs, mask3, kself, vself, selfsel,
      *([cache_k2] * n_pg), *([cache_v2] * n_pg))


def _tail_kernel(x_ref, attn_ref, ssm_ref, sgs_ref, sga_ref, wao_ref, wo_ref, g2_ref, wup_ref, wdn_ref,
                 gf_ref, y_ref):
    attn_out = jnp.dot(attn_ref[...], wao_ref[...], preferred_element_type=F32)
    mix = sgs_ref[...] * ssm_ref[...] + sga_ref[...] * attn_out
    x1 = x_ref[...] + jnp.dot(mix.astype(BF16), wo_ref[...], preferred_element_type=F32)
    hh = _rms_norm(x1, g2_ref[...]).astype(BF16)
    up = jnp.dot(hh, wup_ref[...], preferred_element_type=F32)
    r = jnp.square(jnp.maximum(up, 0.0)).astype(BF16)
    x2 = x1 + jnp.dot(r, wdn_ref[...], preferred_element_type=F32)
    y_ref[...] = _rms_norm(x2, gf_ref[...])


def _tail(x2d, attn, ssm, sgs, sga, wao, wo, g2, wup, wdn, gf, tm):
    n, d_model = x2d.shape
    row = lambda w: pl.BlockSpec((tm, w), lambda i: (i, 0))
    return pl.pallas_call(
        _tail_kernel, grid=(n // tm,),
        in_specs=[row(d_model), row(attn.shape[1]), row(d_model), row(d_model), row(d_model),
                  _const_spec(wao.shape), _const_spec(wo.shape), _const_spec((1, d_model)),
                  _const_spec(wup.shape), _const_spec(wdn.shape), _const_spec((1, d_model))],
        out_specs=row(d_model),
        out_shape=jax.ShapeDtypeStruct((n, d_model), F32),
        compiler_params=pltpu.CompilerParams(dimension_semantics=("arbitrary",),
                                             vmem_limit_bytes=VMEM_LIMIT),
        name="tail",
    )(x2d, attn, ssm, sgs, sga, wao, wo, g2.reshape(1, d_model), wup, wdn, gf.reshape(1, d_model))


def _tiles(n_rows, t_len):
    tm = min(256, n_rows)
    tc = min(128, t_len)
    tq = min(256, t_len)
    return tm, tc, tq, tq


def kernel(x_prompt, x_sample, cache_k, cache_v, cache_idx_k, state_ssm_re, state_ssm_im, page_table,
           norm1_g, w_in, ssm_a_re, ssm_a_im, ssm_log_dt, ssm_b_re, ssm_b_im, ssm_c_re, ssm_c_im,
           ssm_d, w_glu, b_glu, w_attn_out, w_o, norm2_g, w_up, w_down, normf_g):
    bsz, t_len, d_model = x_prompt.shape
    sq, s_len, _ = x_sample.shape
    assert s_len == 1, "the sample path handles one new token per sequence"
    n_pool, page = cache_idx_k.shape[0], cache_idx_k.shape[1]
    n_pages = page_table.shape[1]
    past = n_pages * page
    d_ssm = ssm_d.shape[0]
    d_attn = N_HEADS * HEAD_DIM
    d_kv = N_KV_HEADS * HEAD_DIM
    d_qi = N_IDX_HEADS * IDX_DIM
    dims = (d_ssm, d_attn, d_kv, d_qi)

    c_ki = d_ssm + d_attn + 2 * d_kv + d_qi
    c_g = c_ki + IDX_DIM + N_IDX_HEADS
    w_pack = jnp.concatenate(
        [w_in[:, :c_g], jnp.zeros((d_model, LANES - IDX_DIM - N_IDX_HEADS), w_in.dtype), w_in[:, c_g:]],
        axis=1).astype(BF16)
    wglu_b, wao_b, wo_b = w_glu.astype(BF16), w_attn_out.astype(BF16), w_o.astype(BF16)
    wup_b, wdn_b = w_up.astype(BF16), w_down.astype(BF16)
    s5p = _s5_params(ssm_a_re, ssm_a_im, ssm_log_dt, ssm_b_re, ssm_b_im, ssm_c_re, ssm_c_im)

    n_p = bsz * t_len
    tm, tc, tq, tk = _tiles(n_p, t_len)
    xp = x_prompt.reshape(n_p, d_model)
    tabs_p = _rope_tables(jnp.arange(t_len, dtype=I32), t_len)
    assert tm == tk
    (u, q, kt, kb, vt, _, vtb, qi, kit, kib, _, wit, sgs, sga) = _in_proj(
        xp, bsz, tabs_p, norm1_g, w_pack, tm, dims)
    ssm_out, re_p, im_p = _s5_prompt(u.reshape(bsz, t_len, d_ssm), s5p, ssm_d, wglu_b, b_glu, tc)
    r3 = lambda a: a.reshape(bsz, t_len, a.shape[-1])
    attn = _dsa_prompt(r3(qi), wit, r3(q), r3(kib), r3(kb), vtb, tq, tk)
    y_p = _tail(xp, attn.reshape(n_p, d_attn), ssm_out.reshape(n_p, d_model), sgs, sga,
                wao_b, wo_b, norm2_g, wup_b, wdn_b, normf_g, tm)

    xs = x_sample.reshape(sq, d_model)
    tabs_s = _rope_tables(jnp.full((sq,), past, I32), sq)
    (u_s, q_s, kt_s, kb_s, vt_s, vb_s, _, qi_s, kit_s, kib_s, wi_s, _, sgs_s, sga_s) = _in_proj(
        xs, 1, tabs_s, norm1_g, w_pack, sq, dims)
    sd = state_ssm_re.shape[1] * state_ssm_re.shape[2]
    ssm_s, re_s, im_s = _s5_sample(u_s, state_ssm_re.reshape(sq, sd), state_ssm_im.reshape(sq, sd),
                                   s5p, ssm_d, wglu_b, b_glu)
    n_pg = math.gcd(n_pages, 16)
    scores = _idx_scores(page_table, qi_s.reshape(sq, N_IDX_HEADS, IDX_DIM),
                         wi_s.reshape(sq, N_IDX_HEADS, 1), jnp.transpose(cache_idx_k, (0, 2, 1)), n_pg)
    k_top_s = min(TOPK_MAX, (past + s_len) // 4)
    mask, selfsel = _sample_select(scores, qi_s, kib_s, wi_s, min(512, past), k_top_s)
    q4 = q_s.reshape(sq, N_KV_HEADS, N_HEADS // N_KV_HEADS, HEAD_DIM)
    qs_pad = (q4[:, :, :, None, :] * jnp.eye(N_KV_HEADS, dtype=BF16)[None, :, None, :, None]
              ).reshape(sq, N_HEADS, d_kv)
    feat_major = lambda c: jnp.transpose(c, (0, 2, 3, 1)).reshape(n_pool, d_kv, page)
    attn_s = _sample_attn(page_table, qs_pad, mask.reshape(sq, 1, past),
                          kb_s.reshape(sq, 1, d_kv), vb_s.reshape(sq, 1, d_kv), selfsel.reshape(sq, 1, 1),
                          feat_major(cache_k), feat_major(cache_v), n_pg)
    y_s = _tail(xs, attn_s.reshape(sq, d_attn), ssm_s, sgs_s, sga_s,
                wao_b, wo_b, norm2_g, wup_b, wdn_b, normf_g, sq)

    g_ssm, p_ssm = state_ssm_re.shape[1], state_ssm_re.shape[2]
    kv_out = lambda a: jnp.transpose(a.reshape(a.shape[0], N_KV_HEADS, HEAD_DIM, a.shape[2]), (0, 3, 1, 2))
    return (y_p.reshape(bsz, t_len, d_model), y_s.reshape(sq, s_len, d_model),
            kv_out(kt), kv_out(vt), jnp.transpose(kit, (0, 2, 1)),
            re_p.reshape(bsz, g_ssm, p_ssm), im_p.reshape(bsz, g_ssm, p_ssm),
            kv_out(kt_s).reshape(sq, s_len, N_KV_HEADS, HEAD_DIM),
            kv_out(vt_s).reshape(sq, s_len, N_KV_HEADS, HEAD_DIM),
            jnp.transpose(kit_s, (0, 2, 1)).reshape(sq, s_len, IDX_DIM),
            re_s.reshape(sq, g_ssm, p_ssm), im_s.reshape(sq, g_ssm, p_ssm))
```

```python
import functools
import math

import jax
import jax.numpy as jnp
from jax import lax
from jax.experimental import pallas as pl
from jax.experimental.pallas import tpu as pltpu

F32 = jnp.float32
BF16 = jnp.bfloat16
I32 = jnp.int32

SSM_GROUP = 16
SSM_STATE = 64
N_HEADS = 8
N_KV_HEADS = 2
HEAD_DIM = 64
ROT_DIM = HEAD_DIM // 4
N_IDX_HEADS = 8
IDX_DIM = 64
IDX_ROT_DIM = IDX_DIM // 4
ROPE_THETA = 500000.0
TOPK_MAX = 256
EPS = 1e-6
LOG2_E = math.log2(math.e)

LANES = 128
SUBLANES = 8
MXU_COLS = 256
S5_SEQS = 8
S5_SCAN_COLS = 4
COUNT_WAYS = 4
VT_ROWS = 80
VMEM_LIMIT = 56 * 1024 * 1024

NEG_INF = float("-inf")
FLT_MAX = float(jnp.finfo(jnp.float32).max)
KEY_LOW = -2139095040
NT_DIMS = (((1,), (1,)), ((), ()))


def _const_spec(shape):
    nd = len(shape)
    return pl.BlockSpec(shape, lambda *_: (0,) * nd, pipeline_mode=pl.Buffered(1))


def _rms_norm(x, g):
    ms = jnp.mean(x * x, axis=-1, keepdims=True)
    return x * lax.rsqrt(ms + EPS) * g


def _sigmoid(x):
    return 1.0 / (1.0 + jnp.exp(-x))


def _rope(x, cos_t, sin_a, sin_b):
    return (x * cos_t + pltpu.roll(x, LANES - ROT_DIM // 2, axis=1) * sin_a
            + pltpu.roll(x, ROT_DIM // 2, axis=1) * sin_b)


def _in_proj_kernel(x_ref, g_ref, w_ref, cos_ref, sa_ref, sb_ref,
                    u_ref, q_ref, kt_ref, kb_ref, vt_ref, vb_ref, vtb_ref, qi_ref, kit_ref, kib_ref,
                    wi_ref, wit_ref, sgs_ref, sga_ref, *, d_ssm, d_attn, d_kv, d_qi, d_model):
    assert d_kv == LANES
    tm = x_ref.shape[0]
    h = _rms_norm(x_ref[...], g_ref[...]).astype(BF16)
    cos_t, sin_a, sin_b = cos_ref[...], sa_ref[...], sb_ref[...]

    def proj(c0, width):
        return jnp.dot(h, w_ref[:, c0:c0 + width], preferred_element_type=F32)

    off = 0
    u_ref[...] = proj(off, d_ssm)
    off += d_ssm
    def rope_chunks(c0, width):
        for m in range(width // MXU_COLS):
            wide = proj(c0 + m * MXU_COLS, MXU_COLS)
            for c in range(MXU_COLS // LANES):
                yield (m * MXU_COLS // LANES + c,
                       _rope(wide[:, c * LANES:(c + 1) * LANES], cos_t, sin_a, sin_b))

    for c, r in rope_chunks(off, d_attn):
        q_ref[:, c * LANES:(c + 1) * LANES] = (r * (HEAD_DIM ** -0.5 * LOG2_E)).astype(BF16)
    off += d_attn
    kv = proj(off, 2 * d_kv)
    r = _rope(kv[:, :d_kv], cos_t, sin_a, sin_b)
    kt_ref[...] = r.T
    kb_ref[...] = r.astype(BF16)
    off += d_kv
    vv = kv[:, d_kv:]
    vt = vv.T
    vt_ref[...] = vt
    vb_ref[...] = vv.astype(BF16)
    sub = lax.broadcasted_iota(I32, (VT_ROWS - HEAD_DIM, tm), 0)
    ones_pad = jnp.where(sub == 0, 1.0, 0.0)
    for g in range(N_KV_HEADS):
        vtb_ref[g] = jnp.concatenate([vt[g * HEAD_DIM:(g + 1) * HEAD_DIM, :], ones_pad], axis=0).astype(BF16)
    off += d_kv
    for c, r in rope_chunks(off, d_qi):
        qi_ref[:, c * LANES:(c + 1) * LANES] = r.astype(BF16)
    off += d_qi
    kw = proj(off, LANES)
    lane = lax.broadcasted_iota(I32, kw.shape, 1)
    kr = jnp.where(lane < IDX_DIM, _rope(kw, cos_t, sin_a, sin_b), kw * (N_IDX_HEADS ** -0.5))
    krt = kr.T
    kit_ref[...] = krt[:IDX_DIM, :]
    kib_ref[...] = kr[:, :IDX_DIM].astype(BF16)
    wi_ref[...] = kr[:, IDX_DIM:IDX_DIM + N_IDX_HEADS]
    wit_ref[...] = krt[IDX_DIM:IDX_DIM + N_IDX_HEADS, :]
    off += LANES
    sgs_ref[...] = _sigmoid(proj(off, d_model))
    off += d_model
    sga_ref[...] = _sigmoid(proj(off, d_model))


def _rope_tables(pos, n_rows):
    half = ROT_DIM // 2
    inv = ROPE_THETA ** (-jnp.arange(half, dtype=F32) / half)
    ang = pos.astype(F32)[:, None] * inv[None, :]
    cos, sin = jnp.cos(ang), jnp.sin(ang)
    ones = jnp.ones((n_rows, HEAD_DIM - ROT_DIM), F32)
    zeros = jnp.zeros((n_rows, HEAD_DIM - ROT_DIM), F32)
    zh = jnp.zeros((n_rows, half), F32)
    cos_t = jnp.concatenate([cos, cos, ones], axis=1)
    sin_a = jnp.concatenate([-sin, zh, zeros], axis=1)
    sin_b = jnp.concatenate([zh, sin, zeros], axis=1)
    rep = LANES // HEAD_DIM
    return tuple(jnp.tile(t, (1, rep)) for t in (cos_t, sin_a, sin_b))


def _in_proj(x2d, n_seq, pos_tab, norm_g, w_pack, tm, dims):
    n, d_model = x2d.shape
    d_ssm, d_attn, d_kv, d_qi = dims
    cos_t, sin_a, sin_b = pos_tab
    t_len = n // n_seq
    nt = t_len // tm
    grid = (n // tm,)
    row = lambda w: pl.BlockSpec((tm, w), lambda i: (i, 0))
    tab = pl.BlockSpec((tm, LANES), lambda i: (i % nt, 0))
    feat = lambda r: pl.BlockSpec((None, r, tm), lambda i: (i // nt, 0, i % nt))
    kern = functools.partial(_in_proj_kernel, d_ssm=d_ssm, d_attn=d_attn, d_kv=d_kv, d_qi=d_qi,
                             d_model=d_model)
    outs = [
        (jax.ShapeDtypeStruct((n, d_ssm), F32), row(d_ssm)),
        (jax.ShapeDtypeStruct((n, d_attn), BF16), row(d_attn)),
        (jax.ShapeDtypeStruct((n_seq, d_kv, t_len), F32), feat(d_kv)),
        (jax.ShapeDtypeStruct((n, d_kv), BF16), row(d_kv)),
        (jax.ShapeDtypeStruct((n_seq, d_kv, t_len), F32), feat(d_kv)),
        (jax.ShapeDtypeStruct((n, d_kv), BF16), row(d_kv)),
        (jax.ShapeDtypeStruct((n_seq, nt, N_KV_HEADS, VT_ROWS, tm), BF16),
         pl.BlockSpec((None, None, N_KV_HEADS, VT_ROWS, tm), lambda i: (i // nt, i % nt, 0, 0, 0))),
        (jax.ShapeDtypeStruct((n, d_qi), BF16), row(d_qi)),
        (jax.ShapeDtypeStruct((n_seq, IDX_DIM, t_len), F32), feat(IDX_DIM)),
        (jax.ShapeDtypeStruct((n, IDX_DIM), BF16), row(IDX_DIM)),
        (jax.ShapeDtypeStruct((n, N_IDX_HEADS), F32), row(N_IDX_HEADS)),
        (jax.ShapeDtypeStruct((n_seq, N_IDX_HEADS, t_len), F32), feat(N_IDX_HEADS)),
        (jax.ShapeDtypeStruct((n, d_model), F32), row(d_model)),
        (jax.ShapeDtypeStruct((n, d_model), F32), row(d_model)),
    ]
    out_shapes = [o[0] for o in outs]
    out_specs = [o[1] for o in outs]
    return pl.pallas_call(
        kern, grid=grid,
        in_specs=[row(d_model), _const_spec((1, d_model)), _const_spec(w_pack.shape), tab, tab, tab],
        out_specs=out_specs, out_shape=out_shapes,
        compiler_params=pltpu.CompilerParams(dimension_semantics=("arbitrary",),
                                             vmem_limit_bytes=VMEM_LIMIT),
        name="in_proj",
    )(x2d, norm_g.reshape(1, d_model), w_pack, cos_t, sin_a, sin_b)


def _s5_readout(y, u, dsk_ref, wglu_ref, bglu_ref, d_model):
    y = y + dsk_ref[...] * u
    cdf = 0.5 * (1.0 + jnp.tanh(math.sqrt(2.0 / math.pi) * (y + 0.044715 * (y * y * y))))
    gl = (y * cdf).astype(BF16)
    z = jnp.dot(gl, wglu_ref[...], preferred_element_type=F32) + bglu_ref[...]
    return z[:, :d_model] * _sigmoid(z[:, d_model:])


def _s5_prompt_kernel(u_ref, perm_ref, bcat_ref, are_ref, aim_ref, ccre_ref, ccim_ref, dsk_ref, wglu_ref,
                      bglu_ref, out_ref, sre_ref, sim_ref, hre, him, zs, cr, ci, *, tc, d_model):
    @pl.when(pl.program_id(1) == 0)
    def _():
        cr[...] = jnp.zeros_like(cr)
        ci[...] = jnp.zeros_like(ci)

    nseq = u_ref.shape[0]
    d_ssm, n_col = u_ref.shape[2], hre.shape[0]
    sd = n_col * LANES
    u = u_ref[...].reshape(nseq * tc, d_ssm)
    ub = jnp.dot(perm_ref[...], u.astype(BF16), preferred_element_type=F32).astype(BF16)
    for c in range(n_col):
        kc = (c * LANES * d_ssm // sd) // LANES
        bu = jnp.dot(ub[:, kc * LANES:(kc + 1) * LANES], bcat_ref[c], preferred_element_type=F32)
        hre[c] = bu[:, :LANES]
        him[c] = bu[:, LANES:]

    for c0 in range(0, n_col, S5_SCAN_COLS):
        cs = range(c0, c0 + S5_SCAN_COLS)
        ar = [jnp.broadcast_to(are_ref[:, c * LANES:(c + 1) * LANES], (nseq, LANES)) for c in cs]
        ai = [jnp.broadcast_to(aim_ref[:, c * LANES:(c + 1) * LANES], (nseq, LANES)) for c in cs]

        def step(t, carry, cs=cs, ar=ar, ai=ai):
            rows = pl.ds(pl.multiple_of(t * nseq, nseq), nseq)
            out = []
            for n, c in enumerate(cs):
                pr, pi = carry[n]
                xr = hre[c, rows, :] + (ar[n] * pr - ai[n] * pi)
                xi = him[c, rows, :] + (ar[n] * pi + ai[n] * pr)
                hre[c, rows, :] = xr
                him[c, rows, :] = xi
                out.append((xr, xi))
            return tuple(out)

        init = tuple((cr[:, c * LANES:(c + 1) * LANES], ci[:, c * LANES:(c + 1) * LANES]) for c in cs)
        fin = lax.fori_loop(0, tc, step, init, unroll=4)
        for n, c in enumerate(cs):
            cr[:, c * LANES:(c + 1) * LANES] = fin[n][0]
            ci[:, c * LANES:(c + 1) * LANES] = fin[n][1]
    sre_ref[...] = cr[...]
    sim_ref[...] = ci[...]
    n_out = d_ssm // LANES
    per = n_col // n_out
    wide = lambda ref, j: jnp.concatenate([ref[c] for c in range(j * per, (j + 1) * per)], axis=1).astype(BF16)
    for j in range(n_out):
        zs[j] = (jnp.dot(wide(hre, j), ccre_ref[j], preferred_element_type=F32)
                 + jnp.dot(wide(him, j), ccim_ref[j], preferred_element_type=F32))
    y = jnp.concatenate(
        [jnp.concatenate([zs[j, pl.ds(b, tc, stride=nseq), :] for b in range(nseq)], axis=0)
         for j in range(n_out)], axis=1)
    out_ref[...] = _s5_readout(y, u, dsk_ref, wglu_ref, bglu_ref, d_model).reshape(nseq, tc, d_model)


def _s5_sample_kernel(u_ref, h0r_ref, h0i_ref, are_ref, aim_ref, bre_ref, bim_ref, cre_ref, cim_ref,
                      dsk_ref, wglu_ref, bglu_ref, out_ref, sre_ref, sim_ref, *, d_model):
    u = u_ref[...]
    ub = u.astype(BF16)
    ar, ai = are_ref[...], aim_ref[...]
    h0r, h0i = h0r_ref[...], h0i_ref[...]
    hr = (ar * h0r - ai * h0i) + jnp.dot(ub, bre_ref[...], preferred_element_type=F32)
    hi = (ar * h0i + ai * h0r) + jnp.dot(ub, bim_ref[...], preferred_element_type=F32)
    sre_ref[...] = hr
    sim_ref[...] = hi
    y = (jnp.dot(hr.astype(BF16), cre_ref[...], preferred_element_type=F32)
         + jnp.dot(hi.astype(BF16), cim_ref[...], preferred_element_type=F32))
    out_ref[...] = _s5_readout(y, u, dsk_ref, wglu_ref, bglu_ref, d_model)


def _s5_params(a_re, a_im, log_dt, b_re, b_im, c_re, c_im):
    g, p = a_re.shape
    a_c = lax.complex(a_re.astype(F32), a_im.astype(F32))
    dt = jnp.exp(log_dt.astype(F32))[:, None]
    a_bar = jnp.exp(a_c * dt)
    b_bar = ((a_bar - 1.0) / a_c)[:, :, None] * lax.complex(b_re.astype(F32), b_im.astype(F32))
    eye = jnp.eye(g, dtype=F32)
    n = b_re.shape[2]

    def bmat(b):
        return jnp.einsum('gpn,gh->gnhp', b, eye).reshape(g * n, g * p)

    def cmat(c):
        return jnp.einsum('gnp,gh->gphn', c, eye).reshape(g * p, g * n)

    bre, bim = bmat(jnp.real(b_bar)).astype(BF16), bmat(jnp.imag(b_bar)).astype(BF16)
    cre, cim = cmat(c_re.astype(F32)).astype(BF16), cmat(-c_im.astype(F32)).astype(BF16)
    d_in, sd = bre.shape
    assert d_in % LANES == 0 and sd % LANES == 0 and LANES % (LANES * d_in // sd) == 0
    bcat = jnp.stack([
        jnp.concatenate([m[(c * LANES * d_in // sd) // LANES * LANES:][:LANES, c * LANES:(c + 1) * LANES]
                         for m in (bre, bim)], axis=1)
        for c in range(sd // LANES)], axis=0)
    n_out = d_in // LANES
    span = sd // n_out
    ccre, ccim = (jnp.stack([m[j * span:(j + 1) * span, j * LANES:(j + 1) * LANES] for j in range(n_out)],
                            axis=0) for m in (cre, cim))
    a1 = a_bar.reshape(1, g * p)
    return jnp.real(a1), jnp.imag(a1), bre, bim, cre, cim, bcat, ccre, ccim


def _s5_prompt(u3, s5p, dsk, wglu, bglu, tc):
    bsz, t_len, d_ssm = u3.shape
    are, aim, bre, _, _, _, bcat, ccre, ccim = s5p
    sd = bre.shape[1]
    d_model = wglu.shape[1] // 2
    nseq = math.gcd(bsz, S5_SEQS)
    assert sd % (S5_SCAN_COLS * LANES) == 0 and tc % SUBLANES == 0
    kern = functools.partial(_s5_prompt_kernel, tc=tc, d_model=d_model)
    state = pl.BlockSpec((nseq, sd), lambda b, c: (b, 0))
    dst = jnp.arange(nseq * tc)
    perm = (jnp.arange(nseq * tc)[None, :] == ((dst % nseq) * tc + dst // nseq)[:, None]).astype(BF16)
    return pl.pallas_call(
        kern, grid=(bsz // nseq, t_len // tc),
        in_specs=[pl.BlockSpec((nseq, tc, d_ssm), lambda b, c: (b, c, 0)), _const_spec(perm.shape),
                  _const_spec(bcat.shape), _const_spec(are.shape), _const_spec(aim.shape),
                  _const_spec(ccre.shape), _const_spec(ccim.shape), _const_spec((1, d_ssm)),
                  _const_spec(wglu.shape), _const_spec((1, 2 * d_model))],
        out_specs=[pl.BlockSpec((nseq, tc, d_model), lambda b, c: (b, c, 0)), state, state],
        out_shape=[jax.ShapeDtypeStruct((bsz, t_len, d_model), F32),
                   jax.ShapeDtypeStruct((bsz, sd), F32),
                   jax.ShapeDtypeStruct((bsz, sd), F32)],
        scratch_shapes=[pltpu.VMEM((sd // LANES, nseq * tc, LANES), F32),
                        pltpu.VMEM((sd // LANES, nseq * tc, LANES), F32),
                        pltpu.VMEM((d_ssm // LANES, nseq * tc, LANES), F32),
                        pltpu.VMEM((nseq, sd), F32), pltpu.VMEM((nseq, sd), F32)],
        compiler_params=pltpu.CompilerParams(dimension_semantics=("arbitrary", "arbitrary"),
                                             vmem_limit_bytes=VMEM_LIMIT),
        name="s5_prompt",
    )(u3, perm, bcat, are, aim, ccre, ccim, dsk.reshape(1, d_ssm), wglu, bglu.reshape(1, 2 * d_model))


def _s5_sample(u2, h0r, h0i, s5p, dsk, wglu, bglu):
    n, d_ssm = u2.shape
    are, aim, bre, bim, cre, cim = s5p[:6]
    sd = bre.shape[1]
    d_model = wglu.shape[1] // 2
    kern = functools.partial(_s5_sample_kernel, d_model=d_model)
    return pl.pallas_call(
        kern,
        out_shape=[jax.ShapeDtypeStruct((n, d_model), F32),
                   jax.ShapeDtypeStruct((n, sd), F32),
                   jax.ShapeDtypeStruct((n, sd), F32)],
        compiler_params=pltpu.CompilerParams(vmem_limit_bytes=VMEM_LIMIT),
        name="s5_sample",
    )(u2, h0r, h0i, are, aim, bre, bim, cre, cim, dsk.reshape(1, d_ssm), wglu,
      bglu.reshape(1, 2 * d_model))


def _f2k(x):
    b = lax.bitcast_convert_type(x, I32)
    return jnp.where(b < 0, b ^ 0x7FFFFFFF, b)


def _k2f(k):
    return lax.bitcast_convert_type(jnp.where(k < 0, k ^ 0x7FFFFFFF, k), F32)


def _threshold_search(count_ge, amax, forced, k_top):
    kf = float(k_top)
    lo0 = jnp.where(forced, KEY_LOW, _f2k(-amax))
    hi0 = jnp.where(forced, KEY_LOW + 1, _f2k(amax) + 1)

    def cond(st):
        lo, hi, _ = st
        return jnp.max(jnp.where(hi > lo + 1, 1.0, 0.0)) > 0.5

    def body(st):
        lo, hi, it = st
        active = hi > lo + 1
        mid_i = (lo >> 1) + (hi >> 1) + (lo & hi & 1)
        mk = _f2k(0.5 * _k2f(lo) + 0.5 * _k2f(hi))
        use_f = jnp.logical_and(it % 2 == 0, jnp.logical_and(mk > lo, mk < hi))
        mid = jnp.where(use_f, mk, mid_i)
        c = count_ge(_k2f(mid))
        ge = c >= kf
        nlo = jnp.where(ge, mid, lo)
        nhi = jnp.where(c == kf, mid + 1, jnp.where(ge, hi, mid))
        return jnp.where(active, nlo, lo), jnp.where(active, nhi, hi), it + 1

    lo, _, _ = lax.while_loop(cond, body, (lo0, hi0, jnp.int32(0)))
    return _k2f(lo)


def _tri_exclusive(n):
    r = lax.broadcasted_iota(I32, (n, n), 0)
    c = lax.broadcasted_iota(I32, (n, n), 1)
    return (r < c).astype(BF16)


def _dsa_prompt_kernel(qi_ref, wt_ref, q_ref, ki_ref, k_ref, vt_ref, o_ref,
                       sc_ref, scb_ref, qis_ref, qs_ref, m_ref, acc_ref, *, tq, tk, k_top):
    i = pl.program_id(1)
    nkb = i + 1
    hpg = N_HEADS // N_KV_HEADS
    ngr = tk // SUBLANES
    kf = float(k_top)

    qi = qi_ref[...]
    q = q_ref[...]
    zpad = jnp.zeros((tq, HEAD_DIM), BF16)
    for h in range(N_IDX_HEADS):
        qis_ref[h * tq:(h + 1) * tq, :] = qi[:, h * IDX_DIM:(h + 1) * IDX_DIM]
    for h in range(N_HEADS):
        qh = q[:, h * HEAD_DIM:(h + 1) * HEAD_DIM]
        pair = [qh, zpad] if h // hpg == 0 else [zpad, qh]
        qs_ref[h * tq:(h + 1) * tq, :] = jnp.concatenate(pair, axis=1)

    w8 = wt_ref[...] * (IDX_DIM ** -0.5)
    qpos = i * tq + lax.broadcasted_iota(I32, (SUBLANES, tq), 1)

    def score_blk(j, amx, diagonal):
        k0 = pl.multiple_of(j * tk, tk)
        d = lax.dot_general(ki_ref[pl.ds(k0, tk), :], qis_ref[...], NT_DIMS, preferred_element_type=F32)
        mags = []
        for c in range(tq // LANES):
            acc = jnp.zeros((tk, LANES), F32)
            for h in range(N_IDX_HEADS):
                cols = slice(h * tq + c * LANES, h * tq + (c + 1) * LANES)
                acc = acc + jnp.maximum(d[:, cols], 0.0) * w8[h:h + 1, c * LANES:(c + 1) * LANES]
            mag = jnp.abs(acc)
            if diagonal:
                kpos = j * tk + lax.broadcasted_iota(I32, (tk, LANES), 0)
                causal = kpos <= i * tq + c * LANES + lax.broadcasted_iota(I32, (tk, LANES), 1)
                acc = jnp.where(causal, acc, NEG_INF)
                mag = jnp.where(causal, mag, 0.0)
            sc_ref[j, :, c * LANES:(c + 1) * LANES] = acc
            top = lax.bitcast_convert_type(lax.bitcast_convert_type(acc, I32) & jnp.int32(-65536), F32)
            scb_ref[j, :, c * LANES:(c + 1) * LANES] = top.astype(BF16)
            mags.append(jnp.max(mag.reshape(ngr, SUBLANES, LANES), axis=0))
        return jnp.maximum(amx, jnp.concatenate(mags, axis=1))

    amx = lax.fori_loop(0, nkb - 1, lambda j, a: score_blk(j, a, False), jnp.zeros((SUBLANES, tq), F32))
    amx = score_blk(nkb - 1, amx, True)

    rep = lambda x: jnp.broadcast_to(x, (SUBLANES, tq))
    amax = rep(jnp.max(amx, axis=0, keepdims=True))
    forced = (qpos + 1) <= k_top

    packed = 2 * SUBLANES
    one_b, zero_b = jnp.ones((), BF16), jnp.zeros((), BF16)

    def count16(k16):
        bits = jnp.where(k16 < 0, k16 ^ 0x7FFF, k16) << 16
        tb = jnp.broadcast_to(lax.bitcast_convert_type(bits, F32)[0:1, :], (packed, tq)).astype(BF16)

        def body(j, part):
            blk = scb_ref[j].reshape(tk // packed // COUNT_WAYS, COUNT_WAYS, packed, tq)
            hit = jnp.where(blk >= tb[None, None], one_b, zero_b)
            for g in range(hit.shape[0]):
                part = part + hit[g]
            return part

        part = lax.fori_loop(0, nkb, body, jnp.zeros((COUNT_WAYS, packed, tq), BF16))
        return rep(jnp.sum(jnp.sum(part.astype(F32), axis=0), axis=0, keepdims=True))

    def coarse_pass(lo16, hi16):
        active = hi16 > lo16 + 1
        mid = lo16 + ((hi16 - lo16) >> 1)
        c = count16(mid)
        ge = c >= kf
        nlo = jnp.where(ge, mid, lo16)
        nhi = jnp.where(c == kf, mid, jnp.where(ge, hi16, mid))
        return jnp.where(active, nlo, lo16), jnp.where(active, nhi, hi16)

    def coarse_body(st):
        lo16, hi16, _ = st
        lo16, hi16 = coarse_pass(*coarse_pass(lo16, hi16))
        return lo16, hi16, (jnp.max(jnp.where(hi16 > lo16 + 1, 1.0, 0.0)) > 0.5).astype(I32)

    lo16 = jnp.where(forced, KEY_LOW >> 16, _f2k(-amax) >> 16)
    hi16 = jnp.where(forced, KEY_LOW >> 16, (_f2k(amax) >> 16) + 1)
    lo16, hi16, _ = lax.while_loop(lambda st: st[2] > 0, coarse_body, (lo16, hi16, jnp.int32(1)))
    lo0 = lo16 << 16
    hi0 = jnp.where(hi16 == lo16, lo0 + 1, hi16 << 16)

    def count(thr, strict):
        def body(j, part):
            blk = sc_ref[j].reshape(ngr // COUNT_WAYS, COUNT_WAYS, SUBLANES, tq)
            hit = (blk > thr[None, None]) if strict else (blk >= thr[None, None])
            return part + jnp.sum(jnp.where(hit, 1.0, 0.0), axis=0)

        part = lax.fori_loop(0, nkb, body, jnp.zeros((COUNT_WAYS, SUBLANES, tq), F32))
        return rep(jnp.sum(jnp.sum(part, axis=0), axis=0, keepdims=True))

    def search_pass(lo, hi, value_space):
        active = hi > lo + 1
        span = hi - lo
        mid = lo + lax.shift_right_logical(span, jnp.ones_like(span))
        if value_space:
            mk = _f2k(0.5 * _k2f(lo) + 0.5 * _k2f(hi))
            mid = jnp.where(jnp.logical_and(mk > lo, mk < hi), mk, mid)
        c = count(_k2f(mid), False)
        ge = c >= kf
        nlo = jnp.where(ge, mid, lo)
        nhi = jnp.where(c == kf, mid + 1, jnp.where(ge, hi, mid))
        return jnp.where(active, nlo, lo), jnp.where(active, nhi, hi)

    def search_body(st):
        lo, hi, _ = st
        lo, hi = search_pass(lo, hi, True)
        lo, hi = search_pass(lo, hi, False)
        return lo, hi, (jnp.max(jnp.where(hi > lo + 1, 1.0, 0.0)) > 0.5).astype(I32)

    lo, _, _ = lax.while_loop(lambda st: st[2] > 0, search_body, (lo0, hi0, jnp.int32(1)))
    thr = _k2f(lo)
    c_ge = count(thr, False)
    thr1 = thr[0:1, :]

    @pl.when(jnp.max(jnp.where(c_ge > kf, 1.0, 0.0)) > 0.5)
    def _():
        need = jnp.where(c_ge > kf, kf - count(thr, True), 1e30)[0:1, :]
        r = lax.broadcasted_iota(I32, (tk, tk), 0)
        c = lax.broadcasted_iota(I32, (tk, tk), 1)
        earlier = (c < r).astype(BF16)
        ones_k = jnp.ones((SUBLANES, tk), BF16)

        def fix(j, run):
            sc = sc_ref[j]
            eq = sc == thr1
            eqb = jnp.where(eq, 1.0, 0.0).astype(BF16)
            before = jnp.dot(earlier, eqb, preferred_element_type=F32) + run[0:1, :]
            sc_ref[j] = jnp.where(jnp.logical_and(eq, before >= need), NEG_INF, sc)
            return run + jnp.dot(ones_k, eqb, preferred_element_type=F32)

        lax.fori_loop(0, nkb, fix, jnp.zeros((SUBLANES, tq), F32))

    m_ref[...] = jnp.full(m_ref.shape, NEG_INF, F32)
    acc_ref[...] = jnp.zeros(acc_ref.shape, F32)

    def att_blk(j, carry):
        k0 = pl.multiple_of(j * tk, tk)
        s = lax.dot_general(k_ref[pl.ds(k0, tk), :], qs_ref[...], NT_DIMS, preferred_element_type=F32)
        bias = jnp.where(sc_ref[j] >= thr1, 0.0, NEG_INF)
        for h in range(N_HEADS):
            x = s[:, h * tq:(h + 1) * tq] + bias
            bm = jnp.max(jnp.max(x.reshape(ngr, SUBLANES, tq), axis=0), axis=0, keepdims=True)
            m_old = m_ref[h:h + 1, :]
            m_new = jnp.maximum(m_old, bm)
            m_safe = jnp.where(m_new == NEG_INF, 0.0, m_new)
            p = jnp.exp2(x - m_safe).astype(BF16)
            pv = jnp.dot(vt_ref[j, h // hpg], p, preferred_element_type=F32)
            acc_ref[h] = acc_ref[h] * jnp.exp2(m_old - m_safe) + pv
            m_ref[h:h + 1, :] = m_new
        return carry

    lax.fori_loop(0, nkb, att_blk, 0)

    for pr in range(N_HEADS // 2):
        outs = []
        for h in (2 * pr, 2 * pr + 1):
            a = acc_ref[h]
            outs.append(a[:HEAD_DIM, :] / a[HEAD_DIM:HEAD_DIM + 1, :])
        o2 = jnp.concatenate(outs, axis=0)
        o_ref[:, 2 * pr * HEAD_DIM:(2 * pr + 2) * HEAD_DIM] = o2.T.astype(o_ref.dtype)


def _dsa_prompt(qi, wt, q, kib, kb, vt, tq, tk):
    bsz, t_len, d_qi = qi.shape
    d_attn = q.shape[2]
    d_kv = kb.shape[2]
    nkb = t_len // tk
    k_top = min(TOPK_MAX, t_len // 4)
    assert tq == tk and tq % LANES == 0 and t_len % tq == 0 and vt.shape[1] == nkb
    assert nkb * (tk // (2 * SUBLANES * COUNT_WAYS)) <= 256
    kern = functools.partial(_dsa_prompt_kernel, tq=tq, tk=tk, k_top=k_top)
    tile = lambda w: pl.BlockSpec((None, tq, w), lambda b, i: (b, i, 0))
    full = lambda w: pl.BlockSpec((None, t_len, w), lambda b, i: (b, 0, 0))
    return pl.pallas_call(
        kern, grid=(bsz, t_len // tq),
        in_specs=[tile(d_qi), pl.BlockSpec((None, N_IDX_HEADS, tq), lambda b, i: (b, 0, i)), tile(d_attn),
                  full(IDX_DIM), full(d_kv),
                  pl.BlockSpec((None, nkb, N_KV_HEADS, VT_ROWS, tk), lambda b, i: (b, 0, 0, 0, 0))],
        out_specs=tile(d_attn),
        out_shape=jax.ShapeDtypeStruct((bsz, t_len, d_attn), BF16),
        scratch_shapes=[pltpu.VMEM((nkb, tk, tq), F32),
                        pltpu.VMEM((nkb, tk, tq), BF16),
                        pltpu.VMEM((N_IDX_HEADS * tq, IDX_DIM), BF16),
                        pltpu.VMEM((N_HEADS * tq, N_KV_HEADS * HEAD_DIM), BF16),
                        pltpu.VMEM((N_HEADS, tq), F32),
                        pltpu.VMEM((N_HEADS, VT_ROWS, tq), F32)],
        compiler_params=pltpu.CompilerParams(dimension_semantics=("arbitrary", "arbitrary"),
                                             vmem_limit_bytes=VMEM_LIMIT),
        name="dsa_prompt",
    )(qi, wt, q, kib, kb, vt)


def _page_fetcher(pt_ref, n_pages, streams):
    def copies(seq, slot):
        for p in range(n_pages):
            pg = pt_ref[seq * n_pages + p]
            for hbm, buf, sem in streams:
                page = hbm.shape[2]
                yield pltpu.make_async_copy(hbm.at[pg], buf.at[slot, :, pl.ds(p * page, page)], sem.at[slot])

    def fetch(seq, slot):
        for cp in copies(seq, slot):
            cp.start()

    def wait(seq, slot):
        for cp in copies(seq, slot):
            cp.wait()

    return fetch, wait


def _prefetch_next_and_wait(fetch, wait):
    b = pl.program_id(0)
    slot = b % 2

    @pl.when(b == 0)
    def _():
        fetch(0, 0)

    @pl.when(b + 1 < pl.num_programs(0))
    def _():
        fetch(b + 1, 1 - slot)

    wait(b, slot)
    return b, slot


def _idx_score_kernel(pt_ref, q_ref, w_ref, ci_ref, o_ref, ibuf, sem, *, n_pages):
    b, slot = _prefetch_next_and_wait(*_page_fetcher(pt_ref, n_pages, [(ci_ref, ibuf, sem)]))
    q = q_ref[...]
    w = w_ref[...] * (IDX_DIM ** -0.5)
    d = jnp.dot(q, ibuf[slot].astype(BF16), preferred_element_type=F32)
    o_ref[pl.ds(b % SUBLANES, 1), :] = jnp.sum(jnp.maximum(d, 0.0) * w, axis=0, keepdims=True)


def _idx_scores(page_table, qis, wcol, cache_idx_kt):
    sq, n_pages = page_table.shape
    page = cache_idx_kt.shape[2]
    past = n_pages * page
    kern = functools.partial(_idx_score_kernel, n_pages=n_pages)
    grid_spec = pltpu.PrefetchScalarGridSpec(
        num_scalar_prefetch=1, grid=(sq,),
        in_specs=[pl.BlockSpec((None, N_IDX_HEADS, IDX_DIM), lambda b, pt: (b, 0, 0)),
                  pl.BlockSpec((None, N_IDX_HEADS, 1), lambda b, pt: (b, 0, 0)),
                  pl.BlockSpec(memory_space=pl.ANY)],
        out_specs=pl.BlockSpec((SUBLANES, past), lambda b, pt: (b // SUBLANES, 0)),
        scratch_shapes=[pltpu.VMEM((2, IDX_DIM, past), cache_idx_kt.dtype), pltpu.SemaphoreType.DMA((2,))])
    return pl.pallas_call(
        kern, grid_spec=grid_spec,
        out_shape=jax.ShapeDtypeStruct((sq, past), F32),
        compiler_params=pltpu.CompilerParams(dimension_semantics=("arbitrary",),
                                             vmem_limit_bytes=VMEM_LIMIT),
        name="sample_idx_scores",
    )(page_table.reshape(-1), qis, wcol, cache_idx_kt)


def _sample_select_kernel(sc_ref, qi_ref, ki_ref, wi_ref, mask_ref, selfsel_ref, *, ch, k_top):
    sq, l_past = sc_ref.shape
    nch = l_past // ch
    w = wi_ref[...] * (IDX_DIM ** -0.5)
    qf = qi_ref[...].astype(F32)
    kf32 = ki_ref[...].astype(F32)
    s_self = jnp.zeros((sq, 1), F32)
    for h in range(N_IDX_HEADS):
        dh = jnp.sum(qf[:, h * IDX_DIM:(h + 1) * IDX_DIM] * kf32, axis=1, keepdims=True)
        s_self = s_self + jnp.maximum(dh, 0.0) * w[:, h:h + 1]

    amax = jnp.abs(s_self)
    for c in range(nch):
        amax = jnp.maximum(amax, jnp.max(jnp.abs(sc_ref[:, c * ch:(c + 1) * ch]), axis=1, keepdims=True))

    def count_cmp(thr, strict):
        thr_b = jnp.broadcast_to(thr, (sq, LANES))
        part = jnp.zeros((sq, LANES), F32)
        for c in range(l_past // LANES):
            blk = sc_ref[:, c * LANES:(c + 1) * LANES]
            hit = (blk > thr_b) if strict else (blk >= thr_b)
            part = part + jnp.where(hit, 1.0, 0.0)
        self_hit = (s_self > thr) if strict else (s_self >= thr)
        return jnp.sum(part, axis=1, keepdims=True) + jnp.where(self_hit, 1.0, 0.0)

    kf = float(k_top)
    forced = amax < 0.0
    thr = _threshold_search(lambda t: count_cmp(t, False), amax, forced, k_top)
    need = kf - count_cmp(thr, True)
    tri = _tri_exclusive(ch)
    run = jnp.zeros((sq, 1), F32)
    for c in range(nch):
        sc = sc_ref[:, c * ch:(c + 1) * ch]
        eq = sc == thr
        before = jnp.dot(jnp.where(eq, 1.0, 0.0).astype(BF16), tri, preferred_element_type=F32) + run
        keep = jnp.logical_or(sc > thr, jnp.logical_and(eq, before < need))
        mask_ref[:, c * ch:(c + 1) * ch] = jnp.where(keep, 1.0, 0.0)
        run = run + jnp.sum(jnp.where(eq, 1.0, 0.0), axis=1, keepdims=True)
    self_keep = jnp.logical_or(s_self > thr, jnp.logical_and(s_self == thr, run < need))
    selfsel_ref[...] = jnp.where(self_keep, 1.0, 0.0)


def _sample_select(scores, qi, kib, wi, ch, k_top):
    sq, l_past = scores.shape
    kern = functools.partial(_sample_select_kernel, ch=ch, k_top=k_top)
    return pl.pallas_call(
        kern,
        out_shape=[jax.ShapeDtypeStruct((sq, l_past), F32), jax.ShapeDtypeStruct((sq, 1), F32)],
        compiler_params=pltpu.CompilerParams(vmem_limit_bytes=VMEM_LIMIT),
        name="sample_select",
    )(scores, qi, kib, wi)


def _sample_attn_kernel(pt_ref, q_ref, mask_ref, ks_ref, vs_ref, ss_ref, ck_ref, cv_ref, o_ref,
                        kbuf, vbuf, ksem, vsem, *, n_pages):
    _, slot = _prefetch_next_and_wait(
        *_page_fetcher(pt_ref, n_pages, [(ck_ref, kbuf, ksem), (cv_ref, vbuf, vsem)]))
    hpg = N_HEADS // N_KV_HEADS
    q = q_ref[...]
    s = jnp.dot(q, kbuf[slot].astype(BF16), preferred_element_type=F32)
    sm = jnp.where(mask_ref[...] > 0.5, s, NEG_INF)
    s_self = jnp.sum(q.astype(F32) * ks_ref[...].astype(F32), axis=1, keepdims=True)
    s_self = jnp.where(ss_ref[...] > 0.5, s_self, NEG_INF)
    m = jnp.maximum(jnp.max(sm, axis=1, keepdims=True), s_self)
    m = jnp.where(m == NEG_INF, 0.0, m)
    pb = jnp.exp2(sm - m).astype(BF16)
    p_self = jnp.exp2(s_self - m).astype(BF16).astype(F32)
    l_sum = jnp.sum(pb.astype(F32), axis=1, keepdims=True) + p_self
    acc = (lax.dot_general(pb, vbuf[slot].astype(BF16), NT_DIMS, preferred_element_type=F32)
           + p_self * vs_ref[...].astype(F32))
    o = acc / l_sum
    hrow = lax.broadcasted_iota(I32, o.shape, 0)
    o = jnp.where(hrow < hpg, o, pltpu.roll(o, HEAD_DIM, axis=1))
    o_ref[...] = o[:, :HEAD_DIM].astype(o_ref.dtype)


def _sample_attn(page_table, qs, mask3, kself, vself, selfsel, cache_k2, cache_v2):
    sq, n_pages = page_table.shape
    d_kv, page = cache_k2.shape[1], cache_k2.shape[2]
    past = n_pages * page
    kern = functools.partial(_sample_attn_kernel, n_pages=n_pages)
    per_seq = lambda r, w: pl.BlockSpec((None, r, w), lambda b, pt: (b, 0, 0))
    hbm = pl.BlockSpec(memory_space=pl.ANY)
    grid_spec = pltpu.PrefetchScalarGridSpec(
        num_scalar_prefetch=1, grid=(sq,),
        in_specs=[per_seq(N_HEADS, d_kv), per_seq(1, past), per_seq(1, d_kv), per_seq(1, d_kv),
                  per_seq(1, 1), hbm, hbm],
        out_specs=per_seq(N_HEADS, HEAD_DIM),
        scratch_shapes=[pltpu.VMEM((2, d_kv, past), cache_k2.dtype), pltpu.VMEM((2, d_kv, past), cache_v2.dtype),
                        pltpu.SemaphoreType.DMA((2,)), pltpu.SemaphoreType.DMA((2,))])
    return pl.pallas_call(
        kern, grid_spec=grid_spec,
        out_shape=jax.ShapeDtypeStruct((sq, N_HEADS, HEAD_DIM), BF16),
        compiler_params=pltpu.CompilerParams(dimension_semantics=("arbitrary",),
                                             vmem_limit_bytes=VMEM_LIMIT),
        name="sample_attn",
    )(page_table.reshape(-1), qs, mask3, kself, vself, selfsel, cache_k2, cache_v2)


def _tail_kernel(x_ref, attn_ref, ssm_ref, sgs_ref, sga_ref, wao_ref, wo_ref, g2_ref, wup_ref, wdn_ref,
                 gf_ref, y_ref):
    attn_out = jnp.dot(attn_ref[...], wao_ref[...], preferred_element_type=F32)
    mix = sgs_ref[...] * ssm_ref[...] + sga_ref[...] * attn_out
    x1 = x_ref[...] + jnp.dot(mix.astype(BF16), wo_ref[...], preferred_element_type=F32)
    hh = _rms_norm(x1, g2_ref[...]).astype(BF16)
    up = jnp.dot(hh, wup_ref[...], preferred_element_type=F32)
    r = jnp.square(jnp.maximum(up, 0.0)).astype(BF16)
    x2 = x1 + jnp.dot(r, wdn_ref[...], preferred_element_type=F32)
    y_ref[...] = _rms_norm(x2, gf_ref[...])


def _tail(x2d, attn, ssm, sgs, sga, wao, wo, g2, wup, wdn, gf, tm):
    n, d_model = x2d.shape
    row = lambda w: pl.BlockSpec((tm, w), lambda i: (i, 0))
    return pl.pallas_call(
        _tail_kernel, grid=(n // tm,),
        in_specs=[row(d_model), row(attn.shape[1]), row(d_model), row(d_model), row(d_model),
                  _const_spec(wao.shape), _const_spec(wo.shape), _const_spec((1, d_model)),
                  _const_spec(wup.shape), _const_spec(wdn.shape), _const_spec((1, d_model))],
        out_specs=row(d_model),
        out_shape=jax.ShapeDtypeStruct((n, d_model), F32),
        compiler_params=pltpu.CompilerParams(dimension_semantics=("arbitrary",),
                                             vmem_limit_bytes=VMEM_LIMIT),
        name="tail",
    )(x2d, attn, ssm, sgs, sga, wao, wo, g2.reshape(1, d_model), wup, wdn, gf.reshape(1, d_model))


def _tiles(n_rows, t_len):
    tm = min(256, n_rows)
    tc = min(128, t_len)
    tq = min(256, t_len)
    return tm, tc, tq, tq


def kernel(x_prompt, x_sample, cache_k, cache_v, cache_idx_k, state_ssm_re, state_ssm_im, page_table,
           norm1_g, w_in, ssm_a_re, ssm_a_im, ssm_log_dt, ssm_b_re, ssm_b_im, ssm_c_re, ssm_c_im,
           ssm_d, w_glu, b_glu, w_attn_out, w_o, norm2_g, w_up, w_down, normf_g):
    bsz, t_len, d_model = x_prompt.shape
    sq, s_len, _ = x_sample.shape
    assert s_len == 1, "the sample path handles one new token per sequence"
    n_pool, page = cache_idx_k.shape[0], cache_idx_k.shape[1]
    n_pages = page_table.shape[1]
    past = n_pages * page
    d_ssm = ssm_d.shape[0]
    d_attn = N_HEADS * HEAD_DIM
    d_kv = N_KV_HEADS * HEAD_DIM
    d_qi = N_IDX_HEADS * IDX_DIM
    dims = (d_ssm, d_attn, d_kv, d_qi)

    c_ki = d_ssm + d_attn + 2 * d_kv + d_qi
    c_g = c_ki + IDX_DIM + N_IDX_HEADS
    w_pack = jnp.concatenate(
        [w_in[:, :c_g], jnp.zeros((d_model, LANES - IDX_DIM - N_IDX_HEADS), w_in.dtype), w_in[:, c_g:]],
        axis=1).astype(BF16)
    wglu_b, wao_b, wo_b = w_glu.astype(BF16), w_attn_out.astype(BF16), w_o.astype(BF16)
    wup_b, wdn_b = w_up.astype(BF16), w_down.astype(BF16)
    s5p = _s5_params(ssm_a_re, ssm_a_im, ssm_log_dt, ssm_b_re, ssm_b_im, ssm_c_re, ssm_c_im)

    n_p = bsz * t_len
    tm, tc, tq, tk = _tiles(n_p, t_len)
    xp = x_prompt.reshape(n_p, d_model)
    tabs_p = _rope_tables(jnp.arange(t_len, dtype=I32), t_len)
    assert tm == tk
    (u, q, kt, kb, vt, _, vtb, qi, kit, kib, _, wit, sgs, sga) = _in_proj(
        xp, bsz, tabs_p, norm1_g, w_pack, tm, dims)
    ssm_out, re_p, im_p = _s5_prompt(u.reshape(bsz, t_len, d_ssm), s5p, ssm_d, wglu_b, b_glu, tc)
    r3 = lambda a: a.reshape(bsz, t_len, a.shape[-1])
    attn = _dsa_prompt(r3(qi), wit, r3(q), r3(kib), r3(kb), vtb, tq, tk)
    y_p = _tail(xp, attn.reshape(n_p, d_attn), ssm_out.reshape(n_p, d_model), sgs, sga,
                wao_b, wo_b, norm2_g, wup_b, wdn_b, normf_g, tm)

    xs = x_sample.reshape(sq, d_model)
    tabs_s = _rope_tables(jnp.full((sq,), past, I32), sq)
    (u_s, q_s, kt_s, kb_s, vt_s, vb_s, _, qi_s, kit_s, kib_s, wi_s, _, sgs_s, sga_s) = _in_proj(
        xs, 1, tabs_s, norm1_g, w_pack, sq, dims)
    sd = state_ssm_re.shape[1] * state_ssm_re.shape[2]
    ssm_s, re_s, im_s = _s5_sample(u_s, state_ssm_re.reshape(sq, sd), state_ssm_im.reshape(sq, sd),
                                   s5p, ssm_d, wglu_b, b_glu)
    scores = _idx_scores(page_table, qi_s.reshape(sq, N_IDX_HEADS, IDX_DIM),
                         wi_s.reshape(sq, N_IDX_HEADS, 1), jnp.transpose(cache_idx_k, (0, 2, 1)))
    k_top_s = min(TOPK_MAX, (past + s_len) // 4)
    mask, selfsel = _sample_select(scores, qi_s, kib_s, wi_s, min(512, past), k_top_s)
    q4 = q_s.reshape(sq, N_KV_HEADS, N_HEADS // N_KV_HEADS, HEAD_DIM)
    qs_pad = (q4[:, :, :, None, :] * jnp.eye(N_KV_HEADS, dtype=BF16)[None, :, None, :, None]
              ).reshape(sq, N_HEADS, d_kv)
    feat_major = lambda c: jnp.transpose(c, (0, 2, 3, 1)).reshape(n_pool, d_kv, page)
    attn_s = _sample_attn(page_table, qs_pad, mask.reshape(sq, 1, past),
                          kb_s.reshape(sq, 1, d_kv), vb_s.reshape(sq, 1, d_kv), selfsel.reshape(sq, 1, 1),
                          feat_major(cache_k), feat_major(cache_v))
    y_s = _tail(xs, attn_s.reshape(sq, d_attn), ssm_s, sgs_s, sga_s,
                wao_b, wo_b, norm2_g, wup_b, wdn_b, normf_g, sq)

    g_ssm, p_ssm = state_ssm_re.shape[1], state_ssm_re.shape[2]
    kv_out = lambda a: jnp.transpose(a.reshape(a.shape[0], N_KV_HEADS, HEAD_DIM, a.shape[2]), (0, 3, 1, 2))
    return (y_p.reshape(bsz, t_len, d_model), y_s.reshape(sq, s_len, d_model),
            kv_out(kt), kv_out(vt), jnp.transpose(kit, (0, 2, 1)),
            re_p.reshape(bsz, g_ssm, p_ssm), im_p.reshape(bsz, g_ssm, p_ssm),
            kv_out(kt_s).reshape(sq, s_len, N_KV_HEADS, HEAD_DIM),
            kv_out(vt_s).reshape(sq, s_len, N_KV_HEADS, HEAD_DIM),
            jnp.transpose(kit_s, (0, 2, 1)).reshape(sq, s_len, IDX_DIM),
            re_s.reshape(sq, g_ssm, p_ssm), im_s.reshape(sq, g_ssm, p_ssm))
```

```python
import functools
import math

import jax
import jax.numpy as jnp
from jax import lax
from jax.experimental import pallas as pl
from jax.experimental.pallas import tpu as pltpu

F32 = jnp.float32
BF16 = jnp.bfloat16
I32 = jnp.int32

SSM_GROUP = 16
SSM_STATE = 64
N_HEADS = 8
N_KV_HEADS = 2
HEAD_DIM = 64
ROT_DIM = HEAD_DIM // 4
N_IDX_HEADS = 8
IDX_DIM = 64
IDX_ROT_DIM = IDX_DIM // 4
ROPE_THETA = 500000.0
TOPK_MAX = 256
EPS = 1e-6
LOG2_E = math.log2(math.e)

LANES = 128
SUBLANES = 8
MXU_COLS = 256
S5_SEQS = 8
S5_SCAN_COLS = 4
COUNT_WAYS = 4
VT_ROWS = 80
VMEM_LIMIT = 56 * 1024 * 1024

NEG_INF = float("-inf")
FLT_MAX = float(jnp.finfo(jnp.float32).max)
KEY_LOW = -2139095040
NT_DIMS = (((1,), (1,)), ((), ()))


def _const_spec(shape):
    nd = len(shape)
    return pl.BlockSpec(shape, lambda *_: (0,) * nd, pipeline_mode=pl.Buffered(1))


def _rms_norm(x, g):
    ms = jnp.mean(x * x, axis=-1, keepdims=True)
    return x * lax.rsqrt(ms + EPS) * g


def _sigmoid(x):
    return 1.0 / (1.0 + jnp.exp(-x))


def _rope(x, cos_t, sin_a, sin_b):
    return (x * cos_t + pltpu.roll(x, LANES - ROT_DIM // 2, axis=1) * sin_a
            + pltpu.roll(x, ROT_DIM // 2, axis=1) * sin_b)


def _in_proj_kernel(x_ref, g_ref, w_ref, cos_ref, sa_ref, sb_ref,
                    u_ref, q_ref, kt_ref, kb_ref, vt_ref, vb_ref, vtb_ref, qi_ref, kit_ref, kib_ref,
                    wi_ref, wit_ref, sgs_ref, sga_ref, *, d_ssm, d_attn, d_kv, d_qi, d_model):
    assert d_kv == LANES
    tm = x_ref.shape[0]
    h = _rms_norm(x_ref[...], g_ref[...]).astype(BF16)
    cos_t, sin_a, sin_b = cos_ref[...], sa_ref[...], sb_ref[...]

    def proj(c0, width):
        return jnp.dot(h, w_ref[:, c0:c0 + width], preferred_element_type=F32)

    off = 0
    u_ref[...] = proj(off, d_ssm)
    off += d_ssm
    def rope_chunks(c0, width):
        for m in range(width // MXU_COLS):
            wide = proj(c0 + m * MXU_COLS, MXU_COLS)
            for c in range(MXU_COLS // LANES):
                yield (m * MXU_COLS // LANES + c,
                       _rope(wide[:, c * LANES:(c + 1) * LANES], cos_t, sin_a, sin_b))

    for c, r in rope_chunks(off, d_attn):
        q_ref[:, c * LANES:(c + 1) * LANES] = (r * (HEAD_DIM ** -0.5 * LOG2_E)).astype(BF16)
    off += d_attn
    kv = proj(off, 2 * d_kv)
    r = _rope(kv[:, :d_kv], cos_t, sin_a, sin_b)
    kt_ref[...] = r.T
    kb_ref[...] = r.astype(BF16)
    off += d_kv
    vv = kv[:, d_kv:]
    vt = vv.T
    vt_ref[...] = vt
    vb_ref[...] = vv.astype(BF16)
    sub = lax.broadcasted_iota(I32, (VT_ROWS - HEAD_DIM, tm), 0)
    ones_pad = jnp.where(sub == 0, 1.0, 0.0)
    for g in range(N_KV_HEADS):
        vtb_ref[g] = jnp.concatenate([vt[g * HEAD_DIM:(g + 1) * HEAD_DIM, :], ones_pad], axis=0).astype(BF16)
    off += d_kv
    for c, r in rope_chunks(off, d_qi):
        qi_ref[:, c * LANES:(c + 1) * LANES] = r.astype(BF16)
    off += d_qi
    kw = proj(off, LANES)
    lane = lax.broadcasted_iota(I32, kw.shape, 1)
    kr = jnp.where(lane < IDX_DIM, _rope(kw, cos_t, sin_a, sin_b), kw * (N_IDX_HEADS ** -0.5))
    krt = kr.T
    kit_ref[...] = krt[:IDX_DIM, :]
    kib_ref[...] = kr[:, :IDX_DIM].astype(BF16)
    wi_ref[...] = kr[:, IDX_DIM:IDX_DIM + N_IDX_HEADS]
    wit_ref[...] = krt[IDX_DIM:IDX_DIM + N_IDX_HEADS, :]
    off += LANES
    sgs_ref[...] = _sigmoid(proj(off, d_model))
    off += d_model
    sga_ref[...] = _sigmoid(proj(off, d_model))


def _rope_tables(pos, n_rows):
    half = ROT_DIM // 2
    inv = ROPE_THETA ** (-jnp.arange(half, dtype=F32) / half)
    ang = pos.astype(F32)[:, None] * inv[None, :]
    cos, sin = jnp.cos(ang), jnp.sin(ang)
    ones = jnp.ones((n_rows, HEAD_DIM - ROT_DIM), F32)
    zeros = jnp.zeros((n_rows, HEAD_DIM - ROT_DIM), F32)
    zh = jnp.zeros((n_rows, half), F32)
    cos_t = jnp.concatenate([cos, cos, ones], axis=1)
    sin_a = jnp.concatenate([-sin, zh, zeros], axis=1)
    sin_b = jnp.concatenate([zh, sin, zeros], axis=1)
    rep = LANES // HEAD_DIM
    return tuple(jnp.tile(t, (1, rep)) for t in (cos_t, sin_a, sin_b))


def _in_proj(x2d, n_seq, pos_tab, norm_g, w_pack, tm, dims):
    n, d_model = x2d.shape
    d_ssm, d_attn, d_kv, d_qi = dims
    cos_t, sin_a, sin_b = pos_tab
    t_len = n // n_seq
    nt = t_len // tm
    grid = (n // tm,)
    row = lambda w: pl.BlockSpec((tm, w), lambda i: (i, 0))
    tab = pl.BlockSpec((tm, LANES), lambda i: (i % nt, 0))
    feat = lambda r: pl.BlockSpec((None, r, tm), lambda i: (i // nt, 0, i % nt))
    kern = functools.partial(_in_proj_kernel, d_ssm=d_ssm, d_attn=d_attn, d_kv=d_kv, d_qi=d_qi,
                             d_model=d_model)
    outs = [
        (jax.ShapeDtypeStruct((n, d_ssm), F32), row(d_ssm)),
        (jax.ShapeDtypeStruct((n, d_attn), BF16), row(d_attn)),
        (jax.ShapeDtypeStruct((n_seq, d_kv, t_len), F32), feat(d_kv)),
        (jax.ShapeDtypeStruct((n, d_kv), BF16), row(d_kv)),
        (jax.ShapeDtypeStruct((n_seq, d_kv, t_len), F32), feat(d_kv)),
        (jax.ShapeDtypeStruct((n, d_kv), BF16), row(d_kv)),
        (jax.ShapeDtypeStruct((n_seq, nt, N_KV_HEADS, VT_ROWS, tm), BF16),
         pl.BlockSpec((None, None, N_KV_HEADS, VT_ROWS, tm), lambda i: (i // nt, i % nt, 0, 0, 0))),
        (jax.ShapeDtypeStruct((n, d_qi), BF16), row(d_qi)),
        (jax.ShapeDtypeStruct((n_seq, IDX_DIM, t_len), F32), feat(IDX_DIM)),
        (jax.ShapeDtypeStruct((n, IDX_DIM), BF16), row(IDX_DIM)),
        (jax.ShapeDtypeStruct((n, N_IDX_HEADS), F32), row(N_IDX_HEADS)),
        (jax.ShapeDtypeStruct((n_seq, N_IDX_HEADS, t_len), F32), feat(N_IDX_HEADS)),
        (jax.ShapeDtypeStruct((n, d_model), F32), row(d_model)),
        (jax.ShapeDtypeStruct((n, d_model), F32), row(d_model)),
    ]
    out_shapes = [o[0] for o in outs]
    out_specs = [o[1] for o in outs]
    return pl.pallas_call(
        kern, grid=grid,
        in_specs=[row(d_model), _const_spec((1, d_model)), _const_spec(w_pack.shape), tab, tab, tab],
        out_specs=out_specs, out_shape=out_shapes,
        compiler_params=pltpu.CompilerParams(dimension_semantics=("arbitrary",),
                                             vmem_limit_bytes=VMEM_LIMIT),
        name="in_proj",
    )(x2d, norm_g.reshape(1, d_model), w_pack, cos_t, sin_a, sin_b)


def _s5_readout(y, u, dsk_ref, wglu_ref, bglu_ref, d_model):
    y = y + dsk_ref[...] * u
    cdf = 0.5 * (1.0 + jnp.tanh(math.sqrt(2.0 / math.pi) * (y + 0.044715 * (y * y * y))))
    gl = (y * cdf).astype(BF16)
    z = jnp.dot(gl, wglu_ref[...], preferred_element_type=F32) + bglu_ref[...]
    return z[:, :d_model] * _sigmoid(z[:, d_model:])


def _s5_prompt_kernel(u_ref, perm_ref, bcat_ref, are_ref, aim_ref, ccre_ref, ccim_ref, dsk_ref, wglu_ref,
                      bglu_ref, out_ref, sre_ref, sim_ref, hre, him, zs, cr, ci, *, tc, d_model):
    @pl.when(pl.program_id(1) == 0)
    def _():
        cr[...] = jnp.zeros_like(cr)
        ci[...] = jnp.zeros_like(ci)

    nseq = u_ref.shape[0]
    d_ssm, n_col = u_ref.shape[2], hre.shape[0]
    sd = n_col * LANES
    u = u_ref[...].reshape(nseq * tc, d_ssm)
    ub = jnp.dot(perm_ref[...], u.astype(BF16), preferred_element_type=F32).astype(BF16)
    for c in range(n_col):
        kc = (c * LANES * d_ssm // sd) // LANES
        bu = jnp.dot(ub[:, kc * LANES:(kc + 1) * LANES], bcat_ref[c], preferred_element_type=F32)
        hre[c] = bu[:, :LANES]
        him[c] = bu[:, LANES:]

    for c0 in range(0, n_col, S5_SCAN_COLS):
        cs = range(c0, c0 + S5_SCAN_COLS)
        ar = [jnp.broadcast_to(are_ref[:, c * LANES:(c + 1) * LANES], (nseq, LANES)) for c in cs]
        ai = [jnp.broadcast_to(aim_ref[:, c * LANES:(c + 1) * LANES], (nseq, LANES)) for c in cs]

        def step(t, carry, cs=cs, ar=ar, ai=ai):
            rows = pl.ds(pl.multiple_of(t * nseq, nseq), nseq)
            out = []
            for n, c in enumerate(cs):
                pr, pi = carry[n]
                xr = hre[c, rows, :] + (ar[n] * pr - ai[n] * pi)
                xi = him[c, rows, :] + (ar[n] * pi + ai[n] * pr)
                hre[c, rows, :] = xr
                him[c, rows, :] = xi
                out.append((xr, xi))
            return tuple(out)

        init = tuple((cr[:, c * LANES:(c + 1) * LANES], ci[:, c * LANES:(c + 1) * LANES]) for c in cs)
        fin = lax.fori_loop(0, tc, step, init, unroll=4)
        for n, c in enumerate(cs):
            cr[:, c * LANES:(c + 1) * LANES] = fin[n][0]
            ci[:, c * LANES:(c + 1) * LANES] = fin[n][1]
    sre_ref[...] = cr[...]
    sim_ref[...] = ci[...]
    n_out = d_ssm // LANES
    per = n_col // n_out
    wide = lambda ref, j: jnp.concatenate([ref[c] for c in range(j * per, (j + 1) * per)], axis=1).astype(BF16)
    for j in range(n_out):
        zs[j] = (jnp.dot(wide(hre, j), ccre_ref[j], preferred_element_type=F32)
                 + jnp.dot(wide(him, j), ccim_ref[j], preferred_element_type=F32))
    y = jnp.concatenate(
        [jnp.concatenate([zs[j, pl.ds(b, tc, stride=nseq), :] for b in range(nseq)], axis=0)
         for j in range(n_out)], axis=1)
    out_ref[...] = _s5_readout(y, u, dsk_ref, wglu_ref, bglu_ref, d_model).reshape(nseq, tc, d_model)


def _s5_sample_kernel(u_ref, h0r_ref, h0i_ref, are_ref, aim_ref, bre_ref, bim_ref, cre_ref, cim_ref,
                      dsk_ref, wglu_ref, bglu_ref, out_ref, sre_ref, sim_ref, *, d_model):
    u = u_ref[...]
    ub = u.astype(BF16)
    ar, ai = are_ref[...], aim_ref[...]
    h0r, h0i = h0r_ref[...], h0i_ref[...]
    hr = (ar * h0r - ai * h0i) + jnp.dot(ub, bre_ref[...], preferred_element_type=F32)
    hi = (ar * h0i + ai * h0r) + jnp.dot(ub, bim_ref[...], preferred_element_type=F32)
    sre_ref[...] = hr
    sim_ref[...] = hi
    y = (jnp.dot(hr.astype(BF16), cre_ref[...], preferred_element_type=F32)
         + jnp.dot(hi.astype(BF16), cim_ref[...], preferred_element_type=F32))
    out_ref[...] = _s5_readout(y, u, dsk_ref, wglu_ref, bglu_ref, d_model)


def _s5_params(a_re, a_im, log_dt, b_re, b_im, c_re, c_im):
    g, p = a_re.shape
    a_c = lax.complex(a_re.astype(F32), a_im.astype(F32))
    dt = jnp.exp(log_dt.astype(F32))[:, None]
    a_bar = jnp.exp(a_c * dt)
    b_bar = ((a_bar - 1.0) / a_c)[:, :, None] * lax.complex(b_re.astype(F32), b_im.astype(F32))
    eye = jnp.eye(g, dtype=F32)
    n = b_re.shape[2]

    def bmat(b):
        return jnp.einsum('gpn,gh->gnhp', b, eye).reshape(g * n, g * p)

    def cmat(c):
        return jnp.einsum('gnp,gh->gphn', c, eye).reshape(g * p, g * n)

    bre, bim = bmat(jnp.real(b_bar)).astype(BF16), bmat(jnp.imag(b_bar)).astype(BF16)
    cre, cim = cmat(c_re.astype(F32)).astype(BF16), cmat(-c_im.astype(F32)).astype(BF16)
    d_in, sd = bre.shape
    assert d_in % LANES == 0 and sd % LANES == 0 and LANES % (LANES * d_in // sd) == 0
    bcat = jnp.stack([
        jnp.concatenate([m[(c * LANES * d_in // sd) // LANES * LANES:][:LANES, c * LANES:(c + 1) * LANES]
                         for m in (bre, bim)], axis=1)
        for c in range(sd // LANES)], axis=0)
    n_out = d_in // LANES
    span = sd // n_out
    ccre, ccim = (jnp.stack([m[j * span:(j + 1) * span, j * LANES:(j + 1) * LANES] for j in range(n_out)],
                            axis=0) for m in (cre, cim))
    a1 = a_bar.reshape(1, g * p)
    return jnp.real(a1), jnp.imag(a1), bre, bim, cre, cim, bcat, ccre, ccim


def _s5_prompt(u3, s5p, dsk, wglu, bglu, tc):
    bsz, t_len, d_ssm = u3.shape
    are, aim, bre, _, _, _, bcat, ccre, ccim = s5p
    sd = bre.shape[1]
    d_model = wglu.shape[1] // 2
    nseq = math.gcd(bsz, S5_SEQS)
    assert sd % (S5_SCAN_COLS * LANES) == 0 and tc % SUBLANES == 0
    kern = functools.partial(_s5_prompt_kernel, tc=tc, d_model=d_model)
    state = pl.BlockSpec((nseq, sd), lambda b, c: (b, 0))
    dst = jnp.arange(nseq * tc)
    perm = (jnp.arange(nseq * tc)[None, :] == ((dst % nseq) * tc + dst // nseq)[:, None]).astype(BF16)
    return pl.pallas_call(
        kern, grid=(bsz // nseq, t_len // tc),
        in_specs=[pl.BlockSpec((nseq, tc, d_ssm), lambda b, c: (b, c, 0)), _const_spec(perm.shape),
                  _const_spec(bcat.shape), _const_spec(are.shape), _const_spec(aim.shape),
                  _const_spec(ccre.shape), _const_spec(ccim.shape), _const_spec((1, d_ssm)),
                  _const_spec(wglu.shape), _const_spec((1, 2 * d_model))],
        out_specs=[pl.BlockSpec((nseq, tc, d_model), lambda b, c: (b, c, 0)), state, state],
        out_shape=[jax.ShapeDtypeStruct((bsz, t_len, d_model), F32),
                   jax.ShapeDtypeStruct((bsz, sd), F32),
                   jax.ShapeDtypeStruct((bsz, sd), F32)],
        scratch_shapes=[pltpu.VMEM((sd // LANES, nseq * tc, LANES), F32),
                        pltpu.VMEM((sd // LANES, nseq * tc, LANES), F32),
                        pltpu.VMEM((d_ssm // LANES, nseq * tc, LANES), F32),
                        pltpu.VMEM((nseq, sd), F32), pltpu.VMEM((nseq, sd), F32)],
        compiler_params=pltpu.CompilerParams(dimension_semantics=("arbitrary", "arbitrary"),
                                             vmem_limit_bytes=VMEM_LIMIT),
        name="s5_prompt",
    )(u3, perm, bcat, are, aim, ccre, ccim, dsk.reshape(1, d_ssm), wglu, bglu.reshape(1, 2 * d_model))


def _s5_sample(u2, h0r, h0i, s5p, dsk, wglu, bglu):
    n, d_ssm = u2.shape
    are, aim, bre, bim, cre, cim = s5p[:6]
    sd = bre.shape[1]
    d_model = wglu.shape[1] // 2
    kern = functools.partial(_s5_sample_kernel, d_model=d_model)
    return pl.pallas_call(
        kern,
        out_shape=[jax.ShapeDtypeStruct((n, d_model), F32),
                   jax.ShapeDtypeStruct((n, sd), F32),
                   jax.ShapeDtypeStruct((n, sd), F32)],
        compiler_params=pltpu.CompilerParams(vmem_limit_bytes=VMEM_LIMIT),
        name="s5_sample",
    )(u2, h0r, h0i, are, aim, bre, bim, cre, cim, dsk.reshape(1, d_ssm), wglu,
      bglu.reshape(1, 2 * d_model))


def _f2k(x):
    b = lax.bitcast_convert_type(x, I32)
    return jnp.where(b < 0, b ^ 0x7FFFFFFF, b)


def _k2f(k):
    return lax.bitcast_convert_type(jnp.where(k < 0, k ^ 0x7FFFFFFF, k), F32)


def _threshold_search(count_ge, amax, forced, k_top):
    kf = float(k_top)
    lo0 = jnp.where(forced, KEY_LOW, _f2k(-amax))
    hi0 = jnp.where(forced, KEY_LOW + 1, _f2k(amax) + 1)

    def cond(st):
        lo, hi, _ = st
        return jnp.max(jnp.where(hi > lo + 1, 1.0, 0.0)) > 0.5

    def body(st):
        lo, hi, it = st
        active = hi > lo + 1
        mid_i = (lo >> 1) + (hi >> 1) + (lo & hi & 1)
        mk = _f2k(0.5 * _k2f(lo) + 0.5 * _k2f(hi))
        use_f = jnp.logical_and(it % 2 == 0, jnp.logical_and(mk > lo, mk < hi))
        mid = jnp.where(use_f, mk, mid_i)
        c = count_ge(_k2f(mid))
        ge = c >= kf
        nlo = jnp.where(ge, mid, lo)
        nhi = jnp.where(c == kf, mid + 1, jnp.where(ge, hi, mid))
        return jnp.where(active, nlo, lo), jnp.where(active, nhi, hi), it + 1

    lo, _, _ = lax.while_loop(cond, body, (lo0, hi0, jnp.int32(0)))
    return _k2f(lo)


def _tri_exclusive(n):
    r = lax.broadcasted_iota(I32, (n, n), 0)
    c = lax.broadcasted_iota(I32, (n, n), 1)
    return (r < c).astype(BF16)


def _pipelined_blocks(n, produce, consume, buf_a, buf_b, carry):
    produce(0, buf_a)

    def pair(t, c):
        produce(2 * t + 1, buf_b)
        c = consume(2 * t, buf_a, c)
        produce(2 * t + 2, buf_a)
        return consume(2 * t + 1, buf_b, c)

    carry = lax.fori_loop(0, n // 2, pair, carry)
    return lax.cond(n % 2 == 1, lambda c: consume(n - 1, buf_a, c), lambda c: c, carry)


def _dsa_prompt_kernel(qi_ref, wt_ref, q_ref, ki_ref, k_ref, vt_ref, o_ref,
                       sc_ref, scb_ref, qis_ref, qs_ref, m_ref, acc_ref, sa_ref, sb_ref, *, tq, tk, k_top):
    i = pl.program_id(1)
    nkb = i + 1
    hpg = N_HEADS // N_KV_HEADS
    ngr = tk // SUBLANES
    kf = float(k_top)

    qi = qi_ref[...]
    q = q_ref[...]
    zpad = jnp.zeros((tq, HEAD_DIM), BF16)
    for h in range(N_IDX_HEADS):
        qis_ref[h * tq:(h + 1) * tq, :] = qi[:, h * IDX_DIM:(h + 1) * IDX_DIM]
    for h in range(N_HEADS):
        qh = q[:, h * HEAD_DIM:(h + 1) * HEAD_DIM]
        pair = [qh, zpad] if h // hpg == 0 else [zpad, qh]
        qs_ref[h * tq:(h + 1) * tq, :] = jnp.concatenate(pair, axis=1)

    w8 = wt_ref[...] * (IDX_DIM ** -0.5)
    qpos = i * tq + lax.broadcasted_iota(I32, (SUBLANES, tq), 1)

    def idx_dots(j, dst):
        k0 = pl.multiple_of(jnp.minimum(j, nkb - 1) * tk, tk)
        dst[...] = lax.dot_general(ki_ref[pl.ds(k0, tk), :], qis_ref[...], NT_DIMS, preferred_element_type=F32)

    def score_blk(j, src, amx):
        mags = []
        for c in range(tq // LANES):
            acc = jnp.zeros((tk, LANES), F32)
            for h in range(N_IDX_HEADS):
                cols = slice(h * tq + c * LANES, h * tq + (c + 1) * LANES)
                acc = acc + jnp.maximum(src[:, cols], 0.0) * w8[h:h + 1, c * LANES:(c + 1) * LANES]
            kpos = j * tk + lax.broadcasted_iota(I32, (tk, LANES), 0)
            causal = kpos <= i * tq + c * LANES + lax.broadcasted_iota(I32, (tk, LANES), 1)
            mag = jnp.where(causal, jnp.abs(acc), 0.0)
            acc = jnp.where(causal, acc, NEG_INF)
            sc_ref[j, :, c * LANES:(c + 1) * LANES] = acc
            top = lax.bitcast_convert_type(lax.bitcast_convert_type(acc, I32) & jnp.int32(-65536), F32)
            scb_ref[j, :, c * LANES:(c + 1) * LANES] = top.astype(BF16)
            mags.append(jnp.max(mag.reshape(ngr, SUBLANES, LANES), axis=0))
        return jnp.maximum(amx, jnp.concatenate(mags, axis=1))

    amx = _pipelined_blocks(nkb, idx_dots, score_blk, sa_ref, sb_ref, jnp.zeros((SUBLANES, tq), F32))

    rep = lambda x: jnp.broadcast_to(x, (SUBLANES, tq))
    amax = rep(jnp.max(amx, axis=0, keepdims=True))
    forced = (qpos + 1) <= k_top

    packed = 2 * SUBLANES
    one_b, zero_b = jnp.ones((), BF16), jnp.zeros((), BF16)

    def count16(k16):
        bits = jnp.where(k16 < 0, k16 ^ 0x7FFF, k16) << 16
        tb = jnp.broadcast_to(lax.bitcast_convert_type(bits, F32)[0:1, :], (packed, tq)).astype(BF16)

        def body(j, part):
            blk = scb_ref[j].reshape(tk // packed // COUNT_WAYS, COUNT_WAYS, packed, tq)
            hit = jnp.where(blk >= tb[None, None], one_b, zero_b)
            for g in range(hit.shape[0]):
                part = part + hit[g]
            return part

        part = lax.fori_loop(0, nkb, body, jnp.zeros((COUNT_WAYS, packed, tq), BF16))
        return rep(jnp.sum(jnp.sum(part.astype(F32), axis=0), axis=0, keepdims=True))

    def coarse_pass(lo16, hi16):
        active = hi16 > lo16 + 1
        mid = lo16 + ((hi16 - lo16) >> 1)
        c = count16(mid)
        ge = c >= kf
        nlo = jnp.where(ge, mid, lo16)
        nhi = jnp.where(c == kf, mid, jnp.where(ge, hi16, mid))
        return jnp.where(active, nlo, lo16), jnp.where(active, nhi, hi16)

    def coarse_body(st):
        lo16, hi16, _ = st
        lo16, hi16 = coarse_pass(*coarse_pass(lo16, hi16))
        return lo16, hi16, (jnp.max(jnp.where(hi16 > lo16 + 1, 1.0, 0.0)) > 0.5).astype(I32)

    lo16 = jnp.where(forced, KEY_LOW >> 16, _f2k(-amax) >> 16)
    hi16 = jnp.where(forced, KEY_LOW >> 16, (_f2k(amax) >> 16) + 1)
    lo16, hi16, _ = lax.while_loop(lambda st: st[2] > 0, coarse_body, (lo16, hi16, jnp.int32(1)))
    lo0 = lo16 << 16
    hi0 = jnp.where(hi16 == lo16, lo0 + 1, hi16 << 16)

    def count(thr, strict):
        def body(j, part):
            blk = sc_ref[j].reshape(ngr // COUNT_WAYS, COUNT_WAYS, SUBLANES, tq)
            hit = (blk > thr[None, None]) if strict else (blk >= thr[None, None])
            return part + jnp.sum(jnp.where(hit, 1.0, 0.0), axis=0)

        part = lax.fori_loop(0, nkb, body, jnp.zeros((COUNT_WAYS, SUBLANES, tq), F32))
        return rep(jnp.sum(jnp.sum(part, axis=0), axis=0, keepdims=True))

    def search_pass(lo, hi, value_space):
        active = hi > lo + 1
        span = hi - lo
        mid = lo + lax.shift_right_logical(span, jnp.ones_like(span))
        if value_space:
            mk = _f2k(0.5 * _k2f(lo) + 0.5 * _k2f(hi))
            mid = jnp.where(jnp.logical_and(mk > lo, mk < hi), mk, mid)
        c = count(_k2f(mid), False)
        ge = c >= kf
        nlo = jnp.where(ge, mid, lo)
        nhi = jnp.where(c == kf, mid + 1, jnp.where(ge, hi, mid))
        return jnp.where(active, nlo, lo), jnp.where(active, nhi, hi)

    def search_body(st):
        lo, hi, _ = st
        lo, hi = search_pass(lo, hi, True)
        lo, hi = search_pass(lo, hi, False)
        return lo, hi, (jnp.max(jnp.where(hi > lo + 1, 1.0, 0.0)) > 0.5).astype(I32)

    lo, _, _ = lax.while_loop(lambda st: st[2] > 0, search_body, (lo0, hi0, jnp.int32(1)))
    thr = _k2f(lo)
    c_ge = count(thr, False)
    thr1 = thr[0:1, :]

    @pl.when(jnp.max(jnp.where(c_ge > kf, 1.0, 0.0)) > 0.5)
    def _():
        need = jnp.where(c_ge > kf, kf - count(thr, True), 1e30)[0:1, :]
        r = lax.broadcasted_iota(I32, (tk, tk), 0)
        c = lax.broadcasted_iota(I32, (tk, tk), 1)
        earlier = (c < r).astype(BF16)
        ones_k = jnp.ones((SUBLANES, tk), BF16)

        def fix(j, run):
            sc = sc_ref[j]
            eq = sc == thr1
            eqb = jnp.where(eq, 1.0, 0.0).astype(BF16)
            before = jnp.dot(earlier, eqb, preferred_element_type=F32) + run[0:1, :]
            sc_ref[j] = jnp.where(jnp.logical_and(eq, before >= need), NEG_INF, sc)
            return run + jnp.dot(ones_k, eqb, preferred_element_type=F32)

        lax.fori_loop(0, nkb, fix, jnp.zeros((SUBLANES, tq), F32))

    m_ref[...] = jnp.full(m_ref.shape, NEG_INF, F32)
    acc_ref[...] = jnp.zeros(acc_ref.shape, F32)

    def logits(j, dst):
        jc = jnp.minimum(j, nkb - 1)
        k0 = pl.multiple_of(jc * tk, tk)
        dst[...] = lax.dot_general(k_ref[pl.ds(k0, tk), :], qs_ref[...], NT_DIMS, preferred_element_type=F32)

    def softmax_pv(j, src):
        jc = j
        bias = jnp.where(sc_ref[j] >= thr1, 0.0, NEG_INF)
        for h in range(N_HEADS):
            x = src[:, h * tq:(h + 1) * tq] + bias
            bm = jnp.max(jnp.max(x.reshape(ngr, SUBLANES, tq), axis=0), axis=0, keepdims=True)
            m_old = m_ref[h:h + 1, :]
            m_new = jnp.maximum(m_old, bm)
            m_safe = jnp.where(m_new == NEG_INF, 0.0, m_new)
            p = jnp.exp2(x - m_safe).astype(BF16)
            pv = jnp.dot(vt_ref[jc, h // hpg], p, preferred_element_type=F32)
            acc_ref[h] = acc_ref[h] * jnp.exp2(m_old - m_safe) + pv
            m_ref[h:h + 1, :] = m_new

    def consume(j, src, carry):
        softmax_pv(j, src)
        return carry

    _pipelined_blocks(nkb, logits, consume, sa_ref, sb_ref, jnp.int32(0))

    for pr in range(N_HEADS // 2):
        outs = []
        for h in (2 * pr, 2 * pr + 1):
            a = acc_ref[h]
            outs.append(a[:HEAD_DIM, :] / a[HEAD_DIM:HEAD_DIM + 1, :])
        o2 = jnp.concatenate(outs, axis=0)
        o_ref[:, 2 * pr * HEAD_DIM:(2 * pr + 2) * HEAD_DIM] = o2.T.astype(o_ref.dtype)


def _dsa_prompt(qi, wt, q, kib, kb, vt, tq, tk):
    bsz, t_len, d_qi = qi.shape
    d_attn = q.shape[2]
    d_kv = kb.shape[2]
    nkb = t_len // tk
    k_top = min(TOPK_MAX, t_len // 4)
    assert tq == tk and tq % LANES == 0 and t_len % tq == 0 and vt.shape[1] == nkb
    assert nkb * (tk // (2 * SUBLANES * COUNT_WAYS)) <= 256
    kern = functools.partial(_dsa_prompt_kernel, tq=tq, tk=tk, k_top=k_top)
    tile = lambda w: pl.BlockSpec((None, tq, w), lambda b, i: (b, i, 0))
    full = lambda w: pl.BlockSpec((None, t_len, w), lambda b, i: (b, 0, 0))
    return pl.pallas_call(
        kern, grid=(bsz, t_len // tq),
        in_specs=[tile(d_qi), pl.BlockSpec((None, N_IDX_HEADS, tq), lambda b, i: (b, 0, i)), tile(d_attn),
                  full(IDX_DIM), full(d_kv),
                  pl.BlockSpec((None, nkb, N_KV_HEADS, VT_ROWS, tk), lambda b, i: (b, 0, 0, 0, 0))],
        out_specs=tile(d_attn),
        out_shape=jax.ShapeDtypeStruct((bsz, t_len, d_attn), BF16),
        scratch_shapes=[pltpu.VMEM((nkb, tk, tq), F32),
                        pltpu.VMEM((nkb, tk, tq), BF16),
                        pltpu.VMEM((N_IDX_HEADS * tq, IDX_DIM), BF16),
                        pltpu.VMEM((N_HEADS * tq, N_KV_HEADS * HEAD_DIM), BF16),
                        pltpu.VMEM((N_HEADS, tq), F32),
                        pltpu.VMEM((N_HEADS, VT_ROWS, tq), F32),
                        pltpu.VMEM((tk, N_HEADS * tq), F32),
                        pltpu.VMEM((tk, N_HEADS * tq), F32)],
        compiler_params=pltpu.CompilerParams(dimension_semantics=("arbitrary", "arbitrary"),
                                             vmem_limit_bytes=VMEM_LIMIT),
        name="dsa_prompt",
    )(qi, wt, q, kib, kb, vt)


def _page_fetcher(pt_ref, n_pages, streams):
    def copies(seq, slot):
        for p in range(n_pages):
            pg = pt_ref[seq * n_pages + p]
            for hbm, buf, sem in streams:
                page = hbm.shape[2]
                yield pltpu.make_async_copy(hbm.at[pg], buf.at[slot, :, pl.ds(p * page, page)], sem.at[slot])

    def fetch(seq, slot):
        for cp in copies(seq, slot):
            cp.start()

    def wait(seq, slot):
        for cp in copies(seq, slot):
            cp.wait()

    return fetch, wait


def _prefetch_next_and_wait(fetch, wait):
    b = pl.program_id(0)
    slot = b % 2

    @pl.when(b == 0)
    def _():
        fetch(0, 0)

    @pl.when(b + 1 < pl.num_programs(0))
    def _():
        fetch(b + 1, 1 - slot)

    wait(b, slot)
    return b, slot


def _idx_score_kernel(pt_ref, q_ref, w_ref, ci_ref, o_ref, ibuf, sem, *, n_pages):
    b, slot = _prefetch_next_and_wait(*_page_fetcher(pt_ref, n_pages, [(ci_ref, ibuf, sem)]))
    q = q_ref[...]
    w = w_ref[...] * (IDX_DIM ** -0.5)
    d = jnp.dot(q, ibuf[slot].astype(BF16), preferred_element_type=F32)
    o_ref[pl.ds(b % SUBLANES, 1), :] = jnp.sum(jnp.maximum(d, 0.0) * w, axis=0, keepdims=True)


def _idx_scores(page_table, qis, wcol, cache_idx_kt):
    sq, n_pages = page_table.shape
    page = cache_idx_kt.shape[2]
    past = n_pages * page
    kern = functools.partial(_idx_score_kernel, n_pages=n_pages)
    grid_spec = pltpu.PrefetchScalarGridSpec(
        num_scalar_prefetch=1, grid=(sq,),
        in_specs=[pl.BlockSpec((None, N_IDX_HEADS, IDX_DIM), lambda b, pt: (b, 0, 0)),
                  pl.BlockSpec((None, N_IDX_HEADS, 1), lambda b, pt: (b, 0, 0)),
                  pl.BlockSpec(memory_space=pl.ANY)],
        out_specs=pl.BlockSpec((SUBLANES, past), lambda b, pt: (b // SUBLANES, 0)),
        scratch_shapes=[pltpu.VMEM((2, IDX_DIM, past), cache_idx_kt.dtype), pltpu.SemaphoreType.DMA((2,))])
    return pl.pallas_call(
        kern, grid_spec=grid_spec,
        out_shape=jax.ShapeDtypeStruct((sq, past), F32),
        compiler_params=pltpu.CompilerParams(dimension_semantics=("arbitrary",),
                                             vmem_limit_bytes=VMEM_LIMIT),
        name="sample_idx_scores",
    )(page_table.reshape(-1), qis, wcol, cache_idx_kt)


def _sample_select_kernel(sc_ref, qi_ref, ki_ref, wi_ref, mask_ref, selfsel_ref, *, ch, k_top):
    sq, l_past = sc_ref.shape
    nch = l_past // ch
    w = wi_ref[...] * (IDX_DIM ** -0.5)
    qf = qi_ref[...].astype(F32)
    kf32 = ki_ref[...].astype(F32)
    s_self = jnp.zeros((sq, 1), F32)
    for h in range(N_IDX_HEADS):
        dh = jnp.sum(qf[:, h * IDX_DIM:(h + 1) * IDX_DIM] * kf32, axis=1, keepdims=True)
        s_self = s_self + jnp.maximum(dh, 0.0) * w[:, h:h + 1]

    amax = jnp.abs(s_self)
    for c in range(nch):
        amax = jnp.maximum(amax, jnp.max(jnp.abs(sc_ref[:, c * ch:(c + 1) * ch]), axis=1, keepdims=True))

    def count_cmp(thr, strict):
        thr_b = jnp.broadcast_to(thr, (sq, LANES))
        part = jnp.zeros((sq, LANES), F32)
        for c in range(l_past // LANES):
            blk = sc_ref[:, c * LANES:(c + 1) * LANES]
            hit = (blk > thr_b) if strict else (blk >= thr_b)
            part = part + jnp.where(hit, 1.0, 0.0)
        self_hit = (s_self > thr) if strict else (s_self >= thr)
        return jnp.sum(part, axis=1, keepdims=True) + jnp.where(self_hit, 1.0, 0.0)

    kf = float(k_top)
    forced = amax < 0.0
    thr = _threshold_search(lambda t: count_cmp(t, False), amax, forced, k_top)
    need = kf - count_cmp(thr, True)
    tri = _tri_exclusive(ch)
    run = jnp.zeros((sq, 1), F32)
    for c in range(nch):
        sc = sc_ref[:, c * ch:(c + 1) * ch]
        eq = sc == thr
        before = jnp.dot(jnp.where(eq, 1.0, 0.0).astype(BF16), tri, preferred_element_type=F32) + run
        keep = jnp.logical_or(sc > thr, jnp.logical_and(eq, before < need))
        mask_ref[:, c * ch:(c + 1) * ch] = jnp.where(keep, 1.0, 0.0)
        run = run + jnp.sum(jnp.where(eq, 1.0, 0.0), axis=1, keepdims=True)
    self_keep = jnp.logical_or(s_self > thr, jnp.logical_and(s_self == thr, run < need))
    selfsel_ref[...] = jnp.where(self_keep, 1.0, 0.0)


def _sample_select(scores, qi, kib, wi, ch, k_top):
    sq, l_past = scores.shape
    kern = functools.partial(_sample_select_kernel, ch=ch, k_top=k_top)
    return pl.pallas_call(
        kern,
        out_shape=[jax.ShapeDtypeStruct((sq, l_past), F32), jax.ShapeDtypeStruct((sq, 1), F32)],
        compiler_params=pltpu.CompilerParams(vmem_limit_bytes=VMEM_LIMIT),
        name="sample_select",
    )(scores, qi, kib, wi)


def _sample_attn_kernel(pt_ref, q_ref, mask_ref, ks_ref, vs_ref, ss_ref, ck_ref, cv_ref, o_ref,
                        kbuf, vbuf, ksem, vsem, *, n_pages):
    _, slot = _prefetch_next_and_wait(
        *_page_fetcher(pt_ref, n_pages, [(ck_ref, kbuf, ksem), (cv_ref, vbuf, vsem)]))
    hpg = N_HEADS // N_KV_HEADS
    q = q_ref[...]
    s = jnp.dot(q, kbuf[slot].astype(BF16), preferred_element_type=F32)
    sm = jnp.where(mask_ref[...] > 0.5, s, NEG_INF)
    s_self = jnp.sum(q.astype(F32) * ks_ref[...].astype(F32), axis=1, keepdims=True)
    s_self = jnp.where(ss_ref[...] > 0.5, s_self, NEG_INF)
    m = jnp.maximum(jnp.max(sm, axis=1, keepdims=True), s_self)
    m = jnp.where(m == NEG_INF, 0.0, m)
    pb = jnp.exp2(sm - m).astype(BF16)
    p_self = jnp.exp2(s_self - m).astype(BF16).astype(F32)
    l_sum = jnp.sum(pb.astype(F32), axis=1, keepdims=True) + p_self
    acc = (lax.dot_general(pb, vbuf[slot].astype(BF16), NT_DIMS, preferred_element_type=F32)
           + p_self * vs_ref[...].astype(F32))
    o = acc / l_sum
    hrow = lax.broadcasted_iota(I32, o.shape, 0)
    o = jnp.where(hrow < hpg, o, pltpu.roll(o, HEAD_DIM, axis=1))
    o_ref[...] = o[:, :HEAD_DIM].astype(o_ref.dtype)


def _sample_attn(page_table, qs, mask3, kself, vself, selfsel, cache_k2, cache_v2):
    sq, n_pages = page_table.shape
    d_kv, page = cache_k2.shape[1], cache_k2.shape[2]
    past = n_pages * page
    kern = functools.partial(_sample_attn_kernel, n_pages=n_pages)
    per_seq = lambda r, w: pl.BlockSpec((None, r, w), lambda b, pt: (b, 0, 0))
    hbm = pl.BlockSpec(memory_space=pl.ANY)
    grid_spec = pltpu.PrefetchScalarGridSpec(
        num_scalar_prefetch=1, grid=(sq,),
        in_specs=[per_seq(N_HEADS, d_kv), per_seq(1, past), per_seq(1, d_kv), per_seq(1, d_kv),
                  per_seq(1, 1), hbm, hbm],
        out_specs=per_seq(N_HEADS, HEAD_DIM),
        scratch_shapes=[pltpu.VMEM((2, d_kv, past), cache_k2.dtype), pltpu.VMEM((2, d_kv, past), cache_v2.dtype),
                        pltpu.SemaphoreType.DMA((2,)), pltpu.SemaphoreType.DMA((2,))])
    return pl.pallas_call(
        kern, grid_spec=grid_spec,
        out_shape=jax.ShapeDtypeStruct((sq, N_HEADS, HEAD_DIM), BF16),
        compiler_params=pltpu.CompilerParams(dimension_semantics=("arbitrary",),
                                             vmem_limit_bytes=VMEM_LIMIT),
        name="sample_attn",
    )(page_table.reshape(-1), qs, mask3, kself, vself, selfsel, cache_k2, cache_v2)


def _tail_kernel(x_ref, attn_ref, ssm_ref, sgs_ref, sga_ref, wao_ref, wo_ref, g2_ref, wup_ref, wdn_ref,
                 gf_ref, y_ref):
    attn_out = jnp.dot(attn_ref[...], wao_ref[...], preferred_element_type=F32)
    mix = sgs_ref[...] * ssm_ref[...] + sga_ref[...] * attn_out
    x1 = x_ref[...] + jnp.dot(mix.astype(BF16), wo_ref[...], preferred_element_type=F32)
    hh = _rms_norm(x1, g2_ref[...]).astype(BF16)
    up = jnp.dot(hh, wup_ref[...], preferred_element_type=F32)
    r = jnp.square(jnp.maximum(up, 0.0)).astype(BF16)
    x2 = x1 + jnp.dot(r, wdn_ref[...], preferred_element_type=F32)
    y_ref[...] = _rms_norm(x2, gf_ref[...])


def _tail(x2d, attn, ssm, sgs, sga, wao, wo, g2, wup, wdn, gf, tm):
    n, d_model = x2d.shape
    row = lambda w: pl.BlockSpec((tm, w), lambda i: (i, 0))
    return pl.pallas_call(
        _tail_kernel, grid=(n // tm,),
        in_specs=[row(d_model), row(attn.shape[1]), row(d_model), row(d_model), row(d_model),
                  _const_spec(wao.shape), _const_spec(wo.shape), _const_spec((1, d_model)),
                  _const_spec(wup.shape), _const_spec(wdn.shape), _const_spec((1, d_model))],
        out_specs=row(d_model),
        out_shape=jax.ShapeDtypeStruct((n, d_model), F32),
        compiler_params=pltpu.CompilerParams(dimension_semantics=("arbitrary",),
                                             vmem_limit_bytes=VMEM_LIMIT),
        name="tail",
    )(x2d, attn, ssm, sgs, sga, wao, wo, g2.reshape(1, d_model), wup, wdn, gf.reshape(1, d_model))


def _tiles(n_rows, t_len):
    tm = min(256, n_rows)
    tc = min(128, t_len)
    tq = min(256, t_len)
    return tm, tc, tq, tq


def kernel(x_prompt, x_sample, cache_k, cache_v, cache_idx_k, state_ssm_re, state_ssm_im, page_table,
           norm1_g, w_in, ssm_a_re, ssm_a_im, ssm_log_dt, ssm_b_re, ssm_b_im, ssm_c_re, ssm_c_im,
           ssm_d, w_glu, b_glu, w_attn_out, w_o, norm2_g, w_up, w_down, normf_g):
    bsz, t_len, d_model = x_prompt.shape
    sq, s_len, _ = x_sample.shape
    assert s_len == 1, "the sample path handles one new token per sequence"
    n_pool, page = cache_idx_k.shape[0], cache_idx_k.shape[1]
    n_pages = page_table.shape[1]
    past = n_pages * page
    d_ssm = ssm_d.shape[0]
    d_attn = N_HEADS * HEAD_DIM
    d_kv = N_KV_HEADS * HEAD_DIM
    d_qi = N_IDX_HEADS * IDX_DIM
    dims = (d_ssm, d_attn, d_kv, d_qi)

    c_ki = d_ssm + d_attn + 2 * d_kv + d_qi
    c_g = c_ki + IDX_DIM + N_IDX_HEADS
    w_pack = jnp.concatenate(
        [w_in[:, :c_g], jnp.zeros((d_model, LANES - IDX_DIM - N_IDX_HEADS), w_in.dtype), w_in[:, c_g:]],
        axis=1).astype(BF16)
    wglu_b, wao_b, wo_b = w_glu.astype(BF16), w_attn_out.astype(BF16), w_o.astype(BF16)
    wup_b, wdn_b = w_up.astype(BF16), w_down.astype(BF16)
    s5p = _s5_params(ssm_a_re, ssm_a_im, ssm_log_dt, ssm_b_re, ssm_b_im, ssm_c_re, ssm_c_im)

    n_p = bsz * t_len
    tm, tc, tq, tk = _tiles(n_p, t_len)
    xp = x_prompt.reshape(n_p, d_model)
    tabs_p = _rope_tables(jnp.arange(t_len, dtype=I32), t_len)
    assert tm == tk
    (u, q, kt, kb, vt, _, vtb, qi, kit, kib, _, wit, sgs, sga) = _in_proj(
        xp, bsz, tabs_p, norm1_g, w_pack, tm, dims)
    ssm_out, re_p, im_p = _s5_prompt(u.reshape(bsz, t_len, d_ssm), s5p, ssm_d, wglu_b, b_glu, tc)
    r3 = lambda a: a.reshape(bsz, t_len, a.shape[-1])
    attn = _dsa_prompt(r3(qi), wit, r3(q), r3(kib), r3(kb), vtb, tq, tk)
    y_p = _tail(xp, attn.reshape(n_p, d_attn), ssm_out.reshape(n_p, d_model), sgs, sga,
                wao_b, wo_b, norm2_g, wup_b, wdn_b, normf_g, tm)

    xs = x_sample.reshape(sq, d_model)
    tabs_s = _rope_tables(jnp.full((sq,), past, I32), sq)
    (u_s, q_s, kt_s, kb_s, vt_s, vb_s, _, qi_s, kit_s, kib_s, wi_s, _, sgs_s, sga_s) = _in_proj(
        xs, 1, tabs_s, norm1_g, w_pack, sq, dims)
    sd = state_ssm_re.shape[1] * state_ssm_re.shape[2]
    ssm_s, re_s, im_s = _s5_sample(u_s, state_ssm_re.reshape(sq, sd), state_ssm_im.reshape(sq, sd),
                                   s5p, ssm_d, wglu_b, b_glu)
    scores = _idx_scores(page_table, qi_s.reshape(sq, N_IDX_HEADS, IDX_DIM),
                         wi_s.reshape(sq, N_IDX_HEADS, 1), jnp.transpose(cache_idx_k, (0, 2, 1)))
    k_top_s = min(TOPK_MAX, (past + s_len) // 4)
    mask, selfsel = _sample_select(scores, qi_s, kib_s, wi_s, min(512, past), k_top_s)
    q4 = q_s.reshape(sq, N_KV_HEADS, N_HEADS // N_KV_HEADS, HEAD_DIM)
    qs_pad = (q4[:, :, :, None, :] * jnp.eye(N_KV_HEADS, dtype=BF16)[None, :, None, :, None]
              ).reshape(sq, N_HEADS, d_kv)
    feat_major = lambda c: jnp.transpose(c, (0, 2, 3, 1)).reshape(n_pool, d_kv, page)
    attn_s = _sample_attn(page_table, qs_pad, mask.reshape(sq, 1, past),
                          kb_s.reshape(sq, 1, d_kv), vb_s.reshape(sq, 1, d_kv), selfsel.reshape(sq, 1, 1),
                          feat_major(cache_k), feat_major(cache_v))
    y_s = _tail(xs, attn_s.reshape(sq, d_attn), ssm_s, sgs_s, sga_s,
                wao_b, wo_b, norm2_g, wup_b, wdn_b, normf_g, sq)

    g_ssm, p_ssm = state_ssm_re.shape[1], state_ssm_re.shape[2]
    kv_out = lambda a: jnp.transpose(a.reshape(a.shape[0], N_KV_HEADS, HEAD_DIM, a.shape[2]), (0, 3, 1, 2))
    return (y_p.reshape(bsz, t_len, d_model), y_s.reshape(sq, s_len, d_model),
            kv_out(kt), kv_out(vt), jnp.transpose(kit, (0, 2, 1)),
            re_p.reshape(bsz, g_ssm, p_ssm), im_p.reshape(bsz, g_ssm, p_ssm),
            kv_out(kt_s).reshape(sq, s_len, N_KV_HEADS, HEAD_DIM),
            kv_out(vt_s).reshape(sq, s_len, N_KV_HEADS, HEAD_DIM),
            jnp.transpose(kit_s, (0, 2, 1)).reshape(sq, s_len, IDX_DIM),
            re_s.reshape(sq, g_ssm, p_ssm), im_s.reshape(sq, g_ssm, p_ssm))
```

```python
import functools
import math

import jax
import jax.numpy as jnp
from jax import lax
from jax.experimental import pallas as pl
from jax.experimental.pallas import tpu as pltpu

F32 = jnp.float32
BF16 = jnp.bfloat16
I32 = jnp.int32

SSM_GROUP = 16
SSM_STATE = 64
N_HEADS = 8
N_KV_HEADS = 2
HEAD_DIM = 64
ROT_DIM = HEAD_DIM // 4
N_IDX_HEADS = 8
IDX_DIM = 64
IDX_ROT_DIM = IDX_DIM // 4
ROPE_THETA = 500000.0
TOPK_MAX = 256
EPS = 1e-6
LOG2_E = math.log2(math.e)

LANES = 128
SUBLANES = 8
MXU_COLS = 256
S5_SEQS = 8
S5_SCAN_COLS = 4
COUNT_WAYS = 4
FINE_ROUNDS_UNCHECKED = 3
VT_ROWS = 80
VMEM_LIMIT = 56 * 1024 * 1024

NEG_INF = float("-inf")
FLT_MAX = float(jnp.finfo(jnp.float32).max)
KEY_LOW = -2139095040
NT_DIMS = (((1,), (1,)), ((), ()))


def _const_spec(shape):
    nd = len(shape)
    return pl.BlockSpec(shape, lambda *_: (0,) * nd, pipeline_mode=pl.Buffered(1))


def _rms_norm(x, g):
    ms = jnp.mean(x * x, axis=-1, keepdims=True)
    return x * lax.rsqrt(ms + EPS) * g


def _sigmoid(x):
    return 1.0 / (1.0 + jnp.exp(-x))


def _rope(x, cos_t, sin_a, sin_b):
    return (x * cos_t + pltpu.roll(x, LANES - ROT_DIM // 2, axis=1) * sin_a
            + pltpu.roll(x, ROT_DIM // 2, axis=1) * sin_b)


def _in_proj_kernel(x_ref, g_ref, w_ref, cos_ref, sa_ref, sb_ref,
                    u_ref, q_ref, kt_ref, kb_ref, vt_ref, vb_ref, vtb_ref, qi_ref, kit_ref, kib_ref,
                    wi_ref, wit_ref, sgs_ref, sga_ref, *, d_ssm, d_attn, d_kv, d_qi, d_model):
    assert d_kv == LANES
    tm = x_ref.shape[0]
    h = _rms_norm(x_ref[...], g_ref[...]).astype(BF16)
    cos_t, sin_a, sin_b = cos_ref[...], sa_ref[...], sb_ref[...]

    def proj(c0, width):
        return jnp.dot(h, w_ref[:, c0:c0 + width], preferred_element_type=F32)

    off = 0
    u_ref[...] = proj(off, d_ssm)
    off += d_ssm
    def rope_chunks(c0, width):
        for m in range(width // MXU_COLS):
            wide = proj(c0 + m * MXU_COLS, MXU_COLS)
            for c in range(MXU_COLS // LANES):
                yield (m * MXU_COLS // LANES + c,
                       _rope(wide[:, c * LANES:(c + 1) * LANES], cos_t, sin_a, sin_b))

    for c, r in rope_chunks(off, d_attn):
        q_ref[:, c * LANES:(c + 1) * LANES] = (r * (HEAD_DIM ** -0.5 * LOG2_E)).astype(BF16)
    off += d_attn
    kv = proj(off, 2 * d_kv)
    r = _rope(kv[:, :d_kv], cos_t, sin_a, sin_b)
    kt_ref[...] = r.T
    kb_ref[...] = r.astype(BF16)
    off += d_kv
    vv = kv[:, d_kv:]
    vt = vv.T
    vt_ref[...] = vt
    vb_ref[...] = vv.astype(BF16)
    sub = lax.broadcasted_iota(I32, (VT_ROWS - HEAD_DIM, tm), 0)
    ones_pad = jnp.where(sub == 0, 1.0, 0.0)
    for g in range(N_KV_HEADS):
        vtb_ref[g] = jnp.concatenate([vt[g * HEAD_DIM:(g + 1) * HEAD_DIM, :], ones_pad], axis=0).astype(BF16)
    off += d_kv
    for c, r in rope_chunks(off, d_qi):
        qi_ref[:, c * LANES:(c + 1) * LANES] = r.astype(BF16)
    off += d_qi
    kw = proj(off, LANES)
    lane = lax.broadcasted_iota(I32, kw.shape, 1)
    kr = jnp.where(lane < IDX_DIM, _rope(kw, cos_t, sin_a, sin_b), kw * (N_IDX_HEADS ** -0.5))
    krt = kr.T
    kit_ref[...] = krt[:IDX_DIM, :]
    kib_ref[...] = kr[:, :IDX_DIM].astype(BF16)
    wi_ref[...] = kr[:, IDX_DIM:IDX_DIM + N_IDX_HEADS]
    wit_ref[...] = krt[IDX_DIM:IDX_DIM + N_IDX_HEADS, :]
    off += LANES
    sgs_ref[...] = _sigmoid(proj(off, d_model))
    off += d_model
    sga_ref[...] = _sigmoid(proj(off, d_model))


def _rope_tables(pos, n_rows):
    half = ROT_DIM // 2
    inv = ROPE_THETA ** (-jnp.arange(half, dtype=F32) / half)
    ang = pos.astype(F32)[:, None] * inv[None, :]
    cos, sin = jnp.cos(ang), jnp.sin(ang)
    ones = jnp.ones((n_rows, HEAD_DIM - ROT_DIM), F32)
    zeros = jnp.zeros((n_rows, HEAD_DIM - ROT_DIM), F32)
    zh = jnp.zeros((n_rows, half), F32)
    cos_t = jnp.concatenate([cos, cos, ones], axis=1)
    sin_a = jnp.concatenate([-sin, zh, zeros], axis=1)
    sin_b = jnp.concatenate([zh, sin, zeros], axis=1)
    rep = LANES // HEAD_DIM
    return tuple(jnp.tile(t, (1, rep)) for t in (cos_t, sin_a, sin_b))


def _in_proj(x2d, n_seq, pos_tab, norm_g, w_pack, tm, dims):
    n, d_model = x2d.shape
    d_ssm, d_attn, d_kv, d_qi = dims
    cos_t, sin_a, sin_b = pos_tab
    t_len = n // n_seq
    nt = t_len // tm
    grid = (n // tm,)
    row = lambda w: pl.BlockSpec((tm, w), lambda i: (i, 0))
    tab = pl.BlockSpec((tm, LANES), lambda i: (i % nt, 0))
    feat = lambda r: pl.BlockSpec((None, r, tm), lambda i: (i // nt, 0, i % nt))
    kern = functools.partial(_in_proj_kernel, d_ssm=d_ssm, d_attn=d_attn, d_kv=d_kv, d_qi=d_qi,
                             d_model=d_model)
    outs = [
        (jax.ShapeDtypeStruct((n, d_ssm), F32), row(d_ssm)),
        (jax.ShapeDtypeStruct((n, d_attn), BF16), row(d_attn)),
        (jax.ShapeDtypeStruct((n_seq, d_kv, t_len), F32), feat(d_kv)),
        (jax.ShapeDtypeStruct((n, d_kv), BF16), row(d_kv)),
        (jax.ShapeDtypeStruct((n_seq, d_kv, t_len), F32), feat(d_kv)),
        (jax.ShapeDtypeStruct((n, d_kv), BF16), row(d_kv)),
        (jax.ShapeDtypeStruct((n_seq, nt, N_KV_HEADS, VT_ROWS, tm), BF16),
         pl.BlockSpec((None, None, N_KV_HEADS, VT_ROWS, tm), lambda i: (i // nt, i % nt, 0, 0, 0))),
        (jax.ShapeDtypeStruct((n, d_qi), BF16), row(d_qi)),
        (jax.ShapeDtypeStruct((n_seq, IDX_DIM, t_len), F32), feat(IDX_DIM)),
        (jax.ShapeDtypeStruct((n, IDX_DIM), BF16), row(IDX_DIM)),
        (jax.ShapeDtypeStruct((n, N_IDX_HEADS), F32), row(N_IDX_HEADS)),
        (jax.ShapeDtypeStruct((n_seq, N_IDX_HEADS, t_len), F32), feat(N_IDX_HEADS)),
        (jax.ShapeDtypeStruct((n, d_model), F32), row(d_model)),
        (jax.ShapeDtypeStruct((n, d_model), F32), row(d_model)),
    ]
    out_shapes = [o[0] for o in outs]
    out_specs = [o[1] for o in outs]
    return pl.pallas_call(
        kern, grid=grid,
        in_specs=[row(d_model), _const_spec((1, d_model)), _const_spec(w_pack.shape), tab, tab, tab],
        out_specs=out_specs, out_shape=out_shapes,
        compiler_params=pltpu.CompilerParams(dimension_semantics=("arbitrary",),
                                             vmem_limit_bytes=VMEM_LIMIT),
        name="in_proj",
    )(x2d, norm_g.reshape(1, d_model), w_pack, cos_t, sin_a, sin_b)


def _s5_readout(y, u, dsk_ref, wglu_ref, bglu_ref, d_model):
    y = y + dsk_ref[...] * u
    cdf = 0.5 * (1.0 + jnp.tanh(math.sqrt(2.0 / math.pi) * (y + 0.044715 * (y * y * y))))
    gl = (y * cdf).astype(BF16)
    z = jnp.dot(gl, wglu_ref[...], preferred_element_type=F32) + bglu_ref[...]
    return z[:, :d_model] * _sigmoid(z[:, d_model:])


def _s5_prompt_kernel(u_ref, perm_ref, bcat_ref, are_ref, aim_ref, ccre_ref, ccim_ref, dsk_ref, wglu_ref,
                      bglu_ref, out_ref, sre_ref, sim_ref, hre, him, zs, cr, ci, *, tc, d_model):
    @pl.when(pl.program_id(1) == 0)
    def _():
        cr[...] = jnp.zeros_like(cr)
        ci[...] = jnp.zeros_like(ci)

    nseq = u_ref.shape[0]
    d_ssm, n_col = u_ref.shape[2], hre.shape[0]
    sd = n_col * LANES
    u = u_ref[...].reshape(nseq * tc, d_ssm)
    ub = jnp.dot(perm_ref[...], u.astype(BF16), preferred_element_type=F32).astype(BF16)
    for c in range(n_col):
        kc = (c * LANES * d_ssm // sd) // LANES
        bu = jnp.dot(ub[:, kc * LANES:(kc + 1) * LANES], bcat_ref[c], preferred_element_type=F32)
        hre[c] = bu[:, :LANES]
        him[c] = bu[:, LANES:]

    for c0 in range(0, n_col, S5_SCAN_COLS):
        cs = range(c0, c0 + S5_SCAN_COLS)
        ar = [jnp.broadcast_to(are_ref[:, c * LANES:(c + 1) * LANES], (nseq, LANES)) for c in cs]
        ai = [jnp.broadcast_to(aim_ref[:, c * LANES:(c + 1) * LANES], (nseq, LANES)) for c in cs]

        def step(t, carry, cs=cs, ar=ar, ai=ai):
            rows = pl.ds(pl.multiple_of(t * nseq, nseq), nseq)
            out = []
            for n, c in enumerate(cs):
                pr, pi = carry[n]
                xr = hre[c, rows, :] + (ar[n] * pr - ai[n] * pi)
                xi = him[c, rows, :] + (ar[n] * pi + ai[n] * pr)
                hre[c, rows, :] = xr
                him[c, rows, :] = xi
                out.append((xr, xi))
            return tuple(out)

        init = tuple((cr[:, c * LANES:(c + 1) * LANES], ci[:, c * LANES:(c + 1) * LANES]) for c in cs)
        fin = lax.fori_loop(0, tc, step, init, unroll=4)
        for n, c in enumerate(cs):
            cr[:, c * LANES:(c + 1) * LANES] = fin[n][0]
            ci[:, c * LANES:(c + 1) * LANES] = fin[n][1]
    sre_ref[...] = cr[...]
    sim_ref[...] = ci[...]
    n_out = d_ssm // LANES
    per = n_col // n_out
    wide = lambda ref, j: jnp.concatenate([ref[c] for c in range(j * per, (j + 1) * per)], axis=1).astype(BF16)
    for j in range(n_out):
        zs[j] = (jnp.dot(wide(hre, j), ccre_ref[j], preferred_element_type=F32)
                 + jnp.dot(wide(him, j), ccim_ref[j], preferred_element_type=F32))
    y = jnp.concatenate(
        [jnp.concatenate([zs[j, pl.ds(b, tc, stride=nseq), :] for b in range(nseq)], axis=0)
         for j in range(n_out)], axis=1)
    out_ref[...] = _s5_readout(y, u, dsk_ref, wglu_ref, bglu_ref, d_model).reshape(nseq, tc, d_model)


def _s5_sample_kernel(u_ref, h0r_ref, h0i_ref, are_ref, aim_ref, bre_ref, bim_ref, cre_ref, cim_ref,
                      dsk_ref, wglu_ref, bglu_ref, out_ref, sre_ref, sim_ref, *, d_model):
    u = u_ref[...]
    ub = u.astype(BF16)
    ar, ai = are_ref[...], aim_ref[...]
    h0r, h0i = h0r_ref[...], h0i_ref[...]
    hr = (ar * h0r - ai * h0i) + jnp.dot(ub, bre_ref[...], preferred_element_type=F32)
    hi = (ar * h0i + ai * h0r) + jnp.dot(ub, bim_ref[...], preferred_element_type=F32)
    sre_ref[...] = hr
    sim_ref[...] = hi
    y = (jnp.dot(hr.astype(BF16), cre_ref[...], preferred_element_type=F32)
         + jnp.dot(hi.astype(BF16), cim_ref[...], preferred_element_type=F32))
    out_ref[...] = _s5_readout(y, u, dsk_ref, wglu_ref, bglu_ref, d_model)


def _s5_params(a_re, a_im, log_dt, b_re, b_im, c_re, c_im):
    g, p = a_re.shape
    a_c = lax.complex(a_re.astype(F32), a_im.astype(F32))
    dt = jnp.exp(log_dt.astype(F32))[:, None]
    a_bar = jnp.exp(a_c * dt)
    b_bar = ((a_bar - 1.0) / a_c)[:, :, None] * lax.complex(b_re.astype(F32), b_im.astype(F32))
    eye = jnp.eye(g, dtype=F32)
    n = b_re.shape[2]

    def bmat(b):
        return jnp.einsum('gpn,gh->gnhp', b, eye).reshape(g * n, g * p)

    def cmat(c):
        return jnp.einsum('gnp,gh->gphn', c, eye).reshape(g * p, g * n)

    bre, bim = bmat(jnp.real(b_bar)).astype(BF16), bmat(jnp.imag(b_bar)).astype(BF16)
    cre, cim = cmat(c_re.astype(F32)).astype(BF16), cmat(-c_im.astype(F32)).astype(BF16)
    d_in, sd = bre.shape
    assert d_in % LANES == 0 and sd % LANES == 0 and LANES % (LANES * d_in // sd) == 0
    bcat = jnp.stack([
        jnp.concatenate([m[(c * LANES * d_in // sd) // LANES * LANES:][:LANES, c * LANES:(c + 1) * LANES]
                         for m in (bre, bim)], axis=1)
        for c in range(sd // LANES)], axis=0)
    n_out = d_in // LANES
    span = sd // n_out
    ccre, ccim = (jnp.stack([m[j * span:(j + 1) * span, j * LANES:(j + 1) * LANES] for j in range(n_out)],
                            axis=0) for m in (cre, cim))
    a1 = a_bar.reshape(1, g * p)
    return jnp.real(a1), jnp.imag(a1), bre, bim, cre, cim, bcat, ccre, ccim


def _s5_prompt(u3, s5p, dsk, wglu, bglu, tc):
    bsz, t_len, d_ssm = u3.shape
    are, aim, bre, _, _, _, bcat, ccre, ccim = s5p
    sd = bre.shape[1]
    d_model = wglu.shape[1] // 2
    nseq = math.gcd(bsz, S5_SEQS)
    assert sd % (S5_SCAN_COLS * LANES) == 0 and tc % SUBLANES == 0
    kern = functools.partial(_s5_prompt_kernel, tc=tc, d_model=d_model)
    state = pl.BlockSpec((nseq, sd), lambda b, c: (b, 0))
    dst = jnp.arange(nseq * tc)
    perm = (jnp.arange(nseq * tc)[None, :] == ((dst % nseq) * tc + dst // nseq)[:, None]).astype(BF16)
    return pl.pallas_call(
        kern, grid=(bsz // nseq, t_len // tc),
        in_specs=[pl.BlockSpec((nseq, tc, d_ssm), lambda b, c: (b, c, 0)), _const_spec(perm.shape),
                  _const_spec(bcat.shape), _const_spec(are.shape), _const_spec(aim.shape),
                  _const_spec(ccre.shape), _const_spec(ccim.shape), _const_spec((1, d_ssm)),
                  _const_spec(wglu.shape), _const_spec((1, 2 * d_model))],
        out_specs=[pl.BlockSpec((nseq, tc, d_model), lambda b, c: (b, c, 0)), state, state],
        out_shape=[jax.ShapeDtypeStruct((bsz, t_len, d_model), F32),
                   jax.ShapeDtypeStruct((bsz, sd), F32),
                   jax.ShapeDtypeStruct((bsz, sd), F32)],
        scratch_shapes=[pltpu.VMEM((sd // LANES, nseq * tc, LANES), F32),
                        pltpu.VMEM((sd // LANES, nseq * tc, LANES), F32),
                        pltpu.VMEM((d_ssm // LANES, nseq * tc, LANES), F32),
                        pltpu.VMEM((nseq, sd), F32), pltpu.VMEM((nseq, sd), F32)],
        compiler_params=pltpu.CompilerParams(dimension_semantics=("arbitrary", "arbitrary"),
                                             vmem_limit_bytes=VMEM_LIMIT),
        name="s5_prompt",
    )(u3, perm, bcat, are, aim, ccre, ccim, dsk.reshape(1, d_ssm), wglu, bglu.reshape(1, 2 * d_model))


def _s5_sample(u2, h0r, h0i, s5p, dsk, wglu, bglu):
    n, d_ssm = u2.shape
    are, aim, bre, bim, cre, cim = s5p[:6]
    sd = bre.shape[1]
    d_model = wglu.shape[1] // 2
    kern = functools.partial(_s5_sample_kernel, d_model=d_model)
    return pl.pallas_call(
        kern,
        out_shape=[jax.ShapeDtypeStruct((n, d_model), F32),
                   jax.ShapeDtypeStruct((n, sd), F32),
                   jax.ShapeDtypeStruct((n, sd), F32)],
        compiler_params=pltpu.CompilerParams(vmem_limit_bytes=VMEM_LIMIT),
        name="s5_sample",
    )(u2, h0r, h0i, are, aim, bre, bim, cre, cim, dsk.reshape(1, d_ssm), wglu,
      bglu.reshape(1, 2 * d_model))


def _f2k(x):
    b = lax.bitcast_convert_type(x, I32)
    return jnp.where(b < 0, b ^ 0x7FFFFFFF, b)


def _k2f(k):
    return lax.bitcast_convert_type(jnp.where(k < 0, k ^ 0x7FFFFFFF, k), F32)


def _threshold_search(count_ge, amax, forced, k_top):
    kf = float(k_top)
    lo0 = jnp.where(forced, KEY_LOW, _f2k(-amax))
    hi0 = jnp.where(forced, KEY_LOW + 1, _f2k(amax) + 1)

    def cond(st):
        lo, hi, _ = st
        return jnp.max(jnp.where(hi > lo + 1, 1.0, 0.0)) > 0.5

    def body(st):
        lo, hi, it = st
        active = hi > lo + 1
        mid_i = (lo >> 1) + (hi >> 1) + (lo & hi & 1)
        mk = _f2k(0.5 * _k2f(lo) + 0.5 * _k2f(hi))
        use_f = jnp.logical_and(it % 2 == 0, jnp.logical_and(mk > lo, mk < hi))
        mid = jnp.where(use_f, mk, mid_i)
        c = count_ge(_k2f(mid))
        ge = c >= kf
        nlo = jnp.where(ge, mid, lo)
        nhi = jnp.where(c == kf, mid + 1, jnp.where(ge, hi, mid))
        return jnp.where(active, nlo, lo), jnp.where(active, nhi, hi), it + 1

    lo, _, _ = lax.while_loop(cond, body, (lo0, hi0, jnp.int32(0)))
    return _k2f(lo)


def _tri_exclusive(n):
    r = lax.broadcasted_iota(I32, (n, n), 0)
    c = lax.broadcasted_iota(I32, (n, n), 1)
    return (r < c).astype(BF16)


def _pipelined_blocks(n, produce, consume, buf_a, buf_b, carry):
    produce(0, buf_a)

    def pair(t, c):
        produce(2 * t + 1, buf_b)
        c = consume(2 * t, buf_a, c)
        produce(2 * t + 2, buf_a)
        return consume(2 * t + 1, buf_b, c)

    carry = lax.fori_loop(0, n // 2, pair, carry)
    return lax.cond(n % 2 == 1, lambda c: consume(n - 1, buf_a, c), lambda c: c, carry)


def _dsa_prompt_kernel(qi_ref, wt_ref, q_ref, ki_ref, k_ref, vt_ref, o_ref,
                       sc_ref, scb_ref, qis_ref, qs_ref, m_ref, acc_ref, sa_ref, sb_ref, *, tq, tk, k_top):
    i = pl.program_id(1)
    nkb = i + 1
    hpg = N_HEADS // N_KV_HEADS
    ngr = tk // SUBLANES
    kf = float(k_top)

    qi = qi_ref[...]
    q = q_ref[...]
    zpad = jnp.zeros((tq, HEAD_DIM), BF16)
    for h in range(N_IDX_HEADS):
        qis_ref[h * tq:(h + 1) * tq, :] = qi[:, h * IDX_DIM:(h + 1) * IDX_DIM]
    for h in range(N_HEADS):
        qh = q[:, h * HEAD_DIM:(h + 1) * HEAD_DIM]
        pair = [qh, zpad] if h // hpg == 0 else [zpad, qh]
        qs_ref[h * tq:(h + 1) * tq, :] = jnp.concatenate(pair, axis=1)

    w8 = wt_ref[...] * (IDX_DIM ** -0.5)
    qpos = i * tq + lax.broadcasted_iota(I32, (SUBLANES, tq), 1)

    def idx_dots(j, dst):
        k0 = pl.multiple_of(jnp.minimum(j, nkb - 1) * tk, tk)
        dst[...] = lax.dot_general(ki_ref[pl.ds(k0, tk), :], qis_ref[...], NT_DIMS, preferred_element_type=F32)

    def score_blk(j, src, amx):
        mags = []
        for c in range(tq // LANES):
            acc = jnp.zeros((tk, LANES), F32)
            for h in range(N_IDX_HEADS):
                cols = slice(h * tq + c * LANES, h * tq + (c + 1) * LANES)
                acc = acc + jnp.maximum(src[:, cols], 0.0) * w8[h:h + 1, c * LANES:(c + 1) * LANES]
            kpos = j * tk + lax.broadcasted_iota(I32, (tk, LANES), 0)
            causal = kpos <= i * tq + c * LANES + lax.broadcasted_iota(I32, (tk, LANES), 1)
            mag = jnp.where(causal, jnp.abs(acc), 0.0)
            acc = jnp.where(causal, acc, NEG_INF)
            sc_ref[j, :, c * LANES:(c + 1) * LANES] = acc
            top = lax.bitcast_convert_type(lax.bitcast_convert_type(acc, I32) & jnp.int32(-65536), F32)
            scb_ref[j, :, c * LANES:(c + 1) * LANES] = top.astype(BF16)
            mags.append(jnp.max(mag.reshape(ngr, SUBLANES, LANES), axis=0))
        return jnp.maximum(amx, jnp.concatenate(mags, axis=1))

    amx = _pipelined_blocks(nkb, idx_dots, score_blk, sa_ref, sb_ref, jnp.zeros((SUBLANES, tq), F32))

    rep = lambda x: jnp.broadcast_to(x, (SUBLANES, tq))
    amax = rep(jnp.max(amx, axis=0, keepdims=True))
    forced = (qpos + 1) <= k_top

    packed = 2 * SUBLANES
    one_b, zero_b = jnp.ones((), BF16), jnp.zeros((), BF16)

    def count16(k16):
        bits = jnp.where(k16 < 0, k16 ^ 0x7FFF, k16) << 16
        tb = jnp.broadcast_to(lax.bitcast_convert_type(bits, F32)[0:1, :], (packed, tq)).astype(BF16)

        def body(j, part):
            blk = scb_ref[j].reshape(tk // packed // COUNT_WAYS, COUNT_WAYS, packed, tq)
            hit = jnp.where(blk >= tb[None, None], one_b, zero_b)
            for g in range(hit.shape[0]):
                part = part + hit[g]
            return part

        part = lax.fori_loop(0, nkb, body, jnp.zeros((COUNT_WAYS, packed, tq), BF16))
        return rep(jnp.sum(jnp.sum(part.astype(F32), axis=0), axis=0, keepdims=True))

    def coarse_pass(lo16, hi16):
        active = hi16 > lo16 + 1
        mid = lo16 + ((hi16 - lo16) >> 1)
        c = count16(mid)
        ge = c >= kf
        nlo = jnp.where(ge, mid, lo16)
        nhi = jnp.where(c == kf, mid, jnp.where(ge, hi16, mid))
        return jnp.where(active, nlo, lo16), jnp.where(active, nhi, hi16)

    lo16 = jnp.where(forced, KEY_LOW >> 16, _f2k(-amax) >> 16)
    hi16 = jnp.where(forced, KEY_LOW >> 16, (_f2k(amax) >> 16) + 1)
    lo16, hi16 = lax.fori_loop(0, 8, lambda _, st: coarse_pass(*coarse_pass(*st)), (lo16, hi16))
    lo0 = lo16 << 16
    hi0 = jnp.where(hi16 == lo16, lo0 + 1, hi16 << 16)

    def count(thr, strict):
        def body(j, part):
            blk = sc_ref[j].reshape(ngr // COUNT_WAYS, COUNT_WAYS, SUBLANES, tq)
            hit = (blk > thr[None, None]) if strict else (blk >= thr[None, None])
            return part + jnp.sum(jnp.where(hit, 1.0, 0.0), axis=0)

        part = lax.fori_loop(0, nkb, body, jnp.zeros((COUNT_WAYS, SUBLANES, tq), F32))
        return rep(jnp.sum(jnp.sum(part, axis=0), axis=0, keepdims=True))

    def search_pass(lo, hi, value_space):
        active = hi > lo + 1
        span = hi - lo
        mid = lo + lax.shift_right_logical(span, jnp.ones_like(span))
        if value_space:
            mk = _f2k(0.5 * _k2f(lo) + 0.5 * _k2f(hi))
            mid = jnp.where(jnp.logical_and(mk > lo, mk < hi), mk, mid)
        c = count(_k2f(mid), False)
        ge = c >= kf
        nlo = jnp.where(ge, mid, lo)
        nhi = jnp.where(c == kf, mid + 1, jnp.where(ge, hi, mid))
        return jnp.where(active, nlo, lo), jnp.where(active, nhi, hi)

    def double_round(lo, hi):
        return search_pass(*search_pass(lo, hi, True), False)

    def unresolved(lo, hi):
        return (jnp.max(jnp.where(hi > lo + 1, 1.0, 0.0)) > 0.5).astype(I32)

    def search_body(st):
        lo, hi = double_round(st[0], st[1])
        return lo, hi, unresolved(lo, hi)

    lo, hi = lax.fori_loop(0, FINE_ROUNDS_UNCHECKED, lambda _, s: double_round(*s), (lo0, hi0))
    lo, _, _ = lax.while_loop(lambda s: s[2] > 0, search_body, (lo, hi, unresolved(lo, hi)))
    thr = _k2f(lo)
    c_ge = count(thr, False)
    thr1 = thr[0:1, :]

    @pl.when(jnp.max(jnp.where(c_ge > kf, 1.0, 0.0)) > 0.5)
    def _():
        need = jnp.where(c_ge > kf, kf - count(thr, True), 1e30)[0:1, :]
        r = lax.broadcasted_iota(I32, (tk, tk), 0)
        c = lax.broadcasted_iota(I32, (tk, tk), 1)
        earlier = (c < r).astype(BF16)
        ones_k = jnp.ones((SUBLANES, tk), BF16)

        def fix(j, run):
            sc = sc_ref[j]
            eq = sc == thr1
            eqb = jnp.where(eq, 1.0, 0.0).astype(BF16)
            before = jnp.dot(earlier, eqb, preferred_element_type=F32) + run[0:1, :]
            sc_ref[j] = jnp.where(jnp.logical_and(eq, before >= need), NEG_INF, sc)
            return run + jnp.dot(ones_k, eqb, preferred_element_type=F32)

        lax.fori_loop(0, nkb, fix, jnp.zeros((SUBLANES, tq), F32))

    m_ref[...] = jnp.full(m_ref.shape, NEG_INF, F32)
    acc_ref[...] = jnp.zeros(acc_ref.shape, F32)

    def logits(j, dst):
        jc = jnp.minimum(j, nkb - 1)
        k0 = pl.multiple_of(jc * tk, tk)
        dst[...] = lax.dot_general(k_ref[pl.ds(k0, tk), :], qs_ref[...], NT_DIMS, preferred_element_type=F32)

    def softmax_pv(j, src):
        jc = j
        bias = jnp.where(sc_ref[j] >= thr1, 0.0, NEG_INF)
        for h in range(N_HEADS):
            x = src[:, h * tq:(h + 1) * tq] + bias
            bm = jnp.max(jnp.max(x.reshape(ngr, SUBLANES, tq), axis=0), axis=0, keepdims=True)
            m_old = m_ref[h:h + 1, :]
            m_new = jnp.maximum(m_old, bm)
            m_safe = jnp.where(m_new == NEG_INF, 0.0, m_new)
            p = jnp.exp2(x - m_safe).astype(BF16)
            pv = jnp.dot(vt_ref[jc, h // hpg], p, preferred_element_type=F32)
            acc_ref[h] = acc_ref[h] * jnp.exp2(m_old - m_safe) + pv
            m_ref[h:h + 1, :] = m_new

    def consume(j, src, carry):
        softmax_pv(j, src)
        return carry

    _pipelined_blocks(nkb, logits, consume, sa_ref, sb_ref, jnp.int32(0))

    for pr in range(N_HEADS // 2):
        outs = []
        for h in (2 * pr, 2 * pr + 1):
            a = acc_ref[h]
            outs.append(a[:HEAD_DIM, :] / a[HEAD_DIM:HEAD_DIM + 1, :])
        o2 = jnp.concatenate(outs, axis=0)
        o_ref[:, 2 * pr * HEAD_DIM:(2 * pr + 2) * HEAD_DIM] = o2.T.astype(o_ref.dtype)


def _dsa_prompt(qi, wt, q, kib, kb, vt, tq, tk):
    bsz, t_len, d_qi = qi.shape
    d_attn = q.shape[2]
    d_kv = kb.shape[2]
    nkb = t_len // tk
    k_top = min(TOPK_MAX, t_len // 4)
    assert tq == tk and tq % LANES == 0 and t_len % tq == 0 and vt.shape[1] == nkb
    assert nkb * (tk // (2 * SUBLANES * COUNT_WAYS)) <= 256
    kern = functools.partial(_dsa_prompt_kernel, tq=tq, tk=tk, k_top=k_top)
    tile = lambda w: pl.BlockSpec((None, tq, w), lambda b, i: (b, i, 0))
    full = lambda w: pl.BlockSpec((None, t_len, w), lambda b, i: (b, 0, 0))
    return pl.pallas_call(
        kern, grid=(bsz, t_len // tq),
        in_specs=[tile(d_qi), pl.BlockSpec((None, N_IDX_HEADS, tq), lambda b, i: (b, 0, i)), tile(d_attn),
                  full(IDX_DIM), full(d_kv),
                  pl.BlockSpec((None, nkb, N_KV_HEADS, VT_ROWS, tk), lambda b, i: (b, 0, 0, 0, 0))],
        out_specs=tile(d_attn),
        out_shape=jax.ShapeDtypeStruct((bsz, t_len, d_attn), BF16),
        scratch_shapes=[pltpu.VMEM((nkb, tk, tq), F32),
                        pltpu.VMEM((nkb, tk, tq), BF16),
                        pltpu.VMEM((N_IDX_HEADS * tq, IDX_DIM), BF16),
                        pltpu.VMEM((N_HEADS * tq, N_KV_HEADS * HEAD_DIM), BF16),
                        pltpu.VMEM((N_HEADS, tq), F32),
                        pltpu.VMEM((N_HEADS, VT_ROWS, tq), F32),
                        pltpu.VMEM((tk, N_HEADS * tq), F32),
                        pltpu.VMEM((tk, N_HEADS * tq), F32)],
        compiler_params=pltpu.CompilerParams(dimension_semantics=("arbitrary", "arbitrary"),
                                             vmem_limit_bytes=VMEM_LIMIT),
        name="dsa_prompt",
    )(qi, wt, q, kib, kb, vt)


def _page_fetcher(pt_ref, n_pages, streams):
    def copies(seq, slot):
        for p in range(n_pages):
            pg = pt_ref[seq * n_pages + p]
            for hbm, buf, sem in streams:
                page = hbm.shape[2]
                yield pltpu.make_async_copy(hbm.at[pg], buf.at[slot, :, pl.ds(p * page, page)], sem.at[slot])

    def fetch(seq, slot):
        for cp in copies(seq, slot):
            cp.start()

    def wait(seq, slot):
        for cp in copies(seq, slot):
            cp.wait()

    return fetch, wait


def _prefetch_next_and_wait(fetch, wait):
    b = pl.program_id(0)
    slot = b % 2

    @pl.when(b == 0)
    def _():
        fetch(0, 0)

    @pl.when(b + 1 < pl.num_programs(0))
    def _():
        fetch(b + 1, 1 - slot)

    wait(b, slot)
    return b, slot


def _idx_score_kernel(pt_ref, q_ref, w_ref, ci_ref, o_ref, ibuf, sem, *, n_pages):
    b, slot = _prefetch_next_and_wait(*_page_fetcher(pt_ref, n_pages, [(ci_ref, ibuf, sem)]))
    q = q_ref[...]
    w = w_ref[...] * (IDX_DIM ** -0.5)
    d = jnp.dot(q, ibuf[slot].astype(BF16), preferred_element_type=F32)
    o_ref[pl.ds(b % SUBLANES, 1), :] = jnp.sum(jnp.maximum(d, 0.0) * w, axis=0, keepdims=True)


def _idx_scores(page_table, qis, wcol, cache_idx_kt):
    sq, n_pages = page_table.shape
    page = cache_idx_kt.shape[2]
    past = n_pages * page
    kern = functools.partial(_idx_score_kernel, n_pages=n_pages)
    grid_spec = pltpu.PrefetchScalarGridSpec(
        num_scalar_prefetch=1, grid=(sq,),
        in_specs=[pl.BlockSpec((None, N_IDX_HEADS, IDX_DIM), lambda b, pt: (b, 0, 0)),
                  pl.BlockSpec((None, N_IDX_HEADS, 1), lambda b, pt: (b, 0, 0)),
                  pl.BlockSpec(memory_space=pl.ANY)],
        out_specs=pl.BlockSpec((SUBLANES, past), lambda b, pt: (b // SUBLANES, 0)),
        scratch_shapes=[pltpu.VMEM((2, IDX_DIM, past), cache_idx_kt.dtype), pltpu.SemaphoreType.DMA((2,))])
    return pl.pallas_call(
        kern, grid_spec=grid_spec,
        out_shape=jax.ShapeDtypeStruct((sq, past), F32),
        compiler_params=pltpu.CompilerParams(dimension_semantics=("arbitrary",),
                                             vmem_limit_bytes=VMEM_LIMIT),
        name="sample_idx_scores",
    )(page_table.reshape(-1), qis, wcol, cache_idx_kt)


def _sample_select_kernel(sc_ref, qi_ref, ki_ref, wi_ref, mask_ref, selfsel_ref, *, ch, k_top):
    sq, l_past = sc_ref.shape
    nch = l_past // ch
    w = wi_ref[...] * (IDX_DIM ** -0.5)
    qf = qi_ref[...].astype(F32)
    kf32 = ki_ref[...].astype(F32)
    s_self = jnp.zeros((sq, 1), F32)
    for h in range(N_IDX_HEADS):
        dh = jnp.sum(qf[:, h * IDX_DIM:(h + 1) * IDX_DIM] * kf32, axis=1, keepdims=True)
        s_self = s_self + jnp.maximum(dh, 0.0) * w[:, h:h + 1]

    amax = jnp.abs(s_self)
    for c in range(nch):
        amax = jnp.maximum(amax, jnp.max(jnp.abs(sc_ref[:, c * ch:(c + 1) * ch]), axis=1, keepdims=True))

    def count_cmp(thr, strict):
        thr_b = jnp.broadcast_to(thr, (sq, LANES))
        part = jnp.zeros((sq, LANES), F32)
        for c in range(l_past // LANES):
            blk = sc_ref[:, c * LANES:(c + 1) * LANES]
            hit = (blk > thr_b) if strict else (blk >= thr_b)
            part = part + jnp.where(hit, 1.0, 0.0)
        self_hit = (s_self > thr) if strict else (s_self >= thr)
        return jnp.sum(part, axis=1, keepdims=True) + jnp.where(self_hit, 1.0, 0.0)

    kf = float(k_top)
    forced = amax < 0.0
    thr = _threshold_search(lambda t: count_cmp(t, False), amax, forced, k_top)
    need = kf - count_cmp(thr, True)
    tri = _tri_exclusive(ch)
    run = jnp.zeros((sq, 1), F32)
    for c in range(nch):
        sc = sc_ref[:, c * ch:(c + 1) * ch]
        eq = sc == thr
        before = jnp.dot(jnp.where(eq, 1.0, 0.0).astype(BF16), tri, preferred_element_type=F32) + run
        keep = jnp.logical_or(sc > thr, jnp.logical_and(eq, before < need))
        mask_ref[:, c * ch:(c + 1) * ch] = jnp.where(keep, 1.0, 0.0)
        run = run + jnp.sum(jnp.where(eq, 1.0, 0.0), axis=1, keepdims=True)
    self_keep = jnp.logical_or(s_self > thr, jnp.logical_and(s_self == thr, run < need))
    selfsel_ref[...] = jnp.where(self_keep, 1.0, 0.0)


def _sample_select(scores, qi, kib, wi, ch, k_top):
    sq, l_past = scores.shape
    kern = functools.partial(_sample_select_kernel, ch=ch, k_top=k_top)
    return pl.pallas_call(
        kern,
        out_shape=[jax.ShapeDtypeStruct((sq, l_past), F32), jax.ShapeDtypeStruct((sq, 1), F32)],
        compiler_params=pltpu.CompilerParams(vmem_limit_bytes=VMEM_LIMIT),
        name="sample_select",
    )(scores, qi, kib, wi)


def _sample_attn_kernel(pt_ref, q_ref, mask_ref, ks_ref, vs_ref, ss_ref, ck_ref, cv_ref, o_ref,
                        kbuf, vbuf, ksem, vsem, *, n_pages):
    _, slot = _prefetch_next_and_wait(
        *_page_fetcher(pt_ref, n_pages, [(ck_ref, kbuf, ksem), (cv_ref, vbuf, vsem)]))
    hpg = N_HEADS // N_KV_HEADS
    q = q_ref[...]
    s = jnp.dot(q, kbuf[slot].astype(BF16), preferred_element_type=F32)
    sm = jnp.where(mask_ref[...] > 0.5, s, NEG_INF)
    s_self = jnp.sum(q.astype(F32) * ks_ref[...].astype(F32), axis=1, keepdims=True)
    s_self = jnp.where(ss_ref[...] > 0.5, s_self, NEG_INF)
    m = jnp.maximum(jnp.max(sm, axis=1, keepdims=True), s_self)
    m = jnp.where(m == NEG_INF, 0.0, m)
    pb = jnp.exp2(sm - m).astype(BF16)
    p_self = jnp.exp2(s_self - m).astype(BF16).astype(F32)
    l_sum = jnp.sum(pb.astype(F32), axis=1, keepdims=True) + p_self
    acc = (lax.dot_general(pb, vbuf[slot].astype(BF16), NT_DIMS, preferred_element_type=F32)
           + p_self * vs_ref[...].astype(F32))
    o = acc / l_sum
    hrow = lax.broadcasted_iota(I32, o.shape, 0)
    o = jnp.where(hrow < hpg, o, pltpu.roll(o, HEAD_DIM, axis=1))
    o_ref[...] = o[:, :HEAD_DIM].astype(o_ref.dtype)


def _sample_attn(page_table, qs, mask3, kself, vself, selfsel, cache_k2, cache_v2):
    sq, n_pages = page_table.shape
    d_kv, page = cache_k2.shape[1], cache_k2.shape[2]
    past = n_pages * page
    kern = functools.partial(_sample_attn_kernel, n_pages=n_pages)
    per_seq = lambda r, w: pl.BlockSpec((None, r, w), lambda b, pt: (b, 0, 0))
    hbm = pl.BlockSpec(memory_space=pl.ANY)
    grid_spec = pltpu.PrefetchScalarGridSpec(
        num_scalar_prefetch=1, grid=(sq,),
        in_specs=[per_seq(N_HEADS, d_kv), per_seq(1, past), per_seq(1, d_kv), per_seq(1, d_kv),
                  per_seq(1, 1), hbm, hbm],
        out_specs=per_seq(N_HEADS, HEAD_DIM),
        scratch_shapes=[pltpu.VMEM((2, d_kv, past), cache_k2.dtype), pltpu.VMEM((2, d_kv, past), cache_v2.dtype),
                        pltpu.SemaphoreType.DMA((2,)), pltpu.SemaphoreType.DMA((2,))])
    return pl.pallas_call(
        kern, grid_spec=grid_spec,
        out_shape=jax.ShapeDtypeStruct((sq, N_HEADS, HEAD_DIM), BF16),
        compiler_params=pltpu.CompilerParams(dimension_semantics=("arbitrary",),
                                             vmem_limit_bytes=VMEM_LIMIT),
        name="sample_attn",
    )(page_table.reshape(-1), qs, mask3, kself, vself, selfsel, cache_k2, cache_v2)


def _tail_kernel(x_ref, attn_ref, ssm_ref, sgs_ref, sga_ref, wao_ref, wo_ref, g2_ref, wup_ref, wdn_ref,
                 gf_ref, y_ref):
    attn_out = jnp.dot(attn_ref[...], wao_ref[...], preferred_element_type=F32)
    mix = sgs_ref[...] * ssm_ref[...] + sga_ref[...] * attn_out
    x1 = x_ref[...] + jnp.dot(mix.astype(BF16), wo_ref[...], preferred_element_type=F32)
    hh = _rms_norm(x1, g2_ref[...]).astype(BF16)
    up = jnp.dot(hh, wup_ref[...], preferred_element_type=F32)
    r = jnp.square(jnp.maximum(up, 0.0)).astype(BF16)
    x2 = x1 + jnp.dot(r, wdn_ref[...], preferred_element_type=F32)
    y_ref[...] = _rms_norm(x2, gf_ref[...])


def _tail(x2d, attn, ssm, sgs, sga, wao, wo, g2, wup, wdn, gf, tm):
    n, d_model = x2d.shape
    row = lambda w: pl.BlockSpec((tm, w), lambda i: (i, 0))
    return pl.pallas_call(
        _tail_kernel, grid=(n // tm,),
        in_specs=[row(d_model), row(attn.shape[1]), row(d_model), row(d_model), row(d_model),
                  _const_spec(wao.shape), _const_spec(wo.shape), _const_spec((1, d_model)),
                  _const_spec(wup.shape), _const_spec(wdn.shape), _const_spec((1, d_model))],
        out_specs=row(d_model),
        out_shape=jax.ShapeDtypeStruct((n, d_model), F32),
        compiler_params=pltpu.CompilerParams(dimension_semantics=("arbitrary",),
                                             vmem_limit_bytes=VMEM_LIMIT),
        name="tail",
    )(x2d, attn, ssm, sgs, sga, wao, wo, g2.reshape(1, d_model), wup, wdn, gf.reshape(1, d_model))


def _tiles(n_rows, t_len):
    tm = min(256, n_rows)
    tc = min(128, t_len)
    tq = min(256, t_len)
    return tm, tc, tq, tq


def kernel(x_prompt, x_sample, cache_k, cache_v, cache_idx_k, state_ssm_re, state_ssm_im, page_table,
           norm1_g, w_in, ssm_a_re, ssm_a_im, ssm_log_dt, ssm_b_re, ssm_b_im, ssm_c_re, ssm_c_im,
           ssm_d, w_glu, b_glu, w_attn_out, w_o, norm2_g, w_up, w_down, normf_g):
    bsz, t_len, d_model = x_prompt.shape
    sq, s_len, _ = x_sample.shape
    assert s_len == 1, "the sample path handles one new token per sequence"
    n_pool, page = cache_idx_k.shape[0], cache_idx_k.shape[1]
    n_pages = page_table.shape[1]
    past = n_pages * page
    d_ssm = ssm_d.shape[0]
    d_attn = N_HEADS * HEAD_DIM
    d_kv = N_KV_HEADS * HEAD_DIM
    d_qi = N_IDX_HEADS * IDX_DIM
    dims = (d_ssm, d_attn, d_kv, d_qi)

    c_ki = d_ssm + d_attn + 2 * d_kv + d_qi
    c_g = c_ki + IDX_DIM + N_IDX_HEADS
    w_pack = jnp.concatenate(
        [w_in[:, :c_g], jnp.zeros((d_model, LANES - IDX_DIM - N_IDX_HEADS), w_in.dtype), w_in[:, c_g:]],
        axis=1).astype(BF16)
    wglu_b, wao_b, wo_b = w_glu.astype(BF16), w_attn_out.astype(BF16), w_o.astype(BF16)
    wup_b, wdn_b = w_up.astype(BF16), w_down.astype(BF16)
    s5p = _s5_params(ssm_a_re, ssm_a_im, ssm_log_dt, ssm_b_re, ssm_b_im, ssm_c_re, ssm_c_im)

    n_p = bsz * t_len
    tm, tc, tq, tk = _tiles(n_p, t_len)
    xp = x_prompt.reshape(n_p, d_model)
    tabs_p = _rope_tables(jnp.arange(t_len, dtype=I32), t_len)
    assert tm == tk
    (u, q, kt, kb, vt, _, vtb, qi, kit, kib, _, wit, sgs, sga) = _in_proj(
        xp, bsz, tabs_p, norm1_g, w_pack, tm, dims)
    ssm_out, re_p, im_p = _s5_prompt(u.reshape(bsz, t_len, d_ssm), s5p, ssm_d, wglu_b, b_glu, tc)
    r3 = lambda a: a.reshape(bsz, t_len, a.shape[-1])
    attn = _dsa_prompt(r3(qi), wit, r3(q), r3(kib), r3(kb), vtb, tq, tk)
    y_p = _tail(xp, attn.reshape(n_p, d_attn), ssm_out.reshape(n_p, d_model), sgs, sga,
                wao_b, wo_b, norm2_g, wup_b, wdn_b, normf_g, tm)

    xs = x_sample.reshape(sq, d_model)
    tabs_s = _rope_tables(jnp.full((sq,), past, I32), sq)
    (u_s, q_s, kt_s, kb_s, vt_s, vb_s, _, qi_s, kit_s, kib_s, wi_s, _, sgs_s, sga_s) = _in_proj(
        xs, 1, tabs_s, norm1_g, w_pack, sq, dims)
    sd = state_ssm_re.shape[1] * state_ssm_re.shape[2]
    ssm_s, re_s, im_s = _s5_sample(u_s, state_ssm_re.reshape(sq, sd), state_ssm_im.reshape(sq, sd),
                                   s5p, ssm_d, wglu_b, b_glu)
    scores = _idx_scores(page_table, qi_s.reshape(sq, N_IDX_HEADS, IDX_DIM),
                         wi_s.reshape(sq, N_IDX_HEADS, 1), jnp.transpose(cache_idx_k, (0, 2, 1)))
    k_top_s = min(TOPK_MAX, (past + s_len) // 4)
    mask, selfsel = _sample_select(scores, qi_s, kib_s, wi_s, min(512, past), k_top_s)
    q4 = q_s.reshape(sq, N_KV_HEADS, N_HEADS // N_KV_HEADS, HEAD_DIM)
    qs_pad = (q4[:, :, :, None, :] * jnp.eye(N_KV_HEADS, dtype=BF16)[None, :, None, :, None]
              ).reshape(sq, N_HEADS, d_kv)
    feat_major = lambda c: jnp.transpose(c, (0, 2, 3, 1)).reshape(n_pool, d_kv, page)
    attn_s = _sample_attn(page_table, qs_pad, mask.reshape(sq, 1, past),
                          kb_s.reshape(sq, 1, d_kv), vb_s.reshape(sq, 1, d_kv), selfsel.reshape(sq, 1, 1),
                          feat_major(cache_k), feat_major(cache_v))
    y_s = _tail(xs, attn_s.reshape(sq, d_attn), ssm_s, sgs_s, sga_s,
                wao_b, wo_b, norm2_g, wup_b, wdn_b, normf_g, sq)

    g_ssm, p_ssm = state_ssm_re.shape[1], state_ssm_re.shape[2]
    kv_out = lambda a: jnp.transpose(a.reshape(a.shape[0], N_KV_HEADS, HEAD_DIM, a.shape[2]), (0, 3, 1, 2))
    return (y_p.reshape(bsz, t_len, d_model), y_s.reshape(sq, s_len, d_model),
            kv_out(kt), kv_out(vt), jnp.transpose(kit, (0, 2, 1)),
            re_p.reshape(bsz, g_ssm, p_ssm), im_p.reshape(bsz, g_ssm, p_ssm),
            kv_out(kt_s).reshape(sq, s_len, N_KV_HEADS, HEAD_DIM),
            kv_out(vt_s).reshape(sq, s_len, N_KV_HEADS, HEAD_DIM),
            jnp.transpose(kit_s, (0, 2, 1)).reshape(sq, s_len, IDX_DIM),
            re_s.reshape(sq, g_ssm, p_ssm), im_s.reshape(sq, g_ssm, p_ssm))
```

```python
import functools
import math

import jax
import jax.numpy as jnp
from jax import lax
from jax.experimental import pallas as pl
from jax.experimental.pallas import tpu as pltpu

F32 = jnp.float32
BF16 = jnp.bfloat16
I32 = jnp.int32

SSM_GROUP = 16
SSM_STATE = 64
N_HEADS = 8
N_KV_HEADS = 2
HEAD_DIM = 64
ROT_DIM = HEAD_DIM // 4
N_IDX_HEADS = 8
IDX_DIM = 64
IDX_ROT_DIM = IDX_DIM // 4
ROPE_THETA = 500000.0
TOPK_MAX = 256
EPS = 1e-6
LOG2_E = math.log2(math.e)

LANES = 128
SUBLANES = 8
MXU_COLS = 256
S5_SEQS = 8
S5_SCAN_COLS = 4
COUNT_WAYS = 4
FINE_ROUNDS_UNCHECKED = 3
VT_ROWS = 80
VMEM_LIMIT = 56 * 1024 * 1024

NEG_INF = float("-inf")
FLT_MAX = float(jnp.finfo(jnp.float32).max)
KEY_LOW = -2139095040
KEY_INF = 0x7F800000
NT_DIMS = (((1,), (1,)), ((), ()))


def _const_spec(shape):
    nd = len(shape)
    return pl.BlockSpec(shape, lambda *_: (0,) * nd, pipeline_mode=pl.Buffered(1))


def _rms_norm(x, g):
    ms = jnp.mean(x * x, axis=-1, keepdims=True)
    return x * lax.rsqrt(ms + EPS) * g


def _sigmoid(x):
    return 1.0 / (1.0 + jnp.exp(-x))


def _rope(x, cos_t, sin_a, sin_b):
    return (x * cos_t + pltpu.roll(x, LANES - ROT_DIM // 2, axis=1) * sin_a
            + pltpu.roll(x, ROT_DIM // 2, axis=1) * sin_b)


def _in_proj_kernel(x_ref, g_ref, w_ref, cos_ref, sa_ref, sb_ref,
                    u_ref, q_ref, kt_ref, kb_ref, vt_ref, vb_ref, vtb_ref, qi_ref, kit_ref, kib_ref,
                    wi_ref, wit_ref, sgs_ref, sga_ref, *, d_ssm, d_attn, d_kv, d_qi, d_model):
    assert d_kv == LANES
    tm = x_ref.shape[0]
    h = _rms_norm(x_ref[...], g_ref[...]).astype(BF16)
    cos_t, sin_a, sin_b = cos_ref[...], sa_ref[...], sb_ref[...]

    def proj(c0, width):
        return jnp.dot(h, w_ref[:, c0:c0 + width], preferred_element_type=F32)

    off = 0
    u_ref[...] = proj(off, d_ssm)
    off += d_ssm
    def rope_chunks(c0, width):
        for m in range(width // MXU_COLS):
            wide = proj(c0 + m * MXU_COLS, MXU_COLS)
            for c in range(MXU_COLS // LANES):
                yield (m * MXU_COLS // LANES + c,
                       _rope(wide[:, c * LANES:(c + 1) * LANES], cos_t, sin_a, sin_b))

    for c, r in rope_chunks(off, d_attn):
        q_ref[:, c * LANES:(c + 1) * LANES] = (r * (HEAD_DIM ** -0.5 * LOG2_E)).astype(BF16)
    off += d_attn
    kv = proj(off, 2 * d_kv)
    r = _rope(kv[:, :d_kv], cos_t, sin_a, sin_b)
    kt_ref[...] = r.T
    kb_ref[...] = r.astype(BF16)
    off += d_kv
    vv = kv[:, d_kv:]
    vt = vv.T
    vt_ref[...] = vt
    vb_ref[...] = vv.astype(BF16)
    sub = lax.broadcasted_iota(I32, (VT_ROWS - HEAD_DIM, tm), 0)
    ones_pad = jnp.where(sub == 0, 1.0, 0.0)
    for g in range(N_KV_HEADS):
        vtb_ref[g] = jnp.concatenate([vt[g * HEAD_DIM:(g + 1) * HEAD_DIM, :], ones_pad], axis=0).astype(BF16)
    off += d_kv
    for c, r in rope_chunks(off, d_qi):
        qi_ref[:, c * LANES:(c + 1) * LANES] = r.astype(BF16)
    off += d_qi
    kw = proj(off, LANES)
    lane = lax.broadcasted_iota(I32, kw.shape, 1)
    kr = jnp.where(lane < IDX_DIM, _rope(kw, cos_t, sin_a, sin_b), kw * (N_IDX_HEADS ** -0.5))
    krt = kr.T
    kit_ref[...] = krt[:IDX_DIM, :]
    kib_ref[...] = kr[:, :IDX_DIM].astype(BF16)
    wi_ref[...] = kr[:, IDX_DIM:IDX_DIM + N_IDX_HEADS]
    wit_ref[...] = krt[IDX_DIM:IDX_DIM + N_IDX_HEADS, :]
    off += LANES
    sgs_ref[...] = _sigmoid(proj(off, d_model))
    off += d_model
    sga_ref[...] = _sigmoid(proj(off, d_model))


def _rope_tables(pos, n_rows):
    half = ROT_DIM // 2
    inv = ROPE_THETA ** (-jnp.arange(half, dtype=F32) / half)
    ang = pos.astype(F32)[:, None] * inv[None, :]
    cos, sin = jnp.cos(ang), jnp.sin(ang)
    ones = jnp.ones((n_rows, HEAD_DIM - ROT_DIM), F32)
    zeros = jnp.zeros((n_rows, HEAD_DIM - ROT_DIM), F32)
    zh = jnp.zeros((n_rows, half), F32)
    cos_t = jnp.concatenate([cos, cos, ones], axis=1)
    sin_a = jnp.concatenate([-sin, zh, zeros], axis=1)
    sin_b = jnp.concatenate([zh, sin, zeros], axis=1)
    rep = LANES // HEAD_DIM
    return tuple(jnp.tile(t, (1, rep)) for t in (cos_t, sin_a, sin_b))


def _in_proj(x2d, n_seq, pos_tab, norm_g, w_pack, tm, dims):
    n, d_model = x2d.shape
    d_ssm, d_attn, d_kv, d_qi = dims
    cos_t, sin_a, sin_b = pos_tab
    t_len = n // n_seq
    nt = t_len // tm
    grid = (n // tm,)
    row = lambda w: pl.BlockSpec((tm, w), lambda i: (i, 0))
    tab = pl.BlockSpec((tm, LANES), lambda i: (i % nt, 0))
    feat = lambda r: pl.BlockSpec((None, r, tm), lambda i: (i // nt, 0, i % nt))
    kern = functools.partial(_in_proj_kernel, d_ssm=d_ssm, d_attn=d_attn, d_kv=d_kv, d_qi=d_qi,
                             d_model=d_model)
    outs = [
        (jax.ShapeDtypeStruct((n, d_ssm), F32), row(d_ssm)),
        (jax.ShapeDtypeStruct((n, d_attn), BF16), row(d_attn)),
        (jax.ShapeDtypeStruct((n_seq, d_kv, t_len), F32), feat(d_kv)),
        (jax.ShapeDtypeStruct((n, d_kv), BF16), row(d_kv)),
        (jax.ShapeDtypeStruct((n_seq, d_kv, t_len), F32), feat(d_kv)),
        (jax.ShapeDtypeStruct((n, d_kv), BF16), row(d_kv)),
        (jax.ShapeDtypeStruct((n_seq, nt, N_KV_HEADS, VT_ROWS, tm), BF16),
         pl.BlockSpec((None, None, N_KV_HEADS, VT_ROWS, tm), lambda i: (i // nt, i % nt, 0, 0, 0))),
        (jax.ShapeDtypeStruct((n, d_qi), BF16), row(d_qi)),
        (jax.ShapeDtypeStruct((n_seq, IDX_DIM, t_len), F32), feat(IDX_DIM)),
        (jax.ShapeDtypeStruct((n, IDX_DIM), BF16), row(IDX_DIM)),
        (jax.ShapeDtypeStruct((n, N_IDX_HEADS), F32), row(N_IDX_HEADS)),
        (jax.ShapeDtypeStruct((n_seq, N_IDX_HEADS, t_len), F32), feat(N_IDX_HEADS)),
        (jax.ShapeDtypeStruct((n, d_model), F32), row(d_model)),
        (jax.ShapeDtypeStruct((n, d_model), F32), row(d_model)),
    ]
    out_shapes = [o[0] for o in outs]
    out_specs = [o[1] for o in outs]
    return pl.pallas_call(
        kern, grid=grid,
        in_specs=[row(d_model), _const_spec((1, d_model)), _const_spec(w_pack.shape), tab, tab, tab],
        out_specs=out_specs, out_shape=out_shapes,
        compiler_params=pltpu.CompilerParams(dimension_semantics=("arbitrary",),
                                             vmem_limit_bytes=VMEM_LIMIT),
        name="in_proj",
    )(x2d, norm_g.reshape(1, d_model), w_pack, cos_t, sin_a, sin_b)


def _s5_readout(y, u, dsk_ref, wglu_ref, bglu_ref, d_model):
    y = y + dsk_ref[...] * u
    cdf = 0.5 * (1.0 + jnp.tanh(math.sqrt(2.0 / math.pi) * (y + 0.044715 * (y * y * y))))
    gl = (y * cdf).astype(BF16)
    z = jnp.dot(gl, wglu_ref[...], preferred_element_type=F32) + bglu_ref[...]
    return z[:, :d_model] * _sigmoid(z[:, d_model:])


def _s5_prompt_kernel(u_ref, perm_ref, bcat_ref, are_ref, aim_ref, ccre_ref, ccim_ref, dsk_ref, wglu_ref,
                      bglu_ref, out_ref, sre_ref, sim_ref, hre, him, zs, cr, ci, *, tc, d_model):
    @pl.when(pl.program_id(1) == 0)
    def _():
        cr[...] = jnp.zeros_like(cr)
        ci[...] = jnp.zeros_like(ci)

    nseq = u_ref.shape[0]
    d_ssm, n_col = u_ref.shape[2], hre.shape[0]
    sd = n_col * LANES
    u = u_ref[...].reshape(nseq * tc, d_ssm)
    ub = jnp.dot(perm_ref[...], u.astype(BF16), preferred_element_type=F32).astype(BF16)
    for c in range(n_col):
        kc = (c * LANES * d_ssm // sd) // LANES
        bu = jnp.dot(ub[:, kc * LANES:(kc + 1) * LANES], bcat_ref[c], preferred_element_type=F32)
        hre[c] = bu[:, :LANES]
        him[c] = bu[:, LANES:]

    for c0 in range(0, n_col, S5_SCAN_COLS):
        cs = range(c0, c0 + S5_SCAN_COLS)
        ar = [jnp.broadcast_to(are_ref[:, c * LANES:(c + 1) * LANES], (nseq, LANES)) for c in cs]
        ai = [jnp.broadcast_to(aim_ref[:, c * LANES:(c + 1) * LANES], (nseq, LANES)) for c in cs]

        def step(t, carry, cs=cs, ar=ar, ai=ai):
            rows = pl.ds(pl.multiple_of(t * nseq, nseq), nseq)
            out = []
            for n, c in enumerate(cs):
                pr, pi = carry[n]
                xr = hre[c, rows, :] + (ar[n] * pr - ai[n] * pi)
                xi = him[c, rows, :] + (ar[n] * pi + ai[n] * pr)
                hre[c, rows, :] = xr
                him[c, rows, :] = xi
                out.append((xr, xi))
            return tuple(out)

        init = tuple((cr[:, c * LANES:(c + 1) * LANES], ci[:, c * LANES:(c + 1) * LANES]) for c in cs)
        fin = lax.fori_loop(0, tc, step, init, unroll=4)
        for n, c in enumerate(cs):
            cr[:, c * LANES:(c + 1) * LANES] = fin[n][0]
            ci[:, c * LANES:(c + 1) * LANES] = fin[n][1]
    sre_ref[...] = cr[...]
    sim_ref[...] = ci[...]
    n_out = d_ssm // LANES
    per = n_col // n_out
    wide = lambda ref, j: jnp.concatenate([ref[c] for c in range(j * per, (j + 1) * per)], axis=1).astype(BF16)
    for j in range(n_out):
        zs[j] = (jnp.dot(wide(hre, j), ccre_ref[j], preferred_element_type=F32)
                 + jnp.dot(wide(him, j), ccim_ref[j], preferred_element_type=F32))
    y = jnp.concatenate(
        [jnp.concatenate([zs[j, pl.ds(b, tc, stride=nseq), :] for b in range(nseq)], axis=0)
         for j in range(n_out)], axis=1)
    out_ref[...] = _s5_readout(y, u, dsk_ref, wglu_ref, bglu_ref, d_model).reshape(nseq, tc, d_model)


def _s5_sample_kernel(u_ref, h0r_ref, h0i_ref, are_ref, aim_ref, bre_ref, bim_ref, cre_ref, cim_ref,
                      dsk_ref, wglu_ref, bglu_ref, out_ref, sre_ref, sim_ref, *, d_model):
    u = u_ref[...]
    ub = u.astype(BF16)
    ar, ai = are_ref[...], aim_ref[...]
    h0r, h0i = h0r_ref[...], h0i_ref[...]
    hr = (ar * h0r - ai * h0i) + jnp.dot(ub, bre_ref[...], preferred_element_type=F32)
    hi = (ar * h0i + ai * h0r) + jnp.dot(ub, bim_ref[...], preferred_element_type=F32)
    sre_ref[...] = hr
    sim_ref[...] = hi
    y = (jnp.dot(hr.astype(BF16), cre_ref[...], preferred_element_type=F32)
         + jnp.dot(hi.astype(BF16), cim_ref[...], preferred_element_type=F32))
    out_ref[...] = _s5_readout(y, u, dsk_ref, wglu_ref, bglu_ref, d_model)


def _s5_params(a_re, a_im, log_dt, b_re, b_im, c_re, c_im):
    g, p = a_re.shape
    a_c = lax.complex(a_re.astype(F32), a_im.astype(F32))
    dt = jnp.exp(log_dt.astype(F32))[:, None]
    a_bar = jnp.exp(a_c * dt)
    b_bar = ((a_bar - 1.0) / a_c)[:, :, None] * lax.complex(b_re.astype(F32), b_im.astype(F32))
    eye = jnp.eye(g, dtype=F32)
    n = b_re.shape[2]

    def bmat(b):
        return jnp.einsum('gpn,gh->gnhp', b, eye).reshape(g * n, g * p)

    def cmat(c):
        return jnp.einsum('gnp,gh->gphn', c, eye).reshape(g * p, g * n)

    bre, bim = bmat(jnp.real(b_bar)).astype(BF16), bmat(jnp.imag(b_bar)).astype(BF16)
    cre, cim = cmat(c_re.astype(F32)).astype(BF16), cmat(-c_im.astype(F32)).astype(BF16)
    d_in, sd = bre.shape
    assert d_in % LANES == 0 and sd % LANES == 0 and LANES % (LANES * d_in // sd) == 0
    bcat = jnp.stack([
        jnp.concatenate([m[(c * LANES * d_in // sd) // LANES * LANES:][:LANES, c * LANES:(c + 1) * LANES]
                         for m in (bre, bim)], axis=1)
        for c in range(sd // LANES)], axis=0)
    n_out = d_in // LANES
    span = sd // n_out
    ccre, ccim = (jnp.stack([m[j * span:(j + 1) * span, j * LANES:(j + 1) * LANES] for j in range(n_out)],
                            axis=0) for m in (cre, cim))
    a1 = a_bar.reshape(1, g * p)
    return jnp.real(a1), jnp.imag(a1), bre, bim, cre, cim, bcat, ccre, ccim


def _s5_prompt(u3, s5p, dsk, wglu, bglu, tc):
    bsz, t_len, d_ssm = u3.shape
    are, aim, bre, _, _, _, bcat, ccre, ccim = s5p
    sd = bre.shape[1]
    d_model = wglu.shape[1] // 2
    nseq = math.gcd(bsz, S5_SEQS)
    assert sd % (S5_SCAN_COLS * LANES) == 0 and tc % SUBLANES == 0
    kern = functools.partial(_s5_prompt_kernel, tc=tc, d_model=d_model)
    state = pl.BlockSpec((nseq, sd), lambda b, c: (b, 0))
    dst = jnp.arange(nseq * tc)
    perm = (jnp.arange(nseq * tc)[None, :] == ((dst % nseq) * tc + dst // nseq)[:, None]).astype(BF16)
    return pl.pallas_call(
        kern, grid=(bsz // nseq, t_len // tc),
        in_specs=[pl.BlockSpec((nseq, tc, d_ssm), lambda b, c: (b, c, 0)), _const_spec(perm.shape),
                  _const_spec(bcat.shape), _const_spec(are.shape), _const_spec(aim.shape),
                  _const_spec(ccre.shape), _const_spec(ccim.shape), _const_spec((1, d_ssm)),
                  _const_spec(wglu.shape), _const_spec((1, 2 * d_model))],
        out_specs=[pl.BlockSpec((nseq, tc, d_model), lambda b, c: (b, c, 0)), state, state],
        out_shape=[jax.ShapeDtypeStruct((bsz, t_len, d_model), F32),
                   jax.ShapeDtypeStruct((bsz, sd), F32),
                   jax.ShapeDtypeStruct((bsz, sd), F32)],
        scratch_shapes=[pltpu.VMEM((sd // LANES, nseq * tc, LANES), F32),
                        pltpu.VMEM((sd // LANES, nseq * tc, LANES), F32),
                        pltpu.VMEM((d_ssm // LANES, nseq * tc, LANES), F32),
                        pltpu.VMEM((nseq, sd), F32), pltpu.VMEM((nseq, sd), F32)],
        compiler_params=pltpu.CompilerParams(dimension_semantics=("arbitrary", "arbitrary"),
                                             vmem_limit_bytes=VMEM_LIMIT),
        name="s5_prompt",
    )(u3, perm, bcat, are, aim, ccre, ccim, dsk.reshape(1, d_ssm), wglu, bglu.reshape(1, 2 * d_model))


def _s5_sample(u2, h0r, h0i, s5p, dsk, wglu, bglu):
    n, d_ssm = u2.shape
    are, aim, bre, bim, cre, cim = s5p[:6]
    sd = bre.shape[1]
    d_model = wglu.shape[1] // 2
    kern = functools.partial(_s5_sample_kernel, d_model=d_model)
    return pl.pallas_call(
        kern,
        out_shape=[jax.ShapeDtypeStruct((n, d_model), F32),
                   jax.ShapeDtypeStruct((n, sd), F32),
                   jax.ShapeDtypeStruct((n, sd), F32)],
        compiler_params=pltpu.CompilerParams(vmem_limit_bytes=VMEM_LIMIT),
        name="s5_sample",
    )(u2, h0r, h0i, are, aim, bre, bim, cre, cim, dsk.reshape(1, d_ssm), wglu,
      bglu.reshape(1, 2 * d_model))


def _f2k(x):
    b = lax.bitcast_convert_type(x, I32)
    return jnp.where(b < 0, b ^ 0x7FFFFFFF, b)


def _k2f(k):
    return lax.bitcast_convert_type(jnp.where(k < 0, k ^ 0x7FFFFFFF, k), F32)


def _threshold_search(count_ge, amax, forced, k_top):
    kf = float(k_top)
    lo0 = jnp.where(forced, KEY_LOW, _f2k(-amax))
    hi0 = jnp.where(forced, KEY_LOW + 1, _f2k(amax) + 1)

    def cond(st):
        lo, hi, _ = st
        return jnp.max(jnp.where(hi > lo + 1, 1.0, 0.0)) > 0.5

    def body(st):
        lo, hi, it = st
        active = hi > lo + 1
        mid_i = (lo >> 1) + (hi >> 1) + (lo & hi & 1)
        mk = _f2k(0.5 * _k2f(lo) + 0.5 * _k2f(hi))
        use_f = jnp.logical_and(it % 2 == 0, jnp.logical_and(mk > lo, mk < hi))
        mid = jnp.where(use_f, mk, mid_i)
        c = count_ge(_k2f(mid))
        ge = c >= kf
        nlo = jnp.where(ge, mid, lo)
        nhi = jnp.where(c == kf, mid + 1, jnp.where(ge, hi, mid))
        return jnp.where(active, nlo, lo), jnp.where(active, nhi, hi), it + 1

    lo, _, _ = lax.while_loop(cond, body, (lo0, hi0, jnp.int32(0)))
    return _k2f(lo)


def _tri_exclusive(n):
    r = lax.broadcasted_iota(I32, (n, n), 0)
    c = lax.broadcasted_iota(I32, (n, n), 1)
    return (r < c).astype(BF16)


def _pipelined_blocks(n, produce, consume, buf_a, buf_b, carry):
    produce(0, buf_a)

    def pair(t, c):
        produce(2 * t + 1, buf_b)
        c = consume(2 * t, buf_a, c)
        produce(2 * t + 2, buf_a)
        return consume(2 * t + 1, buf_b, c)

    carry = lax.fori_loop(0, n // 2, pair, carry)
    return lax.cond(n % 2 == 1, lambda c: consume(n - 1, buf_a, c), lambda c: c, carry)


def _dsa_prompt_kernel(qi_ref, wt_ref, q_ref, ki_ref, k_ref, vt_ref, o_ref,
                       sc_ref, scb_ref, qis_ref, qs_ref, m_ref, acc_ref, sa_ref, sb_ref, *, tq, tk, k_top):
    i = pl.program_id(1)
    nkb = i + 1
    hpg = N_HEADS // N_KV_HEADS
    ngr = tk // SUBLANES
    kf = float(k_top)

    qi = qi_ref[...]
    q = q_ref[...]
    zpad = jnp.zeros((tq, HEAD_DIM), BF16)
    for h in range(N_IDX_HEADS):
        qis_ref[h * tq:(h + 1) * tq, :] = qi[:, h * IDX_DIM:(h + 1) * IDX_DIM]
    for h in range(N_HEADS):
        qh = q[:, h * HEAD_DIM:(h + 1) * HEAD_DIM]
        pair = [qh, zpad] if h // hpg == 0 else [zpad, qh]
        qs_ref[h * tq:(h + 1) * tq, :] = jnp.concatenate(pair, axis=1)

    w8 = wt_ref[...] * (IDX_DIM ** -0.5)
    qpos = i * tq + lax.broadcasted_iota(I32, (SUBLANES, tq), 1)

    def idx_dots(j, dst):
        k0 = pl.multiple_of(jnp.minimum(j, nkb - 1) * tk, tk)
        dst[...] = lax.dot_general(ki_ref[pl.ds(k0, tk), :], qis_ref[...], NT_DIMS, preferred_element_type=F32)

    def score_blk(j, src, carry, diagonal=False):
        for c in range(tq // LANES):
            acc = jnp.zeros((tk, LANES), F32)
            for h in range(N_IDX_HEADS):
                cols = slice(h * tq + c * LANES, h * tq + (c + 1) * LANES)
                acc = acc + jnp.maximum(src[:, cols], 0.0) * w8[h:h + 1, c * LANES:(c + 1) * LANES]
            if diagonal:
                kpos = j * tk + lax.broadcasted_iota(I32, (tk, LANES), 0)
                causal = kpos <= i * tq + c * LANES + lax.broadcasted_iota(I32, (tk, LANES), 1)
                acc = jnp.where(causal, acc, NEG_INF)
            sc_ref[j, :, c * LANES:(c + 1) * LANES] = acc
            top = lax.bitcast_convert_type(lax.bitcast_convert_type(acc, I32) & jnp.int32(-65536), F32)
            scb_ref[j, :, c * LANES:(c + 1) * LANES] = top.astype(BF16)
        return carry

    _pipelined_blocks(nkb - 1, idx_dots, score_blk, sa_ref, sb_ref, jnp.int32(0))
    idx_dots(nkb - 1, sa_ref)
    score_blk(nkb - 1, sa_ref, 0, diagonal=True)

    rep = lambda x: jnp.broadcast_to(x, (SUBLANES, tq))
    forced = (qpos + 1) <= k_top

    packed = 2 * SUBLANES
    one_b, zero_b = jnp.ones((), BF16), jnp.zeros((), BF16)

    def count16(k16):
        bits = jnp.where(k16 < 0, k16 ^ 0x7FFF, k16) << 16
        tb = jnp.broadcast_to(lax.bitcast_convert_type(bits, F32)[0:1, :], (packed, tq)).astype(BF16)

        def body(j, part):
            blk = scb_ref[j].reshape(tk // packed // COUNT_WAYS, COUNT_WAYS, packed, tq)
            hit = jnp.where(blk >= tb[None, None], one_b, zero_b)
            for g in range(hit.shape[0]):
                part = part + hit[g]
            return part

        part = lax.fori_loop(0, nkb, body, jnp.zeros((COUNT_WAYS, packed, tq), BF16))
        return rep(jnp.sum(jnp.sum(part.astype(F32), axis=0), axis=0, keepdims=True))

    def coarse_pass(lo16, hi16, c_lo):
        active = hi16 > lo16 + 1
        mid = lo16 + ((hi16 - lo16) >> 1)
        c = count16(mid)
        up = jnp.logical_and(active, c >= kf)
        nhi = jnp.where(c == kf, mid, jnp.where(c >= kf, hi16, mid))
        return jnp.where(up, mid, lo16), jnp.where(active, nhi, hi16), jnp.where(up, c, c_lo)

    lo16 = jnp.full((SUBLANES, tq), KEY_LOW >> 16, I32)
    hi16 = jnp.where(forced, KEY_LOW >> 16, KEY_INF >> 16)
    c_lo = (qpos + 1).astype(F32)
    lo16, hi16, c_lo = lax.fori_loop(0, 8, lambda _, st: coarse_pass(*coarse_pass(*st)), (lo16, hi16, c_lo))
    lo0 = lo16 << 16
    hi0 = jnp.where(hi16 == lo16, lo0 + 1, hi16 << 16)

    def count(thr, strict):
        def body(j, part):
            blk = sc_ref[j].reshape(ngr // COUNT_WAYS, COUNT_WAYS, SUBLANES, tq)
            hit = (blk > thr[None, None]) if strict else (blk >= thr[None, None])
            return part + jnp.sum(jnp.where(hit, 1.0, 0.0), axis=0)

        part = lax.fori_loop(0, nkb, body, jnp.zeros((COUNT_WAYS, SUBLANES, tq), F32))
        return rep(jnp.sum(jnp.sum(part, axis=0), axis=0, keepdims=True))

    def search_pass(lo, hi, c_lo, value_space):
        active = hi > lo + 1
        span = hi - lo
        mid = lo + lax.shift_right_logical(span, jnp.ones_like(span))
        if value_space:
            mk = _f2k(0.5 * _k2f(lo) + 0.5 * _k2f(hi))
            mid = jnp.where(jnp.logical_and(mk > lo, mk < hi), mk, mid)
        c = count(_k2f(mid), False)
        up = jnp.logical_and(active, c >= kf)
        nhi = jnp.where(c == kf, mid + 1, jnp.where(c >= kf, hi, mid))
        return jnp.where(up, mid, lo), jnp.where(active, nhi, hi), jnp.where(up, c, c_lo)

    def double_round(lo, hi, c_lo):
        return search_pass(*search_pass(lo, hi, c_lo, True), False)

    def unresolved(lo, hi):
        return (jnp.max(jnp.where(hi > lo + 1, 1.0, 0.0)) > 0.5).astype(I32)

    def search_body(st):
        lo, hi, c_lo = double_round(*st[:3])
        return lo, hi, c_lo, unresolved(lo, hi)

    lo, hi, c_lo = lax.fori_loop(0, FINE_ROUNDS_UNCHECKED, lambda _, s: double_round(*s), (lo0, hi0, c_lo))
    lo, _, c_ge, _ = lax.while_loop(lambda s: s[3] > 0, search_body, (lo, hi, c_lo, unresolved(lo, hi)))
    thr = _k2f(lo)
    thr1 = thr[0:1, :]

    @pl.when(jnp.max(jnp.where(c_ge > kf, 1.0, 0.0)) > 0.5)
    def _():
        need = jnp.where(c_ge > kf, kf - count(thr, True), 1e30)[0:1, :]
        r = lax.broadcasted_iota(I32, (tk, tk), 0)
        c = lax.broadcasted_iota(I32, (tk, tk), 1)
        earlier = (c < r).astype(BF16)
        ones_k = jnp.ones((SUBLANES, tk), BF16)

        def fix(j, run):
            sc = sc_ref[j]
            eq = sc == thr1
            eqb = jnp.where(eq, 1.0, 0.0).astype(BF16)
            before = jnp.dot(earlier, eqb, preferred_element_type=F32) + run[0:1, :]
            sc_ref[j] = jnp.where(jnp.logical_and(eq, before >= need), NEG_INF, sc)
            return run + jnp.dot(ones_k, eqb, preferred_element_type=F32)

        lax.fori_loop(0, nkb, fix, jnp.zeros((SUBLANES, tq), F32))

    m_ref[...] = jnp.full(m_ref.shape, NEG_INF, F32)
    acc_ref[...] = jnp.zeros(acc_ref.shape, F32)

    def logits(j, dst):
        jc = jnp.minimum(j, nkb - 1)
        k0 = pl.multiple_of(jc * tk, tk)
        dst[...] = lax.dot_general(k_ref[pl.ds(k0, tk), :], qs_ref[...], NT_DIMS, preferred_element_type=F32)

    def softmax_pv(j, src):
        jc = j
        bias = jnp.where(sc_ref[j] >= thr1, 0.0, NEG_INF)
        for h in range(N_HEADS):
            x = src[:, h * tq:(h + 1) * tq] + bias
            bm = jnp.max(jnp.max(x.reshape(ngr, SUBLANES, tq), axis=0), axis=0, keepdims=True)
            m_old = m_ref[h:h + 1, :]
            m_new = jnp.maximum(m_old, bm)
            m_safe = jnp.where(m_new == NEG_INF, 0.0, m_new)
            p = jnp.exp2(x - m_safe).astype(BF16)
            pv = jnp.dot(vt_ref[jc, h // hpg], p, preferred_element_type=F32)
            acc_ref[h] = acc_ref[h] * jnp.exp2(m_old - m_safe) + pv
            m_ref[h:h + 1, :] = m_new

    def consume(j, src, carry):
        softmax_pv(j, src)
        return carry

    _pipelined_blocks(nkb, logits, consume, sa_ref, sb_ref, jnp.int32(0))

    for pr in range(N_HEADS // 2):
        outs = []
        for h in (2 * pr, 2 * pr + 1):
            a = acc_ref[h]
            outs.append(a[:HEAD_DIM, :] / a[HEAD_DIM:HEAD_DIM + 1, :])
        o2 = jnp.concatenate(outs, axis=0)
        o_ref[:, 2 * pr * HEAD_DIM:(2 * pr + 2) * HEAD_DIM] = o2.T.astype(o_ref.dtype)


def _dsa_prompt(qi, wt, q, kib, kb, vt, tq, tk):
    bsz, t_len, d_qi = qi.shape
    d_attn = q.shape[2]
    d_kv = kb.shape[2]
    nkb = t_len // tk
    k_top = min(TOPK_MAX, t_len // 4)
    assert tq == tk and tq % LANES == 0 and t_len % tq == 0 and vt.shape[1] == nkb
    assert nkb * (tk // (2 * SUBLANES * COUNT_WAYS)) <= 256
    kern = functools.partial(_dsa_prompt_kernel, tq=tq, tk=tk, k_top=k_top)
    tile = lambda w: pl.BlockSpec((None, tq, w), lambda b, i: (b, i, 0))
    full = lambda w: pl.BlockSpec((None, t_len, w), lambda b, i: (b, 0, 0))
    return pl.pallas_call(
        kern, grid=(bsz, t_len // tq),
        in_specs=[tile(d_qi), pl.BlockSpec((None, N_IDX_HEADS, tq), lambda b, i: (b, 0, i)), tile(d_attn),
                  full(IDX_DIM), full(d_kv),
                  pl.BlockSpec((None, nkb, N_KV_HEADS, VT_ROWS, tk), lambda b, i: (b, 0, 0, 0, 0))],
        out_specs=tile(d_attn),
        out_shape=jax.ShapeDtypeStruct((bsz, t_len, d_attn), BF16),
        scratch_shapes=[pltpu.VMEM((nkb, tk, tq), F32),
                        pltpu.VMEM((nkb, tk, tq), BF16),
                        pltpu.VMEM((N_IDX_HEADS * tq, IDX_DIM), BF16),
                        pltpu.VMEM((N_HEADS * tq, N_KV_HEADS * HEAD_DIM), BF16),
                        pltpu.VMEM((N_HEADS, tq), F32),
                        pltpu.VMEM((N_HEADS, VT_ROWS, tq), F32),
                        pltpu.VMEM((tk, N_HEADS * tq), F32),
                        pltpu.VMEM((tk, N_HEADS * tq), F32)],
        compiler_params=pltpu.CompilerParams(dimension_semantics=("arbitrary", "arbitrary"),
                                             vmem_limit_bytes=VMEM_LIMIT),
        name="dsa_prompt",
    )(qi, wt, q, kib, kb, vt)


def _page_fetcher(pt_ref, n_pages, streams):
    def copies(seq, slot):
        for p in range(n_pages):
            pg = pt_ref[seq * n_pages + p]
            for hbm, buf, sem in streams:
                page = hbm.shape[2]
                yield pltpu.make_async_copy(hbm.at[pg], buf.at[slot, :, pl.ds(p * page, page)], sem.at[slot])

    def fetch(seq, slot):
        for cp in copies(seq, slot):
            cp.start()

    def wait(seq, slot):
        for cp in copies(seq, slot):
            cp.wait()

    return fetch, wait


def _prefetch_next_and_wait(fetch, wait):
    b = pl.program_id(0)
    slot = b % 2

    @pl.when(b == 0)
    def _():
        fetch(0, 0)

    @pl.when(b + 1 < pl.num_programs(0))
    def _():
        fetch(b + 1, 1 - slot)

    wait(b, slot)
    return b, slot


def _idx_score_kernel(pt_ref, q_ref, w_ref, ci_ref, o_ref, ibuf, sem, *, n_pages):
    b, slot = _prefetch_next_and_wait(*_page_fetcher(pt_ref, n_pages, [(ci_ref, ibuf, sem)]))
    q = q_ref[...]
    w = w_ref[...] * (IDX_DIM ** -0.5)
    d = jnp.dot(q, ibuf[slot].astype(BF16), preferred_element_type=F32)
    o_ref[pl.ds(b % SUBLANES, 1), :] = jnp.sum(jnp.maximum(d, 0.0) * w, axis=0, keepdims=True)


def _idx_scores(page_table, qis, wcol, cache_idx_kt):
    sq, n_pages = page_table.shape
    page = cache_idx_kt.shape[2]
    past = n_pages * page
    kern = functools.partial(_idx_score_kernel, n_pages=n_pages)
    grid_spec = pltpu.PrefetchScalarGridSpec(
        num_scalar_prefetch=1, grid=(sq,),
        in_specs=[pl.BlockSpec((None, N_IDX_HEADS, IDX_DIM), lambda b, pt: (b, 0, 0)),
                  pl.BlockSpec((None, N_IDX_HEADS, 1), lambda b, pt: (b, 0, 0)),
                  pl.BlockSpec(memory_space=pl.ANY)],
        out_specs=pl.BlockSpec((SUBLANES, past), lambda b, pt: (b // SUBLANES, 0)),
        scratch_shapes=[pltpu.VMEM((2, IDX_DIM, past), cache_idx_kt.dtype), pltpu.SemaphoreType.DMA((2,))])
    return pl.pallas_call(
        kern, grid_spec=grid_spec,
        out_shape=jax.ShapeDtypeStruct((sq, past), F32),
        compiler_params=pltpu.CompilerParams(dimension_semantics=("arbitrary",),
                                             vmem_limit_bytes=VMEM_LIMIT),
        name="sample_idx_scores",
    )(page_table.reshape(-1), qis, wcol, cache_idx_kt)


def _sample_select_kernel(sc_ref, qi_ref, ki_ref, wi_ref, mask_ref, selfsel_ref, *, ch, k_top):
    sq, l_past = sc_ref.shape
    nch = l_past // ch
    w = wi_ref[...] * (IDX_DIM ** -0.5)
    qf = qi_ref[...].astype(F32)
    kf32 = ki_ref[...].astype(F32)
    s_self = jnp.zeros((sq, 1), F32)
    for h in range(N_IDX_HEADS):
        dh = jnp.sum(qf[:, h * IDX_DIM:(h + 1) * IDX_DIM] * kf32, axis=1, keepdims=True)
        s_self = s_self + jnp.maximum(dh, 0.0) * w[:, h:h + 1]

    amax = jnp.abs(s_self)
    for c in range(nch):
        amax = jnp.maximum(amax, jnp.max(jnp.abs(sc_ref[:, c * ch:(c + 1) * ch]), axis=1, keepdims=True))

    def count_cmp(thr, strict):
        thr_b = jnp.broadcast_to(thr, (sq, LANES))
        part = jnp.zeros((sq, LANES), F32)
        for c in range(l_past // LANES):
            blk = sc_ref[:, c * LANES:(c + 1) * LANES]
            hit = (blk > thr_b) if strict else (blk >= thr_b)
            part = part + jnp.where(hit, 1.0, 0.0)
        self_hit = (s_self > thr) if strict else (s_self >= thr)
        return jnp.sum(part, axis=1, keepdims=True) + jnp.where(self_hit, 1.0, 0.0)

    kf = float(k_top)
    forced = amax < 0.0
    thr = _threshold_search(lambda t: count_cmp(t, False), amax, forced, k_top)
    need = kf - count_cmp(thr, True)
    tri = _tri_exclusive(ch)
    run = jnp.zeros((sq, 1), F32)
    for c in range(nch):
        sc = sc_ref[:, c * ch:(c + 1) * ch]
        eq = sc == thr
        before = jnp.dot(jnp.where(eq, 1.0, 0.0).astype(BF16), tri, preferred_element_type=F32) + run
        keep = jnp.logical_or(sc > thr, jnp.logical_and(eq, before < need))
        mask_ref[:, c * ch:(c + 1) * ch] = jnp.where(keep, 1.0, 0.0)
        run = run + jnp.sum(jnp.where(eq, 1.0, 0.0), axis=1, keepdims=True)
    self_keep = jnp.logical_or(s_self > thr, jnp.logical_and(s_self == thr, run < need))
    selfsel_ref[...] = jnp.where(self_keep, 1.0, 0.0)


def _sample_select(scores, qi, kib, wi, ch, k_top):
    sq, l_past = scores.shape
    kern = functools.partial(_sample_select_kernel, ch=ch, k_top=k_top)
    return pl.pallas_call(
        kern,
        out_shape=[jax.ShapeDtypeStruct((sq, l_past), F32), jax.ShapeDtypeStruct((sq, 1), F32)],
        compiler_params=pltpu.CompilerParams(vmem_limit_bytes=VMEM_LIMIT),
        name="sample_select",
    )(scores, qi, kib, wi)


def _sample_attn_kernel(pt_ref, q_ref, mask_ref, ks_ref, vs_ref, ss_ref, ck_ref, cv_ref, o_ref,
                        kbuf, vbuf, ksem, vsem, *, n_pages):
    _, slot = _prefetch_next_and_wait(
        *_page_fetcher(pt_ref, n_pages, [(ck_ref, kbuf, ksem), (cv_ref, vbuf, vsem)]))
    hpg = N_HEADS // N_KV_HEADS
    q = q_ref[...]
    s = jnp.dot(q, kbuf[slot].astype(BF16), preferred_element_type=F32)
    sm = jnp.where(mask_ref[...] > 0.5, s, NEG_INF)
    s_self = jnp.sum(q.astype(F32) * ks_ref[...].astype(F32), axis=1, keepdims=True)
    s_self = jnp.where(ss_ref[...] > 0.5, s_self, NEG_INF)
    m = jnp.maximum(jnp.max(sm, axis=1, keepdims=True), s_self)
    m = jnp.where(m == NEG_INF, 0.0, m)
    pb = jnp.exp2(sm - m).astype(BF16)
    p_self = jnp.exp2(s_self - m).astype(BF16).astype(F32)
    l_sum = jnp.sum(pb.astype(F32), axis=1, keepdims=True) + p_self
    acc = (lax.dot_general(pb, vbuf[slot].astype(BF16), NT_DIMS, preferred_element_type=F32)
           + p_self * vs_ref[...].astype(F32))
    o = acc / l_sum
    hrow = lax.broadcasted_iota(I32, o.shape, 0)
    o = jnp.where(hrow < hpg, o, pltpu.roll(o, HEAD_DIM, axis=1))
    o_ref[...] = o[:, :HEAD_DIM].astype(o_ref.dtype)


def _sample_attn(page_table, qs, mask3, kself, vself, selfsel, cache_k2, cache_v2):
    sq, n_pages = page_table.shape
    d_kv, page = cache_k2.shape[1], cache_k2.shape[2]
    past = n_pages * page
    kern = functools.partial(_sample_attn_kernel, n_pages=n_pages)
    per_seq = lambda r, w: pl.BlockSpec((None, r, w), lambda b, pt: (b, 0, 0))
    hbm = pl.BlockSpec(memory_space=pl.ANY)
    grid_spec = pltpu.PrefetchScalarGridSpec(
        num_scalar_prefetch=1, grid=(sq,),
        in_specs=[per_seq(N_HEADS, d_kv), per_seq(1, past), per_seq(1, d_kv), per_seq(1, d_kv),
                  per_seq(1, 1), hbm, hbm],
        out_specs=per_seq(N_HEADS, HEAD_DIM),
        scratch_shapes=[pltpu.VMEM((2, d_kv, past), cache_k2.dtype), pltpu.VMEM((2, d_kv, past), cache_v2.dtype),
                        pltpu.SemaphoreType.DMA((2,)), pltpu.SemaphoreType.DMA((2,))])
    return pl.pallas_call(
        kern, grid_spec=grid_spec,
        out_shape=jax.ShapeDtypeStruct((sq, N_HEADS, HEAD_DIM), BF16),
        compiler_params=pltpu.CompilerParams(dimension_semantics=("arbitrary",),
                                             vmem_limit_bytes=VMEM_LIMIT),
        name="sample_attn",
    )(page_table.reshape(-1), qs, mask3, kself, vself, selfsel, cache_k2, cache_v2)


def _tail_kernel(x_ref, attn_ref, ssm_ref, sgs_ref, sga_ref, wao_ref, wo_ref, g2_ref, wup_ref, wdn_ref,
                 gf_ref, y_ref):
    attn_out = jnp.dot(attn_ref[...], wao_ref[...], preferred_element_type=F32)
    mix = sgs_ref[...] * ssm_ref[...] + sga_ref[...] * attn_out
    x1 = x_ref[...] + jnp.dot(mix.astype(BF16), wo_ref[...], preferred_element_type=F32)
    hh = _rms_norm(x1, g2_ref[...]).astype(BF16)
    up = jnp.dot(hh, wup_ref[...], preferred_element_type=F32)
    r = jnp.square(jnp.maximum(up, 0.0)).astype(BF16)
    x2 = x1 + jnp.dot(r, wdn_ref[...], preferred_element_type=F32)
    y_ref[...] = _rms_norm(x2, gf_ref[...])


def _tail(x2d, attn, ssm, sgs, sga, wao, wo, g2, wup, wdn, gf, tm):
    n, d_model = x2d.shape
    row = lambda w: pl.BlockSpec((tm, w), lambda i: (i, 0))
    return pl.pallas_call(
        _tail_kernel, grid=(n // tm,),
        in_specs=[row(d_model), row(attn.shape[1]), row(d_model), row(d_model), row(d_model),
                  _const_spec(wao.shape), _const_spec(wo.shape), _const_spec((1, d_model)),
                  _const_spec(wup.shape), _const_spec(wdn.shape), _const_spec((1, d_model))],
        out_specs=row(d_model),
        out_shape=jax.ShapeDtypeStruct((n, d_model), F32),
        compiler_params=pltpu.CompilerParams(dimension_semantics=("arbitrary",),
                                             vmem_limit_bytes=VMEM_LIMIT),
        name="tail",
    )(x2d, attn, ssm, sgs, sga, wao, wo, g2.reshape(1, d_model), wup, wdn, gf.reshape(1, d_model))


def _tiles(n_rows, t_len):
    tm = min(256, n_rows)
    tc = min(128, t_len)
    tq = min(256, t_len)
    return tm, tc, tq, tq


def kernel(x_prompt, x_sample, cache_k, cache_v, cache_idx_k, state_ssm_re, state_ssm_im, page_table,
           norm1_g, w_in, ssm_a_re, ssm_a_im, ssm_log_dt, ssm_b_re, ssm_b_im, ssm_c_re, ssm_c_im,
           ssm_d, w_glu, b_glu, w_attn_out, w_o, norm2_g, w_up, w_down, normf_g):
    bsz, t_len, d_model = x_prompt.shape
    sq, s_len, _ = x_sample.shape
    assert s_len == 1, "the sample path handles one new token per sequence"
    n_pool, page = cache_idx_k.shape[0], cache_idx_k.shape[1]
    n_pages = page_table.shape[1]
    past = n_pages * page
    d_ssm = ssm_d.shape[0]
    d_attn = N_HEADS * HEAD_DIM
    d_kv = N_KV_HEADS * HEAD_DIM
    d_qi = N_IDX_HEADS * IDX_DIM
    dims = (d_ssm, d_attn, d_kv, d_qi)

    c_ki = d_ssm + d_attn + 2 * d_kv + d_qi
    c_g = c_ki + IDX_DIM + N_IDX_HEADS
    w_pack = jnp.concatenate(
        [w_in[:, :c_g], jnp.zeros((d_model, LANES - IDX_DIM - N_IDX_HEADS), w_in.dtype), w_in[:, c_g:]],
        axis=1).astype(BF16)
    wglu_b, wao_b, wo_b = w_glu.astype(BF16), w_attn_out.astype(BF16), w_o.astype(BF16)
    wup_b, wdn_b = w_up.astype(BF16), w_down.astype(BF16)
    s5p = _s5_params(ssm_a_re, ssm_a_im, ssm_log_dt, ssm_b_re, ssm_b_im, ssm_c_re, ssm_c_im)

    n_p = bsz * t_len
    tm, tc, tq, tk = _tiles(n_p, t_len)
    xp = x_prompt.reshape(n_p, d_model)
    tabs_p = _rope_tables(jnp.arange(t_len, dtype=I32), t_len)
    assert tm == tk
    (u, q, kt, kb, vt, _, vtb, qi, kit, kib, _, wit, sgs, sga) = _in_proj(
        xp, bsz, tabs_p, norm1_g, w_pack, tm, dims)
    ssm_out, re_p, im_p = _s5_prompt(u.reshape(bsz, t_len, d_ssm), s5p, ssm_d, wglu_b, b_glu, tc)
    r3 = lambda a: a.reshape(bsz, t_len, a.shape[-1])
    attn = _dsa_prompt(r3(qi), wit, r3(q), r3(kib), r3(kb), vtb, tq, tk)
    y_p = _tail(xp, attn.reshape(n_p, d_attn), ssm_out.reshape(n_p, d_model), sgs, sga,
                wao_b, wo_b, norm2_g, wup_b, wdn_b, normf_g, tm)

    xs = x_sample.reshape(sq, d_model)
    tabs_s = _rope_tables(jnp.full((sq,), past, I32), sq)
    (u_s, q_s, kt_s, kb_s, vt_s, vb_s, _, qi_s, kit_s, kib_s, wi_s, _, sgs_s, sga_s) = _in_proj(
        xs, 1, tabs_s, norm1_g, w_pack, sq, dims)
    sd = state_ssm_re.shape[1] * state_ssm_re.shape[2]
    ssm_s, re_s, im_s = _s5_sample(u_s, state_ssm_re.reshape(sq, sd), state_ssm_im.reshape(sq, sd),
                                   s5p, ssm_d, wglu_b, b_glu)
    scores = _idx_scores(page_table, qi_s.reshape(sq, N_IDX_HEADS, IDX_DIM),
                         wi_s.reshape(sq, N_IDX_HEADS, 1), jnp.transpose(cache_idx_k, (0, 2, 1)))
    k_top_s = min(TOPK_MAX, (past + s_len) // 4)
    mask, selfsel = _sample_select(scores, qi_s, kib_s, wi_s, min(512, past), k_top_s)
    q4 = q_s.reshape(sq, N_KV_HEADS, N_HEADS // N_KV_HEADS, HEAD_DIM)
    qs_pad = (q4[:, :, :, None, :] * jnp.eye(N_KV_HEADS, dtype=BF16)[None, :, None, :, None]
              ).reshape(sq, N_HEADS, d_kv)
    feat_major = lambda c: jnp.transpose(c, (0, 2, 3, 1)).reshape(n_pool, d_kv, page)
    attn_s = _sample_attn(page_table, qs_pad, mask.reshape(sq, 1, past),
                          kb_s.reshape(sq, 1, d_kv), vb_s.reshape(sq, 1, d_kv), selfsel.reshape(sq, 1, 1),
                          feat_major(cache_k), feat_major(cache_v))
    y_s = _tail(xs, attn_s.reshape(sq, d_attn), ssm_s, sgs_s, sga_s,
                wao_b, wo_b, norm2_g, wup_b, wdn_b, normf_g, sq)

    g_ssm, p_ssm = state_ssm_re.shape[1], state_ssm_re.shape[2]
    kv_out = lambda a: jnp.transpose(a.reshape(a.shape[0], N_KV_HEADS, HEAD_DIM, a.shape[2]), (0, 3, 1, 2))
    return (y_p.reshape(bsz, t_len, d_model), y_s.reshape(sq, s_len, d_model),
            kv_out(kt), kv_out(vt), jnp.transpose(kit, (0, 2, 1)),
            re_p.reshape(bsz, g_ssm, p_ssm), im_p.reshape(bsz, g_ssm, p_ssm),
            kv_out(kt_s).reshape(sq, s_len, N_KV_HEADS, HEAD_DIM),
            kv_out(vt_s).reshape(sq, s_len, N_KV_HEADS, HEAD_DIM),
            jnp.transpose(kit_s, (0, 2, 1)).reshape(sq, s_len, IDX_DIM),
            re_s.reshape(sq, g_ssm, p_ssm), im_s.reshape(sq, g_ssm, p_ssm))
```

```python
import functools
import math

import jax
import jax.numpy as jnp
from jax import lax
from jax.experimental import pallas as pl
from jax.experimental.pallas import tpu as pltpu

F32 = jnp.float32
BF16 = jnp.bfloat16
I32 = jnp.int32

SSM_GROUP = 16
SSM_STATE = 64
N_HEADS = 8
N_KV_HEADS = 2
HEAD_DIM = 64
ROT_DIM = HEAD_DIM // 4
N_IDX_HEADS = 8
IDX_DIM = 64
IDX_ROT_DIM = IDX_DIM // 4
ROPE_THETA = 500000.0
TOPK_MAX = 256
EPS = 1e-6
LOG2_E = math.log2(math.e)

LANES = 128
SUBLANES = 8
MXU_COLS = 256
S5_SEQS = 8
S5_SCAN_COLS = 4
COUNT_WAYS = 4
FINE_ROUNDS_UNCHECKED = 3
VT_ROWS = 80
VMEM_LIMIT = 56 * 1024 * 1024

NEG_INF = float("-inf")
FLT_MAX = float(jnp.finfo(jnp.float32).max)
KEY_LOW = -2139095040
KEY_INF = 0x7F800000
NT_DIMS = (((1,), (1,)), ((), ()))


def _const_spec(shape):
    nd = len(shape)
    return pl.BlockSpec(shape, lambda *_: (0,) * nd, pipeline_mode=pl.Buffered(1))


def _rms_norm(x, g):
    ms = jnp.mean(x * x, axis=-1, keepdims=True)
    return x * lax.rsqrt(ms + EPS) * g


def _sigmoid(x):
    return 1.0 / (1.0 + jnp.exp(-x))


def _rope(x, cos_t, sin_a, sin_b):
    return (x * cos_t + pltpu.roll(x, LANES - ROT_DIM // 2, axis=1) * sin_a
            + pltpu.roll(x, ROT_DIM // 2, axis=1) * sin_b)


def _in_proj_kernel(x_ref, g_ref, w_ref, cos_ref, sa_ref, sb_ref,
                    u_ref, q_ref, kt_ref, kb_ref, vt_ref, vb_ref, vtb_ref, qi_ref, kit_ref, kib_ref,
                    wi_ref, wit_ref, sgs_ref, sga_ref, *, d_ssm, d_attn, d_kv, d_qi, d_model):
    assert d_kv == LANES
    tm = x_ref.shape[0]
    h = _rms_norm(x_ref[...], g_ref[...]).astype(BF16)
    cos_t, sin_a, sin_b = cos_ref[...], sa_ref[...], sb_ref[...]

    def proj(c0, width):
        return jnp.dot(h, w_ref[:, c0:c0 + width], preferred_element_type=F32)

    off = 0
    u_ref[...] = proj(off, d_ssm)
    off += d_ssm
    def rope_chunks(c0, width):
        for m in range(width // MXU_COLS):
            wide = proj(c0 + m * MXU_COLS, MXU_COLS)
            for c in range(MXU_COLS // LANES):
                yield (m * MXU_COLS // LANES + c,
                       _rope(wide[:, c * LANES:(c + 1) * LANES], cos_t, sin_a, sin_b))

    lane_q = lax.broadcasted_iota(I32, (tm, LANES), 1)
    for c, r in rope_chunks(off, d_attn):
        rs = r * (HEAD_DIM ** -0.5 * LOG2_E)
        swapped = pltpu.roll(rs, HEAD_DIM, axis=1)
        group = (2 * c) // (N_HEADS // N_KV_HEADS)
        keep = (lane_q < HEAD_DIM) if group == 0 else (lane_q >= HEAD_DIM)
        first, second = (rs, swapped) if group == 0 else (swapped, rs)
        q_ref[2 * c * tm:(2 * c + 1) * tm, :] = jnp.where(keep, first, 0.0).astype(BF16)
        q_ref[(2 * c + 1) * tm:(2 * c + 2) * tm, :] = jnp.where(keep, second, 0.0).astype(BF16)
    off += d_attn
    kv = proj(off, 2 * d_kv)
    r = _rope(kv[:, :d_kv], cos_t, sin_a, sin_b)
    kt_ref[...] = r.T
    kb_ref[...] = r.astype(BF16)
    off += d_kv
    vv = kv[:, d_kv:]
    vt = vv.T
    vt_ref[...] = vt
    vb_ref[...] = vv.astype(BF16)
    sub = lax.broadcasted_iota(I32, (VT_ROWS - HEAD_DIM, tm), 0)
    ones_pad = jnp.where(sub == 0, 1.0, 0.0)
    for g in range(N_KV_HEADS):
        vtb_ref[g] = jnp.concatenate([vt[g * HEAD_DIM:(g + 1) * HEAD_DIM, :], ones_pad], axis=0).astype(BF16)
    off += d_kv
    for c, r in rope_chunks(off, d_qi):
        rb = r.astype(BF16)
        qi_ref[2 * c * tm:(2 * c + 1) * tm, :] = rb[:, :IDX_DIM]
        qi_ref[(2 * c + 1) * tm:(2 * c + 2) * tm, :] = rb[:, IDX_DIM:]
    off += d_qi
    kw = proj(off, LANES)
    lane = lax.broadcasted_iota(I32, kw.shape, 1)
    kr = jnp.where(lane < IDX_DIM, _rope(kw, cos_t, sin_a, sin_b), kw * (N_IDX_HEADS ** -0.5))
    krt = kr.T
    kit_ref[...] = krt[:IDX_DIM, :]
    kib_ref[...] = kr[:, :IDX_DIM].astype(BF16)
    wi_ref[...] = kr[:, IDX_DIM:IDX_DIM + N_IDX_HEADS]
    wit_ref[...] = krt[IDX_DIM:IDX_DIM + N_IDX_HEADS, :]
    off += LANES
    sgs_ref[...] = _sigmoid(proj(off, d_model))
    off += d_model
    sga_ref[...] = _sigmoid(proj(off, d_model))


def _rope_tables(pos, n_rows):
    half = ROT_DIM // 2
    inv = ROPE_THETA ** (-jnp.arange(half, dtype=F32) / half)
    ang = pos.astype(F32)[:, None] * inv[None, :]
    cos, sin = jnp.cos(ang), jnp.sin(ang)
    ones = jnp.ones((n_rows, HEAD_DIM - ROT_DIM), F32)
    zeros = jnp.zeros((n_rows, HEAD_DIM - ROT_DIM), F32)
    zh = jnp.zeros((n_rows, half), F32)
    cos_t = jnp.concatenate([cos, cos, ones], axis=1)
    sin_a = jnp.concatenate([-sin, zh, zeros], axis=1)
    sin_b = jnp.concatenate([zh, sin, zeros], axis=1)
    rep = LANES // HEAD_DIM
    return tuple(jnp.tile(t, (1, rep)) for t in (cos_t, sin_a, sin_b))


def _in_proj(x2d, n_seq, pos_tab, norm_g, w_pack, tm, dims):
    n, d_model = x2d.shape
    d_ssm, d_attn, d_kv, d_qi = dims
    cos_t, sin_a, sin_b = pos_tab
    t_len = n // n_seq
    nt = t_len // tm
    grid = (n // tm,)
    row = lambda w: pl.BlockSpec((tm, w), lambda i: (i, 0))
    tab = pl.BlockSpec((tm, LANES), lambda i: (i % nt, 0))
    feat = lambda r: pl.BlockSpec((None, r, tm), lambda i: (i // nt, 0, i % nt))
    stacked = lambda heads, w: pl.BlockSpec((None, None, heads * tm, w), lambda i: (i // nt, i % nt, 0, 0))
    kern = functools.partial(_in_proj_kernel, d_ssm=d_ssm, d_attn=d_attn, d_kv=d_kv, d_qi=d_qi,
                             d_model=d_model)
    outs = [
        (jax.ShapeDtypeStruct((n, d_ssm), F32), row(d_ssm)),
        (jax.ShapeDtypeStruct((n_seq, nt, N_HEADS * tm, d_kv), BF16), stacked(N_HEADS, d_kv)),
        (jax.ShapeDtypeStruct((n_seq, d_kv, t_len), F32), feat(d_kv)),
        (jax.ShapeDtypeStruct((n, d_kv), BF16), row(d_kv)),
        (jax.ShapeDtypeStruct((n_seq, d_kv, t_len), F32), feat(d_kv)),
        (jax.ShapeDtypeStruct((n, d_kv), BF16), row(d_kv)),
        (jax.ShapeDtypeStruct((n_seq, nt, N_KV_HEADS, VT_ROWS, tm), BF16),
         pl.BlockSpec((None, None, N_KV_HEADS, VT_ROWS, tm), lambda i: (i // nt, i % nt, 0, 0, 0))),
        (jax.ShapeDtypeStruct((n_seq, nt, N_IDX_HEADS * tm, IDX_DIM), BF16), stacked(N_IDX_HEADS, IDX_DIM)),
        (jax.ShapeDtypeStruct((n_seq, IDX_DIM, t_len), F32), feat(IDX_DIM)),
        (jax.ShapeDtypeStruct((n, IDX_DIM), BF16), row(IDX_DIM)),
        (jax.ShapeDtypeStruct((n, N_IDX_HEADS), F32), row(N_IDX_HEADS)),
        (jax.ShapeDtypeStruct((n_seq, N_IDX_HEADS, t_len), F32), feat(N_IDX_HEADS)),
        (jax.ShapeDtypeStruct((n, d_model), F32), row(d_model)),
        (jax.ShapeDtypeStruct((n, d_model), F32), row(d_model)),
    ]
    out_shapes = [o[0] for o in outs]
    out_specs = [o[1] for o in outs]
    return pl.pallas_call(
        kern, grid=grid,
        in_specs=[row(d_model), _const_spec((1, d_model)), _const_spec(w_pack.shape), tab, tab, tab],
        out_specs=out_specs, out_shape=out_shapes,
        compiler_params=pltpu.CompilerParams(dimension_semantics=("arbitrary",),
                                             vmem_limit_bytes=VMEM_LIMIT),
        name="in_proj",
    )(x2d, norm_g.reshape(1, d_model), w_pack, cos_t, sin_a, sin_b)


def _s5_readout(y, u, dsk_ref, wglu_ref, bglu_ref, d_model):
    y = y + dsk_ref[...] * u
    cdf = 0.5 * (1.0 + jnp.tanh(math.sqrt(2.0 / math.pi) * (y + 0.044715 * (y * y * y))))
    gl = (y * cdf).astype(BF16)
    z = jnp.dot(gl, wglu_ref[...], preferred_element_type=F32) + bglu_ref[...]
    return z[:, :d_model] * _sigmoid(z[:, d_model:])


def _s5_prompt_kernel(u_ref, perm_ref, bcat_ref, are_ref, aim_ref, ccre_ref, ccim_ref, dsk_ref, wglu_ref,
                      bglu_ref, out_ref, sre_ref, sim_ref, hre, him, zs, cr, ci, *, tc, d_model):
    @pl.when(pl.program_id(1) == 0)
    def _():
        cr[...] = jnp.zeros_like(cr)
        ci[...] = jnp.zeros_like(ci)

    nseq = u_ref.shape[0]
    d_ssm, n_col = u_ref.shape[2], hre.shape[0]
    sd = n_col * LANES
    u = u_ref[...].reshape(nseq * tc, d_ssm)
    ub = jnp.dot(perm_ref[...], u.astype(BF16), preferred_element_type=F32).astype(BF16)
    for c in range(n_col):
        kc = (c * LANES * d_ssm // sd) // LANES
        bu = jnp.dot(ub[:, kc * LANES:(kc + 1) * LANES], bcat_ref[c], preferred_element_type=F32)
        hre[c] = bu[:, :LANES]
        him[c] = bu[:, LANES:]

    for c0 in range(0, n_col, S5_SCAN_COLS):
        cs = range(c0, c0 + S5_SCAN_COLS)
        ar = [jnp.broadcast_to(are_ref[:, c * LANES:(c + 1) * LANES], (nseq, LANES)) for c in cs]
        ai = [jnp.broadcast_to(aim_ref[:, c * LANES:(c + 1) * LANES], (nseq, LANES)) for c in cs]

        def step(t, carry, cs=cs, ar=ar, ai=ai):
            rows = pl.ds(pl.multiple_of(t * nseq, nseq), nseq)
            out = []
            for n, c in enumerate(cs):
                pr, pi = carry[n]
                xr = hre[c, rows, :] + (ar[n] * pr - ai[n] * pi)
                xi = him[c, rows, :] + (ar[n] * pi + ai[n] * pr)
                hre[c, rows, :] = xr
                him[c, rows, :] = xi
                out.append((xr, xi))
            return tuple(out)

        init = tuple((cr[:, c * LANES:(c + 1) * LANES], ci[:, c * LANES:(c + 1) * LANES]) for c in cs)
        fin = lax.fori_loop(0, tc, step, init, unroll=4)
        for n, c in enumerate(cs):
            cr[:, c * LANES:(c + 1) * LANES] = fin[n][0]
            ci[:, c * LANES:(c + 1) * LANES] = fin[n][1]
    sre_ref[...] = cr[...]
    sim_ref[...] = ci[...]
    n_out = d_ssm // LANES
    per = n_col // n_out
    wide = lambda ref, j: jnp.concatenate([ref[c] for c in range(j * per, (j + 1) * per)], axis=1).astype(BF16)
    for j in range(n_out):
        zs[j] = (jnp.dot(wide(hre, j), ccre_ref[j], preferred_element_type=F32)
                 + jnp.dot(wide(him, j), ccim_ref[j], preferred_element_type=F32))
    y = jnp.concatenate(
        [jnp.concatenate([zs[j, pl.ds(b, tc, stride=nseq), :] for b in range(nseq)], axis=0)
         for j in range(n_out)], axis=1)
    out_ref[...] = _s5_readout(y, u, dsk_ref, wglu_ref, bglu_ref, d_model).reshape(nseq, tc, d_model)


def _s5_sample_kernel(u_ref, h0r_ref, h0i_ref, are_ref, aim_ref, bre_ref, bim_ref, cre_ref, cim_ref,
                      dsk_ref, wglu_ref, bglu_ref, out_ref, sre_ref, sim_ref, *, d_model):
    u = u_ref[...]
    ub = u.astype(BF16)
    ar, ai = are_ref[...], aim_ref[...]
    h0r, h0i = h0r_ref[...], h0i_ref[...]
    hr = (ar * h0r - ai * h0i) + jnp.dot(ub, bre_ref[...], preferred_element_type=F32)
    hi = (ar * h0i + ai * h0r) + jnp.dot(ub, bim_ref[...], preferred_element_type=F32)
    sre_ref[...] = hr
    sim_ref[...] = hi
    y = (jnp.dot(hr.astype(BF16), cre_ref[...], preferred_element_type=F32)
         + jnp.dot(hi.astype(BF16), cim_ref[...], preferred_element_type=F32))
    out_ref[...] = _s5_readout(y, u, dsk_ref, wglu_ref, bglu_ref, d_model)


def _s5_params(a_re, a_im, log_dt, b_re, b_im, c_re, c_im):
    g, p = a_re.shape
    a_c = lax.complex(a_re.astype(F32), a_im.astype(F32))
    dt = jnp.exp(log_dt.astype(F32))[:, None]
    a_bar = jnp.exp(a_c * dt)
    b_bar = ((a_bar - 1.0) / a_c)[:, :, None] * lax.complex(b_re.astype(F32), b_im.astype(F32))
    eye = jnp.eye(g, dtype=F32)
    n = b_re.shape[2]

    def bmat(b):
        return jnp.einsum('gpn,gh->gnhp', b, eye).reshape(g * n, g * p)

    def cmat(c):
        return jnp.einsum('gnp,gh->gphn', c, eye).reshape(g * p, g * n)

    bre, bim = bmat(jnp.real(b_bar)).astype(BF16), bmat(jnp.imag(b_bar)).astype(BF16)
    cre, cim = cmat(c_re.astype(F32)).astype(BF16), cmat(-c_im.astype(F32)).astype(BF16)
    d_in, sd = bre.shape
    assert d_in % LANES == 0 and sd % LANES == 0 and LANES % (LANES * d_in // sd) == 0
    bcat = jnp.stack([
        jnp.concatenate([m[(c * LANES * d_in // sd) // LANES * LANES:][:LANES, c * LANES:(c + 1) * LANES]
                         for m in (bre, bim)], axis=1)
        for c in range(sd // LANES)], axis=0)
    n_out = d_in // LANES
    span = sd // n_out
    ccre, ccim = (jnp.stack([m[j * span:(j + 1) * span, j * LANES:(j + 1) * LANES] for j in range(n_out)],
                            axis=0) for m in (cre, cim))
    a1 = a_bar.reshape(1, g * p)
    return jnp.real(a1), jnp.imag(a1), bre, bim, cre, cim, bcat, ccre, ccim


def _s5_prompt(u3, s5p, dsk, wglu, bglu, tc):
    bsz, t_len, d_ssm = u3.shape
    are, aim, bre, _, _, _, bcat, ccre, ccim = s5p
    sd = bre.shape[1]
    d_model = wglu.shape[1] // 2
    nseq = math.gcd(bsz, S5_SEQS)
    assert sd % (S5_SCAN_COLS * LANES) == 0 and tc % SUBLANES == 0
    kern = functools.partial(_s5_prompt_kernel, tc=tc, d_model=d_model)
    state = pl.BlockSpec((nseq, sd), lambda b, c: (b, 0))
    dst = jnp.arange(nseq * tc)
    perm = (jnp.arange(nseq * tc)[None, :] == ((dst % nseq) * tc + dst // nseq)[:, None]).astype(BF16)
    return pl.pallas_call(
        kern, grid=(bsz // nseq, t_len // tc),
        in_specs=[pl.BlockSpec((nseq, tc, d_ssm), lambda b, c: (b, c, 0)), _const_spec(perm.shape),
                  _const_spec(bcat.shape), _const_spec(are.shape), _const_spec(aim.shape),
                  _const_spec(ccre.shape), _const_spec(ccim.shape), _const_spec((1, d_ssm)),
                  _const_spec(wglu.shape), _const_spec((1, 2 * d_model))],
        out_specs=[pl.BlockSpec((nseq, tc, d_model), lambda b, c: (b, c, 0)), state, state],
        out_shape=[jax.ShapeDtypeStruct((bsz, t_len, d_model), F32),
                   jax.ShapeDtypeStruct((bsz, sd), F32),
                   jax.ShapeDtypeStruct((bsz, sd), F32)],
        scratch_shapes=[pltpu.VMEM((sd // LANES, nseq * tc, LANES), F32),
                        pltpu.VMEM((sd // LANES, nseq * tc, LANES), F32),
                        pltpu.VMEM((d_ssm // LANES, nseq * tc, LANES), F32),
                        pltpu.VMEM((nseq, sd), F32), pltpu.VMEM((nseq, sd), F32)],
        compiler_params=pltpu.CompilerParams(dimension_semantics=("arbitrary", "arbitrary"),
                                             vmem_limit_bytes=VMEM_LIMIT),
        name="s5_prompt",
    )(u3, perm, bcat, are, aim, ccre, ccim, dsk.reshape(1, d_ssm), wglu, bglu.reshape(1, 2 * d_model))


def _s5_sample(u2, h0r, h0i, s5p, dsk, wglu, bglu):
    n, d_ssm = u2.shape
    are, aim, bre, bim, cre, cim = s5p[:6]
    sd = bre.shape[1]
    d_model = wglu.shape[1] // 2
    kern = functools.partial(_s5_sample_kernel, d_model=d_model)
    return pl.pallas_call(
        kern,
        out_shape=[jax.ShapeDtypeStruct((n, d_model), F32),
                   jax.ShapeDtypeStruct((n, sd), F32),
                   jax.ShapeDtypeStruct((n, sd), F32)],
        compiler_params=pltpu.CompilerParams(vmem_limit_bytes=VMEM_LIMIT),
        name="s5_sample",
    )(u2, h0r, h0i, are, aim, bre, bim, cre, cim, dsk.reshape(1, d_ssm), wglu,
      bglu.reshape(1, 2 * d_model))


def _f2k(x):
    b = lax.bitcast_convert_type(x, I32)
    return jnp.where(b < 0, b ^ 0x7FFFFFFF, b)


def _k2f(k):
    return lax.bitcast_convert_type(jnp.where(k < 0, k ^ 0x7FFFFFFF, k), F32)


def _threshold_search(count_ge, amax, forced, k_top):
    kf = float(k_top)
    lo0 = jnp.where(forced, KEY_LOW, _f2k(-amax))
    hi0 = jnp.where(forced, KEY_LOW + 1, _f2k(amax) + 1)

    def cond(st):
        lo, hi, _ = st
        return jnp.max(jnp.where(hi > lo + 1, 1.0, 0.0)) > 0.5

    def body(st):
        lo, hi, it = st
        active = hi > lo + 1
        mid_i = (lo >> 1) + (hi >> 1) + (lo & hi & 1)
        mk = _f2k(0.5 * _k2f(lo) + 0.5 * _k2f(hi))
        use_f = jnp.logical_and(it % 2 == 0, jnp.logical_and(mk > lo, mk < hi))
        mid = jnp.where(use_f, mk, mid_i)
        c = count_ge(_k2f(mid))
        ge = c >= kf
        nlo = jnp.where(ge, mid, lo)
        nhi = jnp.where(c == kf, mid + 1, jnp.where(ge, hi, mid))
        return jnp.where(active, nlo, lo), jnp.where(active, nhi, hi), it + 1

    lo, _, _ = lax.while_loop(cond, body, (lo0, hi0, jnp.int32(0)))
    return _k2f(lo)


def _tri_exclusive(n):
    r = lax.broadcasted_iota(I32, (n, n), 0)
    c = lax.broadcasted_iota(I32, (n, n), 1)
    return (r < c).astype(BF16)


def _pipelined_blocks(n, produce, consume, buf_a, buf_b, carry):
    produce(0, buf_a)

    def pair(t, c):
        produce(2 * t + 1, buf_b)
        c = consume(2 * t, buf_a, c)
        produce(2 * t + 2, buf_a)
        return consume(2 * t + 1, buf_b, c)

    carry = lax.fori_loop(0, n // 2, pair, carry)
    return lax.cond(n % 2 == 1, lambda c: consume(n - 1, buf_a, c), lambda c: c, carry)


def _dsa_prompt_kernel(qis_ref, wt_ref, qs_ref, ki_ref, k_ref, vt_ref, o_ref,
                       sc_ref, scb_ref, m_ref, acc_ref, sa_ref, sb_ref, *, tq, tk, k_top):
    i = pl.program_id(1)
    nkb = i + 1
    hpg = N_HEADS // N_KV_HEADS
    ngr = tk // SUBLANES
    kf = float(k_top)

    w8 = wt_ref[...] * (IDX_DIM ** -0.5)
    qpos = i * tq + lax.broadcasted_iota(I32, (SUBLANES, tq), 1)

    def idx_dots(j, dst):
        k0 = pl.multiple_of(jnp.minimum(j, nkb - 1) * tk, tk)
        dst[...] = lax.dot_general(ki_ref[pl.ds(k0, tk), :], qis_ref[...], NT_DIMS, preferred_element_type=F32)

    def score_blk(j, src, carry, diagonal=False):
        for c in range(tq // LANES):
            acc = jnp.zeros((tk, LANES), F32)
            for h in range(N_IDX_HEADS):
                cols = slice(h * tq + c * LANES, h * tq + (c + 1) * LANES)
                acc = acc + jnp.maximum(src[:, cols], 0.0) * w8[h:h + 1, c * LANES:(c + 1) * LANES]
            if diagonal:
                kpos = j * tk + lax.broadcasted_iota(I32, (tk, LANES), 0)
                causal = kpos <= i * tq + c * LANES + lax.broadcasted_iota(I32, (tk, LANES), 1)
                acc = jnp.where(causal, acc, NEG_INF)
            sc_ref[j, :, c * LANES:(c + 1) * LANES] = acc
            top = lax.bitcast_convert_type(lax.bitcast_convert_type(acc, I32) & jnp.int32(-65536), F32)
            scb_ref[j, :, c * LANES:(c + 1) * LANES] = top.astype(BF16)
        return carry

    _pipelined_blocks(nkb - 1, idx_dots, score_blk, sa_ref, sb_ref, jnp.int32(0))
    idx_dots(nkb - 1, sa_ref)
    score_blk(nkb - 1, sa_ref, 0, diagonal=True)

    rep = lambda x: jnp.broadcast_to(x, (SUBLANES, tq))
    forced = (qpos + 1) <= k_top

    packed = 2 * SUBLANES
    one_b, zero_b = jnp.ones((), BF16), jnp.zeros((), BF16)

    def count16(k16):
        bits = jnp.where(k16 < 0, k16 ^ 0x7FFF, k16) << 16
        tb = jnp.broadcast_to(lax.bitcast_convert_type(bits, F32)[0:1, :], (packed, tq)).astype(BF16)

        def body(j, part):
            blk = scb_ref[j].reshape(tk // packed // COUNT_WAYS, COUNT_WAYS, packed, tq)
            hit = jnp.where(blk >= tb[None, None], one_b, zero_b)
            for g in range(hit.shape[0]):
                part = part + hit[g]
            return part

        part = lax.fori_loop(0, nkb, body, jnp.zeros((COUNT_WAYS, packed, tq), BF16))
        return rep(jnp.sum(jnp.sum(part.astype(F32), axis=0), axis=0, keepdims=True))

    def coarse_pass(lo16, hi16, c_lo):
        active = hi16 > lo16 + 1
        mid = lo16 + ((hi16 - lo16) >> 1)
        c = count16(mid)
        up = jnp.logical_and(active, c >= kf)
        nhi = jnp.where(c == kf, mid, jnp.where(c >= kf, hi16, mid))
        return jnp.where(up, mid, lo16), jnp.where(active, nhi, hi16), jnp.where(up, c, c_lo)

    lo16 = jnp.full((SUBLANES, tq), KEY_LOW >> 16, I32)
    hi16 = jnp.where(forced, KEY_LOW >> 16, KEY_INF >> 16)
    c_lo = (qpos + 1).astype(F32)
    lo16, hi16, c_lo = lax.fori_loop(0, 8, lambda _, st: coarse_pass(*coarse_pass(*st)), (lo16, hi16, c_lo))
    lo0 = lo16 << 16
    hi0 = jnp.where(hi16 == lo16, lo0 + 1, hi16 << 16)

    def count(thr, strict):
        def body(j, part):
            blk = sc_ref[j].reshape(ngr // COUNT_WAYS, COUNT_WAYS, SUBLANES, tq)
            hit = (blk > thr[None, None]) if strict else (blk >= thr[None, None])
            return part + jnp.sum(jnp.where(hit, 1.0, 0.0), axis=0)

        part = lax.fori_loop(0, nkb, body, jnp.zeros((COUNT_WAYS, SUBLANES, tq), F32))
        return rep(jnp.sum(jnp.sum(part, axis=0), axis=0, keepdims=True))

    def search_pass(lo, hi, c_lo, value_space):
        active = hi > lo + 1
        span = hi - lo
        mid = lo + lax.shift_right_logical(span, jnp.ones_like(span))
        if value_space:
            mk = _f2k(0.5 * _k2f(lo) + 0.5 * _k2f(hi))
            mid = jnp.where(jnp.logical_and(mk > lo, mk < hi), mk, mid)
        c = count(_k2f(mid), False)
        up = jnp.logical_and(active, c >= kf)
        nhi = jnp.where(c == kf, mid + 1, jnp.where(c >= kf, hi, mid))
        return jnp.where(up, mid, lo), jnp.where(active, nhi, hi), jnp.where(up, c, c_lo)

    def double_round(lo, hi, c_lo):
        return search_pass(*search_pass(lo, hi, c_lo, True), False)

    def unresolved(lo, hi):
        return (jnp.max(jnp.where(hi > lo + 1, 1.0, 0.0)) > 0.5).astype(I32)

    def search_body(st):
        lo, hi, c_lo = double_round(*st[:3])
        return lo, hi, c_lo, unresolved(lo, hi)

    lo, hi, c_lo = lax.fori_loop(0, FINE_ROUNDS_UNCHECKED, lambda _, s: double_round(*s), (lo0, hi0, c_lo))
    lo, _, c_ge, _ = lax.while_loop(lambda s: s[3] > 0, search_body, (lo, hi, c_lo, unresolved(lo, hi)))
    thr = _k2f(lo)
    thr1 = thr[0:1, :]

    @pl.when(jnp.max(jnp.where(c_ge > kf, 1.0, 0.0)) > 0.5)
    def _():
        need = jnp.where(c_ge > kf, kf - count(thr, True), 1e30)[0:1, :]
        r = lax.broadcasted_iota(I32, (tk, tk), 0)
        c = lax.broadcasted_iota(I32, (tk, tk), 1)
        earlier = (c < r).astype(BF16)
        ones_k = jnp.ones((SUBLANES, tk), BF16)

        def fix(j, run):
            sc = sc_ref[j]
            eq = sc == thr1
            eqb = jnp.where(eq, 1.0, 0.0).astype(BF16)
            before = jnp.dot(earlier, eqb, preferred_element_type=F32) + run[0:1, :]
            sc_ref[j] = jnp.where(jnp.logical_and(eq, before >= need), NEG_INF, sc)
            return run + jnp.dot(ones_k, eqb, preferred_element_type=F32)

        lax.fori_loop(0, nkb, fix, jnp.zeros((SUBLANES, tq), F32))

    m_ref[...] = jnp.full(m_ref.shape, NEG_INF, F32)
    acc_ref[...] = jnp.zeros(acc_ref.shape, F32)

    def logits(j, dst):
        jc = jnp.minimum(j, nkb - 1)
        k0 = pl.multiple_of(jc * tk, tk)
        dst[...] = lax.dot_general(k_ref[pl.ds(k0, tk), :], qs_ref[...], NT_DIMS, preferred_element_type=F32)

    def softmax_pv(j, src):
        jc = j
        bias = jnp.where(sc_ref[j] >= thr1, 0.0, NEG_INF)
        for h in range(N_HEADS):
            x = src[:, h * tq:(h + 1) * tq] + bias
            bm = jnp.max(jnp.max(x.reshape(ngr, SUBLANES, tq), axis=0), axis=0, keepdims=True)
            m_old = m_ref[h:h + 1, :]
            m_new = jnp.maximum(m_old, bm)
            m_safe = jnp.where(m_new == NEG_INF, 0.0, m_new)
            p = jnp.exp2(x - m_safe).astype(BF16)
            pv = jnp.dot(vt_ref[jc, h // hpg], p, preferred_element_type=F32)
            acc_ref[h] = acc_ref[h] * jnp.exp2(m_old - m_safe) + pv
            m_ref[h:h + 1, :] = m_new

    def consume(j, src, carry):
        softmax_pv(j, src)
        return carry

    _pipelined_blocks(nkb, logits, consume, sa_ref, sb_ref, jnp.int32(0))

    for pr in range(N_HEADS // 2):
        outs = []
        for h in (2 * pr, 2 * pr + 1):
            a = acc_ref[h]
            outs.append(a[:HEAD_DIM, :] / a[HEAD_DIM:HEAD_DIM + 1, :])
        o2 = jnp.concatenate(outs, axis=0)
        o_ref[:, 2 * pr * HEAD_DIM:(2 * pr + 2) * HEAD_DIM] = o2.T.astype(o_ref.dtype)


def _dsa_prompt(qis, wt, qs, kib, kb, vt, tq, tk):
    bsz, t_len, d_kv = kb.shape
    d_attn = N_HEADS * HEAD_DIM
    nkb = t_len // tk
    k_top = min(TOPK_MAX, t_len // 4)
    assert tq == tk and tq % LANES == 0 and t_len % tq == 0 and vt.shape[1] == nkb
    assert qis.shape == (bsz, nkb, N_IDX_HEADS * tq, IDX_DIM) and qs.shape == (bsz, nkb, N_HEADS * tq, d_kv)
    assert nkb * (tk // (2 * SUBLANES * COUNT_WAYS)) <= 256
    kern = functools.partial(_dsa_prompt_kernel, tq=tq, tk=tk, k_top=k_top)
    tile = lambda w: pl.BlockSpec((None, tq, w), lambda b, i: (b, i, 0))
    full = lambda w: pl.BlockSpec((None, t_len, w), lambda b, i: (b, 0, 0))
    stacked = lambda a: pl.BlockSpec((None, None) + a.shape[2:], lambda b, i: (b, i, 0, 0))
    return pl.pallas_call(
        kern, grid=(bsz, t_len // tq),
        in_specs=[stacked(qis), pl.BlockSpec((None, N_IDX_HEADS, tq), lambda b, i: (b, 0, i)), stacked(qs),
                  full(IDX_DIM), full(d_kv),
                  pl.BlockSpec((None, nkb, N_KV_HEADS, VT_ROWS, tk), lambda b, i: (b, 0, 0, 0, 0))],
        out_specs=tile(d_attn),
        out_shape=jax.ShapeDtypeStruct((bsz, t_len, d_attn), BF16),
        scratch_shapes=[pltpu.VMEM((nkb, tk, tq), F32),
                        pltpu.VMEM((nkb, tk, tq), BF16),
                        pltpu.VMEM((N_HEADS, tq), F32),
                        pltpu.VMEM((N_HEADS, VT_ROWS, tq), F32),
                        pltpu.VMEM((tk, N_HEADS * tq), F32),
                        pltpu.VMEM((tk, N_HEADS * tq), F32)],
        compiler_params=pltpu.CompilerParams(dimension_semantics=("arbitrary", "arbitrary"),
                                             vmem_limit_bytes=VMEM_LIMIT),
        name="dsa_prompt",
    )(qis, wt, qs, kib, kb, vt)


def _page_fetcher(pt_ref, n_pages, streams):
    def copies(seq, slot):
        for p in range(n_pages):
            pg = pt_ref[seq * n_pages + p]
            for hbm, buf, sem in streams:
                page = hbm.shape[2]
                yield pltpu.make_async_copy(hbm.at[pg], buf.at[slot, :, pl.ds(p * page, page)], sem.at[slot])

    def fetch(seq, slot):
        for cp in copies(seq, slot):
            cp.start()

    def wait(seq, slot):
        for cp in copies(seq, slot):
            cp.wait()

    return fetch, wait


def _prefetch_next_and_wait(fetch, wait):
    b = pl.program_id(0)
    slot = b % 2

    @pl.when(b == 0)
    def _():
        fetch(0, 0)

    @pl.when(b + 1 < pl.num_programs(0))
    def _():
        fetch(b + 1, 1 - slot)

    wait(b, slot)
    return b, slot


def _idx_score_kernel(pt_ref, q_ref, w_ref, ci_ref, o_ref, ibuf, sem, *, n_pages):
    b, slot = _prefetch_next_and_wait(*_page_fetcher(pt_ref, n_pages, [(ci_ref, ibuf, sem)]))
    q = q_ref[...]
    w = w_ref[...] * (IDX_DIM ** -0.5)
    d = jnp.dot(q, ibuf[slot].astype(BF16), preferred_element_type=F32)
    o_ref[pl.ds(b % SUBLANES, 1), :] = jnp.sum(jnp.maximum(d, 0.0) * w, axis=0, keepdims=True)


def _idx_scores(page_table, qis, wcol, cache_idx_kt):
    sq, n_pages = page_table.shape
    page = cache_idx_kt.shape[2]
    past = n_pages * page
    kern = functools.partial(_idx_score_kernel, n_pages=n_pages)
    grid_spec = pltpu.PrefetchScalarGridSpec(
        num_scalar_prefetch=1, grid=(sq,),
        in_specs=[pl.BlockSpec((None, N_IDX_HEADS, IDX_DIM), lambda b, pt: (b, 0, 0)),
                  pl.BlockSpec((None, N_IDX_HEADS, 1), lambda b, pt: (b, 0, 0)),
                  pl.BlockSpec(memory_space=pl.ANY)],
        out_specs=pl.BlockSpec((SUBLANES, past), lambda b, pt: (b // SUBLANES, 0)),
        scratch_shapes=[pltpu.VMEM((2, IDX_DIM, past), cache_idx_kt.dtype), pltpu.SemaphoreType.DMA((2,))])
    return pl.pallas_call(
        kern, grid_spec=grid_spec,
        out_shape=jax.ShapeDtypeStruct((sq, past), F32),
        compiler_params=pltpu.CompilerParams(dimension_semantics=("arbitrary",),
                                             vmem_limit_bytes=VMEM_LIMIT),
        name="sample_idx_scores",
    )(page_table.reshape(-1), qis, wcol, cache_idx_kt)


def _sample_select_kernel(sc_ref, qi_ref, ki_ref, wi_ref, mask_ref, selfsel_ref, *, ch, k_top):
    sq, l_past = sc_ref.shape
    nch = l_past // ch
    w = wi_ref[...] * (IDX_DIM ** -0.5)
    qf = qi_ref[...].astype(F32)
    kf32 = ki_ref[...].astype(F32)
    s_self = jnp.zeros((sq, 1), F32)
    for h in range(N_IDX_HEADS):
        dh = jnp.sum(qf[:, h * IDX_DIM:(h + 1) * IDX_DIM] * kf32, axis=1, keepdims=True)
        s_self = s_self + jnp.maximum(dh, 0.0) * w[:, h:h + 1]

    amax = jnp.abs(s_self)
    for c in range(nch):
        amax = jnp.maximum(amax, jnp.max(jnp.abs(sc_ref[:, c * ch:(c + 1) * ch]), axis=1, keepdims=True))

    def count_cmp(thr, strict):
        thr_b = jnp.broadcast_to(thr, (sq, LANES))
        part = jnp.zeros((sq, LANES), F32)
        for c in range(l_past // LANES):
            blk = sc_ref[:, c * LANES:(c + 1) * LANES]
            hit = (blk > thr_b) if strict else (blk >= thr_b)
            part = part + jnp.where(hit, 1.0, 0.0)
        self_hit = (s_self > thr) if strict else (s_self >= thr)
        return jnp.sum(part, axis=1, keepdims=True) + jnp.where(self_hit, 1.0, 0.0)

    kf = float(k_top)
    forced = amax < 0.0
    thr = _threshold_search(lambda t: count_cmp(t, False), amax, forced, k_top)
    need = kf - count_cmp(thr, True)
    tri = _tri_exclusive(ch)
    run = jnp.zeros((sq, 1), F32)
    for c in range(nch):
        sc = sc_ref[:, c * ch:(c + 1) * ch]
        eq = sc == thr
        before = jnp.dot(jnp.where(eq, 1.0, 0.0).astype(BF16), tri, preferred_element_type=F32) + run
        keep = jnp.logical_or(sc > thr, jnp.logical_and(eq, before < need))
        mask_ref[:, c * ch:(c + 1) * ch] = jnp.where(keep, 1.0, 0.0)
        run = run + jnp.sum(jnp.where(eq, 1.0, 0.0), axis=1, keepdims=True)
    self_keep = jnp.logical_or(s_self > thr, jnp.logical_and(s_self == thr, run < need))
    selfsel_ref[...] = jnp.where(self_keep, 1.0, 0.0)


def _sample_select(scores, qi, kib, wi, ch, k_top):
    sq, l_past = scores.shape
    kern = functools.partial(_sample_select_kernel, ch=ch, k_top=k_top)
    return pl.pallas_call(
        kern,
        out_shape=[jax.ShapeDtypeStruct((sq, l_past), F32), jax.ShapeDtypeStruct((sq, 1), F32)],
        compiler_params=pltpu.CompilerParams(vmem_limit_bytes=VMEM_LIMIT),
        name="sample_select",
    )(scores, qi, kib, wi)


def _sample_attn_kernel(pt_ref, q_ref, mask_ref, ks_ref, vs_ref, ss_ref, ck_ref, cv_ref, o_ref,
                        kbuf, vbuf, ksem, vsem, *, n_pages):
    _, slot = _prefetch_next_and_wait(
        *_page_fetcher(pt_ref, n_pages, [(ck_ref, kbuf, ksem), (cv_ref, vbuf, vsem)]))
    hpg = N_HEADS // N_KV_HEADS
    q = q_ref[...]
    s = jnp.dot(q, kbuf[slot].astype(BF16), preferred_element_type=F32)
    sm = jnp.where(mask_ref[...] > 0.5, s, NEG_INF)
    s_self = jnp.sum(q.astype(F32) * ks_ref[...].astype(F32), axis=1, keepdims=True)
    s_self = jnp.where(ss_ref[...] > 0.5, s_self, NEG_INF)
    m = jnp.maximum(jnp.max(sm, axis=1, keepdims=True), s_self)
    m = jnp.where(m == NEG_INF, 0.0, m)
    pb = jnp.exp2(sm - m).astype(BF16)
    p_self = jnp.exp2(s_self - m).astype(BF16).astype(F32)
    l_sum = jnp.sum(pb.astype(F32), axis=1, keepdims=True) + p_self
    acc = (lax.dot_general(pb, vbuf[slot].astype(BF16), NT_DIMS, preferred_element_type=F32)
           + p_self * vs_ref[...].astype(F32))
    o = acc / l_sum
    hrow = lax.broadcasted_iota(I32, o.shape, 0)
    o = jnp.where(hrow < hpg, o, pltpu.roll(o, HEAD_DIM, axis=1))
    o_ref[...] = o[:, :HEAD_DIM].astype(o_ref.dtype)


def _sample_attn(page_table, qs, mask3, kself, vself, selfsel, cache_k2, cache_v2):
    sq, n_pages = page_table.shape
    d_kv, page = cache_k2.shape[1], cache_k2.shape[2]
    past = n_pages * page
    kern = functools.partial(_sample_attn_kernel, n_pages=n_pages)
    per_seq = lambda r, w: pl.BlockSpec((None, r, w), lambda b, pt: (b, 0, 0))
    hbm = pl.BlockSpec(memory_space=pl.ANY)
    grid_spec = pltpu.PrefetchScalarGridSpec(
        num_scalar_prefetch=1, grid=(sq,),
        in_specs=[per_seq(N_HEADS, d_kv), per_seq(1, past), per_seq(1, d_kv), per_seq(1, d_kv),
                  per_seq(1, 1), hbm, hbm],
        out_specs=per_seq(N_HEADS, HEAD_DIM),
        scratch_shapes=[pltpu.VMEM((2, d_kv, past), cache_k2.dtype), pltpu.VMEM((2, d_kv, past), cache_v2.dtype),
                        pltpu.SemaphoreType.DMA((2,)), pltpu.SemaphoreType.DMA((2,))])
    return pl.pallas_call(
        kern, grid_spec=grid_spec,
        out_shape=jax.ShapeDtypeStruct((sq, N_HEADS, HEAD_DIM), BF16),
        compiler_params=pltpu.CompilerParams(dimension_semantics=("arbitrary",),
                                             vmem_limit_bytes=VMEM_LIMIT),
        name="sample_attn",
    )(page_table.reshape(-1), qs, mask3, kself, vself, selfsel, cache_k2, cache_v2)


def _tail_kernel(x_ref, attn_ref, ssm_ref, sgs_ref, sga_ref, wao_ref, wo_ref, g2_ref, wup_ref, wdn_ref,
                 gf_ref, y_ref):
    attn_out = jnp.dot(attn_ref[...], wao_ref[...], preferred_element_type=F32)
    mix = sgs_ref[...] * ssm_ref[...] + sga_ref[...] * attn_out
    x1 = x_ref[...] + jnp.dot(mix.astype(BF16), wo_ref[...], preferred_element_type=F32)
    hh = _rms_norm(x1, g2_ref[...]).astype(BF16)
    up = jnp.dot(hh, wup_ref[...], preferred_element_type=F32)
    r = jnp.square(jnp.maximum(up, 0.0)).astype(BF16)
    x2 = x1 + jnp.dot(r, wdn_ref[...], preferred_element_type=F32)
    y_ref[...] = _rms_norm(x2, gf_ref[...])


def _tail(x2d, attn, ssm, sgs, sga, wao, wo, g2, wup, wdn, gf, tm):
    n, d_model = x2d.shape
    row = lambda w: pl.BlockSpec((tm, w), lambda i: (i, 0))
    return pl.pallas_call(
        _tail_kernel, grid=(n // tm,),
        in_specs=[row(d_model), row(attn.shape[1]), row(d_model), row(d_model), row(d_model),
                  _const_spec(wao.shape), _const_spec(wo.shape), _const_spec((1, d_model)),
                  _const_spec(wup.shape), _const_spec(wdn.shape), _const_spec((1, d_model))],
        out_specs=row(d_model),
        out_shape=jax.ShapeDtypeStruct((n, d_model), F32),
        compiler_params=pltpu.CompilerParams(dimension_semantics=("arbitrary",),
                                             vmem_limit_bytes=VMEM_LIMIT),
        name="tail",
    )(x2d, attn, ssm, sgs, sga, wao, wo, g2.reshape(1, d_model), wup, wdn, gf.reshape(1, d_model))


def _tiles(n_rows, t_len):
    tm = min(256, n_rows)
    tc = min(128, t_len)
    tq = min(256, t_len)
    return tm, tc, tq, tq


def kernel(x_prompt, x_sample, cache_k, cache_v, cache_idx_k, state_ssm_re, state_ssm_im, page_table,
           norm1_g, w_in, ssm_a_re, ssm_a_im, ssm_log_dt, ssm_b_re, ssm_b_im, ssm_c_re, ssm_c_im,
           ssm_d, w_glu, b_glu, w_attn_out, w_o, norm2_g, w_up, w_down, normf_g):
    bsz, t_len, d_model = x_prompt.shape
    sq, s_len, _ = x_sample.shape
    assert s_len == 1, "the sample path handles one new token per sequence"
    n_pool, page = cache_idx_k.shape[0], cache_idx_k.shape[1]
    n_pages = page_table.shape[1]
    past = n_pages * page
    d_ssm = ssm_d.shape[0]
    d_attn = N_HEADS * HEAD_DIM
    d_kv = N_KV_HEADS * HEAD_DIM
    d_qi = N_IDX_HEADS * IDX_DIM
    dims = (d_ssm, d_attn, d_kv, d_qi)

    c_ki = d_ssm + d_attn + 2 * d_kv + d_qi
    c_g = c_ki + IDX_DIM + N_IDX_HEADS
    w_pack = jnp.concatenate(
        [w_in[:, :c_g], jnp.zeros((d_model, LANES - IDX_DIM - N_IDX_HEADS), w_in.dtype), w_in[:, c_g:]],
        axis=1).astype(BF16)
    wglu_b, wao_b, wo_b = w_glu.astype(BF16), w_attn_out.astype(BF16), w_o.astype(BF16)
    wup_b, wdn_b = w_up.astype(BF16), w_down.astype(BF16)
    s5p = _s5_params(ssm_a_re, ssm_a_im, ssm_log_dt, ssm_b_re, ssm_b_im, ssm_c_re, ssm_c_im)

    n_p = bsz * t_len
    tm, tc, tq, tk = _tiles(n_p, t_len)
    xp = x_prompt.reshape(n_p, d_model)
    tabs_p = _rope_tables(jnp.arange(t_len, dtype=I32), t_len)
    assert tm == tk
    (u, q, kt, kb, vt, _, vtb, qi, kit, kib, _, wit, sgs, sga) = _in_proj(
        xp, bsz, tabs_p, norm1_g, w_pack, tm, dims)
    ssm_out, re_p, im_p = _s5_prompt(u.reshape(bsz, t_len, d_ssm), s5p, ssm_d, wglu_b, b_glu, tc)
    r3 = lambda a: a.reshape(bsz, t_len, a.shape[-1])
    attn = _dsa_prompt(qi, wit, q, r3(kib), r3(kb), vtb, tq, tk)
    y_p = _tail(xp, attn.reshape(n_p, d_attn), ssm_out.reshape(n_p, d_model), sgs, sga,
                wao_b, wo_b, norm2_g, wup_b, wdn_b, normf_g, tm)

    xs = x_sample.reshape(sq, d_model)
    tabs_s = _rope_tables(jnp.full((sq,), past, I32), sq)
    (u_s, q_s, kt_s, kb_s, vt_s, vb_s, _, qi_s, kit_s, kib_s, wi_s, _, sgs_s, sga_s) = _in_proj(
        xs, 1, tabs_s, norm1_g, w_pack, sq, dims)
    sd = state_ssm_re.shape[1] * state_ssm_re.shape[2]
    ssm_s, re_s, im_s = _s5_sample(u_s, state_ssm_re.reshape(sq, sd), state_ssm_im.reshape(sq, sd),
                                   s5p, ssm_d, wglu_b, b_glu)
    per_seq = lambda a, heads: jnp.swapaxes(a.reshape(heads, sq, a.shape[-1]), 0, 1)
    qih_s = per_seq(qi_s, N_IDX_HEADS)
    scores = _idx_scores(page_table, qih_s, wi_s.reshape(sq, N_IDX_HEADS, 1),
                         jnp.transpose(cache_idx_k, (0, 2, 1)))
    k_top_s = min(TOPK_MAX, (past + s_len) // 4)
    mask, selfsel = _sample_select(scores, qih_s.reshape(sq, d_qi), kib_s, wi_s, min(512, past), k_top_s)
    qs_pad = per_seq(q_s, N_HEADS)
    feat_major = lambda c: jnp.transpose(c, (0, 2, 3, 1)).reshape(n_pool, d_kv, page)
    attn_s = _sample_attn(page_table, qs_pad, mask.reshape(sq, 1, past),
                          kb_s.reshape(sq, 1, d_kv), vb_s.reshape(sq, 1, d_kv), selfsel.reshape(sq, 1, 1),
                          feat_major(cache_k), feat_major(cache_v))
    y_s = _tail(xs, attn_s.reshape(sq, d_attn), ssm_s, sgs_s, sga_s,
                wao_b, wo_b, norm2_g, wup_b, wdn_b, normf_g, sq)

    g_ssm, p_ssm = state_ssm_re.shape[1], state_ssm_re.shape[2]
    kv_out = lambda a: jnp.transpose(a.reshape(a.shape[0], N_KV_HEADS, HEAD_DIM, a.shape[2]), (0, 3, 1, 2))
    return (y_p.reshape(bsz, t_len, d_model), y_s.reshape(sq, s_len, d_model),
            kv_out(kt), kv_out(vt), jnp.transpose(kit, (0, 2, 1)),
            re_p.reshape(bsz, g_ssm, p_ssm), im_p.reshape(bsz, g_ssm, p_ssm),
            kv_out(kt_s).reshape(sq, s_len, N_KV_HEADS, HEAD_DIM),
            kv_out(vt_s).reshape(sq, s_len, N_KV_HEADS, HEAD_DIM),
            jnp.transpose(kit_s, (0, 2, 1)).reshape(sq, s_len, IDX_DIM),
            re_s.reshape(sq, g_ssm, p_ssm), im_s.reshape(sq, g_ssm, p_ssm))
```

```python
import functools
import math

import jax
import jax.numpy as jnp
from jax import lax
from jax.experimental import pallas as pl
from jax.experimental.pallas import tpu as pltpu

F32 = jnp.float32
BF16 = jnp.bfloat16
I32 = jnp.int32

N_HEADS = 8
N_KV_HEADS = 2
HEAD_DIM = 64
ROT_DIM = HEAD_DIM // 4
N_IDX_HEADS = 8
IDX_DIM = 64
IDX_ROT_DIM = IDX_DIM // 4
assert (IDX_DIM, IDX_ROT_DIM) == (HEAD_DIM, ROT_DIM)
ROPE_THETA = 500000.0
TOPK_MAX = 256
EPS = 1e-6
LOG2_E = math.log2(math.e)

LANES = 128
SUBLANES = 8
MXU_COLS = 256
S5_SEQS = 8
S5_SCAN_COLS = 4
COUNT_WAYS = 4
FINE_ROUNDS_UNCHECKED = 3
VT_ROWS = 80
VMEM_LIMIT = 56 * 1024 * 1024

NEG_INF = float("-inf")
KEY_LOW = -2139095040
KEY_INF = 0x7F800000
NT_DIMS = (((1,), (1,)), ((), ()))


def _const_spec(shape):
    nd = len(shape)
    return pl.BlockSpec(shape, lambda *_: (0,) * nd, pipeline_mode=pl.Buffered(1))


def _rms_norm(x, g):
    ms = jnp.mean(x * x, axis=-1, keepdims=True)
    return x * lax.rsqrt(ms + EPS) * g


def _sigmoid(x):
    return 1.0 / (1.0 + jnp.exp(-x))


def _rope(x, cos_t, sin_a, sin_b):
    return (x * cos_t + pltpu.roll(x, LANES - ROT_DIM // 2, axis=1) * sin_a
            + pltpu.roll(x, ROT_DIM // 2, axis=1) * sin_b)


def _in_proj_kernel(x_ref, g_ref, w_ref, cos_ref, sa_ref, sb_ref,
                    u_ref, q_ref, kt_ref, kb_ref, vt_ref, vb_ref, vtb_ref, qi_ref, kit_ref, kib_ref,
                    wi_ref, wit_ref, sgs_ref, sga_ref, *, d_ssm, d_attn, d_kv, d_qi, d_model):
    assert d_kv == LANES
    tm = x_ref.shape[0]
    h = _rms_norm(x_ref[...], g_ref[...]).astype(BF16)
    cos_t, sin_a, sin_b = cos_ref[...], sa_ref[...], sb_ref[...]

    def proj(c0, width):
        return jnp.dot(h, w_ref[:, c0:c0 + width], preferred_element_type=F32)

    off = 0
    u_ref[...] = proj(off, d_ssm)
    off += d_ssm
    def rope_chunks(c0, width):
        for m in range(width // MXU_COLS):
            wide = proj(c0 + m * MXU_COLS, MXU_COLS)
            for c in range(MXU_COLS // LANES):
                yield (m * MXU_COLS // LANES + c,
                       _rope(wide[:, c * LANES:(c + 1) * LANES], cos_t, sin_a, sin_b))

    lane_q = lax.broadcasted_iota(I32, (tm, LANES), 1)
    for c, r in rope_chunks(off, d_attn):
        rs = r * (HEAD_DIM ** -0.5 * LOG2_E)
        swapped = pltpu.roll(rs, HEAD_DIM, axis=1)
        group = (2 * c) // (N_HEADS // N_KV_HEADS)
        keep = (lane_q < HEAD_DIM) if group == 0 else (lane_q >= HEAD_DIM)
        first, second = (rs, swapped) if group == 0 else (swapped, rs)
        q_ref[2 * c * tm:(2 * c + 1) * tm, :] = jnp.where(keep, first, 0.0).astype(BF16)
        q_ref[(2 * c + 1) * tm:(2 * c + 2) * tm, :] = jnp.where(keep, second, 0.0).astype(BF16)
    off += d_attn
    kv = proj(off, 2 * d_kv)
    r = _rope(kv[:, :d_kv], cos_t, sin_a, sin_b)
    kt_ref[...] = r.T
    kb_ref[...] = r.astype(BF16)
    off += d_kv
    vv = kv[:, d_kv:]
    vt = vv.T
    vt_ref[...] = vt
    vb_ref[...] = vv.astype(BF16)
    sub = lax.broadcasted_iota(I32, (VT_ROWS - HEAD_DIM, tm), 0)
    ones_pad = jnp.where(sub == 0, 1.0, 0.0)
    for g in range(N_KV_HEADS):
        vtb_ref[g] = jnp.concatenate([vt[g * HEAD_DIM:(g + 1) * HEAD_DIM, :], ones_pad], axis=0).astype(BF16)
    off += d_kv
    for c, r in rope_chunks(off, d_qi):
        rb = r.astype(BF16)
        qi_ref[2 * c * tm:(2 * c + 1) * tm, :] = rb[:, :IDX_DIM]
        qi_ref[(2 * c + 1) * tm:(2 * c + 2) * tm, :] = rb[:, IDX_DIM:]
    off += d_qi
    kw = proj(off, LANES)
    lane = lax.broadcasted_iota(I32, kw.shape, 1)
    kr = jnp.where(lane < IDX_DIM, _rope(kw, cos_t, sin_a, sin_b), kw * (N_IDX_HEADS ** -0.5))
    krt = kr.T
    kit_ref[...] = krt[:IDX_DIM, :]
    kib_ref[...] = kr[:, :IDX_DIM].astype(BF16)
    wi_ref[...] = kr[:, IDX_DIM:IDX_DIM + N_IDX_HEADS]
    wit_ref[...] = krt[IDX_DIM:IDX_DIM + N_IDX_HEADS, :]
    off += LANES
    sgs_ref[...] = _sigmoid(proj(off, d_model))
    off += d_model
    sga_ref[...] = _sigmoid(proj(off, d_model))


def _rope_tables(pos, n_rows):
    half = ROT_DIM // 2
    inv = ROPE_THETA ** (-jnp.arange(half, dtype=F32) / half)
    ang = pos.astype(F32)[:, None] * inv[None, :]
    cos, sin = jnp.cos(ang), jnp.sin(ang)
    ones = jnp.ones((n_rows, HEAD_DIM - ROT_DIM), F32)
    zeros = jnp.zeros((n_rows, HEAD_DIM - ROT_DIM), F32)
    zh = jnp.zeros((n_rows, half), F32)
    cos_t = jnp.concatenate([cos, cos, ones], axis=1)
    sin_a = jnp.concatenate([-sin, zh, zeros], axis=1)
    sin_b = jnp.concatenate([zh, sin, zeros], axis=1)
    rep = LANES // HEAD_DIM
    return tuple(jnp.tile(t, (1, rep)) for t in (cos_t, sin_a, sin_b))


def _in_proj(x2d, n_seq, pos_tab, norm_g, w_pack, tm, dims):
    n, d_model = x2d.shape
    d_ssm, d_attn, d_kv, d_qi = dims
    cos_t, sin_a, sin_b = pos_tab
    t_len = n // n_seq
    nt = t_len // tm
    grid = (n // tm,)
    row = lambda w: pl.BlockSpec((tm, w), lambda i: (i, 0))
    tab = pl.BlockSpec((tm, LANES), lambda i: (i % nt, 0))
    feat = lambda r: pl.BlockSpec((None, r, tm), lambda i: (i // nt, 0, i % nt))
    stacked = lambda heads, w: pl.BlockSpec((None, None, heads * tm, w), lambda i: (i // nt, i % nt, 0, 0))
    kern = functools.partial(_in_proj_kernel, d_ssm=d_ssm, d_attn=d_attn, d_kv=d_kv, d_qi=d_qi,
                             d_model=d_model)
    outs = [
        (jax.ShapeDtypeStruct((n, d_ssm), F32), row(d_ssm)),
        (jax.ShapeDtypeStruct((n_seq, nt, N_HEADS * tm, d_kv), BF16), stacked(N_HEADS, d_kv)),
        (jax.ShapeDtypeStruct((n_seq, d_kv, t_len), F32), feat(d_kv)),
        (jax.ShapeDtypeStruct((n, d_kv), BF16), row(d_kv)),
        (jax.ShapeDtypeStruct((n_seq, d_kv, t_len), F32), feat(d_kv)),
        (jax.ShapeDtypeStruct((n, d_kv), BF16), row(d_kv)),
        (jax.ShapeDtypeStruct((n_seq, nt, N_KV_HEADS, VT_ROWS, tm), BF16),
         pl.BlockSpec((None, None, N_KV_HEADS, VT_ROWS, tm), lambda i: (i // nt, i % nt, 0, 0, 0))),
        (jax.ShapeDtypeStruct((n_seq, nt, N_IDX_HEADS * tm, IDX_DIM), BF16), stacked(N_IDX_HEADS, IDX_DIM)),
        (jax.ShapeDtypeStruct((n_seq, IDX_DIM, t_len), F32), feat(IDX_DIM)),
        (jax.ShapeDtypeStruct((n, IDX_DIM), BF16), row(IDX_DIM)),
        (jax.ShapeDtypeStruct((n, N_IDX_HEADS), F32), row(N_IDX_HEADS)),
        (jax.ShapeDtypeStruct((n_seq, N_IDX_HEADS, t_len), F32), feat(N_IDX_HEADS)),
        (jax.ShapeDtypeStruct((n, d_model), F32), row(d_model)),
        (jax.ShapeDtypeStruct((n, d_model), F32), row(d_model)),
    ]
    out_shapes = [o[0] for o in outs]
    out_specs = [o[1] for o in outs]
    return pl.pallas_call(
        kern, grid=grid,
        in_specs=[row(d_model), _const_spec((1, d_model)), _const_spec(w_pack.shape), tab, tab, tab],
        out_specs=out_specs, out_shape=out_shapes,
        compiler_params=pltpu.CompilerParams(dimension_semantics=("arbitrary",),
                                             vmem_limit_bytes=VMEM_LIMIT),
        name="in_proj",
    )(x2d, norm_g.reshape(1, d_model), w_pack, cos_t, sin_a, sin_b)


def _s5_readout(y, u, dsk_ref, wglu_ref, bglu_ref, d_model):
    y = y + dsk_ref[...] * u
    cdf = 0.5 * (1.0 + jnp.tanh(math.sqrt(2.0 / math.pi) * (y + 0.044715 * (y * y * y))))
    gl = (y * cdf).astype(BF16)
    z = jnp.dot(gl, wglu_ref[...], preferred_element_type=F32) + bglu_ref[...]
    return z[:, :d_model] * _sigmoid(z[:, d_model:])


def _s5_prompt_kernel(u_ref, perm_ref, bcat_ref, are_ref, aim_ref, ccre_ref, ccim_ref, dsk_ref, wglu_ref,
                      bglu_ref, out_ref, sre_ref, sim_ref, hre, him, zs, cr, ci, *, tc, d_model):
    @pl.when(pl.program_id(1) == 0)
    def _():
        cr[...] = jnp.zeros_like(cr)
        ci[...] = jnp.zeros_like(ci)

    nseq = u_ref.shape[0]
    d_ssm, n_col = u_ref.shape[2], hre.shape[0]
    sd = n_col * LANES
    u = u_ref[...].reshape(nseq * tc, d_ssm)
    ub = jnp.dot(perm_ref[...], u.astype(BF16), preferred_element_type=F32).astype(BF16)
    for c in range(n_col):
        kc = (c * LANES * d_ssm // sd) // LANES
        bu = jnp.dot(ub[:, kc * LANES:(kc + 1) * LANES], bcat_ref[c], preferred_element_type=F32)
        hre[c] = bu[:, :LANES]
        him[c] = bu[:, LANES:]

    for c0 in range(0, n_col, S5_SCAN_COLS):
        cs = range(c0, c0 + S5_SCAN_COLS)
        ar = [jnp.broadcast_to(are_ref[:, c * LANES:(c + 1) * LANES], (nseq, LANES)) for c in cs]
        ai = [jnp.broadcast_to(aim_ref[:, c * LANES:(c + 1) * LANES], (nseq, LANES)) for c in cs]

        def step(t, carry, cs=cs, ar=ar, ai=ai):
            rows = pl.ds(pl.multiple_of(t * nseq, nseq), nseq)
            out = []
            for n, c in enumerate(cs):
                pr, pi = carry[n]
                xr = hre[c, rows, :] + (ar[n] * pr - ai[n] * pi)
                xi = him[c, rows, :] + (ar[n] * pi + ai[n] * pr)
                hre[c, rows, :] = xr
                him[c, rows, :] = xi
                out.append((xr, xi))
            return tuple(out)

        init = tuple((cr[:, c * LANES:(c + 1) * LANES], ci[:, c * LANES:(c + 1) * LANES]) for c in cs)
        fin = lax.fori_loop(0, tc, step, init, unroll=4)
        for n, c in enumerate(cs):
            cr[:, c * LANES:(c + 1) * LANES] = fin[n][0]
            ci[:, c * LANES:(c + 1) * LANES] = fin[n][1]
    sre_ref[...] = cr[...]
    sim_ref[...] = ci[...]
    n_out = d_ssm // LANES
    per = n_col // n_out
    wide = lambda ref, j: jnp.concatenate([ref[c] for c in range(j * per, (j + 1) * per)], axis=1).astype(BF16)
    for j in range(n_out):
        zs[j] = (jnp.dot(wide(hre, j), ccre_ref[j], preferred_element_type=F32)
                 + jnp.dot(wide(him, j), ccim_ref[j], preferred_element_type=F32))
    y = jnp.concatenate(
        [jnp.concatenate([zs[j, pl.ds(b, tc, stride=nseq), :] for b in range(nseq)], axis=0)
         for j in range(n_out)], axis=1)
    out_ref[...] = _s5_readout(y, u, dsk_ref, wglu_ref, bglu_ref, d_model).reshape(nseq, tc, d_model)


def _s5_sample_kernel(u_ref, h0r_ref, h0i_ref, are_ref, aim_ref, bre_ref, bim_ref, cre_ref, cim_ref,
                      dsk_ref, wglu_ref, bglu_ref, out_ref, sre_ref, sim_ref, *, d_model):
    u = u_ref[...]
    ub = u.astype(BF16)
    ar, ai = are_ref[...], aim_ref[...]
    h0r, h0i = h0r_ref[...], h0i_ref[...]
    hr = (ar * h0r - ai * h0i) + jnp.dot(ub, bre_ref[...], preferred_element_type=F32)
    hi = (ar * h0i + ai * h0r) + jnp.dot(ub, bim_ref[...], preferred_element_type=F32)
    sre_ref[...] = hr
    sim_ref[...] = hi
    y = (jnp.dot(hr.astype(BF16), cre_ref[...], preferred_element_type=F32)
         + jnp.dot(hi.astype(BF16), cim_ref[...], preferred_element_type=F32))
    out_ref[...] = _s5_readout(y, u, dsk_ref, wglu_ref, bglu_ref, d_model)


def _s5_params(a_re, a_im, log_dt, b_re, b_im, c_re, c_im):
    g, p = a_re.shape
    a_c = lax.complex(a_re.astype(F32), a_im.astype(F32))
    dt = jnp.exp(log_dt.astype(F32))[:, None]
    a_bar = jnp.exp(a_c * dt)
    b_bar = ((a_bar - 1.0) / a_c)[:, :, None] * lax.complex(b_re.astype(F32), b_im.astype(F32))
    eye = jnp.eye(g, dtype=F32)
    n = b_re.shape[2]

    def bmat(b):
        return jnp.einsum('gpn,gh->gnhp', b, eye).reshape(g * n, g * p)

    def cmat(c):
        return jnp.einsum('gnp,gh->gphn', c, eye).reshape(g * p, g * n)

    bre, bim = bmat(jnp.real(b_bar)).astype(BF16), bmat(jnp.imag(b_bar)).astype(BF16)
    cre, cim = cmat(c_re.astype(F32)).astype(BF16), cmat(-c_im.astype(F32)).astype(BF16)
    d_in, sd = bre.shape
    assert d_in % LANES == 0 and sd % LANES == 0 and LANES % (LANES * d_in // sd) == 0
    bcat = jnp.stack([
        jnp.concatenate([m[(c * LANES * d_in // sd) // LANES * LANES:][:LANES, c * LANES:(c + 1) * LANES]
                         for m in (bre, bim)], axis=1)
        for c in range(sd // LANES)], axis=0)
    n_out = d_in // LANES
    span = sd // n_out
    ccre, ccim = (jnp.stack([m[j * span:(j + 1) * span, j * LANES:(j + 1) * LANES] for j in range(n_out)],
                            axis=0) for m in (cre, cim))
    a1 = a_bar.reshape(1, g * p)
    return jnp.real(a1), jnp.imag(a1), bre, bim, cre, cim, bcat, ccre, ccim


def _s5_prompt(u3, s5p, dsk, wglu, bglu, tc):
    bsz, t_len, d_ssm = u3.shape
    are, aim, bre, _, _, _, bcat, ccre, ccim = s5p
    sd = bre.shape[1]
    d_model = wglu.shape[1] // 2
    nseq = math.gcd(bsz, S5_SEQS)
    assert sd % (S5_SCAN_COLS * LANES) == 0 and tc % SUBLANES == 0
    kern = functools.partial(_s5_prompt_kernel, tc=tc, d_model=d_model)
    state = pl.BlockSpec((nseq, sd), lambda b, c: (b, 0))
    dst = jnp.arange(nseq * tc)
    perm = (jnp.arange(nseq * tc)[None, :] == ((dst % nseq) * tc + dst // nseq)[:, None]).astype(BF16)
    return pl.pallas_call(
        kern, grid=(bsz // nseq, t_len // tc),
        in_specs=[pl.BlockSpec((nseq, tc, d_ssm), lambda b, c: (b, c, 0)), _const_spec(perm.shape),
                  _const_spec(bcat.shape), _const_spec(are.shape), _const_spec(aim.shape),
                  _const_spec(ccre.shape), _const_spec(ccim.shape), _const_spec((1, d_ssm)),
                  _const_spec(wglu.shape), _const_spec((1, 2 * d_model))],
        out_specs=[pl.BlockSpec((nseq, tc, d_model), lambda b, c: (b, c, 0)), state, state],
        out_shape=[jax.ShapeDtypeStruct((bsz, t_len, d_model), F32),
                   jax.ShapeDtypeStruct((bsz, sd), F32),
                   jax.ShapeDtypeStruct((bsz, sd), F32)],
        scratch_shapes=[pltpu.VMEM((sd // LANES, nseq * tc, LANES), F32),
                        pltpu.VMEM((sd // LANES, nseq * tc, LANES), F32),
                        pltpu.VMEM((d_ssm // LANES, nseq * tc, LANES), F32),
                        pltpu.VMEM((nseq, sd), F32), pltpu.VMEM((nseq, sd), F32)],
        compiler_params=pltpu.CompilerParams(dimension_semantics=("arbitrary", "arbitrary"),
                                             vmem_limit_bytes=VMEM_LIMIT),
        name="s5_prompt",
    )(u3, perm, bcat, are, aim, ccre, ccim, dsk.reshape(1, d_ssm), wglu, bglu.reshape(1, 2 * d_model))


def _s5_sample(u2, h0r, h0i, s5p, dsk, wglu, bglu):
    n, d_ssm = u2.shape
    are, aim, bre, bim, cre, cim = s5p[:6]
    sd = bre.shape[1]
    d_model = wglu.shape[1] // 2
    kern = functools.partial(_s5_sample_kernel, d_model=d_model)
    return pl.pallas_call(
        kern,
        out_shape=[jax.ShapeDtypeStruct((n, d_model), F32),
                   jax.ShapeDtypeStruct((n, sd), F32),
                   jax.ShapeDtypeStruct((n, sd), F32)],
        compiler_params=pltpu.CompilerParams(vmem_limit_bytes=VMEM_LIMIT),
        name="s5_sample",
    )(u2, h0r, h0i, are, aim, bre, bim, cre, cim, dsk.reshape(1, d_ssm), wglu,
      bglu.reshape(1, 2 * d_model))


def _f2k(x):
    b = lax.bitcast_convert_type(x, I32)
    return jnp.where(b < 0, b ^ 0x7FFFFFFF, b)


def _k2f(k):
    return lax.bitcast_convert_type(jnp.where(k < 0, k ^ 0x7FFFFFFF, k), F32)


def _threshold_search(count_ge, amax, forced, k_top):
    kf = float(k_top)
    lo0 = jnp.where(forced, KEY_LOW, _f2k(-amax))
    hi0 = jnp.where(forced, KEY_LOW + 1, _f2k(amax) + 1)

    def cond(st):
        lo, hi, _ = st
        return jnp.max(jnp.where(hi > lo + 1, 1.0, 0.0)) > 0.5

    def body(st):
        lo, hi, it = st
        active = hi > lo + 1
        mid_i = (lo >> 1) + (hi >> 1) + (lo & hi & 1)
        mk = _f2k(0.5 * _k2f(lo) + 0.5 * _k2f(hi))
        use_f = jnp.logical_and(it % 2 == 0, jnp.logical_and(mk > lo, mk < hi))
        mid = jnp.where(use_f, mk, mid_i)
        c = count_ge(_k2f(mid))
        ge = c >= kf
        nlo = jnp.where(ge, mid, lo)
        nhi = jnp.where(c == kf, mid + 1, jnp.where(ge, hi, mid))
        return jnp.where(active, nlo, lo), jnp.where(active, nhi, hi), it + 1

    lo, _, _ = lax.while_loop(cond, body, (lo0, hi0, jnp.int32(0)))
    return _k2f(lo)


def _tri_exclusive(n):
    r = lax.broadcasted_iota(I32, (n, n), 0)
    c = lax.broadcasted_iota(I32, (n, n), 1)
    return (r < c).astype(BF16)


def _pipelined_blocks(n, produce, consume, buf_a, buf_b, carry):
    produce(0, buf_a)

    def pair(t, c):
        produce(2 * t + 1, buf_b)
        c = consume(2 * t, buf_a, c)
        produce(2 * t + 2, buf_a)
        return consume(2 * t + 1, buf_b, c)

    carry = lax.fori_loop(0, n // 2, pair, carry)
    return lax.cond(n % 2 == 1, lambda c: consume(n - 1, buf_a, c), lambda c: c, carry)


def _dsa_prompt_kernel(qis_ref, wt_ref, qs_ref, ki_ref, k_ref, vt_ref, o_ref,
                       sc_ref, scb_ref, m_ref, acc_ref, sa_ref, sb_ref, *, tq, tk, k_top):
    i = pl.program_id(1)
    nkb = i + 1
    hpg = N_HEADS // N_KV_HEADS
    ngr = tk // SUBLANES
    kf = float(k_top)

    w8 = wt_ref[...] * (IDX_DIM ** -0.5)
    qpos = i * tq + lax.broadcasted_iota(I32, (SUBLANES, tq), 1)

    def idx_dots(j, dst):
        k0 = pl.multiple_of(jnp.minimum(j, nkb - 1) * tk, tk)
        dst[...] = lax.dot_general(ki_ref[pl.ds(k0, tk), :], qis_ref[...], NT_DIMS, preferred_element_type=F32)

    def score_blk(j, src, carry, diagonal=False):
        for c in range(tq // LANES):
            acc = jnp.zeros((tk, LANES), F32)
            for h in range(N_IDX_HEADS):
                cols = slice(h * tq + c * LANES, h * tq + (c + 1) * LANES)
                acc = acc + jnp.maximum(src[:, cols], 0.0) * w8[h:h + 1, c * LANES:(c + 1) * LANES]
            if diagonal:
                kpos = j * tk + lax.broadcasted_iota(I32, (tk, LANES), 0)
                causal = kpos <= i * tq + c * LANES + lax.broadcasted_iota(I32, (tk, LANES), 1)
                acc = jnp.where(causal, acc, NEG_INF)
            sc_ref[j, :, c * LANES:(c + 1) * LANES] = acc
            top = lax.bitcast_convert_type(lax.bitcast_convert_type(acc, I32) & jnp.int32(-65536), F32)
            scb_ref[j, :, c * LANES:(c + 1) * LANES] = top.astype(BF16)
        return carry

    _pipelined_blocks(nkb - 1, idx_dots, score_blk, sa_ref, sb_ref, jnp.int32(0))
    idx_dots(nkb - 1, sa_ref)
    score_blk(nkb - 1, sa_ref, 0, diagonal=True)

    rep = lambda x: jnp.broadcast_to(x, (SUBLANES, tq))
    forced = (qpos + 1) <= k_top

    packed = 2 * SUBLANES
    one_b, zero_b = jnp.ones((), BF16), jnp.zeros((), BF16)

    def count16(k16):
        bits = jnp.where(k16 < 0, k16 ^ 0x7FFF, k16) << 16
        tb = jnp.broadcast_to(lax.bitcast_convert_type(bits, F32)[0:1, :], (packed, tq)).astype(BF16)

        def body(j, part):
            blk = scb_ref[j].reshape(tk // packed // COUNT_WAYS, COUNT_WAYS, packed, tq)
            hit = jnp.where(blk >= tb[None, None], one_b, zero_b)
            for g in range(hit.shape[0]):
                part = part + hit[g]
            return part

        part = lax.fori_loop(0, nkb, body, jnp.zeros((COUNT_WAYS, packed, tq), BF16))
        return rep(jnp.sum(jnp.sum(part.astype(F32), axis=0), axis=0, keepdims=True))

    def coarse_pass(lo16, hi16, c_lo):
        active = hi16 > lo16 + 1
        mid = lo16 + ((hi16 - lo16) >> 1)
        c = count16(mid)
        up = jnp.logical_and(active, c >= kf)
        nhi = jnp.where(c == kf, mid, jnp.where(c >= kf, hi16, mid))
        return jnp.where(up, mid, lo16), jnp.where(active, nhi, hi16), jnp.where(up, c, c_lo)

    lo16 = jnp.full((SUBLANES, tq), KEY_LOW >> 16, I32)
    hi16 = jnp.where(forced, KEY_LOW >> 16, KEY_INF >> 16)
    c_lo = (qpos + 1).astype(F32)
    lo16, hi16, c_lo = lax.fori_loop(0, 8, lambda _, st: coarse_pass(*coarse_pass(*st)), (lo16, hi16, c_lo))
    lo0 = lo16 << 16
    hi0 = jnp.where(hi16 == lo16, lo0 + 1, hi16 << 16)

    def count(thr, strict):
        def body(j, part):
            blk = sc_ref[j].reshape(ngr // COUNT_WAYS, COUNT_WAYS, SUBLANES, tq)
            hit = (blk > thr[None, None]) if strict else (blk >= thr[None, None])
            return part + jnp.sum(jnp.where(hit, 1.0, 0.0), axis=0)

        part = lax.fori_loop(0, nkb, body, jnp.zeros((COUNT_WAYS, SUBLANES, tq), F32))
        return rep(jnp.sum(jnp.sum(part, axis=0), axis=0, keepdims=True))

    def search_pass(lo, hi, c_lo, value_space):
        active = hi > lo + 1
        span = hi - lo
        mid = lo + lax.shift_right_logical(span, jnp.ones_like(span))
        if value_space:
            mk = _f2k(0.5 * _k2f(lo) + 0.5 * _k2f(hi))
            mid = jnp.where(jnp.logical_and(mk > lo, mk < hi), mk, mid)
        c = count(_k2f(mid), False)
        up = jnp.logical_and(active, c >= kf)
        nhi = jnp.where(c == kf, mid + 1, jnp.where(c >= kf, hi, mid))
        return jnp.where(up, mid, lo), jnp.where(active, nhi, hi), jnp.where(up, c, c_lo)

    def double_round(lo, hi, c_lo):
        return search_pass(*search_pass(lo, hi, c_lo, True), False)

    def unresolved(lo, hi):
        return (jnp.max(jnp.where(hi > lo + 1, 1.0, 0.0)) > 0.5).astype(I32)

    def search_body(st):
        lo, hi, c_lo = double_round(*st[:3])
        return lo, hi, c_lo, unresolved(lo, hi)

    lo, hi, c_lo = lax.fori_loop(0, FINE_ROUNDS_UNCHECKED, lambda _, s: double_round(*s), (lo0, hi0, c_lo))
    lo, _, c_ge, _ = lax.while_loop(lambda s: s[3] > 0, search_body, (lo, hi, c_lo, unresolved(lo, hi)))
    thr = _k2f(lo)
    thr1 = thr[0:1, :]

    @pl.when(jnp.max(jnp.where(c_ge > kf, 1.0, 0.0)) > 0.5)
    def _():
        need = jnp.where(c_ge > kf, kf - count(thr, True), 1e30)[0:1, :]
        r = lax.broadcasted_iota(I32, (tk, tk), 0)
        c = lax.broadcasted_iota(I32, (tk, tk), 1)
        earlier = (c < r).astype(BF16)
        ones_k = jnp.ones((SUBLANES, tk), BF16)

        def fix(j, run):
            sc = sc_ref[j]
            eq = sc == thr1
            eqb = jnp.where(eq, 1.0, 0.0).astype(BF16)
            before = jnp.dot(earlier, eqb, preferred_element_type=F32) + run[0:1, :]
            sc_ref[j] = jnp.where(jnp.logical_and(eq, before >= need), NEG_INF, sc)
            return run + jnp.dot(ones_k, eqb, preferred_element_type=F32)

        lax.fori_loop(0, nkb, fix, jnp.zeros((SUBLANES, tq), F32))

    m_ref[...] = jnp.full(m_ref.shape, NEG_INF, F32)
    acc_ref[...] = jnp.zeros(acc_ref.shape, F32)

    def logits(j, dst):
        jc = jnp.minimum(j, nkb - 1)
        k0 = pl.multiple_of(jc * tk, tk)
        dst[...] = lax.dot_general(k_ref[pl.ds(k0, tk), :], qs_ref[...], NT_DIMS, preferred_element_type=F32)

    def softmax_pv(j, src):
        bias = jnp.where(sc_ref[j] >= thr1, 0.0, NEG_INF)
        for h in range(N_HEADS):
            x = src[:, h * tq:(h + 1) * tq] + bias
            bm = jnp.max(jnp.max(x.reshape(ngr, SUBLANES, tq), axis=0), axis=0, keepdims=True)
            m_old = m_ref[h:h + 1, :]
            m_new = jnp.maximum(m_old, bm)
            m_safe = jnp.where(m_new == NEG_INF, 0.0, m_new)
            p = jnp.exp2(x - m_safe).astype(BF16)
            pv = jnp.dot(vt_ref[j, h // hpg], p, preferred_element_type=F32)
            acc_ref[h] = acc_ref[h] * jnp.exp2(m_old - m_safe) + pv
            m_ref[h:h + 1, :] = m_new

    def consume(j, src, carry):
        softmax_pv(j, src)
        return carry

    _pipelined_blocks(nkb, logits, consume, sa_ref, sb_ref, jnp.int32(0))

    for pr in range(N_HEADS // 2):
        outs = []
        for h in (2 * pr, 2 * pr + 1):
            a = acc_ref[h]
            outs.append(a[:HEAD_DIM, :] / a[HEAD_DIM:HEAD_DIM + 1, :])
        o2 = jnp.concatenate(outs, axis=0)
        o_ref[:, 2 * pr * HEAD_DIM:(2 * pr + 2) * HEAD_DIM] = o2.T.astype(o_ref.dtype)


def _dsa_prompt(qis, wt, qs, kib, kb, vt, tq, tk):
    bsz, t_len, d_kv = kb.shape
    d_attn = N_HEADS * HEAD_DIM
    nkb = t_len // tk
    k_top = min(TOPK_MAX, t_len // 4)
    assert tq == tk and tq % LANES == 0 and t_len % tq == 0 and vt.shape[1] == nkb
    assert qis.shape == (bsz, nkb, N_IDX_HEADS * tq, IDX_DIM) and qs.shape == (bsz, nkb, N_HEADS * tq, d_kv)
    assert nkb * (tk // (2 * SUBLANES * COUNT_WAYS)) <= 256
    kern = functools.partial(_dsa_prompt_kernel, tq=tq, tk=tk, k_top=k_top)
    tile = lambda w: pl.BlockSpec((None, tq, w), lambda b, i: (b, i, 0))
    full = lambda w: pl.BlockSpec((None, t_len, w), lambda b, i: (b, 0, 0))
    stacked = lambda a: pl.BlockSpec((None, None) + a.shape[2:], lambda b, i: (b, i, 0, 0))
    return pl.pallas_call(
        kern, grid=(bsz, t_len // tq),
        in_specs=[stacked(qis), pl.BlockSpec((None, N_IDX_HEADS, tq), lambda b, i: (b, 0, i)), stacked(qs),
                  full(IDX_DIM), full(d_kv),
                  pl.BlockSpec((None, nkb, N_KV_HEADS, VT_ROWS, tk), lambda b, i: (b, 0, 0, 0, 0))],
        out_specs=tile(d_attn),
        out_shape=jax.ShapeDtypeStruct((bsz, t_len, d_attn), BF16),
        scratch_shapes=[pltpu.VMEM((nkb, tk, tq), F32),
                        pltpu.VMEM((nkb, tk, tq), BF16),
                        pltpu.VMEM((N_HEADS, tq), F32),
                        pltpu.VMEM((N_HEADS, VT_ROWS, tq), F32),
                        pltpu.VMEM((tk, N_HEADS * tq), F32),
                        pltpu.VMEM((tk, N_HEADS * tq), F32)],
        compiler_params=pltpu.CompilerParams(dimension_semantics=("arbitrary", "arbitrary"),
                                             vmem_limit_bytes=VMEM_LIMIT),
        name="dsa_prompt",
    )(qis, wt, qs, kib, kb, vt)


def _page_fetcher(pt_ref, n_pages, streams):
    def copies(seq, slot):
        for p in range(n_pages):
            pg = pt_ref[seq * n_pages + p]
            for hbm, buf, sem in streams:
                page = hbm.shape[2]
                yield pltpu.make_async_copy(hbm.at[pg], buf.at[slot, :, pl.ds(p * page, page)], sem.at[slot])

    def fetch(seq, slot):
        for cp in copies(seq, slot):
            cp.start()

    def wait(seq, slot):
        for cp in copies(seq, slot):
            cp.wait()

    return fetch, wait


def _prefetch_next_and_wait(fetch, wait):
    b = pl.program_id(0)
    slot = b % 2

    @pl.when(b == 0)
    def _():
        fetch(0, 0)

    @pl.when(b + 1 < pl.num_programs(0))
    def _():
        fetch(b + 1, 1 - slot)

    wait(b, slot)
    return b, slot


def _idx_score_kernel(pt_ref, q_ref, w_ref, ci_ref, o_ref, ibuf, sem, *, n_pages):
    b, slot = _prefetch_next_and_wait(*_page_fetcher(pt_ref, n_pages, [(ci_ref, ibuf, sem)]))
    q = q_ref[...]
    w = w_ref[...] * (IDX_DIM ** -0.5)
    d = jnp.dot(q, ibuf[slot].astype(BF16), preferred_element_type=F32)
    o_ref[pl.ds(b % SUBLANES, 1), :] = jnp.sum(jnp.maximum(d, 0.0) * w, axis=0, keepdims=True)


def _idx_scores(page_table, qis, wcol, cache_idx_kt):
    sq, n_pages = page_table.shape
    page = cache_idx_kt.shape[2]
    past = n_pages * page
    kern = functools.partial(_idx_score_kernel, n_pages=n_pages)
    grid_spec = pltpu.PrefetchScalarGridSpec(
        num_scalar_prefetch=1, grid=(sq,),
        in_specs=[pl.BlockSpec((None, N_IDX_HEADS, IDX_DIM), lambda b, pt: (b, 0, 0)),
                  pl.BlockSpec((None, N_IDX_HEADS, 1), lambda b, pt: (b, 0, 0)),
                  pl.BlockSpec(memory_space=pl.ANY)],
        out_specs=pl.BlockSpec((SUBLANES, past), lambda b, pt: (b // SUBLANES, 0)),
        scratch_shapes=[pltpu.VMEM((2, IDX_DIM, past), cache_idx_kt.dtype), pltpu.SemaphoreType.DMA((2,))])
    return pl.pallas_call(
        kern, grid_spec=grid_spec,
        out_shape=jax.ShapeDtypeStruct((sq, past), F32),
        compiler_params=pltpu.CompilerParams(dimension_semantics=("arbitrary",),
                                             vmem_limit_bytes=VMEM_LIMIT),
        name="sample_idx_scores",
    )(page_table.reshape(-1), qis, wcol, cache_idx_kt)


def _sample_select_kernel(sc_ref, qi_ref, ki_ref, wi_ref, mask_ref, selfsel_ref, *, ch, k_top):
    sq, l_past = sc_ref.shape
    nch = l_past // ch
    w = wi_ref[...] * (IDX_DIM ** -0.5)
    qf = qi_ref[...].astype(F32)
    kf32 = ki_ref[...].astype(F32)
    s_self = jnp.zeros((sq, 1), F32)
    for h in range(N_IDX_HEADS):
        dh = jnp.sum(qf[:, h * IDX_DIM:(h + 1) * IDX_DIM] * kf32, axis=1, keepdims=True)
        s_self = s_self + jnp.maximum(dh, 0.0) * w[:, h:h + 1]

    amax = jnp.abs(s_self)
    for c in range(nch):
        amax = jnp.maximum(amax, jnp.max(jnp.abs(sc_ref[:, c * ch:(c + 1) * ch]), axis=1, keepdims=True))

    def count_cmp(thr, strict):
        thr_b = jnp.broadcast_to(thr, (sq, LANES))
        part = jnp.zeros((sq, LANES), F32)
        for c in range(l_past // LANES):
            blk = sc_ref[:, c * LANES:(c + 1) * LANES]
            hit = (blk > thr_b) if strict else (blk >= thr_b)
            part = part + jnp.where(hit, 1.0, 0.0)
        self_hit = (s_self > thr) if strict else (s_self >= thr)
        return jnp.sum(part, axis=1, keepdims=True) + jnp.where(self_hit, 1.0, 0.0)

    kf = float(k_top)
    forced = amax < 0.0
    thr = _threshold_search(lambda t: count_cmp(t, False), amax, forced, k_top)
    need = kf - count_cmp(thr, True)
    tri = _tri_exclusive(ch)
    run = jnp.zeros((sq, 1), F32)
    for c in range(nch):
        sc = sc_ref[:, c * ch:(c + 1) * ch]
        eq = sc == thr
        before = jnp.dot(jnp.where(eq, 1.0, 0.0).astype(BF16), tri, preferred_element_type=F32) + run
        keep = jnp.logical_or(sc > thr, jnp.logical_and(eq, before < need))
        mask_ref[:, c * ch:(c + 1) * ch] = jnp.where(keep, 1.0, 0.0)
        run = run + jnp.sum(jnp.where(eq, 1.0, 0.0), axis=1, keepdims=True)
    self_keep = jnp.logical_or(s_self > thr, jnp.logical_and(s_self == thr, run < need))
    selfsel_ref[...] = jnp.where(self_keep, 1.0, 0.0)


def _sample_select(scores, qi, kib, wi, ch, k_top):
    sq, l_past = scores.shape
    kern = functools.partial(_sample_select_kernel, ch=ch, k_top=k_top)
    return pl.pallas_call(
        kern,
        out_shape=[jax.ShapeDtypeStruct((sq, l_past), F32), jax.ShapeDtypeStruct((sq, 1), F32)],
        compiler_params=pltpu.CompilerParams(vmem_limit_bytes=VMEM_LIMIT),
        name="sample_select",
    )(scores, qi, kib, wi)


def _sample_attn_kernel(pt_ref, q_ref, mask_ref, ks_ref, vs_ref, ss_ref, ck_ref, cv_ref, o_ref,
                        kbuf, vbuf, ksem, vsem, *, n_pages):
    _, slot = _prefetch_next_and_wait(
        *_page_fetcher(pt_ref, n_pages, [(ck_ref, kbuf, ksem), (cv_ref, vbuf, vsem)]))
    hpg = N_HEADS // N_KV_HEADS
    q = q_ref[...]
    s = jnp.dot(q, kbuf[slot].astype(BF16), preferred_element_type=F32)
    sm = jnp.where(mask_ref[...] > 0.5, s, NEG_INF)
    s_self = jnp.sum(q.astype(F32) * ks_ref[...].astype(F32), axis=1, keepdims=True)
    s_self = jnp.where(ss_ref[...] > 0.5, s_self, NEG_INF)
    m = jnp.maximum(jnp.max(sm, axis=1, keepdims=True), s_self)
    m = jnp.where(m == NEG_INF, 0.0, m)
    pb = jnp.exp2(sm - m).astype(BF16)
    p_self = jnp.exp2(s_self - m).astype(BF16).astype(F32)
    l_sum = jnp.sum(pb.astype(F32), axis=1, keepdims=True) + p_self
    acc = (lax.dot_general(pb, vbuf[slot].astype(BF16), NT_DIMS, preferred_element_type=F32)
           + p_self * vs_ref[...].astype(F32))
    o = acc / l_sum
    hrow = lax.broadcasted_iota(I32, o.shape, 0)
    o = jnp.where(hrow < hpg, o, pltpu.roll(o, HEAD_DIM, axis=1))
    o_ref[...] = o[:, :HEAD_DIM].astype(o_ref.dtype)


def _sample_attn(page_table, qs, mask3, kself, vself, selfsel, cache_k2, cache_v2):
    sq, n_pages = page_table.shape
    d_kv, page = cache_k2.shape[1], cache_k2.shape[2]
    past = n_pages * page
    kern = functools.partial(_sample_attn_kernel, n_pages=n_pages)
    per_seq = lambda r, w: pl.BlockSpec((None, r, w), lambda b, pt: (b, 0, 0))
    hbm = pl.BlockSpec(memory_space=pl.ANY)
    grid_spec = pltpu.PrefetchScalarGridSpec(
        num_scalar_prefetch=1, grid=(sq,),
        in_specs=[per_seq(N_HEADS, d_kv), per_seq(1, past), per_seq(1, d_kv), per_seq(1, d_kv),
                  per_seq(1, 1), hbm, hbm],
        out_specs=per_seq(N_HEADS, HEAD_DIM),
        scratch_shapes=[pltpu.VMEM((2, d_kv, past), cache_k2.dtype), pltpu.VMEM((2, d_kv, past), cache_v2.dtype),
                        pltpu.SemaphoreType.DMA((2,)), pltpu.SemaphoreType.DMA((2,))])
    return pl.pallas_call(
        kern, grid_spec=grid_spec,
        out_shape=jax.ShapeDtypeStruct((sq, N_HEADS, HEAD_DIM), BF16),
        compiler_params=pltpu.CompilerParams(dimension_semantics=("arbitrary",),
                                             vmem_limit_bytes=VMEM_LIMIT),
        name="sample_attn",
    )(page_table.reshape(-1), qs, mask3, kself, vself, selfsel, cache_k2, cache_v2)


def _tail_kernel(x_ref, attn_ref, ssm_ref, sgs_ref, sga_ref, wao_ref, wo_ref, g2_ref, wup_ref, wdn_ref,
                 gf_ref, y_ref):
    attn_out = jnp.dot(attn_ref[...], wao_ref[...], preferred_element_type=F32)
    mix = sgs_ref[...] * ssm_ref[...] + sga_ref[...] * attn_out
    x1 = x_ref[...] + jnp.dot(mix.astype(BF16), wo_ref[...], preferred_element_type=F32)
    hh = _rms_norm(x1, g2_ref[...]).astype(BF16)
    up = jnp.dot(hh, wup_ref[...], preferred_element_type=F32)
    r = jnp.square(jnp.maximum(up, 0.0)).astype(BF16)
    x2 = x1 + jnp.dot(r, wdn_ref[...], preferred_element_type=F32)
    y_ref[...] = _rms_norm(x2, gf_ref[...])


def _tail(x2d, attn, ssm, sgs, sga, wao, wo, g2, wup, wdn, gf, tm):
    n, d_model = x2d.shape
    row = lambda w: pl.BlockSpec((tm, w), lambda i: (i, 0))
    return pl.pallas_call(
        _tail_kernel, grid=(n // tm,),
        in_specs=[row(d_model), row(attn.shape[1]), row(d_model), row(d_model), row(d_model),
                  _const_spec(wao.shape), _const_spec(wo.shape), _const_spec((1, d_model)),
                  _const_spec(wup.shape), _const_spec(wdn.shape), _const_spec((1, d_model))],
        out_specs=row(d_model),
        out_shape=jax.ShapeDtypeStruct((n, d_model), F32),
        compiler_params=pltpu.CompilerParams(dimension_semantics=("arbitrary",),
                                             vmem_limit_bytes=VMEM_LIMIT),
        name="tail",
    )(x2d, attn, ssm, sgs, sga, wao, wo, g2.reshape(1, d_model), wup, wdn, gf.reshape(1, d_model))


def _tiles(n_rows, t_len):
    tm = min(256, n_rows)
    tc = min(128, t_len)
    tq = min(256, t_len)
    return tm, tc, tq, tq


def kernel(x_prompt, x_sample, cache_k, cache_v, cache_idx_k, state_ssm_re, state_ssm_im, page_table,
           norm1_g, w_in, ssm_a_re, ssm_a_im, ssm_log_dt, ssm_b_re, ssm_b_im, ssm_c_re, ssm_c_im,
           ssm_d, w_glu, b_glu, w_attn_out, w_o, norm2_g, w_up, w_down, normf_g):
    bsz, t_len, d_model = x_prompt.shape
    sq, s_len, _ = x_sample.shape
    assert s_len == 1, "the sample path handles one new token per sequence"
    n_pool, page = cache_idx_k.shape[0], cache_idx_k.shape[1]
    n_pages = page_table.shape[1]
    past = n_pages * page
    d_ssm = ssm_d.shape[0]
    d_attn = N_HEADS * HEAD_DIM
    d_kv = N_KV_HEADS * HEAD_DIM
    d_qi = N_IDX_HEADS * IDX_DIM
    dims = (d_ssm, d_attn, d_kv, d_qi)

    c_ki = d_ssm + d_attn + 2 * d_kv + d_qi
    c_g = c_ki + IDX_DIM + N_IDX_HEADS
    w_pack = jnp.concatenate(
        [w_in[:, :c_g], jnp.zeros((d_model, LANES - IDX_DIM - N_IDX_HEADS), w_in.dtype), w_in[:, c_g:]],
        axis=1).astype(BF16)
    wglu_b, wao_b, wo_b = w_glu.astype(BF16), w_attn_out.astype(BF16), w_o.astype(BF16)
    wup_b, wdn_b = w_up.astype(BF16), w_down.astype(BF16)
    s5p = _s5_params(ssm_a_re, ssm_a_im, ssm_log_dt, ssm_b_re, ssm_b_im, ssm_c_re, ssm_c_im)

    n_p = bsz * t_len
    tm, tc, tq, tk = _tiles(n_p, t_len)
    xp = x_prompt.reshape(n_p, d_model)
    tabs_p = _rope_tables(jnp.arange(t_len, dtype=I32), t_len)
    assert tm == tk
    (u, q, kt, kb, vt, _, vtb, qi, kit, kib, _, wit, sgs, sga) = _in_proj(
        xp, bsz, tabs_p, norm1_g, w_pack, tm, dims)
    ssm_out, re_p, im_p = _s5_prompt(u.reshape(bsz, t_len, d_ssm), s5p, ssm_d, wglu_b, b_glu, tc)
    r3 = lambda a: a.reshape(bsz, t_len, a.shape[-1])
    attn = _dsa_prompt(qi, wit, q, r3(kib), r3(kb), vtb, tq, tk)
    y_p = _tail(xp, attn.reshape(n_p, d_attn), ssm_out.reshape(n_p, d_model), sgs, sga,
                wao_b, wo_b, norm2_g, wup_b, wdn_b, normf_g, tm)

    xs = x_sample.reshape(sq, d_model)
    tabs_s = _rope_tables(jnp.full((sq,), past, I32), sq)
    (u_s, q_s, kt_s, kb_s, vt_s, vb_s, _, qi_s, kit_s, kib_s, wi_s, _, sgs_s, sga_s) = _in_proj(
        xs, 1, tabs_s, norm1_g, w_pack, sq, dims)
    sd = state_ssm_re.shape[1] * state_ssm_re.shape[2]
    ssm_s, re_s, im_s = _s5_sample(u_s, state_ssm_re.reshape(sq, sd), state_ssm_im.reshape(sq, sd),
                                   s5p, ssm_d, wglu_b, b_glu)
    per_seq = lambda a, heads: jnp.swapaxes(a.reshape(heads, sq, a.shape[-1]), 0, 1)
    qih_s = per_seq(qi_s, N_IDX_HEADS)
    scores = _idx_scores(page_table, qih_s, wi_s.reshape(sq, N_IDX_HEADS, 1),
                         jnp.transpose(cache_idx_k, (0, 2, 1)))
    k_top_s = min(TOPK_MAX, (past + s_len) // 4)
    mask, selfsel = _sample_select(scores, qih_s.reshape(sq, d_qi), kib_s, wi_s, min(512, past), k_top_s)
    qs_pad = per_seq(q_s, N_HEADS)
    feat_major = lambda c: jnp.transpose(c, (0, 2, 3, 1)).reshape(n_pool, d_kv, page)
    attn_s = _sample_attn(page_table, qs_pad, mask.reshape(sq, 1, past),
                          kb_s.reshape(sq, 1, d_kv), vb_s.reshape(sq, 1, d_kv), selfsel.reshape(sq, 1, 1),
                          feat_major(cache_k), feat_major(cache_v))
    y_s = _tail(xs, attn_s.reshape(sq, d_attn), ssm_s, sgs_s, sga_s,
                wao_b, wo_b, norm2_g, wup_b, wdn_b, normf_g, sq)

    g_ssm, p_ssm = state_ssm_re.shape[1], state_ssm_re.shape[2]
    kv_out = lambda a: jnp.transpose(a.reshape(a.shape[0], N_KV_HEADS, HEAD_DIM, a.shape[2]), (0, 3, 1, 2))
    return (y_p.reshape(bsz, t_len, d_model), y_s.reshape(sq, s_len, d_model),
            kv_out(kt), kv_out(vt), jnp.transpose(kit, (0, 2, 1)),
            re_p.reshape(bsz, g_ssm, p_ssm), im_p.reshape(bsz, g_ssm, p_ssm),
            kv_out(kt_s).reshape(sq, s_len, N_KV_HEADS, HEAD_DIM),
            kv_out(vt_s).reshape(sq, s_len, N_KV_HEADS, HEAD_DIM),
            jnp.transpose(kit_s, (0, 2, 1)).reshape(sq, s_len, IDX_DIM),
            re_s.reshape(sq, g_ssm, p_ssm), im_s.reshape(sq, g_ssm, p_ssm))
```

```python
import functools
import math

import jax
import jax.numpy as jnp
from jax import lax
from jax.experimental import pallas as pl
from jax.experimental.pallas import tpu as pltpu

F32 = jnp.float32
BF16 = jnp.bfloat16
I32 = jnp.int32

N_HEADS = 8
N_KV_HEADS = 2
HEAD_DIM = 64
ROT_DIM = HEAD_DIM // 4
N_IDX_HEADS = 8
IDX_DIM = 64
IDX_ROT_DIM = IDX_DIM // 4
assert (IDX_DIM, IDX_ROT_DIM) == (HEAD_DIM, ROT_DIM)
ROPE_THETA = 500000.0
TOPK_MAX = 256
EPS = 1e-6
LOG2_E = math.log2(math.e)

LANES = 128
SUBLANES = 8
MXU_COLS = 256
S5_SEQS = 8
S5_SCAN_COLS = 4
COUNT_WAYS = 4
FINE_ROUNDS_UNCHECKED = 3
VT_ROWS = 80
VMEM_LIMIT = 56 * 1024 * 1024

NEG_INF = float("-inf")
KEY_LOW = -2139095040
KEY_INF = 0x7F800000
NT_DIMS = (((1,), (1,)), ((), ()))


def _const_spec(shape):
    nd = len(shape)
    return pl.BlockSpec(shape, lambda *_: (0,) * nd, pipeline_mode=pl.Buffered(1))


def _rms_norm(x, g):
    ms = jnp.mean(x * x, axis=-1, keepdims=True)
    return x * lax.rsqrt(ms + EPS) * g


def _sigmoid(x):
    return 1.0 / (1.0 + jnp.exp(-x))


def _rope(x, cos_t, sin_a, sin_b):
    return (x * cos_t + pltpu.roll(x, LANES - ROT_DIM // 2, axis=1) * sin_a
            + pltpu.roll(x, ROT_DIM // 2, axis=1) * sin_b)


def _in_proj_kernel(x_ref, g_ref, w_ref, cos_ref, sa_ref, sb_ref,
                    u_ref, q_ref, kt_ref, kb_ref, vt_ref, vb_ref, vtb_ref, qi_ref, kit_ref, kib_ref,
                    wi_ref, wit_ref, sgs_ref, sga_ref, *, d_ssm, d_attn, d_kv, d_qi, d_model):
    assert d_kv == LANES
    tm = x_ref.shape[0]
    h = _rms_norm(x_ref[...], g_ref[...]).astype(BF16)
    cos_t, sin_a, sin_b = cos_ref[...], sa_ref[...], sb_ref[...]

    def proj(c0, width):
        return jnp.dot(h, w_ref[:, c0:c0 + width], preferred_element_type=F32)

    off = 0
    u_ref[...] = proj(off, d_ssm)
    off += d_ssm
    def rope_chunks(c0, width):
        for m in range(width // MXU_COLS):
            wide = proj(c0 + m * MXU_COLS, MXU_COLS)
            for c in range(MXU_COLS // LANES):
                yield (m * MXU_COLS // LANES + c,
                       _rope(wide[:, c * LANES:(c + 1) * LANES], cos_t, sin_a, sin_b))

    lane_q = lax.broadcasted_iota(I32, (tm, LANES), 1)
    for c, r in rope_chunks(off, d_attn):
        rs = r * (HEAD_DIM ** -0.5 * LOG2_E)
        swapped = pltpu.roll(rs, HEAD_DIM, axis=1)
        group = (2 * c) // (N_HEADS // N_KV_HEADS)
        keep = (lane_q < HEAD_DIM) if group == 0 else (lane_q >= HEAD_DIM)
        first, second = (rs, swapped) if group == 0 else (swapped, rs)
        q_ref[2 * c * tm:(2 * c + 1) * tm, :] = jnp.where(keep, first, 0.0).astype(BF16)
        q_ref[(2 * c + 1) * tm:(2 * c + 2) * tm, :] = jnp.where(keep, second, 0.0).astype(BF16)
    off += d_attn
    kv = proj(off, 2 * d_kv)
    r = _rope(kv[:, :d_kv], cos_t, sin_a, sin_b)
    kt_ref[...] = r.T
    kb_ref[...] = r.astype(BF16)
    off += d_kv
    vv = kv[:, d_kv:]
    vt = vv.T
    vt_ref[...] = vt
    vb_ref[...] = vv.astype(BF16)
    sub = lax.broadcasted_iota(I32, (VT_ROWS - HEAD_DIM, tm), 0)
    ones_pad = jnp.where(sub == 0, 1.0, 0.0)
    for g in range(N_KV_HEADS):
        vtb_ref[g] = jnp.concatenate([vt[g * HEAD_DIM:(g + 1) * HEAD_DIM, :], ones_pad], axis=0).astype(BF16)
    off += d_kv
    for c, r in rope_chunks(off, d_qi):
        rb = r.astype(BF16)
        qi_ref[2 * c * tm:(2 * c + 1) * tm, :] = rb[:, :IDX_DIM]
        qi_ref[(2 * c + 1) * tm:(2 * c + 2) * tm, :] = rb[:, IDX_DIM:]
    off += d_qi
    kw = proj(off, LANES)
    lane = lax.broadcasted_iota(I32, kw.shape, 1)
    kr = jnp.where(lane < IDX_DIM, _rope(kw, cos_t, sin_a, sin_b), kw * (N_IDX_HEADS ** -0.5))
    krt = kr.T
    kit_ref[...] = krt[:IDX_DIM, :]
    kib_ref[...] = kr[:, :IDX_DIM].astype(BF16)
    wi_ref[...] = kr[:, IDX_DIM:IDX_DIM + N_IDX_HEADS]
    wit_ref[...] = krt[IDX_DIM:IDX_DIM + N_IDX_HEADS, :]
    off += LANES
    sgs_ref[...] = _sigmoid(proj(off, d_model))
    off += d_model
    sga_ref[...] = _sigmoid(proj(off, d_model))


def _rope_tables(pos, n_rows):
    half = ROT_DIM // 2
    inv = ROPE_THETA ** (-jnp.arange(half, dtype=F32) / half)
    ang = pos.astype(F32)[:, None] * inv[None, :]
    cos, sin = jnp.cos(ang), jnp.sin(ang)
    ones = jnp.ones((n_rows, HEAD_DIM - ROT_DIM), F32)
    zeros = jnp.zeros((n_rows, HEAD_DIM - ROT_DIM), F32)
    zh = jnp.zeros((n_rows, half), F32)
    cos_t = jnp.concatenate([cos, cos, ones], axis=1)
    sin_a = jnp.concatenate([-sin, zh, zeros], axis=1)
    sin_b = jnp.concatenate([zh, sin, zeros], axis=1)
    rep = LANES // HEAD_DIM
    return tuple(jnp.tile(t, (1, rep)) for t in (cos_t, sin_a, sin_b))


def _in_proj(x2d, n_seq, pos_tab, norm_g, w_pack, tm, dims):
    n, d_model = x2d.shape
    d_ssm, d_attn, d_kv, d_qi = dims
    cos_t, sin_a, sin_b = pos_tab
    t_len = n // n_seq
    nt = t_len // tm
    grid = (n // tm,)
    row = lambda w: pl.BlockSpec((tm, w), lambda i: (i, 0))
    tab = pl.BlockSpec((tm, LANES), lambda i: (i % nt, 0))
    feat = lambda r: pl.BlockSpec((None, r, tm), lambda i: (i // nt, 0, i % nt))
    stacked = lambda heads, w: pl.BlockSpec((None, None, heads * tm, w), lambda i: (i // nt, i % nt, 0, 0))
    kern = functools.partial(_in_proj_kernel, d_ssm=d_ssm, d_attn=d_attn, d_kv=d_kv, d_qi=d_qi,
                             d_model=d_model)
    outs = [
        (jax.ShapeDtypeStruct((n, d_ssm), F32), row(d_ssm)),
        (jax.ShapeDtypeStruct((n_seq, nt, N_HEADS * tm, d_kv), BF16), stacked(N_HEADS, d_kv)),
        (jax.ShapeDtypeStruct((n_seq, d_kv, t_len), F32), feat(d_kv)),
        (jax.ShapeDtypeStruct((n, d_kv), BF16), row(d_kv)),
        (jax.ShapeDtypeStruct((n_seq, d_kv, t_len), F32), feat(d_kv)),
        (jax.ShapeDtypeStruct((n, d_kv), BF16), row(d_kv)),
        (jax.ShapeDtypeStruct((n_seq, nt, N_KV_HEADS, VT_ROWS, tm), BF16),
         pl.BlockSpec((None, None, N_KV_HEADS, VT_ROWS, tm), lambda i: (i // nt, i % nt, 0, 0, 0))),
        (jax.ShapeDtypeStruct((n_seq, nt, N_IDX_HEADS * tm, IDX_DIM), BF16), stacked(N_IDX_HEADS, IDX_DIM)),
        (jax.ShapeDtypeStruct((n_seq, IDX_DIM, t_len), F32), feat(IDX_DIM)),
        (jax.ShapeDtypeStruct((n, IDX_DIM), BF16), row(IDX_DIM)),
        (jax.ShapeDtypeStruct((n, N_IDX_HEADS), F32), row(N_IDX_HEADS)),
        (jax.ShapeDtypeStruct((n_seq, N_IDX_HEADS, t_len), F32), feat(N_IDX_HEADS)),
        (jax.ShapeDtypeStruct((n, d_model), F32), row(d_model)),
        (jax.ShapeDtypeStruct((n, d_model), F32), row(d_model)),
    ]
    out_shapes = [o[0] for o in outs]
    out_specs = [o[1] for o in outs]
    return pl.pallas_call(
        kern, grid=grid,
        in_specs=[row(d_model), _const_spec((1, d_model)), _const_spec(w_pack.shape), tab, tab, tab],
        out_specs=out_specs, out_shape=out_shapes,
        compiler_params=pltpu.CompilerParams(dimension_semantics=("arbitrary",),
                                             vmem_limit_bytes=VMEM_LIMIT),
        name="in_proj",
    )(x2d, norm_g.reshape(1, d_model), w_pack, cos_t, sin_a, sin_b)


def _s5_readout(y, u, dsk_ref, wglu_ref, bglu_ref, d_model):
    y = y + dsk_ref[...] * u
    cdf = 0.5 * (1.0 + jnp.tanh(math.sqrt(2.0 / math.pi) * (y + 0.044715 * (y * y * y))))
    gl = (y * cdf).astype(BF16)
    z = jnp.dot(gl, wglu_ref[...], preferred_element_type=F32) + bglu_ref[...]
    return z[:, :d_model] * _sigmoid(z[:, d_model:])


def _s5_prompt_kernel(u_ref, perm_ref, bcat_ref, are_ref, aim_ref, ccre_ref, ccim_ref, dsk_ref, wglu_ref,
                      bglu_ref, out_ref, sre_ref, sim_ref, hre, him, zs, cr, ci, *, tc, d_model):
    @pl.when(pl.program_id(1) == 0)
    def _():
        cr[...] = jnp.zeros_like(cr)
        ci[...] = jnp.zeros_like(ci)

    nseq = u_ref.shape[0]
    d_ssm, n_col = u_ref.shape[2], hre.shape[0]
    sd = n_col * LANES
    u = u_ref[...].reshape(nseq * tc, d_ssm)
    ub = jnp.dot(perm_ref[...], u.astype(BF16), preferred_element_type=F32).astype(BF16)
    for c in range(n_col):
        kc = (c * LANES * d_ssm // sd) // LANES
        bu = jnp.dot(ub[:, kc * LANES:(kc + 1) * LANES], bcat_ref[c], preferred_element_type=F32)
        hre[c] = bu[:, :LANES]
        him[c] = bu[:, LANES:]

    for c0 in range(0, n_col, S5_SCAN_COLS):
        cs = range(c0, c0 + S5_SCAN_COLS)
        ar = [jnp.broadcast_to(are_ref[:, c * LANES:(c + 1) * LANES], (nseq, LANES)) for c in cs]
        ai = [jnp.broadcast_to(aim_ref[:, c * LANES:(c + 1) * LANES], (nseq, LANES)) for c in cs]

        def step(t, carry, cs=cs, ar=ar, ai=ai):
            rows = pl.ds(pl.multiple_of(t * nseq, nseq), nseq)
            out = []
            for n, c in enumerate(cs):
                pr, pi = carry[n]
                xr = hre[c, rows, :] + (ar[n] * pr - ai[n] * pi)
                xi = him[c, rows, :] + (ar[n] * pi + ai[n] * pr)
                hre[c, rows, :] = xr
                him[c, rows, :] = xi
                out.append((xr, xi))
            return tuple(out)

        init = tuple((cr[:, c * LANES:(c + 1) * LANES], ci[:, c * LANES:(c + 1) * LANES]) for c in cs)
        fin = lax.fori_loop(0, tc, step, init, unroll=4)
        for n, c in enumerate(cs):
            cr[:, c * LANES:(c + 1) * LANES] = fin[n][0]
            ci[:, c * LANES:(c + 1) * LANES] = fin[n][1]
    sre_ref[...] = cr[...]
    sim_ref[...] = ci[...]
    n_out = d_ssm // LANES
    per = n_col // n_out
    wide = lambda ref, j: jnp.concatenate([ref[c] for c in range(j * per, (j + 1) * per)], axis=1).astype(BF16)
    for j in range(n_out):
        zs[j] = (jnp.dot(wide(hre, j), ccre_ref[j], preferred_element_type=F32)
                 + jnp.dot(wide(him, j), ccim_ref[j], preferred_element_type=F32))
    y = jnp.concatenate(
        [jnp.concatenate([zs[j, pl.ds(b, tc, stride=nseq), :] for b in range(nseq)], axis=0)
         for j in range(n_out)], axis=1)
    out_ref[...] = _s5_readout(y, u, dsk_ref, wglu_ref, bglu_ref, d_model).reshape(nseq, tc, d_model)


def _s5_sample_kernel(u_ref, h0r_ref, h0i_ref, are_ref, aim_ref, bre_ref, bim_ref, cre_ref, cim_ref,
                      dsk_ref, wglu_ref, bglu_ref, out_ref, sre_ref, sim_ref, *, d_model):
    u = u_ref[...]
    ub = u.astype(BF16)
    ar, ai = are_ref[...], aim_ref[...]
    h0r, h0i = h0r_ref[...], h0i_ref[...]
    hr = (ar * h0r - ai * h0i) + jnp.dot(ub, bre_ref[...], preferred_element_type=F32)
    hi = (ar * h0i + ai * h0r) + jnp.dot(ub, bim_ref[...], preferred_element_type=F32)
    sre_ref[...] = hr
    sim_ref[...] = hi
    y = (jnp.dot(hr.astype(BF16), cre_ref[...], preferred_element_type=F32)
         + jnp.dot(hi.astype(BF16), cim_ref[...], preferred_element_type=F32))
    out_ref[...] = _s5_readout(y, u, dsk_ref, wglu_ref, bglu_ref, d_model)


def _s5_params(a_re, a_im, log_dt, b_re, b_im, c_re, c_im):
    g, p = a_re.shape
    a_c = lax.complex(a_re.astype(F32), a_im.astype(F32))
    dt = jnp.exp(log_dt.astype(F32))[:, None]
    a_bar = jnp.exp(a_c * dt)
    b_bar = ((a_bar - 1.0) / a_c)[:, :, None] * lax.complex(b_re.astype(F32), b_im.astype(F32))
    eye = jnp.eye(g, dtype=F32)
    n = b_re.shape[2]

    def bmat(b):
        return jnp.einsum('gpn,gh->gnhp', b, eye).reshape(g * n, g * p)

    def cmat(c):
        return jnp.einsum('gnp,gh->gphn', c, eye).reshape(g * p, g * n)

    bre, bim = bmat(jnp.real(b_bar)).astype(BF16), bmat(jnp.imag(b_bar)).astype(BF16)
    cre, cim = cmat(c_re.astype(F32)).astype(BF16), cmat(-c_im.astype(F32)).astype(BF16)
    d_in, sd = bre.shape
    assert d_in % LANES == 0 and sd % LANES == 0 and LANES % (LANES * d_in // sd) == 0
    bcat = jnp.stack([
        jnp.concatenate([m[(c * LANES * d_in // sd) // LANES * LANES:][:LANES, c * LANES:(c + 1) * LANES]
                         for m in (bre, bim)], axis=1)
        for c in range(sd // LANES)], axis=0)
    n_out = d_in // LANES
    span = sd // n_out
    ccre, ccim = (jnp.stack([m[j * span:(j + 1) * span, j * LANES:(j + 1) * LANES] for j in range(n_out)],
                            axis=0) for m in (cre, cim))
    a1 = a_bar.reshape(1, g * p)
    return jnp.real(a1), jnp.imag(a1), bre, bim, cre, cim, bcat, ccre, ccim


def _s5_prompt(u3, s5p, dsk, wglu, bglu, tc):
    bsz, t_len, d_ssm = u3.shape
    are, aim, bre, _, _, _, bcat, ccre, ccim = s5p
    sd = bre.shape[1]
    d_model = wglu.shape[1] // 2
    nseq = math.gcd(bsz, S5_SEQS)
    assert sd % (S5_SCAN_COLS * LANES) == 0 and tc % SUBLANES == 0
    kern = functools.partial(_s5_prompt_kernel, tc=tc, d_model=d_model)
    state = pl.BlockSpec((nseq, sd), lambda b, c: (b, 0))
    dst = jnp.arange(nseq * tc)
    perm = (jnp.arange(nseq * tc)[None, :] == ((dst % nseq) * tc + dst // nseq)[:, None]).astype(BF16)
    return pl.pallas_call(
        kern, grid=(bsz // nseq, t_len // tc),
        in_specs=[pl.BlockSpec((nseq, tc, d_ssm), lambda b, c: (b, c, 0)), _const_spec(perm.shape),
                  _const_spec(bcat.shape), _const_spec(are.shape), _const_spec(aim.shape),
                  _const_spec(ccre.shape), _const_spec(ccim.shape), _const_spec((1, d_ssm)),
                  _const_spec(wglu.shape), _const_spec((1, 2 * d_model))],
        out_specs=[pl.BlockSpec((nseq, tc, d_model), lambda b, c: (b, c, 0)), state, state],
        out_shape=[jax.ShapeDtypeStruct((bsz, t_len, d_model), F32),
                   jax.ShapeDtypeStruct((bsz, sd), F32),
                   jax.ShapeDtypeStruct((bsz, sd), F32)],
        scratch_shapes=[pltpu.VMEM((sd // LANES, nseq * tc, LANES), F32),
                        pltpu.VMEM((sd // LANES, nseq * tc, LANES), F32),
                        pltpu.VMEM((d_ssm // LANES, nseq * tc, LANES), F32),
                        pltpu.VMEM((nseq, sd), F32), pltpu.VMEM((nseq, sd), F32)],
        compiler_params=pltpu.CompilerParams(dimension_semantics=("arbitrary", "arbitrary"),
                                             vmem_limit_bytes=VMEM_LIMIT),
        name="s5_prompt",
    )(u3, perm, bcat, are, aim, ccre, ccim, dsk.reshape(1, d_ssm), wglu, bglu.reshape(1, 2 * d_model))


def _s5_sample(u2, h0r, h0i, s5p, dsk, wglu, bglu):
    n, d_ssm = u2.shape
    are, aim, bre, bim, cre, cim = s5p[:6]
    sd = bre.shape[1]
    d_model = wglu.shape[1] // 2
    kern = functools.partial(_s5_sample_kernel, d_model=d_model)
    return pl.pallas_call(
        kern,
        out_shape=[jax.ShapeDtypeStruct((n, d_model), F32),
                   jax.ShapeDtypeStruct((n, sd), F32),
                   jax.ShapeDtypeStruct((n, sd), F32)],
        compiler_params=pltpu.CompilerParams(vmem_limit_bytes=VMEM_LIMIT),
        name="s5_sample",
    )(u2, h0r, h0i, are, aim, bre, bim, cre, cim, dsk.reshape(1, d_ssm), wglu,
      bglu.reshape(1, 2 * d_model))


def _f2k(x):
    b = lax.bitcast_convert_type(x, I32)
    return jnp.where(b < 0, b ^ 0x7FFFFFFF, b)


def _k2f(k):
    return lax.bitcast_convert_type(jnp.where(k < 0, k ^ 0x7FFFFFFF, k), F32)


def _threshold_search(count_ge, amax, forced, k_top):
    kf = float(k_top)
    lo0 = jnp.where(forced, KEY_LOW, _f2k(-amax))
    hi0 = jnp.where(forced, KEY_LOW + 1, _f2k(amax) + 1)

    def cond(st):
        lo, hi, _ = st
        return jnp.max(jnp.where(hi > lo + 1, 1.0, 0.0)) > 0.5

    def body(st):
        lo, hi, it = st
        active = hi > lo + 1
        mid_i = (lo >> 1) + (hi >> 1) + (lo & hi & 1)
        mk = _f2k(0.5 * _k2f(lo) + 0.5 * _k2f(hi))
        use_f = jnp.logical_and(it % 2 == 0, jnp.logical_and(mk > lo, mk < hi))
        mid = jnp.where(use_f, mk, mid_i)
        c = count_ge(_k2f(mid))
        ge = c >= kf
        nlo = jnp.where(ge, mid, lo)
        nhi = jnp.where(c == kf, mid + 1, jnp.where(ge, hi, mid))
        return jnp.where(active, nlo, lo), jnp.where(active, nhi, hi), it + 1

    lo, _, _ = lax.while_loop(cond, body, (lo0, hi0, jnp.int32(0)))
    return _k2f(lo)


def _tri_exclusive(n):
    r = lax.broadcasted_iota(I32, (n, n), 0)
    c = lax.broadcasted_iota(I32, (n, n), 1)
    return (r < c).astype(BF16)


def _pipelined_blocks(n, produce, consume, buf_a, buf_b, carry):
    produce(0, buf_a)

    def pair(t, c):
        produce(2 * t + 1, buf_b)
        c = consume(2 * t, buf_a, c)
        produce(2 * t + 2, buf_a)
        return consume(2 * t + 1, buf_b, c)

    carry = lax.fori_loop(0, n // 2, pair, carry)
    return lax.cond(n % 2 == 1, lambda c: consume(n - 1, buf_a, c), lambda c: c, carry)


def _dsa_prompt_kernel(qis_ref, wt_ref, qs_ref, ki_ref, k_ref, vt_ref, o_ref,
                       sc_ref, scb_ref, m_ref, acc_ref, sa_ref, sb_ref, *, tq, tk, k_top):
    i = pl.program_id(1)
    nkb = i + 1
    hpg = N_HEADS // N_KV_HEADS
    ngr = tk // SUBLANES
    kf = float(k_top)

    w8 = wt_ref[...] * (IDX_DIM ** -0.5)
    qpos = i * tq + lax.broadcasted_iota(I32, (SUBLANES, tq), 1)

    def idx_dots(j, dst):
        k0 = pl.multiple_of(jnp.minimum(j, nkb - 1) * tk, tk)
        dst[...] = lax.dot_general(ki_ref[pl.ds(k0, tk), :], qis_ref[...], NT_DIMS, preferred_element_type=F32)

    def score_blk(j, src, carry, diagonal=False):
        for c in range(tq // LANES):
            acc = jnp.zeros((tk, LANES), F32)
            for h in range(N_IDX_HEADS):
                cols = slice(h * tq + c * LANES, h * tq + (c + 1) * LANES)
                acc = acc + jnp.maximum(src[:, cols], 0.0) * w8[h:h + 1, c * LANES:(c + 1) * LANES]
            if diagonal:
                kpos = j * tk + lax.broadcasted_iota(I32, (tk, LANES), 0)
                causal = kpos <= i * tq + c * LANES + lax.broadcasted_iota(I32, (tk, LANES), 1)
                acc = jnp.where(causal, acc, NEG_INF)
            sc_ref[j, :, c * LANES:(c + 1) * LANES] = acc
            top = lax.bitcast_convert_type(lax.bitcast_convert_type(acc, I32) & jnp.int32(-65536), F32)
            scb_ref[j, :, c * LANES:(c + 1) * LANES] = top.astype(BF16)
        return carry

    _pipelined_blocks(nkb - 1, idx_dots, score_blk, sa_ref, sb_ref, jnp.int32(0))
    idx_dots(nkb - 1, sa_ref)
    score_blk(nkb - 1, sa_ref, 0, diagonal=True)

    rep = lambda x: jnp.broadcast_to(x, (SUBLANES, tq))
    forced = (qpos + 1) <= k_top

    packed = 2 * SUBLANES
    one_b, zero_b = jnp.ones((), BF16), jnp.zeros((), BF16)

    def count16(k16):
        bits = jnp.where(k16 < 0, k16 ^ 0x7FFF, k16) << 16
        tb = jnp.broadcast_to(lax.bitcast_convert_type(bits, F32)[0:1, :], (packed, tq)).astype(BF16)

        def body(j, part):
            blk = scb_ref[j].reshape(tk // packed // COUNT_WAYS, COUNT_WAYS, packed, tq)
            hit = jnp.where(blk >= tb[None, None], one_b, zero_b)
            for g in range(hit.shape[0]):
                part = part + hit[g]
            return part

        part = lax.fori_loop(0, nkb, body, jnp.zeros((COUNT_WAYS, packed, tq), BF16))
        return rep(jnp.sum(jnp.sum(part.astype(F32), axis=0), axis=0, keepdims=True))

    def coarse_pass(lo16, hi16, c_lo):
        active = hi16 > lo16 + 1
        mid = lo16 + ((hi16 - lo16) >> 1)
        c = count16(mid)
        up = jnp.logical_and(active, c >= kf)
        nhi = jnp.where(c == kf, mid, jnp.where(c >= kf, hi16, mid))
        return jnp.where(up, mid, lo16), jnp.where(active, nhi, hi16), jnp.where(up, c, c_lo)

    lo16 = jnp.full((SUBLANES, tq), KEY_LOW >> 16, I32)
    hi16 = jnp.where(forced, KEY_LOW >> 16, KEY_INF >> 16)
    c_lo = (qpos + 1).astype(F32)
    lo16, hi16, c_lo = lax.fori_loop(0, 8, lambda _, st: coarse_pass(*coarse_pass(*st)), (lo16, hi16, c_lo))
    lo0 = lo16 << 16
    hi0 = jnp.where(hi16 == lo16, lo0 + 1, hi16 << 16)

    def count(thr, strict):
        def body(j, part):
            blk = sc_ref[j].reshape(ngr // COUNT_WAYS, COUNT_WAYS, SUBLANES, tq)
            hit = (blk > thr[None, None]) if strict else (blk >= thr[None, None])
            return part + jnp.sum(jnp.where(hit, 1.0, 0.0), axis=0)

        part = lax.fori_loop(0, nkb, body, jnp.zeros((COUNT_WAYS, SUBLANES, tq), F32))
        return rep(jnp.sum(jnp.sum(part, axis=0), axis=0, keepdims=True))

    def search_pass(lo, hi, c_lo, value_space):
        active = hi > lo + 1
        span = hi - lo
        mid = lo + lax.shift_right_logical(span, jnp.ones_like(span))
        if value_space:
            mk = _f2k(0.5 * _k2f(lo) + 0.5 * _k2f(hi))
            mid = jnp.where(jnp.logical_and(mk > lo, mk < hi), mk, mid)
        mid = jnp.where(lo == 0, 1, mid)
        c = count(_k2f(mid), False)
        up = jnp.logical_and(active, c >= kf)
        nhi = jnp.where(c == kf, mid + 1, jnp.where(c >= kf, hi, mid))
        return jnp.where(up, mid, lo), jnp.where(active, nhi, hi), jnp.where(up, c, c_lo)

    def double_round(lo, hi, c_lo):
        return search_pass(*search_pass(lo, hi, c_lo, True), False)

    def unresolved(lo, hi):
        return (jnp.max(jnp.where(hi > lo + 1, 1.0, 0.0)) > 0.5).astype(I32)

    def search_body(st):
        lo, hi, c_lo = double_round(*st[:3])
        return lo, hi, c_lo, unresolved(lo, hi)

    lo, hi, c_lo = lax.fori_loop(0, FINE_ROUNDS_UNCHECKED, lambda _, s: double_round(*s), (lo0, hi0, c_lo))
    lo, _, c_ge, _ = lax.while_loop(lambda s: s[3] > 0, search_body, (lo, hi, c_lo, unresolved(lo, hi)))
    thr = _k2f(lo)
    thr1 = thr[0:1, :]

    @pl.when(jnp.max(jnp.where(c_ge > kf, 1.0, 0.0)) > 0.5)
    def _():
        need = jnp.where(c_ge > kf, kf - count(thr, True), 1e30)[0:1, :]
        r = lax.broadcasted_iota(I32, (tk, tk), 0)
        c = lax.broadcasted_iota(I32, (tk, tk), 1)
        earlier = (c < r).astype(BF16)
        ones_k = jnp.ones((SUBLANES, tk), BF16)

        def fix(j, run):
            sc = sc_ref[j]
            eq = sc == thr1
            eqb = jnp.where(eq, 1.0, 0.0).astype(BF16)
            before = jnp.dot(earlier, eqb, preferred_element_type=F32) + run[0:1, :]
            sc_ref[j] = jnp.where(jnp.logical_and(eq, before >= need), NEG_INF, sc)
            return run + jnp.dot(ones_k, eqb, preferred_element_type=F32)

        lax.fori_loop(0, nkb, fix, jnp.zeros((SUBLANES, tq), F32))

    m_ref[...] = jnp.full(m_ref.shape, NEG_INF, F32)
    acc_ref[...] = jnp.zeros(acc_ref.shape, F32)

    def logits(j, dst):
        jc = jnp.minimum(j, nkb - 1)
        k0 = pl.multiple_of(jc * tk, tk)
        dst[...] = lax.dot_general(k_ref[pl.ds(k0, tk), :], qs_ref[...], NT_DIMS, preferred_element_type=F32)

    def softmax_pv(j, src):
        bias = jnp.where(sc_ref[j] >= thr1, 0.0, NEG_INF)
        for h in range(N_HEADS):
            x = src[:, h * tq:(h + 1) * tq] + bias
            bm = jnp.max(jnp.max(x.reshape(ngr, SUBLANES, tq), axis=0), axis=0, keepdims=True)
            m_old = m_ref[h:h + 1, :]
            m_new = jnp.maximum(m_old, bm)
            m_safe = jnp.where(m_new == NEG_INF, 0.0, m_new)
            p = jnp.exp2(x - m_safe).astype(BF16)
            pv = jnp.dot(vt_ref[j, h // hpg], p, preferred_element_type=F32)
            acc_ref[h] = acc_ref[h] * jnp.exp2(m_old - m_safe) + pv
            m_ref[h:h + 1, :] = m_new

    def consume(j, src, carry):
        softmax_pv(j, src)
        return carry

    _pipelined_blocks(nkb, logits, consume, sa_ref, sb_ref, jnp.int32(0))

    for pr in range(N_HEADS // 2):
        outs = []
        for h in (2 * pr, 2 * pr + 1):
            a = acc_ref[h]
            outs.append(a[:HEAD_DIM, :] / a[HEAD_DIM:HEAD_DIM + 1, :])
        o2 = jnp.concatenate(outs, axis=0)
        o_ref[:, 2 * pr * HEAD_DIM:(2 * pr + 2) * HEAD_DIM] = o2.T.astype(o_ref.dtype)


def _dsa_prompt(qis, wt, qs, kib, kb, vt, tq, tk):
    bsz, t_len, d_kv = kb.shape
    d_attn = N_HEADS * HEAD_DIM
    nkb = t_len // tk
    k_top = min(TOPK_MAX, t_len // 4)
    assert tq == tk and tq % LANES == 0 and t_len % tq == 0 and vt.shape[1] == nkb
    assert qis.shape == (bsz, nkb, N_IDX_HEADS * tq, IDX_DIM) and qs.shape == (bsz, nkb, N_HEADS * tq, d_kv)
    assert nkb * (tk // (2 * SUBLANES * COUNT_WAYS)) <= 256
    kern = functools.partial(_dsa_prompt_kernel, tq=tq, tk=tk, k_top=k_top)
    tile = lambda w: pl.BlockSpec((None, tq, w), lambda b, i: (b, i, 0))
    full = lambda w: pl.BlockSpec((None, t_len, w), lambda b, i: (b, 0, 0))
    stacked = lambda a: pl.BlockSpec((None, None) + a.shape[2:], lambda b, i: (b, i, 0, 0))
    return pl.pallas_call(
        kern, grid=(bsz, t_len // tq),
        in_specs=[stacked(qis), pl.BlockSpec((None, N_IDX_HEADS, tq), lambda b, i: (b, 0, i)), stacked(qs),
                  full(IDX_DIM), full(d_kv),
                  pl.BlockSpec((None, nkb, N_KV_HEADS, VT_ROWS, tk), lambda b, i: (b, 0, 0, 0, 0))],
        out_specs=tile(d_attn),
        out_shape=jax.ShapeDtypeStruct((bsz, t_len, d_attn), BF16),
        scratch_shapes=[pltpu.VMEM((nkb, tk, tq), F32),
                        pltpu.VMEM((nkb, tk, tq), BF16),
                        pltpu.VMEM((N_HEADS, tq), F32),
                        pltpu.VMEM((N_HEADS, VT_ROWS, tq), F32),
                        pltpu.VMEM((tk, N_HEADS * tq), F32),
                        pltpu.VMEM((tk, N_HEADS * tq), F32)],
        compiler_params=pltpu.CompilerParams(dimension_semantics=("arbitrary", "arbitrary"),
                                             vmem_limit_bytes=VMEM_LIMIT),
        name="dsa_prompt",
    )(qis, wt, qs, kib, kb, vt)


def _page_fetcher(pt_ref, n_pages, streams):
    def copies(seq, slot):
        for p in range(n_pages):
            pg = pt_ref[seq * n_pages + p]
            for hbm, buf, sem in streams:
                page = hbm.shape[2]
                yield pltpu.make_async_copy(hbm.at[pg], buf.at[slot, :, pl.ds(p * page, page)], sem.at[slot])

    def fetch(seq, slot):
        for cp in copies(seq, slot):
            cp.start()

    def wait(seq, slot):
        for cp in copies(seq, slot):
            cp.wait()

    return fetch, wait


def _prefetch_next_and_wait(fetch, wait):
    b = pl.program_id(0)
    slot = b % 2

    @pl.when(b == 0)
    def _():
        fetch(0, 0)

    @pl.when(b + 1 < pl.num_programs(0))
    def _():
        fetch(b + 1, 1 - slot)

    wait(b, slot)
    return b, slot


def _idx_score_kernel(pt_ref, q_ref, w_ref, ci_ref, o_ref, ibuf, sem, *, n_pages):
    b, slot = _prefetch_next_and_wait(*_page_fetcher(pt_ref, n_pages, [(ci_ref, ibuf, sem)]))
    q = q_ref[...]
    w = w_ref[...] * (IDX_DIM ** -0.5)
    d = jnp.dot(q, ibuf[slot].astype(BF16), preferred_element_type=F32)
    o_ref[pl.ds(b % SUBLANES, 1), :] = jnp.sum(jnp.maximum(d, 0.0) * w, axis=0, keepdims=True)


def _idx_scores(page_table, qis, wcol, cache_idx_kt):
    sq, n_pages = page_table.shape
    page = cache_idx_kt.shape[2]
    past = n_pages * page
    kern = functools.partial(_idx_score_kernel, n_pages=n_pages)
    grid_spec = pltpu.PrefetchScalarGridSpec(
        num_scalar_prefetch=1, grid=(sq,),
        in_specs=[pl.BlockSpec((None, N_IDX_HEADS, IDX_DIM), lambda b, pt: (b, 0, 0)),
                  pl.BlockSpec((None, N_IDX_HEADS, 1), lambda b, pt: (b, 0, 0)),
                  pl.BlockSpec(memory_space=pl.ANY)],
        out_specs=pl.BlockSpec((SUBLANES, past), lambda b, pt: (b // SUBLANES, 0)),
        scratch_shapes=[pltpu.VMEM((2, IDX_DIM, past), cache_idx_kt.dtype), pltpu.SemaphoreType.DMA((2,))])
    return pl.pallas_call(
        kern, grid_spec=grid_spec,
        out_shape=jax.ShapeDtypeStruct((sq, past), F32),
        compiler_params=pltpu.CompilerParams(dimension_semantics=("arbitrary",),
                                             vmem_limit_bytes=VMEM_LIMIT),
        name="sample_idx_scores",
    )(page_table.reshape(-1), qis, wcol, cache_idx_kt)


def _sample_select_kernel(sc_ref, qi_ref, ki_ref, wi_ref, mask_ref, selfsel_ref, *, ch, k_top):
    sq, l_past = sc_ref.shape
    nch = l_past // ch
    w = wi_ref[...] * (IDX_DIM ** -0.5)
    qf = qi_ref[...].astype(F32)
    kf32 = ki_ref[...].astype(F32)
    s_self = jnp.zeros((sq, 1), F32)
    for h in range(N_IDX_HEADS):
        dh = jnp.sum(qf[:, h * IDX_DIM:(h + 1) * IDX_DIM] * kf32, axis=1, keepdims=True)
        s_self = s_self + jnp.maximum(dh, 0.0) * w[:, h:h + 1]

    amax = jnp.abs(s_self)
    for c in range(nch):
        amax = jnp.maximum(amax, jnp.max(jnp.abs(sc_ref[:, c * ch:(c + 1) * ch]), axis=1, keepdims=True))

    def count_cmp(thr, strict):
        thr_b = jnp.broadcast_to(thr, (sq, LANES))
        part = jnp.zeros((sq, LANES), F32)
        for c in range(l_past // LANES):
            blk = sc_ref[:, c * LANES:(c + 1) * LANES]
            hit = (blk > thr_b) if strict else (blk >= thr_b)
            part = part + jnp.where(hit, 1.0, 0.0)
        self_hit = (s_self > thr) if strict else (s_self >= thr)
        return jnp.sum(part, axis=1, keepdims=True) + jnp.where(self_hit, 1.0, 0.0)

    kf = float(k_top)
    forced = amax < 0.0
    thr = _threshold_search(lambda t: count_cmp(t, False), amax, forced, k_top)
    need = kf - count_cmp(thr, True)
    tri = _tri_exclusive(ch)
    run = jnp.zeros((sq, 1), F32)
    for c in range(nch):
        sc = sc_ref[:, c * ch:(c + 1) * ch]
        eq = sc == thr
        before = jnp.dot(jnp.where(eq, 1.0, 0.0).astype(BF16), tri, preferred_element_type=F32) + run
        keep = jnp.logical_or(sc > thr, jnp.logical_and(eq, before < need))
        mask_ref[:, c * ch:(c + 1) * ch] = jnp.where(keep, 1.0, 0.0)
        run = run + jnp.sum(jnp.where(eq, 1.0, 0.0), axis=1, keepdims=True)
    self_keep = jnp.logical_or(s_self > thr, jnp.logical_and(s_self == thr, run < need))
    selfsel_ref[...] = jnp.where(self_keep, 1.0, 0.0)


def _sample_select(scores, qi, kib, wi, ch, k_top):
    sq, l_past = scores.shape
    kern = functools.partial(_sample_select_kernel, ch=ch, k_top=k_top)
    return pl.pallas_call(
        kern,
        out_shape=[jax.ShapeDtypeStruct((sq, l_past), F32), jax.ShapeDtypeStruct((sq, 1), F32)],
        compiler_params=pltpu.CompilerParams(vmem_limit_bytes=VMEM_LIMIT),
        name="sample_select",
    )(scores, qi, kib, wi)


def _sample_attn_kernel(pt_ref, q_ref, mask_ref, ks_ref, vs_ref, ss_ref, ck_ref, cv_ref, o_ref,
                        kbuf, vbuf, ksem, vsem, *, n_pages):
    _, slot = _prefetch_next_and_wait(
        *_page_fetcher(pt_ref, n_pages, [(ck_ref, kbuf, ksem), (cv_ref, vbuf, vsem)]))
    hpg = N_HEADS // N_KV_HEADS
    q = q_ref[...]
    s = jnp.dot(q, kbuf[slot].astype(BF16), preferred_element_type=F32)
    sm = jnp.where(mask_ref[...] > 0.5, s, NEG_INF)
    s_self = jnp.sum(q.astype(F32) * ks_ref[...].astype(F32), axis=1, keepdims=True)
    s_self = jnp.where(ss_ref[...] > 0.5, s_self, NEG_INF)
    m = jnp.maximum(jnp.max(sm, axis=1, keepdims=True), s_self)
    m = jnp.where(m == NEG_INF, 0.0, m)
    pb = jnp.exp2(sm - m).astype(BF16)
    p_self = jnp.exp2(s_self - m).astype(BF16).astype(F32)
    l_sum = jnp.sum(pb.astype(F32), axis=1, keepdims=True) + p_self
    acc = (lax.dot_general(pb, vbuf[slot].astype(BF16), NT_DIMS, preferred_element_type=F32)
           + p_self * vs_ref[...].astype(F32))
    o = acc / l_sum
    hrow = lax.broadcasted_iota(I32, o.shape, 0)
    o = jnp.where(hrow < hpg, o, pltpu.roll(o, HEAD_DIM, axis=1))
    o_ref[...] = o[:, :HEAD_DIM].astype(o_ref.dtype)


def _sample_attn(page_table, qs, mask3, kself, vself, selfsel, cache_k2, cache_v2):
    sq, n_pages = page_table.shape
    d_kv, page = cache_k2.shape[1], cache_k2.shape[2]
    past = n_pages * page
    kern = functools.partial(_sample_attn_kernel, n_pages=n_pages)
    per_seq = lambda r, w: pl.BlockSpec((None, r, w), lambda b, pt: (b, 0, 0))
    hbm = pl.BlockSpec(memory_space=pl.ANY)
    grid_spec = pltpu.PrefetchScalarGridSpec(
        num_scalar_prefetch=1, grid=(sq,),
        in_specs=[per_seq(N_HEADS, d_kv), per_seq(1, past), per_seq(1, d_kv), per_seq(1, d_kv),
                  per_seq(1, 1), hbm, hbm],
        out_specs=per_seq(N_HEADS, HEAD_DIM),
        scratch_shapes=[pltpu.VMEM((2, d_kv, past), cache_k2.dtype), pltpu.VMEM((2, d_kv, past), cache_v2.dtype),
                        pltpu.SemaphoreType.DMA((2,)), pltpu.SemaphoreType.DMA((2,))])
    return pl.pallas_call(
        kern, grid_spec=grid_spec,
        out_shape=jax.ShapeDtypeStruct((sq, N_HEADS, HEAD_DIM), BF16),
        compiler_params=pltpu.CompilerParams(dimension_semantics=("arbitrary",),
                                             vmem_limit_bytes=VMEM_LIMIT),
        name="sample_attn",
    )(page_table.reshape(-1), qs, mask3, kself, vself, selfsel, cache_k2, cache_v2)


def _tail_kernel(x_ref, attn_ref, ssm_ref, sgs_ref, sga_ref, wao_ref, wo_ref, g2_ref, wup_ref, wdn_ref,
                 gf_ref, y_ref):
    attn_out = jnp.dot(attn_ref[...], wao_ref[...], preferred_element_type=F32)
    mix = sgs_ref[...] * ssm_ref[...] + sga_ref[...] * attn_out
    x1 = x_ref[...] + jnp.dot(mix.astype(BF16), wo_ref[...], preferred_element_type=F32)
    hh = _rms_norm(x1, g2_ref[...]).astype(BF16)
    up = jnp.dot(hh, wup_ref[...], preferred_element_type=F32)
    r = jnp.square(jnp.maximum(up, 0.0)).astype(BF16)
    x2 = x1 + jnp.dot(r, wdn_ref[...], preferred_element_type=F32)
    y_ref[...] = _rms_norm(x2, gf_ref[...])


def _tail(x2d, attn, ssm, sgs, sga, wao, wo, g2, wup, wdn, gf, tm):
    n, d_model = x2d.shape
    row = lambda w: pl.BlockSpec((tm, w), lambda i: (i, 0))
    return pl.pallas_call(
        _tail_kernel, grid=(n // tm,),
        in_specs=[row(d_model), row(attn.shape[1]), row(d_model), row(d_model), row(d_model),
                  _const_spec(wao.shape), _const_spec(wo.shape), _const_spec((1, d_model)),
                  _const_spec(wup.shape), _const_spec(wdn.shape), _const_spec((1, d_model))],
        out_specs=row(d_model),
        out_shape=jax.ShapeDtypeStruct((n, d_model), F32),
        compiler_params=pltpu.CompilerParams(dimension_semantics=("arbitrary",),
                                             vmem_limit_bytes=VMEM_LIMIT),
        name="tail",
    )(x2d, attn, ssm, sgs, sga, wao, wo, g2.reshape(1, d_model), wup, wdn, gf.reshape(1, d_model))


def _tiles(n_rows, t_len):
    tm = min(256, n_rows)
    tc = min(128, t_len)
    tq = min(256, t_len)
    return tm, tc, tq, tq


def kernel(x_prompt, x_sample, cache_k, cache_v, cache_idx_k, state_ssm_re, state_ssm_im, page_table,
           norm1_g, w_in, ssm_a_re, ssm_a_im, ssm_log_dt, ssm_b_re, ssm_b_im, ssm_c_re, ssm_c_im,
           ssm_d, w_glu, b_glu, w_attn_out, w_o, norm2_g, w_up, w_down, normf_g):
    bsz, t_len, d_model = x_prompt.shape
    sq, s_len, _ = x_sample.shape
    assert s_len == 1, "the sample path handles one new token per sequence"
    n_pool, page = cache_idx_k.shape[0], cache_idx_k.shape[1]
    n_pages = page_table.shape[1]
    past = n_pages * page
    d_ssm = ssm_d.shape[0]
    d_attn = N_HEADS * HEAD_DIM
    d_kv = N_KV_HEADS * HEAD_DIM
    d_qi = N_IDX_HEADS * IDX_DIM
    dims = (d_ssm, d_attn, d_kv, d_qi)

    c_ki = d_ssm + d_attn + 2 * d_kv + d_qi
    c_g = c_ki + IDX_DIM + N_IDX_HEADS
    w_pack = jnp.concatenate(
        [w_in[:, :c_g], jnp.zeros((d_model, LANES - IDX_DIM - N_IDX_HEADS), w_in.dtype), w_in[:, c_g:]],
        axis=1).astype(BF16)
    wglu_b, wao_b, wo_b = w_glu.astype(BF16), w_attn_out.astype(BF16), w_o.astype(BF16)
    wup_b, wdn_b = w_up.astype(BF16), w_down.astype(BF16)
    s5p = _s5_params(ssm_a_re, ssm_a_im, ssm_log_dt, ssm_b_re, ssm_b_im, ssm_c_re, ssm_c_im)

    n_p = bsz * t_len
    tm, tc, tq, tk = _tiles(n_p, t_len)
    xp = x_prompt.reshape(n_p, d_model)
    tabs_p = _rope_tables(jnp.arange(t_len, dtype=I32), t_len)
    assert tm == tk
    (u, q, kt, kb, vt, _, vtb, qi, kit, kib, _, wit, sgs, sga) = _in_proj(
        xp, bsz, tabs_p, norm1_g, w_pack, tm, dims)
    ssm_out, re_p, im_p = _s5_prompt(u.reshape(bsz, t_len, d_ssm), s5p, ssm_d, wglu_b, b_glu, tc)
    r3 = lambda a: a.reshape(bsz, t_len, a.shape[-1])
    attn = _dsa_prompt(qi, wit, q, r3(kib), r3(kb), vtb, tq, tk)
    y_p = _tail(xp, attn.reshape(n_p, d_attn), ssm_out.reshape(n_p, d_model), sgs, sga,
                wao_b, wo_b, norm2_g, wup_b, wdn_b, normf_g, tm)

    xs = x_sample.reshape(sq, d_model)
    tabs_s = _rope_tables(jnp.full((sq,), past, I32), sq)
    (u_s, q_s, kt_s, kb_s, vt_s, vb_s, _, qi_s, kit_s, kib_s, wi_s, _, sgs_s, sga_s) = _in_proj(
        xs, 1, tabs_s, norm1_g, w_pack, sq, dims)
    sd = state_ssm_re.shape[1] * state_ssm_re.shape[2]
    ssm_s, re_s, im_s = _s5_sample(u_s, state_ssm_re.reshape(sq, sd), state_ssm_im.reshape(sq, sd),
                                   s5p, ssm_d, wglu_b, b_glu)
    per_seq = lambda a, heads: jnp.swapaxes(a.reshape(heads, sq, a.shape[-1]), 0, 1)
    qih_s = per_seq(qi_s, N_IDX_HEADS)
    scores = _idx_scores(page_table, qih_s, wi_s.reshape(sq, N_IDX_HEADS, 1),
                         jnp.transpose(cache_idx_k, (0, 2, 1)))
    k_top_s = min(TOPK_MAX, (past + s_len) // 4)
    mask, selfsel = _sample_select(scores, qih_s.reshape(sq, d_qi), kib_s, wi_s, min(512, past), k_top_s)
    qs_pad = per_seq(q_s, N_HEADS)
    feat_major = lambda c: jnp.transpose(c, (0, 2, 3, 1)).reshape(n_pool, d_kv, page)
    attn_s = _sample_attn(page_table, qs_pad, mask.reshape(sq, 1, past),
                          kb_s.reshape(sq, 1, d_kv), vb_s.reshape(sq, 1, d_kv), selfsel.reshape(sq, 1, 1),
                          feat_major(cache_k), feat_major(cache_v))
    y_s = _tail(xs, attn_s.reshape(sq, d_attn), ssm_s, sgs_s, sga_s,
                wao_b, wo_b, norm2_g, wup_b, wdn_b, normf_g, sq)

    g_ssm, p_ssm = state_ssm_re.shape[1], state_ssm_re.shape[2]
    kv_out = lambda a: jnp.transpose(a.reshape(a.shape[0], N_KV_HEADS, HEAD_DIM, a.shape[2]), (0, 3, 1, 2))
    return (y_p.reshape(bsz, t_len, d_model), y_s.reshape(sq, s_len, d_model),
            kv_out(kt), kv_out(vt), jnp.transpose(kit, (0, 2, 1)),
            re_p.reshape(bsz, g_ssm, p_ssm), im_p.reshape(bsz, g_ssm, p_ssm),
            kv_out(kt_s).reshape(sq, s_len, N_KV_HEADS, HEAD_DIM),
            kv_out(vt_s).reshape(sq, s_len, N_KV_HEADS, HEAD_DIM),
            jnp.transpose(kit_s, (0, 2, 1)).reshape(sq, s_len, IDX_DIM),
            re_s.reshape(sq, g_ssm, p_ssm), im_s.reshape(sq, g_ssm, p_ssm))
```

```python
import functools
import math

import jax
import jax.numpy as jnp
from jax import lax
from jax.experimental import pallas as pl
from jax.experimental.pallas import tpu as pltpu

F32 = jnp.float32
BF16 = jnp.bfloat16
I32 = jnp.int32

N_HEADS = 8
N_KV_HEADS = 2
HEAD_DIM = 64
ROT_DIM = HEAD_DIM // 4
N_IDX_HEADS = 8
IDX_DIM = 64
IDX_ROT_DIM = IDX_DIM // 4
assert (IDX_DIM, IDX_ROT_DIM) == (HEAD_DIM, ROT_DIM)
ROPE_THETA = 500000.0
TOPK_MAX = 256
EPS = 1e-6
LOG2_E = math.log2(math.e)

LANES = 128
SUBLANES = 8
MXU_COLS = 256
S5_SEQS = 8
S5_SCAN_COLS = 4
COUNT_WAYS = 4
FINE_ROUNDS_UNCHECKED = 3
VT_ROWS = 80
VMEM_LIMIT = 56 * 1024 * 1024

NEG_INF = float("-inf")
KEY_LOW = -2139095040
KEY_INF = 0x7F800000
KEY_MIN_NORMAL = 0x00800000
NT_DIMS = (((1,), (1,)), ((), ()))


def _const_spec(shape):
    nd = len(shape)
    return pl.BlockSpec(shape, lambda *_: (0,) * nd, pipeline_mode=pl.Buffered(1))


def _rms_norm(x, g):
    ms = jnp.mean(x * x, axis=-1, keepdims=True)
    return x * lax.rsqrt(ms + EPS) * g


def _sigmoid(x):
    return 1.0 / (1.0 + jnp.exp(-x))


def _rope(x, cos_t, sin_a, sin_b):
    return (x * cos_t + pltpu.roll(x, LANES - ROT_DIM // 2, axis=1) * sin_a
            + pltpu.roll(x, ROT_DIM // 2, axis=1) * sin_b)


def _in_proj_kernel(x_ref, g_ref, w_ref, cos_ref, sa_ref, sb_ref,
                    u_ref, q_ref, kt_ref, kb_ref, vt_ref, vb_ref, vtb_ref, qi_ref, kit_ref, kib_ref,
                    wi_ref, wit_ref, sgs_ref, sga_ref, *, d_ssm, d_attn, d_kv, d_qi, d_model):
    assert d_kv == LANES
    tm = x_ref.shape[0]
    h = _rms_norm(x_ref[...], g_ref[...]).astype(BF16)
    cos_t, sin_a, sin_b = cos_ref[...], sa_ref[...], sb_ref[...]

    def proj(c0, width):
        return jnp.dot(h, w_ref[:, c0:c0 + width], preferred_element_type=F32)

    off = 0
    u_ref[...] = proj(off, d_ssm)
    off += d_ssm
    def rope_chunks(c0, width):
        for m in range(width // MXU_COLS):
            wide = proj(c0 + m * MXU_COLS, MXU_COLS)
            for c in range(MXU_COLS // LANES):
                yield (m * MXU_COLS // LANES + c,
                       _rope(wide[:, c * LANES:(c + 1) * LANES], cos_t, sin_a, sin_b))

    lane_q = lax.broadcasted_iota(I32, (tm, LANES), 1)
    for c, r in rope_chunks(off, d_attn):
        rs = r * (HEAD_DIM ** -0.5 * LOG2_E)
        swapped = pltpu.roll(rs, HEAD_DIM, axis=1)
        group = (2 * c) // (N_HEADS // N_KV_HEADS)
        keep = (lane_q < HEAD_DIM) if group == 0 else (lane_q >= HEAD_DIM)
        first, second = (rs, swapped) if group == 0 else (swapped, rs)
        q_ref[2 * c * tm:(2 * c + 1) * tm, :] = jnp.where(keep, first, 0.0).astype(BF16)
        q_ref[(2 * c + 1) * tm:(2 * c + 2) * tm, :] = jnp.where(keep, second, 0.0).astype(BF16)
    off += d_attn
    kv = proj(off, 2 * d_kv)
    r = _rope(kv[:, :d_kv], cos_t, sin_a, sin_b)
    kt_ref[...] = r.T
    kb_ref[...] = r.astype(BF16)
    off += d_kv
    vv = kv[:, d_kv:]
    vt = vv.T
    vt_ref[...] = vt
    vb_ref[...] = vv.astype(BF16)
    sub = lax.broadcasted_iota(I32, (VT_ROWS - HEAD_DIM, tm), 0)
    ones_pad = jnp.where(sub == 0, 1.0, 0.0)
    for g in range(N_KV_HEADS):
        vtb_ref[g] = jnp.concatenate([vt[g * HEAD_DIM:(g + 1) * HEAD_DIM, :], ones_pad], axis=0).astype(BF16)
    off += d_kv
    for c, r in rope_chunks(off, d_qi):
        rb = r.astype(BF16)
        qi_ref[2 * c * tm:(2 * c + 1) * tm, :] = rb[:, :IDX_DIM]
        qi_ref[(2 * c + 1) * tm:(2 * c + 2) * tm, :] = rb[:, IDX_DIM:]
    off += d_qi
    kw = proj(off, LANES)
    lane = lax.broadcasted_iota(I32, kw.shape, 1)
    kr = jnp.where(lane < IDX_DIM, _rope(kw, cos_t, sin_a, sin_b), kw * (N_IDX_HEADS ** -0.5))
    krt = kr.T
    kit_ref[...] = krt[:IDX_DIM, :]
    kib_ref[...] = kr[:, :IDX_DIM].astype(BF16)
    wi_ref[...] = kr[:, IDX_DIM:IDX_DIM + N_IDX_HEADS]
    wit_ref[...] = krt[IDX_DIM:IDX_DIM + N_IDX_HEADS, :]
    off += LANES
    sgs_ref[...] = _sigmoid(proj(off, d_model))
    off += d_model
    sga_ref[...] = _sigmoid(proj(off, d_model))


def _rope_tables(pos, n_rows):
    half = ROT_DIM // 2
    inv = ROPE_THETA ** (-jnp.arange(half, dtype=F32) / half)
    ang = pos.astype(F32)[:, None] * inv[None, :]
    cos, sin = jnp.cos(ang), jnp.sin(ang)
    ones = jnp.ones((n_rows, HEAD_DIM - ROT_DIM), F32)
    zeros = jnp.zeros((n_rows, HEAD_DIM - ROT_DIM), F32)
    zh = jnp.zeros((n_rows, half), F32)
    cos_t = jnp.concatenate([cos, cos, ones], axis=1)
    sin_a = jnp.concatenate([-sin, zh, zeros], axis=1)
    sin_b = jnp.concatenate([zh, sin, zeros], axis=1)
    rep = LANES // HEAD_DIM
    return tuple(jnp.tile(t, (1, rep)) for t in (cos_t, sin_a, sin_b))


def _in_proj(x2d, n_seq, pos_tab, norm_g, w_pack, tm, dims):
    n, d_model = x2d.shape
    d_ssm, d_attn, d_kv, d_qi = dims
    cos_t, sin_a, sin_b = pos_tab
    t_len = n // n_seq
    nt = t_len // tm
    grid = (n // tm,)
    row = lambda w: pl.BlockSpec((tm, w), lambda i: (i, 0))
    tab = pl.BlockSpec((tm, LANES), lambda i: (i % nt, 0))
    feat = lambda r: pl.BlockSpec((None, r, tm), lambda i: (i // nt, 0, i % nt))
    stacked = lambda heads, w: pl.BlockSpec((None, None, heads * tm, w), lambda i: (i // nt, i % nt, 0, 0))
    kern = functools.partial(_in_proj_kernel, d_ssm=d_ssm, d_attn=d_attn, d_kv=d_kv, d_qi=d_qi,
                             d_model=d_model)
    outs = [
        (jax.ShapeDtypeStruct((n, d_ssm), F32), row(d_ssm)),
        (jax.ShapeDtypeStruct((n_seq, nt, N_HEADS * tm, d_kv), BF16), stacked(N_HEADS, d_kv)),
        (jax.ShapeDtypeStruct((n_seq, d_kv, t_len), F32), feat(d_kv)),
        (jax.ShapeDtypeStruct((n, d_kv), BF16), row(d_kv)),
        (jax.ShapeDtypeStruct((n_seq, d_kv, t_len), F32), feat(d_kv)),
        (jax.ShapeDtypeStruct((n, d_kv), BF16), row(d_kv)),
        (jax.ShapeDtypeStruct((n_seq, nt, N_KV_HEADS, VT_ROWS, tm), BF16),
         pl.BlockSpec((None, None, N_KV_HEADS, VT_ROWS, tm), lambda i: (i // nt, i % nt, 0, 0, 0))),
        (jax.ShapeDtypeStruct((n_seq, nt, N_IDX_HEADS * tm, IDX_DIM), BF16), stacked(N_IDX_HEADS, IDX_DIM)),
        (jax.ShapeDtypeStruct((n_seq, IDX_DIM, t_len), F32), feat(IDX_DIM)),
        (jax.ShapeDtypeStruct((n, IDX_DIM), BF16), row(IDX_DIM)),
        (jax.ShapeDtypeStruct((n, N_IDX_HEADS), F32), row(N_IDX_HEADS)),
        (jax.ShapeDtypeStruct((n_seq, N_IDX_HEADS, t_len), F32), feat(N_IDX_HEADS)),
        (jax.ShapeDtypeStruct((n, d_model), F32), row(d_model)),
        (jax.ShapeDtypeStruct((n, d_model), F32), row(d_model)),
    ]
    out_shapes = [o[0] for o in outs]
    out_specs = [o[1] for o in outs]
    return pl.pallas_call(
        kern, grid=grid,
        in_specs=[row(d_model), _const_spec((1, d_model)), _const_spec(w_pack.shape), tab, tab, tab],
        out_specs=out_specs, out_shape=out_shapes,
        compiler_params=pltpu.CompilerParams(dimension_semantics=("arbitrary",),
                                             vmem_limit_bytes=VMEM_LIMIT),
        name="in_proj",
    )(x2d, norm_g.reshape(1, d_model), w_pack, cos_t, sin_a, sin_b)


def _s5_readout(y, u, dsk_ref, wglu_ref, bglu_ref, d_model):
    y = y + dsk_ref[...] * u
    cdf = 0.5 * (1.0 + jnp.tanh(math.sqrt(2.0 / math.pi) * (y + 0.044715 * (y * y * y))))
    gl = (y * cdf).astype(BF16)
    z = jnp.dot(gl, wglu_ref[...], preferred_element_type=F32) + bglu_ref[...]
    return z[:, :d_model] * _sigmoid(z[:, d_model:])


def _s5_prompt_kernel(u_ref, perm_ref, bcat_ref, are_ref, aim_ref, ccre_ref, ccim_ref, dsk_ref, wglu_ref,
                      bglu_ref, out_ref, sre_ref, sim_ref, hre, him, zs, cr, ci, *, tc, d_model):
    @pl.when(pl.program_id(1) == 0)
    def _():
        cr[...] = jnp.zeros_like(cr)
        ci[...] = jnp.zeros_like(ci)

    nseq = u_ref.shape[0]
    d_ssm, n_col = u_ref.shape[2], hre.shape[0]
    sd = n_col * LANES
    u = u_ref[...].reshape(nseq * tc, d_ssm)
    ub = jnp.dot(perm_ref[...], u.astype(BF16), preferred_element_type=F32).astype(BF16)
    for c in range(n_col):
        kc = (c * LANES * d_ssm // sd) // LANES
        bu = jnp.dot(ub[:, kc * LANES:(kc + 1) * LANES], bcat_ref[c], preferred_element_type=F32)
        hre[c] = bu[:, :LANES]
        him[c] = bu[:, LANES:]

    for c0 in range(0, n_col, S5_SCAN_COLS):
        cs = range(c0, c0 + S5_SCAN_COLS)
        ar = [jnp.broadcast_to(are_ref[:, c * LANES:(c + 1) * LANES], (nseq, LANES)) for c in cs]
        ai = [jnp.broadcast_to(aim_ref[:, c * LANES:(c + 1) * LANES], (nseq, LANES)) for c in cs]

        def step(t, carry, cs=cs, ar=ar, ai=ai):
            rows = pl.ds(pl.multiple_of(t * nseq, nseq), nseq)
            out = []
            for n, c in enumerate(cs):
                pr, pi = carry[n]
                xr = hre[c, rows, :] + (ar[n] * pr - ai[n] * pi)
                xi = him[c, rows, :] + (ar[n] * pi + ai[n] * pr)
                hre[c, rows, :] = xr
                him[c, rows, :] = xi
                out.append((xr, xi))
            return tuple(out)

        init = tuple((cr[:, c * LANES:(c + 1) * LANES], ci[:, c * LANES:(c + 1) * LANES]) for c in cs)
        fin = lax.fori_loop(0, tc, step, init, unroll=4)
        for n, c in enumerate(cs):
            cr[:, c * LANES:(c + 1) * LANES] = fin[n][0]
            ci[:, c * LANES:(c + 1) * LANES] = fin[n][1]
    sre_ref[...] = cr[...]
    sim_ref[...] = ci[...]
    n_out = d_ssm // LANES
    per = n_col // n_out
    wide = lambda ref, j: jnp.concatenate([ref[c] for c in range(j * per, (j + 1) * per)], axis=1).astype(BF16)
    for j in range(n_out):
        zs[j] = (jnp.dot(wide(hre, j), ccre_ref[j], preferred_element_type=F32)
                 + jnp.dot(wide(him, j), ccim_ref[j], preferred_element_type=F32))
    y = jnp.concatenate(
        [jnp.concatenate([zs[j, pl.ds(b, tc, stride=nseq), :] for b in range(nseq)], axis=0)
         for j in range(n_out)], axis=1)
    out_ref[...] = _s5_readout(y, u, dsk_ref, wglu_ref, bglu_ref, d_model).reshape(nseq, tc, d_model)


def _s5_sample_kernel(u_ref, h0r_ref, h0i_ref, are_ref, aim_ref, bre_ref, bim_ref, cre_ref, cim_ref,
                      dsk_ref, wglu_ref, bglu_ref, out_ref, sre_ref, sim_ref, *, d_model):
    u = u_ref[...]
    ub = u.astype(BF16)
    ar, ai = are_ref[...], aim_ref[...]
    h0r, h0i = h0r_ref[...], h0i_ref[...]
    hr = (ar * h0r - ai * h0i) + jnp.dot(ub, bre_ref[...], preferred_element_type=F32)
    hi = (ar * h0i + ai * h0r) + jnp.dot(ub, bim_ref[...], preferred_element_type=F32)
    sre_ref[...] = hr
    sim_ref[...] = hi
    y = (jnp.dot(hr.astype(BF16), cre_ref[...], preferred_element_type=F32)
         + jnp.dot(hi.astype(BF16), cim_ref[...], preferred_element_type=F32))
    out_ref[...] = _s5_readout(y, u, dsk_ref, wglu_ref, bglu_ref, d_model)


def _s5_params(a_re, a_im, log_dt, b_re, b_im, c_re, c_im):
    g, p = a_re.shape
    a_c = lax.complex(a_re.astype(F32), a_im.astype(F32))
    dt = jnp.exp(log_dt.astype(F32))[:, None]
    a_bar = jnp.exp(a_c * dt)
    b_bar = ((a_bar - 1.0) / a_c)[:, :, None] * lax.complex(b_re.astype(F32), b_im.astype(F32))
    eye = jnp.eye(g, dtype=F32)
    n = b_re.shape[2]

    def bmat(b):
        return jnp.einsum('gpn,gh->gnhp', b, eye).reshape(g * n, g * p)

    def cmat(c):
        return jnp.einsum('gnp,gh->gphn', c, eye).reshape(g * p, g * n)

    bre, bim = bmat(jnp.real(b_bar)).astype(BF16), bmat(jnp.imag(b_bar)).astype(BF16)
    cre, cim = cmat(c_re.astype(F32)).astype(BF16), cmat(-c_im.astype(F32)).astype(BF16)
    d_in, sd = bre.shape
    assert d_in % LANES == 0 and sd % LANES == 0 and LANES % (LANES * d_in // sd) == 0
    bcat = jnp.stack([
        jnp.concatenate([m[(c * LANES * d_in // sd) // LANES * LANES:][:LANES, c * LANES:(c + 1) * LANES]
                         for m in (bre, bim)], axis=1)
        for c in range(sd // LANES)], axis=0)
    n_out = d_in // LANES
    span = sd // n_out
    ccre, ccim = (jnp.stack([m[j * span:(j + 1) * span, j * LANES:(j + 1) * LANES] for j in range(n_out)],
                            axis=0) for m in (cre, cim))
    a1 = a_bar.reshape(1, g * p)
    return jnp.real(a1), jnp.imag(a1), bre, bim, cre, cim, bcat, ccre, ccim


def _s5_prompt(u3, s5p, dsk, wglu, bglu, tc):
    bsz, t_len, d_ssm = u3.shape
    are, aim, bre, _, _, _, bcat, ccre, ccim = s5p
    sd = bre.shape[1]
    d_model = wglu.shape[1] // 2
    nseq = math.gcd(bsz, S5_SEQS)
    assert sd % (S5_SCAN_COLS * LANES) == 0 and tc % SUBLANES == 0
    kern = functools.partial(_s5_prompt_kernel, tc=tc, d_model=d_model)
    state = pl.BlockSpec((nseq, sd), lambda b, c: (b, 0))
    dst = jnp.arange(nseq * tc)
    perm = (jnp.arange(nseq * tc)[None, :] == ((dst % nseq) * tc + dst // nseq)[:, None]).astype(BF16)
    return pl.pallas_call(
        kern, grid=(bsz // nseq, t_len // tc),
        in_specs=[pl.BlockSpec((nseq, tc, d_ssm), lambda b, c: (b, c, 0)), _const_spec(perm.shape),
                  _const_spec(bcat.shape), _const_spec(are.shape), _const_spec(aim.shape),
                  _const_spec(ccre.shape), _const_spec(ccim.shape), _const_spec((1, d_ssm)),
                  _const_spec(wglu.shape), _const_spec((1, 2 * d_model))],
        out_specs=[pl.BlockSpec((nseq, tc, d_model), lambda b, c: (b, c, 0)), state, state],
        out_shape=[jax.ShapeDtypeStruct((bsz, t_len, d_model), F32),
                   jax.ShapeDtypeStruct((bsz, sd), F32),
                   jax.ShapeDtypeStruct((bsz, sd), F32)],
        scratch_shapes=[pltpu.VMEM((sd // LANES, nseq * tc, LANES), F32),
                        pltpu.VMEM((sd // LANES, nseq * tc, LANES), F32),
                        pltpu.VMEM((d_ssm // LANES, nseq * tc, LANES), F32),
                        pltpu.VMEM((nseq, sd), F32), pltpu.VMEM((nseq, sd), F32)],
        compiler_params=pltpu.CompilerParams(dimension_semantics=("arbitrary", "arbitrary"),
                                             vmem_limit_bytes=VMEM_LIMIT),
        name="s5_prompt",
    )(u3, perm, bcat, are, aim, ccre, ccim, dsk.reshape(1, d_ssm), wglu, bglu.reshape(1, 2 * d_model))


def _s5_sample(u2, h0r, h0i, s5p, dsk, wglu, bglu):
    n, d_ssm = u2.shape
    are, aim, bre, bim, cre, cim = s5p[:6]
    sd = bre.shape[1]
    d_model = wglu.shape[1] // 2
    kern = functools.partial(_s5_sample_kernel, d_model=d_model)
    return pl.pallas_call(
        kern,
        out_shape=[jax.ShapeDtypeStruct((n, d_model), F32),
                   jax.ShapeDtypeStruct((n, sd), F32),
                   jax.ShapeDtypeStruct((n, sd), F32)],
        compiler_params=pltpu.CompilerParams(vmem_limit_bytes=VMEM_LIMIT),
        name="s5_sample",
    )(u2, h0r, h0i, are, aim, bre, bim, cre, cim, dsk.reshape(1, d_ssm), wglu,
      bglu.reshape(1, 2 * d_model))


def _f2k(x):
    b = lax.bitcast_convert_type(x, I32)
    return jnp.where(b < 0, b ^ 0x7FFFFFFF, b)


def _k2f(k):
    return lax.bitcast_convert_type(jnp.where(k < 0, k ^ 0x7FFFFFFF, k), F32)


def _threshold_search(count_ge, amax, forced, k_top):
    kf = float(k_top)
    lo0 = jnp.where(forced, KEY_LOW, _f2k(-amax))
    hi0 = jnp.where(forced, KEY_LOW + 1, _f2k(amax) + 1)

    def cond(st):
        lo, hi, _ = st
        return jnp.max(jnp.where(hi > lo + 1, 1.0, 0.0)) > 0.5

    def body(st):
        lo, hi, it = st
        active = hi > lo + 1
        mid_i = (lo >> 1) + (hi >> 1) + (lo & hi & 1)
        mk = _f2k(0.5 * _k2f(lo) + 0.5 * _k2f(hi))
        use_f = jnp.logical_and(it % 2 == 0, jnp.logical_and(mk > lo, mk < hi))
        mid = jnp.where(use_f, mk, mid_i)
        c = count_ge(_k2f(mid))
        ge = c >= kf
        nlo = jnp.where(ge, mid, lo)
        nhi = jnp.where(c == kf, mid + 1, jnp.where(ge, hi, mid))
        return jnp.where(active, nlo, lo), jnp.where(active, nhi, hi), it + 1

    lo, _, _ = lax.while_loop(cond, body, (lo0, hi0, jnp.int32(0)))
    return _k2f(lo)


def _tri_exclusive(n):
    r = lax.broadcasted_iota(I32, (n, n), 0)
    c = lax.broadcasted_iota(I32, (n, n), 1)
    return (r < c).astype(BF16)


def _pipelined_blocks(n, produce, consume, buf_a, buf_b, carry):
    produce(0, buf_a)

    def pair(t, c):
        produce(2 * t + 1, buf_b)
        c = consume(2 * t, buf_a, c)
        produce(2 * t + 2, buf_a)
        return consume(2 * t + 1, buf_b, c)

    carry = lax.fori_loop(0, n // 2, pair, carry)
    return lax.cond(n % 2 == 1, lambda c: consume(n - 1, buf_a, c), lambda c: c, carry)


def _dsa_prompt_kernel(qis_ref, wt_ref, qs_ref, ki_ref, k_ref, vt_ref, o_ref,
                       sc_ref, scb_ref, m_ref, acc_ref, sa_ref, sb_ref, *, tq, tk, k_top):
    i = pl.program_id(1)
    nkb = i + 1
    hpg = N_HEADS // N_KV_HEADS
    ngr = tk // SUBLANES
    kf = float(k_top)

    w8 = wt_ref[...] * (IDX_DIM ** -0.5)
    qpos = i * tq + lax.broadcasted_iota(I32, (SUBLANES, tq), 1)

    def idx_dots(j, dst):
        k0 = pl.multiple_of(jnp.minimum(j, nkb - 1) * tk, tk)
        dst[...] = lax.dot_general(ki_ref[pl.ds(k0, tk), :], qis_ref[...], NT_DIMS, preferred_element_type=F32)

    def score_blk(j, src, carry, diagonal=False):
        for c in range(tq // LANES):
            acc = jnp.zeros((tk, LANES), F32)
            for h in range(N_IDX_HEADS):
                cols = slice(h * tq + c * LANES, h * tq + (c + 1) * LANES)
                acc = acc + jnp.maximum(src[:, cols], 0.0) * w8[h:h + 1, c * LANES:(c + 1) * LANES]
            if diagonal:
                kpos = j * tk + lax.broadcasted_iota(I32, (tk, LANES), 0)
                causal = kpos <= i * tq + c * LANES + lax.broadcasted_iota(I32, (tk, LANES), 1)
                acc = jnp.where(causal, acc, NEG_INF)
            sc_ref[j, :, c * LANES:(c + 1) * LANES] = acc
            top = lax.bitcast_convert_type(lax.bitcast_convert_type(acc, I32) & jnp.int32(-65536), F32)
            scb_ref[j, :, c * LANES:(c + 1) * LANES] = top.astype(BF16)
        return carry

    _pipelined_blocks(nkb - 1, idx_dots, score_blk, sa_ref, sb_ref, jnp.int32(0))
    idx_dots(nkb - 1, sa_ref)
    score_blk(nkb - 1, sa_ref, 0, diagonal=True)

    rep = lambda x: jnp.broadcast_to(x, (SUBLANES, tq))
    forced = (qpos + 1) <= k_top

    packed = 2 * SUBLANES
    one_b, zero_b = jnp.ones((), BF16), jnp.zeros((), BF16)

    def count16(k16):
        bits = jnp.where(k16 < 0, k16 ^ 0x7FFF, k16) << 16
        tb = jnp.broadcast_to(lax.bitcast_convert_type(bits, F32)[0:1, :], (packed, tq)).astype(BF16)

        def body(j, part):
            blk = scb_ref[j].reshape(tk // packed // COUNT_WAYS, COUNT_WAYS, packed, tq)
            hit = jnp.where(blk >= tb[None, None], one_b, zero_b)
            for g in range(hit.shape[0]):
                part = part + hit[g]
            return part

        part = lax.fori_loop(0, nkb, body, jnp.zeros((COUNT_WAYS, packed, tq), BF16))
        return rep(jnp.sum(jnp.sum(part.astype(F32), axis=0), axis=0, keepdims=True))

    def coarse_pass(lo16, hi16, c_lo):
        active = hi16 > lo16 + 1
        mid = lo16 + ((hi16 - lo16) >> 1)
        c = count16(mid)
        up = jnp.logical_and(active, c >= kf)
        nhi = jnp.where(c == kf, mid, jnp.where(c >= kf, hi16, mid))
        return jnp.where(up, mid, lo16), jnp.where(active, nhi, hi16), jnp.where(up, c, c_lo)

    lo16 = jnp.full((SUBLANES, tq), KEY_LOW >> 16, I32)
    hi16 = jnp.where(forced, KEY_LOW >> 16, KEY_INF >> 16)
    c_lo = (qpos + 1).astype(F32)
    lo16, hi16, c_lo = lax.fori_loop(0, 8, lambda _, st: coarse_pass(*coarse_pass(*st)), (lo16, hi16, c_lo))
    lo0 = lo16 << 16
    hi0 = jnp.where(hi16 == lo16, lo0 + 1, hi16 << 16)

    def count(thr, strict):
        def body(j, part):
            blk = sc_ref[j].reshape(ngr // COUNT_WAYS, COUNT_WAYS, SUBLANES, tq)
            hit = (blk > thr[None, None]) if strict else (blk >= thr[None, None])
            return part + jnp.sum(jnp.where(hit, 1.0, 0.0), axis=0)

        part = lax.fori_loop(0, nkb, body, jnp.zeros((COUNT_WAYS, SUBLANES, tq), F32))
        return rep(jnp.sum(jnp.sum(part, axis=0), axis=0, keepdims=True))

    def search_pass(lo, hi, c_lo, value_space, probe_ends=False):
        active = hi > lo + 1
        span = hi - lo
        mid = lo + lax.shift_right_logical(span, jnp.ones_like(span))
        if value_space:
            mk = _f2k(0.5 * _k2f(lo) + 0.5 * _k2f(hi))
            mid = jnp.where(jnp.logical_and(mk > lo, mk < hi), mk, mid)
        if probe_ends:
            near_zero = jnp.logical_and(lo >= 0, hi <= KEY_MIN_NORMAL)
            mid = jnp.where(near_zero, lo + 1 if value_space else hi - 1, mid)
        c = count(_k2f(mid), False)
        up = jnp.logical_and(active, c >= kf)
        nhi = jnp.where(c == kf, mid + 1, jnp.where(c >= kf, hi, mid))
        return jnp.where(up, mid, lo), jnp.where(active, nhi, hi), jnp.where(up, c, c_lo)

    def double_round(lo, hi, c_lo, probe_ends=False):
        return search_pass(*search_pass(lo, hi, c_lo, True, probe_ends), False, probe_ends)

    def unresolved(lo, hi):
        return (jnp.max(jnp.where(hi > lo + 1, 1.0, 0.0)) > 0.5).astype(I32)

    def search_body(st):
        lo, hi, c_lo = double_round(*st[:3])
        return lo, hi, c_lo, unresolved(lo, hi)

    st = double_round(lo0, hi0, c_lo, probe_ends=True)
    lo, hi, c_lo = lax.fori_loop(0, FINE_ROUNDS_UNCHECKED - 1, lambda _, s: double_round(*s), st)
    lo, _, c_ge, _ = lax.while_loop(lambda s: s[3] > 0, search_body, (lo, hi, c_lo, unresolved(lo, hi)))
    thr = _k2f(lo)
    thr1 = thr[0:1, :]

    @pl.when(jnp.max(jnp.where(c_ge > kf, 1.0, 0.0)) > 0.5)
    def _():
        need = jnp.where(c_ge > kf, kf - count(thr, True), 1e30)[0:1, :]
        r = lax.broadcasted_iota(I32, (tk, tk), 0)
        c = lax.broadcasted_iota(I32, (tk, tk), 1)
        earlier = (c < r).astype(BF16)
        ones_k = jnp.ones((SUBLANES, tk), BF16)

        def fix(j, run):
            sc = sc_ref[j]
            eq = sc == thr1
            eqb = jnp.where(eq, 1.0, 0.0).astype(BF16)
            before = jnp.dot(earlier, eqb, preferred_element_type=F32) + run[0:1, :]
            sc_ref[j] = jnp.where(jnp.logical_and(eq, before >= need), NEG_INF, sc)
            return run + jnp.dot(ones_k, eqb, preferred_element_type=F32)

        lax.fori_loop(0, nkb, fix, jnp.zeros((SUBLANES, tq), F32))

    m_ref[...] = jnp.full(m_ref.shape, NEG_INF, F32)
    acc_ref[...] = jnp.zeros(acc_ref.shape, F32)

    def logits(j, dst):
        jc = jnp.minimum(j, nkb - 1)
        k0 = pl.multiple_of(jc * tk, tk)
        dst[...] = lax.dot_general(k_ref[pl.ds(k0, tk), :], qs_ref[...], NT_DIMS, preferred_element_type=F32)

    def softmax_pv(j, src):
        bias = jnp.where(sc_ref[j] >= thr1, 0.0, NEG_INF)
        for h in range(N_HEADS):
            x = src[:, h * tq:(h + 1) * tq] + bias
            bm = jnp.max(jnp.max(x.reshape(ngr, SUBLANES, tq), axis=0), axis=0, keepdims=True)
            m_old = m_ref[h:h + 1, :]
            m_new = jnp.maximum(m_old, bm)
            m_safe = jnp.where(m_new == NEG_INF, 0.0, m_new)
            p = jnp.exp2(x - m_safe).astype(BF16)
            pv = jnp.dot(vt_ref[j, h // hpg], p, preferred_element_type=F32)
            acc_ref[h] = acc_ref[h] * jnp.exp2(m_old - m_safe) + pv
            m_ref[h:h + 1, :] = m_new

    def consume(j, src, carry):
        softmax_pv(j, src)
        return carry

    _pipelined_blocks(nkb, logits, consume, sa_ref, sb_ref, jnp.int32(0))

    for pr in range(N_HEADS // 2):
        outs = []
        for h in (2 * pr, 2 * pr + 1):
            a = acc_ref[h]
            outs.append(a[:HEAD_DIM, :] / a[HEAD_DIM:HEAD_DIM + 1, :])
        o2 = jnp.concatenate(outs, axis=0)
        o_ref[:, 2 * pr * HEAD_DIM:(2 * pr + 2) * HEAD_DIM] = o2.T.astype(o_ref.dtype)


def _dsa_prompt(qis, wt, qs, kib, kb, vt, tq, tk):
    bsz, t_len, d_kv = kb.shape
    d_attn = N_HEADS * HEAD_DIM
    nkb = t_len // tk
    k_top = min(TOPK_MAX, t_len // 4)
    assert tq == tk and tq % LANES == 0 and t_len % tq == 0 and vt.shape[1] == nkb
    assert qis.shape == (bsz, nkb, N_IDX_HEADS * tq, IDX_DIM) and qs.shape == (bsz, nkb, N_HEADS * tq, d_kv)
    assert nkb * (tk // (2 * SUBLANES * COUNT_WAYS)) <= 256
    kern = functools.partial(_dsa_prompt_kernel, tq=tq, tk=tk, k_top=k_top)
    tile = lambda w: pl.BlockSpec((None, tq, w), lambda b, i: (b, i, 0))
    full = lambda w: pl.BlockSpec((None, t_len, w), lambda b, i: (b, 0, 0))
    stacked = lambda a: pl.BlockSpec((None, None) + a.shape[2:], lambda b, i: (b, i, 0, 0))
    return pl.pallas_call(
        kern, grid=(bsz, t_len // tq),
        in_specs=[stacked(qis), pl.BlockSpec((None, N_IDX_HEADS, tq), lambda b, i: (b, 0, i)), stacked(qs),
                  full(IDX_DIM), full(d_kv),
                  pl.BlockSpec((None, nkb, N_KV_HEADS, VT_ROWS, tk), lambda b, i: (b, 0, 0, 0, 0))],
        out_specs=tile(d_attn),
        out_shape=jax.ShapeDtypeStruct((bsz, t_len, d_attn), BF16),
        scratch_shapes=[pltpu.VMEM((nkb, tk, tq), F32),
                        pltpu.VMEM((nkb, tk, tq), BF16),
                        pltpu.VMEM((N_HEADS, tq), F32),
                        pltpu.VMEM((N_HEADS, VT_ROWS, tq), F32),
                        pltpu.VMEM((tk, N_HEADS * tq), F32),
                        pltpu.VMEM((tk, N_HEADS * tq), F32)],
        compiler_params=pltpu.CompilerParams(dimension_semantics=("arbitrary", "arbitrary"),
                                             vmem_limit_bytes=VMEM_LIMIT),
        name="dsa_prompt",
    )(qis, wt, qs, kib, kb, vt)


def _page_fetcher(pt_ref, n_pages, streams):
    def copies(seq, slot):
        for p in range(n_pages):
            pg = pt_ref[seq * n_pages + p]
            for hbm, buf, sem in streams:
                page = hbm.shape[2]
                yield pltpu.make_async_copy(hbm.at[pg], buf.at[slot, :, pl.ds(p * page, page)], sem.at[slot])

    def fetch(seq, slot):
        for cp in copies(seq, slot):
            cp.start()

    def wait(seq, slot):
        for cp in copies(seq, slot):
            cp.wait()

    return fetch, wait


def _prefetch_next_and_wait(fetch, wait):
    b = pl.program_id(0)
    slot = b % 2

    @pl.when(b == 0)
    def _():
        fetch(0, 0)

    @pl.when(b + 1 < pl.num_programs(0))
    def _():
        fetch(b + 1, 1 - slot)

    wait(b, slot)
    return b, slot


def _idx_score_kernel(pt_ref, q_ref, w_ref, ci_ref, o_ref, ibuf, sem, *, n_pages):
    b, slot = _prefetch_next_and_wait(*_page_fetcher(pt_ref, n_pages, [(ci_ref, ibuf, sem)]))
    q = q_ref[...]
    w = w_ref[...] * (IDX_DIM ** -0.5)
    d = jnp.dot(q, ibuf[slot].astype(BF16), preferred_element_type=F32)
    o_ref[pl.ds(b % SUBLANES, 1), :] = jnp.sum(jnp.maximum(d, 0.0) * w, axis=0, keepdims=True)


def _idx_scores(page_table, qis, wcol, cache_idx_kt):
    sq, n_pages = page_table.shape
    page = cache_idx_kt.shape[2]
    past = n_pages * page
    kern = functools.partial(_idx_score_kernel, n_pages=n_pages)
    grid_spec = pltpu.PrefetchScalarGridSpec(
        num_scalar_prefetch=1, grid=(sq,),
        in_specs=[pl.BlockSpec((None, N_IDX_HEADS, IDX_DIM), lambda b, pt: (b, 0, 0)),
                  pl.BlockSpec((None, N_IDX_HEADS, 1), lambda b, pt: (b, 0, 0)),
                  pl.BlockSpec(memory_space=pl.ANY)],
        out_specs=pl.BlockSpec((SUBLANES, past), lambda b, pt: (b // SUBLANES, 0)),
        scratch_shapes=[pltpu.VMEM((2, IDX_DIM, past), cache_idx_kt.dtype), pltpu.SemaphoreType.DMA((2,))])
    return pl.pallas_call(
        kern, grid_spec=grid_spec,
        out_shape=jax.ShapeDtypeStruct((sq, past), F32),
        compiler_params=pltpu.CompilerParams(dimension_semantics=("arbitrary",),
                                             vmem_limit_bytes=VMEM_LIMIT),
        name="sample_idx_scores",
    )(page_table.reshape(-1), qis, wcol, cache_idx_kt)


def _sample_select_kernel(sc_ref, qi_ref, ki_ref, wi_ref, mask_ref, selfsel_ref, *, ch, k_top):
    sq, l_past = sc_ref.shape
    nch = l_past // ch
    w = wi_ref[...] * (IDX_DIM ** -0.5)
    qf = qi_ref[...].astype(F32)
    kf32 = ki_ref[...].astype(F32)
    s_self = jnp.zeros((sq, 1), F32)
    for h in range(N_IDX_HEADS):
        dh = jnp.sum(qf[:, h * IDX_DIM:(h + 1) * IDX_DIM] * kf32, axis=1, keepdims=True)
        s_self = s_self + jnp.maximum(dh, 0.0) * w[:, h:h + 1]

    amax = jnp.abs(s_self)
    for c in range(nch):
        amax = jnp.maximum(amax, jnp.max(jnp.abs(sc_ref[:, c * ch:(c + 1) * ch]), axis=1, keepdims=True))

    def count_cmp(thr, strict):
        thr_b = jnp.broadcast_to(thr, (sq, LANES))
        part = jnp.zeros((sq, LANES), F32)
        for c in range(l_past // LANES):
            blk = sc_ref[:, c * LANES:(c + 1) * LANES]
            hit = (blk > thr_b) if strict else (blk >= thr_b)
            part = part + jnp.where(hit, 1.0, 0.0)
        self_hit = (s_self > thr) if strict else (s_self >= thr)
        return jnp.sum(part, axis=1, keepdims=True) + jnp.where(self_hit, 1.0, 0.0)

    kf = float(k_top)
    forced = amax < 0.0
    thr = _threshold_search(lambda t: count_cmp(t, False), amax, forced, k_top)
    need = kf - count_cmp(thr, True)
    tri = _tri_exclusive(ch)
    run = jnp.zeros((sq, 1), F32)
    for c in range(nch):
        sc = sc_ref[:, c * ch:(c + 1) * ch]
        eq = sc == thr
        before = jnp.dot(jnp.where(eq, 1.0, 0.0).astype(BF16), tri, preferred_element_type=F32) + run
        keep = jnp.logical_or(sc > thr, jnp.logical_and(eq, before < need))
        mask_ref[:, c * ch:(c + 1) * ch] = jnp.where(keep, 1.0, 0.0)
        run = run + jnp.sum(jnp.where(eq, 1.0, 0.0), axis=1, keepdims=True)
    self_keep = jnp.logical_or(s_self > thr, jnp.logical_and(s_self == thr, run < need))
    selfsel_ref[...] = jnp.where(self_keep, 1.0, 0.0)


def _sample_select(scores, qi, kib, wi, ch, k_top):
    sq, l_past = scores.shape
    kern = functools.partial(_sample_select_kernel, ch=ch, k_top=k_top)
    return pl.pallas_call(
        kern,
        out_shape=[jax.ShapeDtypeStruct((sq, l_past), F32), jax.ShapeDtypeStruct((sq, 1), F32)],
        compiler_params=pltpu.CompilerParams(vmem_limit_bytes=VMEM_LIMIT),
        name="sample_select",
    )(scores, qi, kib, wi)


def _sample_attn_kernel(pt_ref, q_ref, mask_ref, ks_ref, vs_ref, ss_ref, ck_ref, cv_ref, o_ref,
                        kbuf, vbuf, ksem, vsem, *, n_pages):
    _, slot = _prefetch_next_and_wait(
        *_page_fetcher(pt_ref, n_pages, [(ck_ref, kbuf, ksem), (cv_ref, vbuf, vsem)]))
    hpg = N_HEADS // N_KV_HEADS
    q = q_ref[...]
    s = jnp.dot(q, kbuf[slot].astype(BF16), preferred_element_type=F32)
    sm = jnp.where(mask_ref[...] > 0.5, s, NEG_INF)
    s_self = jnp.sum(q.astype(F32) * ks_ref[...].astype(F32), axis=1, keepdims=True)
    s_self = jnp.where(ss_ref[...] > 0.5, s_self, NEG_INF)
    m = jnp.maximum(jnp.max(sm, axis=1, keepdims=True), s_self)
    m = jnp.where(m == NEG_INF, 0.0, m)
    pb = jnp.exp2(sm - m).astype(BF16)
    p_self = jnp.exp2(s_self - m).astype(BF16).astype(F32)
    l_sum = jnp.sum(pb.astype(F32), axis=1, keepdims=True) + p_self
    acc = (lax.dot_general(pb, vbuf[slot].astype(BF16), NT_DIMS, preferred_element_type=F32)
           + p_self * vs_ref[...].astype(F32))
    o = acc / l_sum
    hrow = lax.broadcasted_iota(I32, o.shape, 0)
    o = jnp.where(hrow < hpg, o, pltpu.roll(o, HEAD_DIM, axis=1))
    o_ref[...] = o[:, :HEAD_DIM].astype(o_ref.dtype)


def _sample_attn(page_table, qs, mask3, kself, vself, selfsel, cache_k2, cache_v2):
    sq, n_pages = page_table.shape
    d_kv, page = cache_k2.shape[1], cache_k2.shape[2]
    past = n_pages * page
    kern = functools.partial(_sample_attn_kernel, n_pages=n_pages)
    per_seq = lambda r, w: pl.BlockSpec((None, r, w), lambda b, pt: (b, 0, 0))
    hbm = pl.BlockSpec(memory_space=pl.ANY)
    grid_spec = pltpu.PrefetchScalarGridSpec(
        num_scalar_prefetch=1, grid=(sq,),
        in_specs=[per_seq(N_HEADS, d_kv), per_seq(1, past), per_seq(1, d_kv), per_seq(1, d_kv),
                  per_seq(1, 1), hbm, hbm],
        out_specs=per_seq(N_HEADS, HEAD_DIM),
        scratch_shapes=[pltpu.VMEM((2, d_kv, past), cache_k2.dtype), pltpu.VMEM((2, d_kv, past), cache_v2.dtype),
                        pltpu.SemaphoreType.DMA((2,)), pltpu.SemaphoreType.DMA((2,))])
    return pl.pallas_call(
        kern, grid_spec=grid_spec,
        out_shape=jax.ShapeDtypeStruct((sq, N_HEADS, HEAD_DIM), BF16),
        compiler_params=pltpu.CompilerParams(dimension_semantics=("arbitrary",),
                                             vmem_limit_bytes=VMEM_LIMIT),
        name="sample_attn",
    )(page_table.reshape(-1), qs, mask3, kself, vself, selfsel, cache_k2, cache_v2)


def _tail_kernel(x_ref, attn_ref, ssm_ref, sgs_ref, sga_ref, wao_ref, wo_ref, g2_ref, wup_ref, wdn_ref,
                 gf_ref, y_ref):
    attn_out = jnp.dot(attn_ref[...], wao_ref[...], preferred_element_type=F32)
    mix = sgs_ref[...] * ssm_ref[...] + sga_ref[...] * attn_out
    x1 = x_ref[...] + jnp.dot(mix.astype(BF16), wo_ref[...], preferred_element_type=F32)
    hh = _rms_norm(x1, g2_ref[...]).astype(BF16)
    up = jnp.dot(hh, wup_ref[...], preferred_element_type=F32)
    r = jnp.square(jnp.maximum(up, 0.0)).astype(BF16)
    x2 = x1 + jnp.dot(r, wdn_ref[...], preferred_element_type=F32)
    y_ref[...] = _rms_norm(x2, gf_ref[...])


def _tail(x2d, attn, ssm, sgs, sga, wao, wo, g2, wup, wdn, gf, tm):
    n, d_model = x2d.shape
    row = lambda w: pl.BlockSpec((tm, w), lambda i: (i, 0))
    return pl.pallas_call(
        _tail_kernel, grid=(n // tm,),
        in_specs=[row(d_model), row(attn.shape[1]), row(d_model), row(d_model), row(d_model),
                  _const_spec(wao.shape), _const_spec(wo.shape), _const_spec((1, d_model)),
                  _const_spec(wup.shape), _const_spec(wdn.shape), _const_spec((1, d_model))],
        out_specs=row(d_model),
        out_shape=jax.ShapeDtypeStruct((n, d_model), F32),
        compiler_params=pltpu.CompilerParams(dimension_semantics=("arbitrary",),
                                             vmem_limit_bytes=VMEM_LIMIT),
        name="tail",
    )(x2d, attn, ssm, sgs, sga, wao, wo, g2.reshape(1, d_model), wup, wdn, gf.reshape(1, d_model))


def _tiles(n_rows, t_len):
    tm = min(256, n_rows)
    tc = min(128, t_len)
    tq = min(256, t_len)
    return tm, tc, tq, tq


def kernel(x_prompt, x_sample, cache_k, cache_v, cache_idx_k, state_ssm_re, state_ssm_im, page_table,
           norm1_g, w_in, ssm_a_re, ssm_a_im, ssm_log_dt, ssm_b_re, ssm_b_im, ssm_c_re, ssm_c_im,
           ssm_d, w_glu, b_glu, w_attn_out, w_o, norm2_g, w_up, w_down, normf_g):
    bsz, t_len, d_model = x_prompt.shape
    sq, s_len, _ = x_sample.shape
    assert s_len == 1, "the sample path handles one new token per sequence"
    n_pool, page = cache_idx_k.shape[0], cache_idx_k.shape[1]
    n_pages = page_table.shape[1]
    past = n_pages * page
    d_ssm = ssm_d.shape[0]
    d_attn = N_HEADS * HEAD_DIM
    d_kv = N_KV_HEADS * HEAD_DIM
    d_qi = N_IDX_HEADS * IDX_DIM
    dims = (d_ssm, d_attn, d_kv, d_qi)

    c_ki = d_ssm + d_attn + 2 * d_kv + d_qi
    c_g = c_ki + IDX_DIM + N_IDX_HEADS
    w_pack = jnp.concatenate(
        [w_in[:, :c_g], jnp.zeros((d_model, LANES - IDX_DIM - N_IDX_HEADS), w_in.dtype), w_in[:, c_g:]],
        axis=1).astype(BF16)
    wglu_b, wao_b, wo_b = w_glu.astype(BF16), w_attn_out.astype(BF16), w_o.astype(BF16)
    wup_b, wdn_b = w_up.astype(BF16), w_down.astype(BF16)
    s5p = _s5_params(ssm_a_re, ssm_a_im, ssm_log_dt, ssm_b_re, ssm_b_im, ssm_c_re, ssm_c_im)

    n_p = bsz * t_len
    tm, tc, tq, tk = _tiles(n_p, t_len)
    xp = x_prompt.reshape(n_p, d_model)
    tabs_p = _rope_tables(jnp.arange(t_len, dtype=I32), t_len)
    assert tm == tk
    (u, q, kt, kb, vt, _, vtb, qi, kit, kib, _, wit, sgs, sga) = _in_proj(
        xp, bsz, tabs_p, norm1_g, w_pack, tm, dims)
    ssm_out, re_p, im_p = _s5_prompt(u.reshape(bsz, t_len, d_ssm), s5p, ssm_d, wglu_b, b_glu, tc)
    r3 = lambda a: a.reshape(bsz, t_len, a.shape[-1])
    attn = _dsa_prompt(qi, wit, q, r3(kib), r3(kb), vtb, tq, tk)
    y_p = _tail(xp, attn.reshape(n_p, d_attn), ssm_out.reshape(n_p, d_model), sgs, sga,
                wao_b, wo_b, norm2_g, wup_b, wdn_b, normf_g, tm)

    xs = x_sample.reshape(sq, d_model)
    tabs_s = _rope_tables(jnp.full((sq,), past, I32), sq)
    (u_s, q_s, kt_s, kb_s, vt_s, vb_s, _, qi_s, kit_s, kib_s, wi_s, _, sgs_s, sga_s) = _in_proj(
        xs, 1, tabs_s, norm1_g, w_pack, sq, dims)
    sd = state_ssm_re.shape[1] * state_ssm_re.shape[2]
    ssm_s, re_s, im_s = _s5_sample(u_s, state_ssm_re.reshape(sq, sd), state_ssm_im.reshape(sq, sd),
                                   s5p, ssm_d, wglu_b, b_glu)
    per_seq = lambda a, heads: jnp.swapaxes(a.reshape(heads, sq, a.shape[-1]), 0, 1)
    qih_s = per_seq(qi_s, N_IDX_HEADS)
    scores = _idx_scores(page_table, qih_s, wi_s.reshape(sq, N_IDX_HEADS, 1),
                         jnp.transpose(cache_idx_k, (0, 2, 1)))
    k_top_s = min(TOPK_MAX, (past + s_len) // 4)
    mask, selfsel = _sample_select(scores, qih_s.reshape(sq, d_qi), kib_s, wi_s, min(512, past), k_top_s)
    qs_pad = per_seq(q_s, N_HEADS)
    feat_major = lambda c: jnp.transpose(c, (0, 2, 3, 1)).reshape(n_pool, d_kv, page)
    attn_s = _sample_attn(page_table, qs_pad, mask.reshape(sq, 1, past),
                          kb_s.reshape(sq, 1, d_kv), vb_s.reshape(sq, 1, d_kv), selfsel.reshape(sq, 1, 1),
                          feat_major(cache_k), feat_major(cache_v))
    y_s = _tail(xs, attn_s.reshape(sq, d_attn), ssm_s, sgs_s, sga_s,
                wao_b, wo_b, norm2_g, wup_b, wdn_b, normf_g, sq)

    g_ssm, p_ssm = state_ssm_re.shape[1], state_ssm_re.shape[2]
    kv_out = lambda a: jnp.transpose(a.reshape(a.shape[0], N_KV_HEADS, HEAD_DIM, a.shape[2]), (0, 3, 1, 2))
    return (y_p.reshape(bsz, t_len, d_model), y_s.reshape(sq, s_len, d_model),
            kv_out(kt), kv_out(vt), jnp.transpose(kit, (0, 2, 1)),
            re_p.reshape(bsz, g_ssm, p_ssm), im_p.reshape(bsz, g_ssm, p_ssm),
            kv_out(kt_s).reshape(sq, s_len, N_KV_HEADS, HEAD_DIM),
            kv_out(vt_s).reshape(sq, s_len, N_KV_HEADS, HEAD_DIM),
            jnp.transpose(kit_s, (0, 2, 1)).reshape(sq, s_len, IDX_DIM),
            re_s.reshape(sq, g_ssm, p_ssm), im_s.reshape(sq, g_ssm, p_ssm))
```

```python
import functools
import math

import jax
import jax.numpy as jnp
from jax import lax
from jax.experimental import pallas as pl
from jax.experimental.pallas import tpu as pltpu

F32 = jnp.float32
BF16 = jnp.bfloat16
I32 = jnp.int32

N_HEADS = 8
N_KV_HEADS = 2
HEAD_DIM = 64
ROT_DIM = HEAD_DIM // 4
N_IDX_HEADS = 8
IDX_DIM = 64
IDX_ROT_DIM = IDX_DIM // 4
assert (IDX_DIM, IDX_ROT_DIM) == (HEAD_DIM, ROT_DIM)
ROPE_THETA = 500000.0
TOPK_MAX = 256
EPS = 1e-6
LOG2_E = math.log2(math.e)

LANES = 128
SUBLANES = 8
MXU_COLS = 256
S5_SEQS = 8
S5_SCAN_COLS = 4
COUNT_WAYS = 4
FINE_ROUNDS_UNCHECKED = 3
VT_ROWS = 80
VMEM_LIMIT = 56 * 1024 * 1024

NEG_INF = float("-inf")
KEY_LOW = -2139095040
KEY_INF = 0x7F800000
KEY_MIN_NORMAL = 0x00800000
NT_DIMS = (((1,), (1,)), ((), ()))


def _const_spec(shape):
    nd = len(shape)
    return pl.BlockSpec(shape, lambda *_: (0,) * nd, pipeline_mode=pl.Buffered(1))


def _rms_norm(x, g):
    ms = jnp.mean(x * x, axis=-1, keepdims=True)
    return x * lax.rsqrt(ms + EPS) * g


def _sigmoid(x):
    return 1.0 / (1.0 + jnp.exp(-x))


def _rope(x, cos_t, sin_a, sin_b):
    return (x * cos_t + pltpu.roll(x, LANES - ROT_DIM // 2, axis=1) * sin_a
            + pltpu.roll(x, ROT_DIM // 2, axis=1) * sin_b)


def _in_proj_kernel(x_ref, g_ref, w_ref, cos_ref, sa_ref, sb_ref,
                    u_ref, q_ref, kt_ref, kb_ref, vt_ref, vb_ref, vtb_ref, qi_ref, kit_ref, kib_ref,
                    wi_ref, wit_ref, sgs_ref, sga_ref, *, d_ssm, d_attn, d_kv, d_qi, d_model):
    assert d_kv == LANES
    tm = x_ref.shape[0]
    h = _rms_norm(x_ref[...], g_ref[...]).astype(BF16)
    cos_t, sin_a, sin_b = cos_ref[...], sa_ref[...], sb_ref[...]

    def proj(c0, width):
        return jnp.dot(h, w_ref[:, c0:c0 + width], preferred_element_type=F32)

    off = 0
    u_ref[...] = proj(off, d_ssm)
    off += d_ssm
    def rope_chunks(c0, width):
        for m in range(width // MXU_COLS):
            wide = proj(c0 + m * MXU_COLS, MXU_COLS)
            for c in range(MXU_COLS // LANES):
                yield (m * MXU_COLS // LANES + c,
                       _rope(wide[:, c * LANES:(c + 1) * LANES], cos_t, sin_a, sin_b))

    lane_q = lax.broadcasted_iota(I32, (tm, LANES), 1)
    for c, r in rope_chunks(off, d_attn):
        rs = r * (HEAD_DIM ** -0.5 * LOG2_E)
        swapped = pltpu.roll(rs, HEAD_DIM, axis=1)
        group = (2 * c) // (N_HEADS // N_KV_HEADS)
        keep = (lane_q < HEAD_DIM) if group == 0 else (lane_q >= HEAD_DIM)
        first, second = (rs, swapped) if group == 0 else (swapped, rs)
        q_ref[2 * c * tm:(2 * c + 1) * tm, :] = jnp.where(keep, first, 0.0).astype(BF16)
        q_ref[(2 * c + 1) * tm:(2 * c + 2) * tm, :] = jnp.where(keep, second, 0.0).astype(BF16)
    off += d_attn
    kv = proj(off, 2 * d_kv)
    r = _rope(kv[:, :d_kv], cos_t, sin_a, sin_b)
    kt_ref[...] = r.T
    kb_ref[...] = r.astype(BF16)
    off += d_kv
    vv = kv[:, d_kv:]
    vt = vv.T
    vt_ref[...] = vt
    vb_ref[...] = vv.astype(BF16)
    sub = lax.broadcasted_iota(I32, (VT_ROWS - HEAD_DIM, tm), 0)
    ones_pad = jnp.where(sub == 0, 1.0, 0.0)
    for g in range(N_KV_HEADS):
        vtb_ref[g] = jnp.concatenate([vt[g * HEAD_DIM:(g + 1) * HEAD_DIM, :], ones_pad], axis=0).astype(BF16)
    off += d_kv
    for c, r in rope_chunks(off, d_qi):
        rb = r.astype(BF16)
        qi_ref[2 * c * tm:(2 * c + 1) * tm, :] = rb[:, :IDX_DIM]
        qi_ref[(2 * c + 1) * tm:(2 * c + 2) * tm, :] = rb[:, IDX_DIM:]
    off += d_qi
    kw = proj(off, LANES)
    lane = lax.broadcasted_iota(I32, kw.shape, 1)
    kr = jnp.where(lane < IDX_DIM, _rope(kw, cos_t, sin_a, sin_b), kw * (N_IDX_HEADS ** -0.5))
    krt = kr.T
    kit_ref[...] = krt[:IDX_DIM, :]
    kib_ref[...] = kr[:, :IDX_DIM].astype(BF16)
    wi_ref[...] = kr[:, IDX_DIM:IDX_DIM + N_IDX_HEADS]
    wit_ref[...] = krt[IDX_DIM:IDX_DIM + N_IDX_HEADS, :]
    off += LANES
    sgs_ref[...] = _sigmoid(proj(off, d_model))
    off += d_model
    sga_ref[...] = _sigmoid(proj(off, d_model))


def _rope_tables(pos, n_rows):
    half = ROT_DIM // 2
    inv = ROPE_THETA ** (-jnp.arange(half, dtype=F32) / half)
    ang = pos.astype(F32)[:, None] * inv[None, :]
    cos, sin = jnp.cos(ang), jnp.sin(ang)
    ones = jnp.ones((n_rows, HEAD_DIM - ROT_DIM), F32)
    zeros = jnp.zeros((n_rows, HEAD_DIM - ROT_DIM), F32)
    zh = jnp.zeros((n_rows, half), F32)
    cos_t = jnp.concatenate([cos, cos, ones], axis=1)
    sin_a = jnp.concatenate([-sin, zh, zeros], axis=1)
    sin_b = jnp.concatenate([zh, sin, zeros], axis=1)
    rep = LANES // HEAD_DIM
    return tuple(jnp.tile(t, (1, rep)) for t in (cos_t, sin_a, sin_b))


def _in_proj(x2d, n_seq, pos_tab, norm_g, w_pack, tm, dims):
    n, d_model = x2d.shape
    d_ssm, d_attn, d_kv, d_qi = dims
    cos_t, sin_a, sin_b = pos_tab
    t_len = n // n_seq
    nt = t_len // tm
    grid = (n // tm,)
    row = lambda w: pl.BlockSpec((tm, w), lambda i: (i, 0))
    tab = pl.BlockSpec((tm, LANES), lambda i: (i % nt, 0))
    feat = lambda r: pl.BlockSpec((None, r, tm), lambda i: (i // nt, 0, i % nt))
    stacked = lambda heads, w: pl.BlockSpec((None, None, heads * tm, w), lambda i: (i // nt, i % nt, 0, 0))
    kern = functools.partial(_in_proj_kernel, d_ssm=d_ssm, d_attn=d_attn, d_kv=d_kv, d_qi=d_qi,
                             d_model=d_model)
    outs = [
        (jax.ShapeDtypeStruct((n, d_ssm), F32), row(d_ssm)),
        (jax.ShapeDtypeStruct((n_seq, nt, N_HEADS * tm, d_kv), BF16), stacked(N_HEADS, d_kv)),
        (jax.ShapeDtypeStruct((n_seq, d_kv, t_len), F32), feat(d_kv)),
        (jax.ShapeDtypeStruct((n, d_kv), BF16), row(d_kv)),
        (jax.ShapeDtypeStruct((n_seq, d_kv, t_len), F32), feat(d_kv)),
        (jax.ShapeDtypeStruct((n, d_kv), BF16), row(d_kv)),
        (jax.ShapeDtypeStruct((n_seq, nt, N_KV_HEADS, VT_ROWS, tm), BF16),
         pl.BlockSpec((None, None, N_KV_HEADS, VT_ROWS, tm), lambda i: (i // nt, i % nt, 0, 0, 0))),
        (jax.ShapeDtypeStruct((n_seq, nt, N_IDX_HEADS * tm, IDX_DIM), BF16), stacked(N_IDX_HEADS, IDX_DIM)),
        (jax.ShapeDtypeStruct((n_seq, IDX_DIM, t_len), F32), feat(IDX_DIM)),
        (jax.ShapeDtypeStruct((n, IDX_DIM), BF16), row(IDX_DIM)),
        (jax.ShapeDtypeStruct((n, N_IDX_HEADS), F32), row(N_IDX_HEADS)),
        (jax.ShapeDtypeStruct((n_seq, N_IDX_HEADS, t_len), F32), feat(N_IDX_HEADS)),
        (jax.ShapeDtypeStruct((n, d_model), F32), row(d_model)),
        (jax.ShapeDtypeStruct((n, d_model), F32), row(d_model)),
    ]
    out_shapes = [o[0] for o in outs]
    out_specs = [o[1] for o in outs]
    return pl.pallas_call(
        kern, grid=grid,
        in_specs=[row(d_model), _const_spec((1, d_model)), _const_spec(w_pack.shape), tab, tab, tab],
        out_specs=out_specs, out_shape=out_shapes,
        compiler_params=pltpu.CompilerParams(dimension_semantics=("arbitrary",),
                                             vmem_limit_bytes=VMEM_LIMIT),
        name="in_proj",
    )(x2d, norm_g.reshape(1, d_model), w_pack, cos_t, sin_a, sin_b)


def _s5_readout(y, u, dsk_ref, wglu_ref, bglu_ref, d_model):
    y = y + dsk_ref[...] * u
    cdf = 0.5 * (1.0 + jnp.tanh(math.sqrt(2.0 / math.pi) * (y + 0.044715 * (y * y * y))))
    gl = (y * cdf).astype(BF16)
    z = jnp.dot(gl, wglu_ref[...], preferred_element_type=F32) + bglu_ref[...]
    return z[:, :d_model] * _sigmoid(z[:, d_model:])


def _s5_prompt_kernel(u_ref, perm_ref, bcat_ref, are_ref, aim_ref, ccre_ref, ccim_ref, dsk_ref, wglu_ref,
                      bglu_ref, out_ref, sre_ref, sim_ref, hre, him, zs, cr, ci, *, tc, d_model):
    @pl.when(pl.program_id(1) == 0)
    def _():
        cr[...] = jnp.zeros_like(cr)
        ci[...] = jnp.zeros_like(ci)

    nseq = u_ref.shape[0]
    d_ssm, n_col = u_ref.shape[2], hre.shape[0]
    sd = n_col * LANES
    u = u_ref[...].reshape(nseq * tc, d_ssm)
    ub = jnp.dot(perm_ref[...], u.astype(BF16), preferred_element_type=F32).astype(BF16)
    for c in range(n_col):
        kc = (c * LANES * d_ssm // sd) // LANES
        bu = jnp.dot(ub[:, kc * LANES:(kc + 1) * LANES], bcat_ref[c], preferred_element_type=F32)
        hre[c] = bu[:, :LANES]
        him[c] = bu[:, LANES:]

    for c0 in range(0, n_col, S5_SCAN_COLS):
        cs = range(c0, c0 + S5_SCAN_COLS)
        ar = [jnp.broadcast_to(are_ref[:, c * LANES:(c + 1) * LANES], (nseq, LANES)) for c in cs]
        ai = [jnp.broadcast_to(aim_ref[:, c * LANES:(c + 1) * LANES], (nseq, LANES)) for c in cs]

        def step(t, carry, cs=cs, ar=ar, ai=ai):
            rows = pl.ds(pl.multiple_of(t * nseq, nseq), nseq)
            out = []
            for n, c in enumerate(cs):
                pr, pi = carry[n]
                xr = hre[c, rows, :] + (ar[n] * pr - ai[n] * pi)
                xi = him[c, rows, :] + (ar[n] * pi + ai[n] * pr)
                hre[c, rows, :] = xr
                him[c, rows, :] = xi
                out.append((xr, xi))
            return tuple(out)

        init = tuple((cr[:, c * LANES:(c + 1) * LANES], ci[:, c * LANES:(c + 1) * LANES]) for c in cs)
        fin = lax.fori_loop(0, tc, step, init, unroll=4)
        for n, c in enumerate(cs):
            cr[:, c * LANES:(c + 1) * LANES] = fin[n][0]
            ci[:, c * LANES:(c + 1) * LANES] = fin[n][1]
    sre_ref[...] = cr[...]
    sim_ref[...] = ci[...]
    n_out = d_ssm // LANES
    per = n_col // n_out
    wide = lambda ref, j: jnp.concatenate([ref[c] for c in range(j * per, (j + 1) * per)], axis=1).astype(BF16)
    for j in range(n_out):
        zs[j] = (jnp.dot(wide(hre, j), ccre_ref[j], preferred_element_type=F32)
                 + jnp.dot(wide(him, j), ccim_ref[j], preferred_element_type=F32))
    y = jnp.concatenate(
        [jnp.concatenate([zs[j, pl.ds(b, tc, stride=nseq), :] for b in range(nseq)], axis=0)
         for j in range(n_out)], axis=1)
    out_ref[...] = _s5_readout(y, u, dsk_ref, wglu_ref, bglu_ref, d_model).reshape(nseq, tc, d_model)


def _s5_sample_kernel(u_ref, h0r_ref, h0i_ref, are_ref, aim_ref, bre_ref, bim_ref, cre_ref, cim_ref,
                      dsk_ref, wglu_ref, bglu_ref, out_ref, sre_ref, sim_ref, *, d_model):
    u = u_ref[...]
    ub = u.astype(BF16)
    ar, ai = are_ref[...], aim_ref[...]
    h0r, h0i = h0r_ref[...], h0i_ref[...]
    hr = (ar * h0r - ai * h0i) + jnp.dot(ub, bre_ref[...], preferred_element_type=F32)
    hi = (ar * h0i + ai * h0r) + jnp.dot(ub, bim_ref[...], preferred_element_type=F32)
    sre_ref[...] = hr
    sim_ref[...] = hi
    y = (jnp.dot(hr.astype(BF16), cre_ref[...], preferred_element_type=F32)
         + jnp.dot(hi.astype(BF16), cim_ref[...], preferred_element_type=F32))
    out_ref[...] = _s5_readout(y, u, dsk_ref, wglu_ref, bglu_ref, d_model)


def _s5_params(a_re, a_im, log_dt, b_re, b_im, c_re, c_im):
    g, p = a_re.shape
    a_c = lax.complex(a_re.astype(F32), a_im.astype(F32))
    dt = jnp.exp(log_dt.astype(F32))[:, None]
    a_bar = jnp.exp(a_c * dt)
    b_bar = ((a_bar - 1.0) / a_c)[:, :, None] * lax.complex(b_re.astype(F32), b_im.astype(F32))
    eye = jnp.eye(g, dtype=F32)
    n = b_re.shape[2]

    def bmat(b):
        return jnp.einsum('gpn,gh->gnhp', b, eye).reshape(g * n, g * p)

    def cmat(c):
        return jnp.einsum('gnp,gh->gphn', c, eye).reshape(g * p, g * n)

    bre, bim = bmat(jnp.real(b_bar)).astype(BF16), bmat(jnp.imag(b_bar)).astype(BF16)
    cre, cim = cmat(c_re.astype(F32)).astype(BF16), cmat(-c_im.astype(F32)).astype(BF16)
    d_in, sd = bre.shape
    assert d_in % LANES == 0 and sd % LANES == 0 and LANES % (LANES * d_in // sd) == 0
    bcat = jnp.stack([
        jnp.concatenate([m[(c * LANES * d_in // sd) // LANES * LANES:][:LANES, c * LANES:(c + 1) * LANES]
                         for m in (bre, bim)], axis=1)
        for c in range(sd // LANES)], axis=0)
    n_out = d_in // LANES
    span = sd // n_out
    ccre, ccim = (jnp.stack([m[j * span:(j + 1) * span, j * LANES:(j + 1) * LANES] for j in range(n_out)],
                            axis=0) for m in (cre, cim))
    a1 = a_bar.reshape(1, g * p)
    return jnp.real(a1), jnp.imag(a1), bre, bim, cre, cim, bcat, ccre, ccim


def _s5_prompt(u3, s5p, dsk, wglu, bglu, tc):
    bsz, t_len, d_ssm = u3.shape
    are, aim, bre, _, _, _, bcat, ccre, ccim = s5p
    sd = bre.shape[1]
    d_model = wglu.shape[1] // 2
    nseq = math.gcd(bsz, S5_SEQS)
    assert sd % (S5_SCAN_COLS * LANES) == 0 and tc % SUBLANES == 0
    kern = functools.partial(_s5_prompt_kernel, tc=tc, d_model=d_model)
    state = pl.BlockSpec((nseq, sd), lambda b, c: (b, 0))
    dst = jnp.arange(nseq * tc)
    perm = (jnp.arange(nseq * tc)[None, :] == ((dst % nseq) * tc + dst // nseq)[:, None]).astype(BF16)
    return pl.pallas_call(
        kern, grid=(bsz // nseq, t_len // tc),
        in_specs=[pl.BlockSpec((nseq, tc, d_ssm), lambda b, c: (b, c, 0)), _const_spec(perm.shape),
                  _const_spec(bcat.shape), _const_spec(are.shape), _const_spec(aim.shape),
                  _const_spec(ccre.shape), _const_spec(ccim.shape), _const_spec((1, d_ssm)),
                  _const_spec(wglu.shape), _const_spec((1, 2 * d_model))],
        out_specs=[pl.BlockSpec((nseq, tc, d_model), lambda b, c: (b, c, 0)), state, state],
        out_shape=[jax.ShapeDtypeStruct((bsz, t_len, d_model), F32),
                   jax.ShapeDtypeStruct((bsz, sd), F32),
                   jax.ShapeDtypeStruct((bsz, sd), F32)],
        scratch_shapes=[pltpu.VMEM((sd // LANES, nseq * tc, LANES), F32),
                        pltpu.VMEM((sd // LANES, nseq * tc, LANES), F32),
                        pltpu.VMEM((d_ssm // LANES, nseq * tc, LANES), F32),
                        pltpu.VMEM((nseq, sd), F32), pltpu.VMEM((nseq, sd), F32)],
        compiler_params=pltpu.CompilerParams(dimension_semantics=("arbitrary", "arbitrary"),
                                             vmem_limit_bytes=VMEM_LIMIT),
        name="s5_prompt",
    )(u3, perm, bcat, are, aim, ccre, ccim, dsk.reshape(1, d_ssm), wglu, bglu.reshape(1, 2 * d_model))


def _s5_sample(u2, h0r, h0i, s5p, dsk, wglu, bglu):
    n, d_ssm = u2.shape
    are, aim, bre, bim, cre, cim = s5p[:6]
    sd = bre.shape[1]
    d_model = wglu.shape[1] // 2
    kern = functools.partial(_s5_sample_kernel, d_model=d_model)
    return pl.pallas_call(
        kern,
        out_shape=[jax.ShapeDtypeStruct((n, d_model), F32),
                   jax.ShapeDtypeStruct((n, sd), F32),
                   jax.ShapeDtypeStruct((n, sd), F32)],
        compiler_params=pltpu.CompilerParams(vmem_limit_bytes=VMEM_LIMIT),
        name="s5_sample",
    )(u2, h0r, h0i, are, aim, bre, bim, cre, cim, dsk.reshape(1, d_ssm), wglu,
      bglu.reshape(1, 2 * d_model))


def _f2k(x):
    b = lax.bitcast_convert_type(x, I32)
    return jnp.where(b < 0, b ^ 0x7FFFFFFF, b)


def _k2f(k):
    return lax.bitcast_convert_type(jnp.where(k < 0, k ^ 0x7FFFFFFF, k), F32)


def _threshold_search(count_ge, amax, forced, k_top):
    kf = float(k_top)
    lo0 = jnp.where(forced, KEY_LOW, _f2k(-amax))
    hi0 = jnp.where(forced, KEY_LOW + 1, _f2k(amax) + 1)

    def cond(st):
        lo, hi, _ = st
        return jnp.max(jnp.where(hi > lo + 1, 1.0, 0.0)) > 0.5

    def body(st):
        lo, hi, it = st
        active = hi > lo + 1
        mid_i = (lo >> 1) + (hi >> 1) + (lo & hi & 1)
        mk = _f2k(0.5 * _k2f(lo) + 0.5 * _k2f(hi))
        use_f = jnp.logical_and(it % 2 == 0, jnp.logical_and(mk > lo, mk < hi))
        mid = jnp.where(use_f, mk, mid_i)
        c = count_ge(_k2f(mid))
        ge = c >= kf
        nlo = jnp.where(ge, mid, lo)
        nhi = jnp.where(c == kf, mid + 1, jnp.where(ge, hi, mid))
        return jnp.where(active, nlo, lo), jnp.where(active, nhi, hi), it + 1

    lo, _, _ = lax.while_loop(cond, body, (lo0, hi0, jnp.int32(0)))
    return _k2f(lo)


def _tri_exclusive(n):
    r = lax.broadcasted_iota(I32, (n, n), 0)
    c = lax.broadcasted_iota(I32, (n, n), 1)
    return (r < c).astype(BF16)


def _pipelined_blocks(n, produce, consume, buf_a, buf_b, carry):
    produce(0, buf_a)

    def pair(t, c):
        produce(2 * t + 1, buf_b)
        c = consume(2 * t, buf_a, c)
        produce(2 * t + 2, buf_a)
        return consume(2 * t + 1, buf_b, c)

    carry = lax.fori_loop(0, n // 2, pair, carry)
    return lax.cond(n % 2 == 1, lambda c: consume(n - 1, buf_a, c), lambda c: c, carry)


def _dsa_prompt_kernel(qis_ref, wt_ref, qs_ref, ki_ref, k_ref, vt_ref, o_ref,
                       sc_ref, scb_ref, m_ref, acc_ref, sa_ref, sb_ref, run_ref, *, tq, tk, k_top):
    i = pl.program_id(1)
    nkb = i + 1
    hpg = N_HEADS // N_KV_HEADS
    ngr = tk // SUBLANES
    kf = float(k_top)

    w8 = wt_ref[...] * (IDX_DIM ** -0.5)
    qpos = i * tq + lax.broadcasted_iota(I32, (SUBLANES, tq), 1)

    def idx_dots(j, dst):
        k0 = pl.multiple_of(jnp.minimum(j, nkb - 1) * tk, tk)
        dst[...] = lax.dot_general(ki_ref[pl.ds(k0, tk), :], qis_ref[...], NT_DIMS, preferred_element_type=F32)

    def score_blk(j, src, carry, diagonal=False):
        for c in range(tq // LANES):
            acc = jnp.zeros((tk, LANES), F32)
            for h in range(N_IDX_HEADS):
                cols = slice(h * tq + c * LANES, h * tq + (c + 1) * LANES)
                acc = acc + jnp.maximum(src[:, cols], 0.0) * w8[h:h + 1, c * LANES:(c + 1) * LANES]
            if diagonal:
                kpos = j * tk + lax.broadcasted_iota(I32, (tk, LANES), 0)
                causal = kpos <= i * tq + c * LANES + lax.broadcasted_iota(I32, (tk, LANES), 1)
                acc = jnp.where(causal, acc, NEG_INF)
            sc_ref[j, :, c * LANES:(c + 1) * LANES] = acc
            top = lax.bitcast_convert_type(lax.bitcast_convert_type(acc, I32) & jnp.int32(-65536), F32)
            scb_ref[j, :, c * LANES:(c + 1) * LANES] = top.astype(BF16)
        return carry

    _pipelined_blocks(nkb - 1, idx_dots, score_blk, sa_ref, sb_ref, jnp.int32(0))
    idx_dots(nkb - 1, sa_ref)
    score_blk(nkb - 1, sa_ref, 0, diagonal=True)

    rep = lambda x: jnp.broadcast_to(x, (SUBLANES, tq))
    forced = (qpos + 1) <= k_top

    packed = 2 * SUBLANES
    one_b, zero_b = jnp.ones((), BF16), jnp.zeros((), BF16)

    def count16(k16):
        bits = jnp.where(k16 < 0, k16 ^ 0x7FFF, k16) << 16
        tb = jnp.broadcast_to(lax.bitcast_convert_type(bits, F32)[0:1, :], (packed, tq)).astype(BF16)

        def body(j, part):
            blk = scb_ref[j].reshape(tk // packed // COUNT_WAYS, COUNT_WAYS, packed, tq)
            hit = jnp.where(blk >= tb[None, None], one_b, zero_b)
            for g in range(hit.shape[0]):
                part = part + hit[g]
            return part

        part = lax.fori_loop(0, nkb, body, jnp.zeros((COUNT_WAYS, packed, tq), BF16))
        return rep(jnp.sum(jnp.sum(part.astype(F32), axis=0), axis=0, keepdims=True))

    def coarse_pass(lo16, hi16, c_lo):
        active = hi16 > lo16 + 1
        mid = lo16 + ((hi16 - lo16) >> 1)
        c = count16(mid)
        up = jnp.logical_and(active, c >= kf)
        nhi = jnp.where(c == kf, mid, jnp.where(c >= kf, hi16, mid))
        return jnp.where(up, mid, lo16), jnp.where(active, nhi, hi16), jnp.where(up, c, c_lo)

    lo16 = jnp.full((SUBLANES, tq), KEY_LOW >> 16, I32)
    hi16 = jnp.where(forced, KEY_LOW >> 16, KEY_INF >> 16)
    c_lo = (qpos + 1).astype(F32)
    lo16, hi16, c_lo = lax.fori_loop(0, 8, lambda _, st: coarse_pass(*coarse_pass(*st)), (lo16, hi16, c_lo))
    lo0 = lo16 << 16
    hi0 = jnp.where(hi16 == lo16, lo0 + 1, hi16 << 16)

    def count(thr, strict):
        def body(j, part):
            blk = sc_ref[j].reshape(ngr // COUNT_WAYS, COUNT_WAYS, SUBLANES, tq)
            hit = (blk > thr[None, None]) if strict else (blk >= thr[None, None])
            return part + jnp.sum(jnp.where(hit, 1.0, 0.0), axis=0)

        part = lax.fori_loop(0, nkb, body, jnp.zeros((COUNT_WAYS, SUBLANES, tq), F32))
        return rep(jnp.sum(jnp.sum(part, axis=0), axis=0, keepdims=True))

    def search_pass(lo, hi, c_lo, value_space, probe_ends=False):
        active = hi > lo + 1
        span = hi - lo
        mid = lo + lax.shift_right_logical(span, jnp.ones_like(span))
        if value_space:
            mk = _f2k(0.5 * _k2f(lo) + 0.5 * _k2f(hi))
            mid = jnp.where(jnp.logical_and(mk > lo, mk < hi), mk, mid)
        if probe_ends:
            near_zero = jnp.logical_and(lo >= 0, hi <= KEY_MIN_NORMAL)
            mid = jnp.where(near_zero, lo + 1 if value_space else hi - 1, mid)
        c = count(_k2f(mid), False)
        up = jnp.logical_and(active, c >= kf)
        nhi = jnp.where(c == kf, mid + 1, jnp.where(c >= kf, hi, mid))
        return jnp.where(up, mid, lo), jnp.where(active, nhi, hi), jnp.where(up, c, c_lo)

    def double_round(lo, hi, c_lo, probe_ends=False):
        return search_pass(*search_pass(lo, hi, c_lo, True, probe_ends), False, probe_ends)

    def unresolved(lo, hi):
        return (jnp.max(jnp.where(hi > lo + 1, 1.0, 0.0)) > 0.5).astype(I32)

    def search_body(st):
        lo, hi, c_lo = double_round(*st[:3])
        return lo, hi, c_lo, unresolved(lo, hi)

    st = double_round(lo0, hi0, c_lo, probe_ends=True)
    lo, hi, c_lo = lax.fori_loop(0, FINE_ROUNDS_UNCHECKED - 1, lambda _, s: double_round(*s), st)
    lo, _, c_ge, _ = lax.while_loop(lambda s: s[3] > 0, search_body, (lo, hi, c_lo, unresolved(lo, hi)))
    thr = _k2f(lo)
    thr1 = thr[0:1, :]

    @pl.when(jnp.max(jnp.where(c_ge > kf, 1.0, 0.0)) > 0.5)
    def _():
        need = jnp.where(c_ge > kf, kf - count(thr, True), 1e30)[0:1, :]
        r = lax.broadcasted_iota(I32, (tk, tk), 0)
        c = lax.broadcasted_iota(I32, (tk, tk), 1)
        earlier = (c < r).astype(BF16)

        def totals(j, run):
            run_ref[j] = run
            eqf = jnp.where(sc_ref[j] == thr1, 1.0, 0.0).reshape(ngr, SUBLANES, tq)
            return run + rep(jnp.sum(jnp.sum(eqf, axis=0), axis=0, keepdims=True))

        lax.fori_loop(0, nkb, totals, jnp.zeros((SUBLANES, tq), F32))

        def prefix(j, dst):
            eqb = jnp.where(sc_ref[jnp.minimum(j, nkb - 1)] == thr1, 1.0, 0.0).astype(BF16)
            dst[:, 0:tq] = jnp.dot(earlier, eqb, preferred_element_type=F32)

        def fix(j, src, carry):
            sc = sc_ref[j]
            before = src[:, 0:tq] + run_ref[j][0:1, :]
            sc_ref[j] = jnp.where(jnp.logical_and(sc == thr1, before >= need), NEG_INF, sc)
            return carry

        _pipelined_blocks(nkb, prefix, fix, sa_ref, sb_ref, jnp.int32(0))

    m_ref[...] = jnp.full(m_ref.shape, NEG_INF, F32)
    acc_ref[...] = jnp.zeros(acc_ref.shape, F32)

    def logits(j, dst):
        jc = jnp.minimum(j, nkb - 1)
        k0 = pl.multiple_of(jc * tk, tk)
        dst[...] = lax.dot_general(k_ref[pl.ds(k0, tk), :], qs_ref[...], NT_DIMS, preferred_element_type=F32)

    def softmax_pv(j, src):
        bias = jnp.where(sc_ref[j] >= thr1, 0.0, NEG_INF)
        for h in range(N_HEADS):
            x = src[:, h * tq:(h + 1) * tq] + bias
            bm = jnp.max(jnp.max(x.reshape(ngr, SUBLANES, tq), axis=0), axis=0, keepdims=True)
            m_old = m_ref[h:h + 1, :]
            m_new = jnp.maximum(m_old, bm)
            m_safe = jnp.where(m_new == NEG_INF, 0.0, m_new)
            p = jnp.exp2(x - m_safe).astype(BF16)
            pv = jnp.dot(vt_ref[j, h // hpg], p, preferred_element_type=F32)
            acc_ref[h] = acc_ref[h] * jnp.exp2(m_old - m_safe) + pv
            m_ref[h:h + 1, :] = m_new

    def consume(j, src, carry):
        softmax_pv(j, src)
        return carry

    _pipelined_blocks(nkb, logits, consume, sa_ref, sb_ref, jnp.int32(0))

    for pr in range(N_HEADS // 2):
        outs = []
        for h in (2 * pr, 2 * pr + 1):
            a = acc_ref[h]
            outs.append(a[:HEAD_DIM, :] / a[HEAD_DIM:HEAD_DIM + 1, :])
        o2 = jnp.concatenate(outs, axis=0)
        o_ref[:, 2 * pr * HEAD_DIM:(2 * pr + 2) * HEAD_DIM] = o2.T.astype(o_ref.dtype)


def _dsa_prompt(qis, wt, qs, kib, kb, vt, tq, tk):
    bsz, t_len, d_kv = kb.shape
    d_attn = N_HEADS * HEAD_DIM
    nkb = t_len // tk
    k_top = min(TOPK_MAX, t_len // 4)
    assert tq == tk and tq % LANES == 0 and t_len % tq == 0 and vt.shape[1] == nkb
    assert qis.shape == (bsz, nkb, N_IDX_HEADS * tq, IDX_DIM) and qs.shape == (bsz, nkb, N_HEADS * tq, d_kv)
    assert nkb * (tk // (2 * SUBLANES * COUNT_WAYS)) <= 256
    kern = functools.partial(_dsa_prompt_kernel, tq=tq, tk=tk, k_top=k_top)
    tile = lambda w: pl.BlockSpec((None, tq, w), lambda b, i: (b, i, 0))
    full = lambda w: pl.BlockSpec((None, t_len, w), lambda b, i: (b, 0, 0))
    stacked = lambda a: pl.BlockSpec((None, None) + a.shape[2:], lambda b, i: (b, i, 0, 0))
    return pl.pallas_call(
        kern, grid=(bsz, t_len // tq),
        in_specs=[stacked(qis), pl.BlockSpec((None, N_IDX_HEADS, tq), lambda b, i: (b, 0, i)), stacked(qs),
                  full(IDX_DIM), full(d_kv),
                  pl.BlockSpec((None, nkb, N_KV_HEADS, VT_ROWS, tk), lambda b, i: (b, 0, 0, 0, 0))],
        out_specs=tile(d_attn),
        out_shape=jax.ShapeDtypeStruct((bsz, t_len, d_attn), BF16),
        scratch_shapes=[pltpu.VMEM((nkb, tk, tq), F32),
                        pltpu.VMEM((nkb, tk, tq), BF16),
                        pltpu.VMEM((N_HEADS, tq), F32),
                        pltpu.VMEM((N_HEADS, VT_ROWS, tq), F32),
                        pltpu.VMEM((tk, N_HEADS * tq), F32),
                        pltpu.VMEM((tk, N_HEADS * tq), F32),
                        pltpu.VMEM((nkb, SUBLANES, tq), F32)],
        compiler_params=pltpu.CompilerParams(dimension_semantics=("arbitrary", "arbitrary"),
                                             vmem_limit_bytes=VMEM_LIMIT),
        name="dsa_prompt",
    )(qis, wt, qs, kib, kb, vt)


def _page_fetcher(pt_ref, n_pages, streams):
    def copies(seq, slot):
        for p in range(n_pages):
            pg = pt_ref[seq * n_pages + p]
            for hbm, buf, sem in streams:
                page = hbm.shape[2]
                yield pltpu.make_async_copy(hbm.at[pg], buf.at[slot, :, pl.ds(p * page, page)], sem.at[slot])

    def fetch(seq, slot):
        for cp in copies(seq, slot):
            cp.start()

    def wait(seq, slot):
        for cp in copies(seq, slot):
            cp.wait()

    return fetch, wait


def _prefetch_next_and_wait(fetch, wait):
    b = pl.program_id(0)
    slot = b % 2

    @pl.when(b == 0)
    def _():
        fetch(0, 0)

    @pl.when(b + 1 < pl.num_programs(0))
    def _():
        fetch(b + 1, 1 - slot)

    wait(b, slot)
    return b, slot


def _idx_score_kernel(pt_ref, q_ref, w_ref, ci_ref, o_ref, ibuf, sem, *, n_pages):
    b, slot = _prefetch_next_and_wait(*_page_fetcher(pt_ref, n_pages, [(ci_ref, ibuf, sem)]))
    q = q_ref[...]
    w = w_ref[...] * (IDX_DIM ** -0.5)
    d = jnp.dot(q, ibuf[slot].astype(BF16), preferred_element_type=F32)
    o_ref[pl.ds(b % SUBLANES, 1), :] = jnp.sum(jnp.maximum(d, 0.0) * w, axis=0, keepdims=True)


def _idx_scores(page_table, qis, wcol, cache_idx_kt):
    sq, n_pages = page_table.shape
    page = cache_idx_kt.shape[2]
    past = n_pages * page
    kern = functools.partial(_idx_score_kernel, n_pages=n_pages)
    grid_spec = pltpu.PrefetchScalarGridSpec(
        num_scalar_prefetch=1, grid=(sq,),
        in_specs=[pl.BlockSpec((None, N_IDX_HEADS, IDX_DIM), lambda b, pt: (b, 0, 0)),
                  pl.BlockSpec((None, N_IDX_HEADS, 1), lambda b, pt: (b, 0, 0)),
                  pl.BlockSpec(memory_space=pl.ANY)],
        out_specs=pl.BlockSpec((SUBLANES, past), lambda b, pt: (b // SUBLANES, 0)),
        scratch_shapes=[pltpu.VMEM((2, IDX_DIM, past), cache_idx_kt.dtype), pltpu.SemaphoreType.DMA((2,))])
    return pl.pallas_call(
        kern, grid_spec=grid_spec,
        out_shape=jax.ShapeDtypeStruct((sq, past), F32),
        compiler_params=pltpu.CompilerParams(dimension_semantics=("arbitrary",),
                                             vmem_limit_bytes=VMEM_LIMIT),
        name="sample_idx_scores",
    )(page_table.reshape(-1), qis, wcol, cache_idx_kt)


def _sample_select_kernel(sc_ref, qi_ref, ki_ref, wi_ref, mask_ref, selfsel_ref, *, ch, k_top):
    sq, l_past = sc_ref.shape
    nch = l_past // ch
    w = wi_ref[...] * (IDX_DIM ** -0.5)
    qf = qi_ref[...].astype(F32)
    kf32 = ki_ref[...].astype(F32)
    s_self = jnp.zeros((sq, 1), F32)
    for h in range(N_IDX_HEADS):
        dh = jnp.sum(qf[:, h * IDX_DIM:(h + 1) * IDX_DIM] * kf32, axis=1, keepdims=True)
        s_self = s_self + jnp.maximum(dh, 0.0) * w[:, h:h + 1]

    amax = jnp.abs(s_self)
    for c in range(nch):
        amax = jnp.maximum(amax, jnp.max(jnp.abs(sc_ref[:, c * ch:(c + 1) * ch]), axis=1, keepdims=True))

    def count_cmp(thr, strict):
        thr_b = jnp.broadcast_to(thr, (sq, LANES))
        part = jnp.zeros((sq, LANES), F32)
        for c in range(l_past // LANES):
            blk = sc_ref[:, c * LANES:(c + 1) * LANES]
            hit = (blk > thr_b) if strict else (blk >= thr_b)
            part = part + jnp.where(hit, 1.0, 0.0)
        self_hit = (s_self > thr) if strict else (s_self >= thr)
        return jnp.sum(part, axis=1, keepdims=True) + jnp.where(self_hit, 1.0, 0.0)

    kf = float(k_top)
    forced = amax < 0.0
    thr = _threshold_search(lambda t: count_cmp(t, False), amax, forced, k_top)
    need = kf - count_cmp(thr, True)
    tri = _tri_exclusive(ch)
    run = jnp.zeros((sq, 1), F32)
    for c in range(nch):
        sc = sc_ref[:, c * ch:(c + 1) * ch]
        eq = sc == thr
        before = jnp.dot(jnp.where(eq, 1.0, 0.0).astype(BF16), tri, preferred_element_type=F32) + run
        keep = jnp.logical_or(sc > thr, jnp.logical_and(eq, before < need))
        mask_ref[:, c * ch:(c + 1) * ch] = jnp.where(keep, 1.0, 0.0)
        run = run + jnp.sum(jnp.where(eq, 1.0, 0.0), axis=1, keepdims=True)
    self_keep = jnp.logical_or(s_self > thr, jnp.logical_and(s_self == thr, run < need))
    selfsel_ref[...] = jnp.where(self_keep, 1.0, 0.0)


def _sample_select(scores, qi, kib, wi, ch, k_top):
    sq, l_past = scores.shape
    kern = functools.partial(_sample_select_kernel, ch=ch, k_top=k_top)
    return pl.pallas_call(
        kern,
        out_shape=[jax.ShapeDtypeStruct((sq, l_past), F32), jax.ShapeDtypeStruct((sq, 1), F32)],
        compiler_params=pltpu.CompilerParams(vmem_limit_bytes=VMEM_LIMIT),
        name="sample_select",
    )(scores, qi, kib, wi)


def _sample_attn_kernel(pt_ref, q_ref, mask_ref, ks_ref, vs_ref, ss_ref, ck_ref, cv_ref, o_ref,
                        kbuf, vbuf, ksem, vsem, *, n_pages):
    _, slot = _prefetch_next_and_wait(
        *_page_fetcher(pt_ref, n_pages, [(ck_ref, kbuf, ksem), (cv_ref, vbuf, vsem)]))
    hpg = N_HEADS // N_KV_HEADS
    q = q_ref[...]
    s = jnp.dot(q, kbuf[slot].astype(BF16), preferred_element_type=F32)
    sm = jnp.where(mask_ref[...] > 0.5, s, NEG_INF)
    s_self = jnp.sum(q.astype(F32) * ks_ref[...].astype(F32), axis=1, keepdims=True)
    s_self = jnp.where(ss_ref[...] > 0.5, s_self, NEG_INF)
    m = jnp.maximum(jnp.max(sm, axis=1, keepdims=True), s_self)
    m = jnp.where(m == NEG_INF, 0.0, m)
    pb = jnp.exp2(sm - m).astype(BF16)
    p_self = jnp.exp2(s_self - m).astype(BF16).astype(F32)
    l_sum = jnp.sum(pb.astype(F32), axis=1, keepdims=True) + p_self
    acc = (lax.dot_general(pb, vbuf[slot].astype(BF16), NT_DIMS, preferred_element_type=F32)
           + p_self * vs_ref[...].astype(F32))
    o = acc / l_sum
    hrow = lax.broadcasted_iota(I32, o.shape, 0)
    o = jnp.where(hrow < hpg, o, pltpu.roll(o, HEAD_DIM, axis=1))
    o_ref[...] = o[:, :HEAD_DIM].astype(o_ref.dtype)


def _sample_attn(page_table, qs, mask3, kself, vself, selfsel, cache_k2, cache_v2):
    sq, n_pages = page_table.shape
    d_kv, page = cache_k2.shape[1], cache_k2.shape[2]
    past = n_pages * page
    kern = functools.partial(_sample_attn_kernel, n_pages=n_pages)
    per_seq = lambda r, w: pl.BlockSpec((None, r, w), lambda b, pt: (b, 0, 0))
    hbm = pl.BlockSpec(memory_space=pl.ANY)
    grid_spec = pltpu.PrefetchScalarGridSpec(
        num_scalar_prefetch=1, grid=(sq,),
        in_specs=[per_seq(N_HEADS, d_kv), per_seq(1, past), per_seq(1, d_kv), per_seq(1, d_kv),
                  per_seq(1, 1), hbm, hbm],
        out_specs=per_seq(N_HEADS, HEAD_DIM),
        scratch_shapes=[pltpu.VMEM((2, d_kv, past), cache_k2.dtype), pltpu.VMEM((2, d_kv, past), cache_v2.dtype),
                        pltpu.SemaphoreType.DMA((2,)), pltpu.SemaphoreType.DMA((2,))])
    return pl.pallas_call(
        kern, grid_spec=grid_spec,
        out_shape=jax.ShapeDtypeStruct((sq, N_HEADS, HEAD_DIM), BF16),
        compiler_params=pltpu.CompilerParams(dimension_semantics=("arbitrary",),
                                             vmem_limit_bytes=VMEM_LIMIT),
        name="sample_attn",
    )(page_table.reshape(-1), qs, mask3, kself, vself, selfsel, cache_k2, cache_v2)


def _tail_kernel(x_ref, attn_ref, ssm_ref, sgs_ref, sga_ref, wao_ref, wo_ref, g2_ref, wup_ref, wdn_ref,
                 gf_ref, y_ref):
    attn_out = jnp.dot(attn_ref[...], wao_ref[...], preferred_element_type=F32)
    mix = sgs_ref[...] * ssm_ref[...] + sga_ref[...] * attn_out
    x1 = x_ref[...] + jnp.dot(mix.astype(BF16), wo_ref[...], preferred_element_type=F32)
    hh = _rms_norm(x1, g2_ref[...]).astype(BF16)
    up = jnp.dot(hh, wup_ref[...], preferred_element_type=F32)
    r = jnp.square(jnp.maximum(up, 0.0)).astype(BF16)
    x2 = x1 + jnp.dot(r, wdn_ref[...], preferred_element_type=F32)
    y_ref[...] = _rms_norm(x2, gf_ref[...])


def _tail(x2d, attn, ssm, sgs, sga, wao, wo, g2, wup, wdn, gf, tm):
    n, d_model = x2d.shape
    row = lambda w: pl.BlockSpec((tm, w), lambda i: (i, 0))
    return pl.pallas_call(
        _tail_kernel, grid=(n // tm,),
        in_specs=[row(d_model), row(attn.shape[1]), row(d_model), row(d_model), row(d_model),
                  _const_spec(wao.shape), _const_spec(wo.shape), _const_spec((1, d_model)),
                  _const_spec(wup.shape), _const_spec(wdn.shape), _const_spec((1, d_model))],
        out_specs=row(d_model),
        out_shape=jax.ShapeDtypeStruct((n, d_model), F32),
        compiler_params=pltpu.CompilerParams(dimension_semantics=("arbitrary",),
                                             vmem_limit_bytes=VMEM_LIMIT),
        name="tail",
    )(x2d, attn, ssm, sgs, sga, wao, wo, g2.reshape(1, d_model), wup, wdn, gf.reshape(1, d_model))


def _tiles(n_rows, t_len):
    tm = min(256, n_rows)
    tc = min(128, t_len)
    tq = min(256, t_len)
    return tm, tc, tq, tq


def kernel(x_prompt, x_sample, cache_k, cache_v, cache_idx_k, state_ssm_re, state_ssm_im, page_table,
           norm1_g, w_in, ssm_a_re, ssm_a_im, ssm_log_dt, ssm_b_re, ssm_b_im, ssm_c_re, ssm_c_im,
           ssm_d, w_glu, b_glu, w_attn_out, w_o, norm2_g, w_up, w_down, normf_g):
    bsz, t_len, d_model = x_prompt.shape
    sq, s_len, _ = x_sample.shape
    assert s_len == 1, "the sample path handles one new token per sequence"
    n_pool, page = cache_idx_k.shape[0], cache_idx_k.shape[1]
    n_pages = page_table.shape[1]
    past = n_pages * page
    d_ssm = ssm_d.shape[0]
    d_attn = N_HEADS * HEAD_DIM
    d_kv = N_KV_HEADS * HEAD_DIM
    d_qi = N_IDX_HEADS * IDX_DIM
    dims = (d_ssm, d_attn, d_kv, d_qi)

    c_ki = d_ssm + d_attn + 2 * d_kv + d_qi
    c_g = c_ki + IDX_DIM + N_IDX_HEADS
    w_pack = jnp.concatenate(
        [w_in[:, :c_g], jnp.zeros((d_model, LANES - IDX_DIM - N_IDX_HEADS), w_in.dtype), w_in[:, c_g:]],
        axis=1).astype(BF16)
    wglu_b, wao_b, wo_b = w_glu.astype(BF16), w_attn_out.astype(BF16), w_o.astype(BF16)
    wup_b, wdn_b = w_up.astype(BF16), w_down.astype(BF16)
    s5p = _s5_params(ssm_a_re, ssm_a_im, ssm_log_dt, ssm_b_re, ssm_b_im, ssm_c_re, ssm_c_im)

    n_p = bsz * t_len
    tm, tc, tq, tk = _tiles(n_p, t_len)
    xp = x_prompt.reshape(n_p, d_model)
    tabs_p = _rope_tables(jnp.arange(t_len, dtype=I32), t_len)
    assert tm == tk
    (u, q, kt, kb, vt, _, vtb, qi, kit, kib, _, wit, sgs, sga) = _in_proj(
        xp, bsz, tabs_p, norm1_g, w_pack, tm, dims)
    ssm_out, re_p, im_p = _s5_prompt(u.reshape(bsz, t_len, d_ssm), s5p, ssm_d, wglu_b, b_glu, tc)
    r3 = lambda a: a.reshape(bsz, t_len, a.shape[-1])
    attn = _dsa_prompt(qi, wit, q, r3(kib), r3(kb), vtb, tq, tk)
    y_p = _tail(xp, attn.reshape(n_p, d_attn), ssm_out.reshape(n_p, d_model), sgs, sga,
                wao_b, wo_b, norm2_g, wup_b, wdn_b, normf_g, tm)

    xs = x_sample.reshape(sq, d_model)
    tabs_s = _rope_tables(jnp.full((sq,), past, I32), sq)
    (u_s, q_s, kt_s, kb_s, vt_s, vb_s, _, qi_s, kit_s, kib_s, wi_s, _, sgs_s, sga_s) = _in_proj(
        xs, 1, tabs_s, norm1_g, w_pack, sq, dims)
    sd = state_ssm_re.shape[1] * state_ssm_re.shape[2]
    ssm_s, re_s, im_s = _s5_sample(u_s, state_ssm_re.reshape(sq, sd), state_ssm_im.reshape(sq, sd),
                                   s5p, ssm_d, wglu_b, b_glu)
    per_seq = lambda a, heads: jnp.swapaxes(a.reshape(heads, sq, a.shape[-1]), 0, 1)
    qih_s = per_seq(qi_s, N_IDX_HEADS)
    scores = _idx_scores(page_table, qih_s, wi_s.reshape(sq, N_IDX_HEADS, 1),
                         jnp.transpose(cache_idx_k, (0, 2, 1)))
    k_top_s = min(TOPK_MAX, (past + s_len) // 4)
    mask, selfsel = _sample_select(scores, qih_s.reshape(sq, d_qi), kib_s, wi_s, min(512, past), k_top_s)
    qs_pad = per_seq(q_s, N_HEADS)
    feat_major = lambda c: jnp.transpose(c, (0, 2, 3, 1)).reshape(n_pool, d_kv, page)
    attn_s = _sample_attn(page_table, qs_pad, mask.reshape(sq, 1, past),
                          kb_s.reshape(sq, 1, d_kv), vb_s.reshape(sq, 1, d_kv), selfsel.reshape(sq, 1, 1),
                          feat_major(cache_k), feat_major(cache_v))
    y_s = _tail(xs, attn_s.reshape(sq, d_attn), ssm_s, sgs_s, sga_s,
                wao_b, wo_b, norm2_g, wup_b, wdn_b, normf_g, sq)

    g_ssm, p_ssm = state_ssm_re.shape[1], state_ssm_re.shape[2]
    kv_out = lambda a: jnp.transpose(a.reshape(a.shape[0], N_KV_HEADS, HEAD_DIM, a.shape[2]), (0, 3, 1, 2))
    return (y_p.reshape(bsz, t_len, d_model), y_s.reshape(sq, s_len, d_model),
            kv_out(kt), kv_out(vt), jnp.transpose(kit, (0, 2, 1)),
            re_p.reshape(bsz, g_ssm, p_ssm), im_p.reshape(bsz, g_ssm, p_ssm),
            kv_out(kt_s).reshape(sq, s_len, N_KV_HEADS, HEAD_DIM),
            kv_out(vt_s).reshape(sq, s_len, N_KV_HEADS, HEAD_DIM),
            jnp.transpose(kit_s, (0, 2, 1)).reshape(sq, s_len, IDX_DIM),
            re_s.reshape(sq, g_ssm, p_ssm), im_s.reshape(sq, g_ssm, p_ssm))
```
